```python
import math
import jax, jax.numpy as jnp
from jax import lax
import numpy as np

D_MODEL = 1024
BATCH = 4
SEQ = 4096
DEPTH = 4

N_MIXERS = 4
EPS = 1e-6
NEG = -1e30
BIG = 1e30
T5_BUCKETS = 32
T5_MAX_DIST = 128
ATTN_HEADS = 16
HEAD_DIM = D_MODEL // ATTN_HEADS
KV_HEADS = 4
GQA = ATTN_HEADS // KV_HEADS
Q_BLOCK = 128
SWA_WINDOW = 128
RWKV_HEAD = 64
RWKV_HEADS = D_MODEL // RWKV_HEAD
RWKV_LORA_W = 64
RWKV_LORA_A = 64
RWKV_GN_EPS = 64e-5
NSA_CMP_LEN = 32
NSA_CMP_STRIDE = 16
NSA_CMP_HIDDEN = 128
NSA_SEL_LEN = 64
NSA_TOPK = 16
NSA_WINDOW = 512
NSA_SEL_QCHUNK = 64
LRU_WIDTH = 1280
LRU_BLOCKS = 16
LRU_BLOCK = LRU_WIDTH // LRU_BLOCKS
LRU_C = 8.0
CONV_WIDTH = 4

kernel_name = "hybrid_swa_rwkv7_nsa_rglru_trunk"


def _layers_of(m):
    return len(range(m, DEPTH, N_MIXERS))


def rms_norm(x, g):
    xf = x.astype(jnp.float32)
    y = xf * lax.rsqrt(jnp.mean(xf * xf, axis=-1, keepdims=True) + EPS)
    return (y * g.astype(jnp.float32)).astype(x.dtype)


def t5_bucket(dist):
    max_exact = T5_BUCKETS // 2
    d = jnp.maximum(dist, 0)
    df = jnp.maximum(d, 1).astype(jnp.float32)
    large = max_exact + (jnp.log(df / max_exact) / math.log(T5_MAX_DIST / max_exact)
                         * (T5_BUCKETS - max_exact)).astype(jnp.int32)
    large = jnp.minimum(large, T5_BUCKETS - 1)
    return jnp.where(d < max_exact, d, large)


def q_heads(t, B, T):
    return t.reshape(B, T, KV_HEADS, GQA, HEAD_DIM).transpose(0, 2, 3, 1, 4)


def kv_heads(t, B, T):
    return t.reshape(B, T, KV_HEADS, HEAD_DIM).transpose(0, 2, 1, 3)


def merge_heads(o, B, T):
    return o.transpose(0, 3, 1, 2, 4).reshape(B, T, ATTN_HEADS * HEAD_DIM)


def banded_attention(q, k, v, t5_table, window, sinks=None):
    B, G, R, T, Dh = q.shape
    nb = T // Q_BLOCK
    nprev = -(-window // Q_BLOCK)
    kc = (nprev + 1) * Q_BLOCK
    qb = q.reshape(B, G, R, nb, Q_BLOCK, Dh)
    pad = ((0, 0), (0, 0), (nprev * Q_BLOCK, 0), (0, 0))
    kp = jnp.pad(k, pad).reshape(B, G, nb + nprev, Q_BLOCK, Dh)
    vp = jnp.pad(v, pad).reshape(B, G, nb + nprev, Q_BLOCK, Dh)
    kb = jnp.concatenate([kp[:, :, j:j + nb] for j in range(nprev + 1)], axis=3)
    vb = jnp.concatenate([vp[:, :, j:j + nb] for j in range(nprev + 1)], axis=3)
    s = jnp.einsum('bgrnqd,bgnkd->bgrnqk', qb, kb,
                   preferred_element_type=jnp.float32) * (Dh ** -0.5)
    dist = nprev * Q_BLOCK + jnp.arange(Q_BLOCK)[:, None] - jnp.arange(kc)[None, :]
    kpos = (jnp.arange(nb)[:, None, None] - nprev) * Q_BLOCK + jnp.arange(kc)[None, None, :]
    valid = (dist >= 0) & (dist < window) & (kpos >= 0)
    bias = jnp.take(t5_table, t5_bucket(dist), axis=0)
    bias = jnp.moveaxis(bias, -1, 0).reshape(G, R, Q_BLOCK, kc).astype(jnp.float32)
    s = jnp.where(valid, s + bias[None, :, :, None], NEG)
    if sinks is None:
        p = jax.nn.softmax(s, axis=-1)
    else:
        sk = jnp.broadcast_to(sinks.astype(jnp.float32).reshape(1, G, R, 1, 1, 1), s.shape[:-1] + (1,))
        p = jax.nn.softmax(jnp.concatenate([s, sk], axis=-1), axis=-1)[..., :-1]
    o = jnp.einsum('bgrnqk,bgnkd->bgrnqd', p.astype(v.dtype), vb)
    return o.reshape(B, G, R, T, Dh)


def swa_sink_mixer(xn, w_in, sinks, w_out, t5_table):
    B, T, _ = xn.shape
    nq, nkv = ATTN_HEADS * HEAD_DIM, KV_HEADS * HEAD_DIM
    q, k, v, z = jnp.split(xn @ w_in, [nq, nq + nkv, nq + 2 * nkv], axis=-1)
    o = banded_attention(q_heads(q, B, T), kv_heads(k, B, T), kv_heads(v, B, T),
                         t5_table, SWA_WINDOW, sinks)
    return (merge_heads(o, B, T) * jax.nn.silu(z)) @ w_out


def rwkv7_mixer(xn, mu, w_in, w0, w1, w2, a0, a1, a2, k_k, k_a, r_k, lnx_w, lnx_b, w_out):
    B, T, D = xn.shape
    H, N = RWKV_HEADS, RWKV_HEAD
    C = H * N
    f32 = jnp.float32
    xx = jnp.pad(xn, ((0, 0), (1, 0), (0, 0)))[:, :-1] - xn
    lerp = xn[None] + xx[None] * mu[:, None, None, :]
    rkvz = jnp.einsum('sbtd,dsc->sbtc', lerp[:4], w_in.reshape(D, 4, C))
    r, k, v, z = rkvz[0], rkvz[1], rkvz[2], rkvz[3]
    w = -jax.nn.softplus(-(w0 + jnp.tanh(lerp[4] @ w1) @ w2)) - 0.5
    a = jax.nn.sigmoid(a0 + (lerp[5] @ a1) @ a2)
    hs = lambda t: t.astype(f32).reshape(B, T, H, N)
    kk = hs(k * k_k)
    kk = kk / jnp.maximum(jnp.sqrt(jnp.sum(kk * kk, axis=-1, keepdims=True)), 1e-12)
    k = hs(k * (1 + (a - 1) * k_a))
    r, v, a = hs(r), hs(v), hs(a)
    decay = jnp.exp(-jnp.exp(hs(w)))
    aa, bb = -kk, kk * a

    def step(S, inp):
        r_t, w_t, k_t, v_t, a_t, b_t = inp
        sa = jnp.einsum('bhij,bhj->bhi', S, a_t)
        S = S * w_t[:, :, None, :] + sa[..., None] * b_t[:, :, None, :] + v_t[..., None] * k_t[:, :, None, :]
        return S, jnp.einsum('bhij,bhj->bhi', S, r_t)

    xs = tuple(jnp.moveaxis(t, 1, 0) for t in (r, decay, k, v, aa, bb))
    _, y = lax.scan(step, jnp.zeros((B, H, N, N), f32), xs)
    y = jnp.moveaxis(y, 0, 1)
    mean = jnp.mean(y, axis=-1, keepdims=True)
    var = jnp.mean(jnp.square(y - mean), axis=-1, keepdims=True)
    y = ((y - mean) * lax.rsqrt(var + RWKV_GN_EPS)).reshape(B, T, C) * lnx_w.astype(f32) + lnx_b.astype(f32)
    bonus = jnp.sum(r * k * r_k.astype(f32), axis=-1, keepdims=True) * v
    y = (y + bonus.reshape(B, T, C)) * jax.nn.silu(z.astype(f32))
    return y.astype(xn.dtype) @ w_out


def nsa_mixer(xn, w_in, cmp_pos_k, cmp_k_w1, cmp_k_w2, cmp_pos_v, cmp_v_w1, cmp_v_w2, w_out, t5_table):
    B, T, _ = xn.shape
    G, R, Dh = KV_HEADS, GQA, HEAD_DIM
    f32 = jnp.float32
    nq, nkv = ATTN_HEADS * Dh, G * Dh
    splits = np.cumsum([nq] + [nkv] * 6 + [3 * ATTN_HEADS]).tolist()
    q, kc, vc, ks, vs, kw, vw, gates, z = jnp.split(xn @ w_in, splits, axis=-1)
    q = q_heads(q, B, T)
    kc, vc, ks, vs, kw, vw = (kv_heads(t, B, T) for t in (kc, vc, ks, vs, kw, vw))
    scale = Dh ** -0.5
    tpos = jnp.arange(T)

    n_cmp = (T - NSA_CMP_LEN) // NSA_CMP_STRIDE + 1
    tok_idx = np.arange(n_cmp)[:, None] * NSA_CMP_STRIDE + np.arange(NSA_CMP_LEN)[None, :]

    def compress(t, pos, w1, w2):
        blk = (t[:, :, tok_idx] + pos).reshape(B, G, n_cmp, NSA_CMP_LEN * Dh)
        return jax.nn.silu(blk @ w1) @ w2

    k_cmp = compress(kc, cmp_pos_k, cmp_k_w1, cmp_k_w2)
    v_cmp = compress(vc, cmp_pos_v, cmp_v_w1, cmp_v_w2)
    cmp_start = jnp.arange(n_cmp) * NSA_CMP_STRIDE
    cmp_end = cmp_start + NSA_CMP_LEN - 1
    cmp_ok = cmp_end[None, :] <= tpos[:, None]
    s_c = jnp.einsum('bgrtd,bgnd->bgrtn', q, k_cmp, preferred_element_type=f32) * scale
    p_c = jax.nn.softmax(jnp.where(cmp_ok, s_c, NEG), axis=-1) * cmp_ok
    o_cmp = jnp.einsum('bgrtn,bgnd->bgrtd', p_c.astype(v_cmp.dtype), v_cmp)

    n_sel = T // NSA_SEL_LEN
    k_top = min(NSA_TOPK, n_sel)
    sel_start = jnp.arange(n_sel) * NSA_SEL_LEN
    overlap = ((cmp_start[:, None] < sel_start[None, :] + NSA_SEL_LEN)
               & (cmp_end[:, None] >= sel_start[None, :])).astype(f32)
    imp = jnp.einsum('bgrtn,ns->bgts', p_c, overlap)
    blk = jnp.arange(n_sel)[None, :]
    cur = (tpos // NSA_SEL_LEN)[:, None]
    forced = (blk == 0) | (blk == cur) | (blk == cur - 1)
    future = sel_start[None, :] > tpos[:, None]
    imp = jnp.where(forced, BIG, jnp.where(future, NEG, imp))
    _, sel_idx = lax.top_k(imp, k_top)

    ks_blk = ks.reshape(B, G, n_sel, NSA_SEL_LEN, Dh)
    vs_blk = vs.reshape(B, G, n_sel, NSA_SEL_LEN, Dh)
    QC = NSA_SEL_QCHUNK
    nch = T // QC
    q_ch = jnp.moveaxis(q.reshape(B, G, R, nch, QC, Dh), 3, 0)
    idx_ch = jnp.moveaxis(sel_idx.reshape(B, G, nch, QC, k_top), 2, 0)
    pos_ch = tpos.reshape(nch, QC)
    bi = jnp.arange(B)[:, None, None, None]
    gi = jnp.arange(G)[None, :, None, None]
    gi5 = jnp.arange(G)[None, :, None, None, None]
    table_g = t5_table.reshape(T5_BUCKETS, G, R)

    def sel_chunk(args):
        qc, ic, pc = args
        kg = ks_blk[bi, gi, ic]
        vg = vs_blk[bi, gi, ic]
        kpos = ic[..., None] * NSA_SEL_LEN + jnp.arange(NSA_SEL_LEN)
        dist = pc[None, None, :, None, None] - kpos
        s = jnp.einsum('bgrqd,bgqskd->bgrqsk', qc, kg, preferred_element_type=f32) * scale
        bias = jnp.moveaxis(table_g[t5_bucket(dist), gi5], -1, 2).astype(f32)
        s = jnp.where((dist >= 0)[:, :, None], s + bias, NEG)
        p = jax.nn.softmax(s.reshape(B, G, R, QC, k_top * NSA_SEL_LEN), axis=-1)
        p = p.reshape(B, G, R, QC, k_top, NSA_SEL_LEN)
        return jnp.einsum('bgrqsk,bgqskd->bgrqd', p.astype(vg.dtype), vg)

    o_sel = lax.map(sel_chunk, (q_ch, idx_ch, pos_ch))
    o_sel = jnp.moveaxis(o_sel, 0, 3).reshape(B, G, R, T, Dh)

    o_win = banded_attention(q, kw, vw, t5_table, NSA_WINDOW)

    g = jax.nn.sigmoid(gates).reshape(B, T, 3, G, R).transpose(2, 0, 3, 4, 1)[..., None]
    o = g[0] * o_cmp + g[1] * o_sel + g[2] * o_win
    return (merge_heads(o, B, T) * jax.nn.silu(z)) @ w_out


def rglru_mixer(xn, w_in, conv_w, conv_b, gate_a_w, gate_a_b, gate_x_w, gate_x_b, lam, w_out):
    B, T, _ = xn.shape
    f32 = jnp.float32
    u, z = jnp.split(xn @ w_in, 2, axis=-1)
    u = lax.conv_general_dilated(u, conv_w[:, None, :], window_strides=(1,),
                                 padding=[(CONV_WIDTH - 1, 0)],
                                 dimension_numbers=('NWC', 'WIO', 'NWC'),
                                 feature_group_count=LRU_WIDTH) + conv_b
    ub = u.reshape(B, T, LRU_BLOCKS, LRU_BLOCK)
    r = jax.nn.sigmoid(jnp.einsum('btnc,ncd->btnd', ub, gate_a_w).reshape(B, T, LRU_WIDTH) + gate_a_b)
    i = jax.nn.sigmoid(jnp.einsum('btnc,ncd->btnd', ub, gate_x_w).reshape(B, T, LRU_WIDTH) + gate_x_b)
    log_a = -LRU_C * r.astype(f32) * jax.nn.softplus(-lam.astype(f32))
    a = jnp.exp(log_a)
    b = jnp.sqrt(-jnp.expm1(2.0 * log_a)) * (i * u).astype(f32)

    def combine(left, right):
        a1, b1 = left
        a2, b2 = right
        return a1 * a2, a2 * b1 + b2

    _, h = lax.associative_scan(combine, (a, b), axis=1)
    return (h.astype(xn.dtype) * jax.nn.silu(z)) @ w_out


def setup_inputs(seed: int = 0) -> dict:
    key = jax.random.key(seed)
    keys = iter(jax.random.split(key, 64))
    f32 = jnp.float32

    def nrm(shape, scale):
        return jax.random.normal(next(keys), shape, f32) * scale

    def dense(shape):
        return nrm(shape, shape[-2] ** -0.5)

    def unif(shape, lo, hi):
        return jax.random.uniform(next(keys), shape, f32, lo, hi)

    LA, LB, LC, LD = (_layers_of(m) for m in range(N_MIXERS))
    C = RWKV_HEADS * RWKV_HEAD
    nq, nkv = ATTN_HEADS * HEAD_DIM, KV_HEADS * HEAD_DIM
    a_cols = 2 * nq + 2 * nkv
    c_cols = 2 * nq + 6 * nkv + 3 * ATTN_HEADS
    lam_u = unif((LD, LRU_WIDTH), 0.9, 0.999)
    return {
        "x": nrm((BATCH, SEQ, D_MODEL), 1.0),
        "t5_table": nrm((T5_BUCKETS, ATTN_HEADS), 0.5),
        "norm_g": 1.0 + nrm((DEPTH, D_MODEL), 0.05),
        "final_g": 1.0 + nrm((D_MODEL,), 0.05),
        "a_w_in": dense((LA, D_MODEL, a_cols)),
        "a_sinks": nrm((LA, ATTN_HEADS), 0.5),
        "a_w_out": dense((LA, nq, D_MODEL)),
        "b_mu": unif((LB, 6, D_MODEL), 0.0, 1.0),
        "b_w_in": dense((LB, D_MODEL, 4 * C)),
        "b_w0": unif((LB, C), -6.0, 0.0),
        "b_w1": dense((LB, D_MODEL, RWKV_LORA_W)),
        "b_w2": dense((LB, RWKV_LORA_W, C)),
        "b_a0": nrm((LB, C), 0.5),
        "b_a1": dense((LB, D_MODEL, RWKV_LORA_A)),
        "b_a2": dense((LB, RWKV_LORA_A, C)),
        "b_k_k": 0.85 + nrm((LB, C), 0.1),
        "b_k_a": 1.0 + nrm((LB, C), 0.1),
        "b_r_k": nrm((LB, RWKV_HEADS, RWKV_HEAD), 0.1),
        "b_lnx_w": 1.0 + nrm((LB, C), 0.05),
        "b_lnx_b": nrm((LB, C), 0.01),
        "b_w_out": dense((LB, C, D_MODEL)),
        "c_w_in": dense((LC, D_MODEL, c_cols)),
        "c_cmp_pos_k": nrm((LC, NSA_CMP_LEN, HEAD_DIM), 0.1),
        "c_cmp_k_w1": dense((LC, NSA_CMP_LEN * HEAD_DIM, NSA_CMP_HIDDEN)),
        "c_cmp_k_w2": dense((LC, NSA_CMP_HIDDEN, HEAD_DIM)),
        "c_cmp_pos_v": nrm((LC, NSA_CMP_LEN, HEAD_DIM), 0.1),
        "c_cmp_v_w1": dense((LC, NSA_CMP_LEN * HEAD_DIM, NSA_CMP_HIDDEN)),
        "c_cmp_v_w2": dense((LC, NSA_CMP_HIDDEN, HEAD_DIM)),
        "c_w_out": dense((LC, nq, D_MODEL)),
        "d_w_in": dense((LD, D_MODEL, 2 * LRU_WIDTH)),
        "d_conv_w": nrm((LD, CONV_WIDTH, LRU_WIDTH), CONV_WIDTH ** -0.5),
        "d_conv_b": nrm((LD, LRU_WIDTH), 0.01),
        "d_gate_a_w": dense((LD, LRU_BLOCKS, LRU_BLOCK, LRU_BLOCK)),
        "d_gate_a_b": nrm((LD, LRU_WIDTH), 0.01),
        "d_gate_x_w": dense((LD, LRU_BLOCKS, LRU_BLOCK, LRU_BLOCK)),
        "d_gate_x_b": nrm((LD, LRU_WIDTH), 0.01),
        "d_lambda": jnp.log(lam_u) - jnp.log1p(-lam_u),
        "d_w_out": dense((LD, LRU_WIDTH, D_MODEL)),
    }


def reference(x, t5_table, norm_g, final_g,
              a_w_in, a_sinks, a_w_out,
              b_mu, b_w_in, b_w0, b_w1, b_w2, b_a0, b_a1, b_a2, b_k_k, b_k_a, b_r_k,
              b_lnx_w, b_lnx_b, b_w_out,
              c_w_in, c_cmp_pos_k, c_cmp_k_w1, c_cmp_k_w2, c_cmp_pos_v, c_cmp_v_w1, c_cmp_v_w2, c_w_out,
              d_w_in, d_conv_w, d_conv_b, d_gate_a_w, d_gate_a_b, d_gate_x_w, d_gate_x_b,
              d_lambda, d_w_out):
    for layer in range(DEPTH):
        m, j = layer % N_MIXERS, layer // N_MIXERS
        xn = rms_norm(x, norm_g[layer])
        if m == 0:
            y = swa_sink_mixer(xn, a_w_in[j], a_sinks[j], a_w_out[j], t5_table)
        elif m == 1:
            y = rwkv7_mixer(xn, b_mu[j], b_w_in[j], b_w0[j], b_w1[j], b_w2[j], b_a0[j], b_a1[j],
                            b_a2[j], b_k_k[j], b_k_a[j], b_r_k[j], b_lnx_w[j], b_lnx_b[j], b_w_out[j])
        elif m == 2:
            y = nsa_mixer(xn, c_w_in[j], c_cmp_pos_k[j], c_cmp_k_w1[j], c_cmp_k_w2[j],
                          c_cmp_pos_v[j], c_cmp_v_w1[j], c_cmp_v_w2[j], c_w_out[j], t5_table)
        else:
            y = rglru_mixer(xn, d_w_in[j], d_conv_w[j], d_conv_b[j], d_gate_a_w[j], d_gate_a_b[j],
                            d_gate_x_w[j], d_gate_x_b[j], d_lambda[j], d_w_out[j])
        x = x + y.astype(x.dtype)
    return rms_norm(x, final_g)
```

```python
import functools
import math

import jax
import jax.numpy as jnp
from jax import lax
from jax.experimental import pallas as pl
from jax.experimental.pallas import tpu as pltpu

F32 = jnp.float32
BF16 = jnp.bfloat16

EPS = 1e-6
NEG = -1e30
BIG = 1e30
T5_BUCKETS = 32
T5_MAX_DIST = 128
ATTN_HEADS = 16
HEAD_DIM = 64
KV_HEADS = 4
GQA = ATTN_HEADS // KV_HEADS
Q_BLOCK = 128
SWA_WINDOW = 128
RWKV_HEAD = 64
RWKV_GN_EPS = 64e-5
NSA_CMP_LEN = 32
NSA_CMP_STRIDE = 16
NSA_SEL_LEN = 64
NSA_TOPK = 16
NSA_WINDOW = 512
LRU_BLOCKS = 16
LRU_C = 8.0
CONV_WIDTH = 4

LANES = 128
VMEM_LIMIT = 56 * 1024 * 1024
ROW_TILE = 256
RWKV_CHUNK = 64
RWKV_BLOCK = 128
LRU_TILE = 256


def _cparams(*sem):
    return pltpu.CompilerParams(dimension_semantics=sem, vmem_limit_bytes=VMEM_LIMIT)


def _bdot(a, b):
    return jnp.dot(a.astype(BF16), b.astype(BF16), preferred_element_type=F32)


def _bdot_nt(a, b):
    return lax.dot_general(a.astype(BF16), b.astype(BF16), (((1,), (1,)), ((), ())),
                           preferred_element_type=F32)


def _bdot_tn(a, b):
    return lax.dot_general(a.astype(BF16), b.astype(BF16), (((0,), (0,)), ((), ())),
                           preferred_element_type=F32)


def _split2(x):
    hi = x.astype(BF16)
    lo = (x - hi.astype(F32)).astype(BF16)
    return hi, lo


def _dot3(a, b, dot=_bdot):
    ah, al = _split2(a)
    bh, bl = _split2(b)
    return dot(ah, bh) + (dot(ah, bl) + dot(al, bh))


def _dot_exact_rhs(a, b01, dot=_bdot):
    ah = a.astype(BF16)
    r1 = a - ah.astype(F32)
    am = r1.astype(BF16)
    al = (r1 - am.astype(F32)).astype(BF16)
    return dot(ah, b01) + (dot(am, b01) + dot(al, b01))


def _rms(x, g):
    return x * lax.rsqrt(jnp.mean(x * x, axis=-1, keepdims=True) + EPS) * g


def _silu(z):
    return z * jax.nn.sigmoid(z)


def _gate_column(gates, c):
    col = lax.broadcasted_iota(jnp.int32, gates.shape, 1)
    return jax.nn.sigmoid(jnp.sum(jnp.where(col == c, gates, 0.0), axis=-1, keepdims=True))


def _norm_proj_kernel(x_ref, g_ref, *refs, n_out, head_major):
    w_refs, o_refs = refs[:n_out], refs[n_out:]
    xb = _rms(x_ref[...], g_ref[...]).astype(BF16)
    for w_ref, o_ref, hm in zip(w_refs, o_refs, head_major):
        width = w_ref.shape[1]
        for c0 in range(0, width, 512):
            cw = min(512, width - c0)
            acc = jnp.dot(xb, w_ref[:, c0:c0 + cw], preferred_element_type=F32)
            if hm:
                for j in range(cw // HEAD_DIM):
                    o_ref[0, (c0 // HEAD_DIM) + j] = acc[:, j * HEAD_DIM:(j + 1) * HEAD_DIM].astype(o_ref.dtype)
            else:
                o_ref[:, c0:c0 + cw] = acc.astype(o_ref.dtype)


def _norm_proj(x2d, g, weights, head_major, dtypes, batch, seq):
    n, d = x2d.shape
    tm = min(ROW_TILE, seq)
    tiles_per_seq = seq // tm
    in_specs = [pl.BlockSpec((tm, d), lambda i: (i, 0)), pl.BlockSpec((1, d), lambda i: (0, 0))]
    out_specs, out_shapes = [], []
    for w, hm, dt in zip(weights, head_major, dtypes):
        width = w.shape[1]
        in_specs.append(pl.BlockSpec((d, width), lambda i: (0, 0)))
        if hm:
            nh = width // HEAD_DIM
            out_shapes.append(jax.ShapeDtypeStruct((batch, nh, seq, HEAD_DIM), dt))
            out_specs.append(pl.BlockSpec((1, nh, tm, HEAD_DIM),
                                          lambda i: (i // tiles_per_seq, 0, i % tiles_per_seq, 0)))
        else:
            out_shapes.append(jax.ShapeDtypeStruct((n, width), dt))
            out_specs.append(pl.BlockSpec((tm, width), lambda i: (i, 0)))
    kern = functools.partial(_norm_proj_kernel, n_out=len(weights), head_major=tuple(head_major))
    return pl.pallas_call(
        kern, grid=(n // tm,), in_specs=in_specs, out_specs=out_specs, out_shape=out_shapes,
        compiler_params=_cparams("parallel"), name="norm_proj",
    )(x2d, g.reshape(1, d), *[w.astype(BF16) for w in weights])


def _out_proj_kernel(*refs, n_a, has_z, has_g):
    x_ref = refs[0]
    a_refs = refs[1:1 + n_a]
    pos = 1 + n_a
    z_ref = refs[pos] if has_z else None
    pos += int(has_z)
    w_ref = refs[pos]
    pos += 1
    g_ref = refs[pos] if has_g else None
    o_ref = refs[-1]
    a = a_refs[0][...].astype(F32)
    for r in a_refs[1:]:
        a = a + r[...].astype(F32)
    if has_z:
        a = a * _silu(z_ref[...])
    y = x_ref[...] + jnp.dot(a.astype(BF16), w_ref[...], preferred_element_type=F32)
    if has_g:
        y = _rms(y, g_ref[...])
    o_ref[...] = y


def _out_proj(x2d, a_list, z, w, final_g=None):
    n, d = x2d.shape
    c = w.shape[0]
    tm = min(ROW_TILE, n)
    row = lambda i: (i, 0)
    fixed = lambda i: (0, 0)
    in_specs = [pl.BlockSpec((tm, d), row)] + [pl.BlockSpec((tm, c), row) for _ in a_list]
    args = [x2d] + list(a_list)
    if z is not None:
        in_specs.append(pl.BlockSpec((tm, c), row))
        args.append(z)
    in_specs.append(pl.BlockSpec((c, d), fixed))
    args.append(w.astype(BF16))
    if final_g is not None:
        in_specs.append(pl.BlockSpec((1, d), fixed))
        args.append(final_g.reshape(1, d))
    kern = functools.partial(_out_proj_kernel, n_a=len(a_list), has_z=z is not None,
                             has_g=final_g is not None)
    return pl.pallas_call(
        kern, grid=(n // tm,), in_specs=in_specs, out_specs=pl.BlockSpec((tm, d), row),
        out_shape=jax.ShapeDtypeStruct((n, d), F32), compiler_params=_cparams("parallel"),
        name="out_proj",
    )(*args)


def _t5_bucket(dist):
    max_exact = T5_BUCKETS // 2
    d = jnp.maximum(dist, 0)
    df = jnp.maximum(d, 1).astype(F32)
    large = max_exact + (jnp.log(df / max_exact) / math.log(T5_MAX_DIST / max_exact)
                         * (T5_BUCKETS - max_exact)).astype(jnp.int32)
    large = jnp.minimum(large, T5_BUCKETS - 1)
    return jnp.where(d < max_exact, d, large)


def _band_bias(t5_table, nprev):
    kc = (nprev + 1) * Q_BLOCK
    dist = nprev * Q_BLOCK + jnp.arange(Q_BLOCK)[:, None] - jnp.arange(kc)[None, :]
    bias = jnp.take(t5_table, _t5_bucket(dist), axis=0)
    return jnp.moveaxis(bias, -1, 0).astype(F32)


def _band_attn_kernel(*refs, nprev, window, has_sink, gate_col):
    q_ref, k_ref, v_ref, bias_ref = refs[:4]
    pos = 4
    sink_ref = refs[pos] if has_sink else None
    pos += int(has_sink)
    gate_ref = refs[pos] if gate_col is not None else None
    o_ref = refs[-1]
    n = pl.program_id(1)
    tq = Q_BLOCK
    kc = (nprev + 1) * tq
    qi = lax.broadcasted_iota(jnp.int32, (tq, kc), 0)
    kj = lax.broadcasted_iota(jnp.int32, (tq, kc), 1)
    dist = nprev * tq + qi - kj
    kpos = (n - nprev) * tq + kj
    valid = ((dist >= 0) & (dist < window) & (kpos >= 0))[None]
    starts = [pl.multiple_of(jnp.maximum(n - nprev + j, 0) * tq, tq) for j in range(nprev + 1)]
    for g in range(KV_HEADS):
        kg = jnp.concatenate([k_ref[0, g, pl.ds(s, tq), :] for s in starts], axis=0)
        vg = jnp.concatenate([v_ref[0, g, pl.ds(s, tq), :] for s in starts], axis=0)
        qg = q_ref[0, GQA * g:GQA * (g + 1)].reshape(GQA * tq, HEAD_DIM) * (HEAD_DIM ** -0.5)
        s = _bdot_nt(qg, kg).reshape(GQA, tq, kc) + bias_ref[GQA * g:GQA * (g + 1)]
        s = jnp.where(valid, s, NEG)
        m = jnp.max(s, axis=-1, keepdims=True)
        if has_sink:
            sk = sink_ref[GQA * g:GQA * (g + 1)]
            m = jnp.maximum(m, sk)
        p = jnp.exp(s - m)
        l = jnp.sum(p, axis=-1, keepdims=True)
        if has_sink:
            l = l + jnp.exp(sk - m)
        o = _bdot(p.reshape(GQA * tq, kc), vg).reshape(GQA, tq, HEAD_DIM) / l
        outs = []
        for r in range(GQA):
            oh = o[r]
            if gate_col is not None:
                c = gate_col + GQA * g + r
                oh = oh * jax.nn.sigmoid(gate_ref[0, :, c:c + 1])
            outs.append(oh)
        o_ref[0, :, g * GQA * HEAD_DIM:(g + 1) * GQA * HEAD_DIM] = jnp.concatenate(outs, axis=-1)


def _band_attn(q, k, v, bias, window, sinks=None, gates=None, gate_col=None):
    b, _, t, _ = q.shape
    nprev = -(-window // Q_BLOCK)
    kc = (nprev + 1) * Q_BLOCK
    in_specs = [
        pl.BlockSpec((1, ATTN_HEADS, Q_BLOCK, HEAD_DIM), lambda i, n: (i, 0, n, 0)),
        pl.BlockSpec((1, KV_HEADS, t, HEAD_DIM), lambda i, n: (i, 0, 0, 0)),
        pl.BlockSpec((1, KV_HEADS, t, HEAD_DIM), lambda i, n: (i, 0, 0, 0)),
        pl.BlockSpec((ATTN_HEADS, Q_BLOCK, kc), lambda i, n: (0, 0, 0)),
    ]
    args = [q, k, v, bias]
    if sinks is not None:
        in_specs.append(pl.BlockSpec((ATTN_HEADS, 1, 1), lambda i, n: (0, 0, 0)))
        args.append(sinks.astype(F32).reshape(ATTN_HEADS, 1, 1))
    if gates is not None:
        in_specs.append(pl.BlockSpec((1, Q_BLOCK, LANES), lambda i, n: (i, n, 0)))
        args.append(gates)
    kern = functools.partial(_band_attn_kernel, nprev=nprev, window=window,
                             has_sink=sinks is not None, gate_col=gate_col)
    d = ATTN_HEADS * HEAD_DIM
    return pl.pallas_call(
        kern, grid=(b, t // Q_BLOCK), in_specs=in_specs,
        out_specs=pl.BlockSpec((1, Q_BLOCK, d), lambda i, n: (i, n, 0)),
        out_shape=jax.ShapeDtypeStruct((b, t, d), F32),
        compiler_params=_cparams("parallel", "parallel"), name="band_attn",
    )(*args)


def _swa_layer(x2d, g, w_in, sinks, w_out, t5_table, final_g, batch, seq):
    nq, nkv = ATTN_HEADS * HEAD_DIM, KV_HEADS * HEAD_DIM
    ws = [w_in[:, :nq], w_in[:, nq:nq + nkv], w_in[:, nq + nkv:nq + 2 * nkv], w_in[:, nq + 2 * nkv:]]
    q, k, v, z = _norm_proj(x2d, g, ws, [True, True, True, False], [BF16, BF16, BF16, F32], batch, seq)
    bias = _band_bias(t5_table, -(-SWA_WINDOW // Q_BLOCK))
    o = _band_attn(q, k, v, bias, SWA_WINDOW, sinks=sinks)
    return _out_proj(x2d, [o.reshape(x2d.shape)], z, w_out, final_g)


def _rwkv_pre_kernel(x_ref, xp_ref, g_ref, mu_ref, wr_ref, wk_ref, wv_ref, wz_ref,
                     w0_ref, w1_ref, w2_ref, a0_ref, a1_ref, a2_ref,
                     r_ref, k_ref, v_ref, z_ref, lw_ref, a_ref, *, tiles_per_seq):
    i = pl.program_id(0)
    g = g_ref[...]
    xn = _rms(x_ref[...], g)
    prev = _rms(xp_ref[...], g)[7:8]
    prev = jnp.where(i % tiles_per_seq == 0, 0.0, prev)
    row = lax.broadcasted_iota(jnp.int32, xn.shape, 0)
    xprev = jnp.where(row == 0, prev, pltpu.roll(xn, 1, 0))
    xx = xprev - xn
    lerp = lambda s: xn + xx * mu_ref[s:s + 1, :]
    r_ref[...] = _bdot(lerp(0), wr_ref[...])
    k_ref[...] = _bdot(lerp(1), wk_ref[...])
    v_ref[...] = _bdot(lerp(2), wv_ref[...])
    z_ref[...] = _bdot(lerp(3), wz_ref[...])
    wl = w0_ref[...] + _bdot(jnp.tanh(_bdot(lerp(4), w1_ref[...])), w2_ref[...])
    sp = jnp.maximum(-wl, 0.0) + jnp.log1p(jnp.exp(-jnp.abs(wl)))
    lw_ref[...] = -jnp.exp(-sp - 0.5)
    al = a0_ref[...] + _bdot(_bdot(lerp(5), a1_ref[...]), a2_ref[...])
    a_ref[...] = jax.nn.sigmoid(al)


def _rwkv_pre(x2d, g, mu, w_in, w0, w1, w2, a0, a1, a2, seq):
    n, d = x2d.shape
    c = w0.shape[0]
    tm = min(ROW_TILE, seq)
    tiles_per_seq = seq // tm
    row = lambda i: (i, 0)
    fixed = lambda i: (0, 0)
    ws = [w_in[:, s * c:(s + 1) * c].astype(BF16) for s in range(4)]
    in_specs = [
        pl.BlockSpec((tm, d), row),
        pl.BlockSpec((8, d), lambda i: (jnp.maximum(i * (tm // 8) - 1, 0), 0)),
        pl.BlockSpec((1, d), fixed), pl.BlockSpec((6, d), fixed),
    ] + [pl.BlockSpec((d, c), fixed)] * 4 + [
        pl.BlockSpec((1, c), fixed), pl.BlockSpec(w1.shape, fixed), pl.BlockSpec(w2.shape, fixed),
        pl.BlockSpec((1, c), fixed), pl.BlockSpec(a1.shape, fixed), pl.BlockSpec(a2.shape, fixed),
    ]
    out = jax.ShapeDtypeStruct((n, c), F32)
    kern = functools.partial(_rwkv_pre_kernel, tiles_per_seq=tiles_per_seq)
    return pl.pallas_call(
        kern, grid=(n // tm,), in_specs=in_specs, out_specs=[pl.BlockSpec((tm, c), row)] * 6,
        out_shape=[out] * 6, compiler_params=_cparams("parallel"), name="rwkv_pre",
    )(x2d, x2d, g.reshape(1, d), mu, *ws, w0.reshape(1, c), w1.astype(BF16), w2.astype(BF16),
      a0.reshape(1, c), a1.astype(BF16), a2.astype(BF16))


def _rwkv_chunk_ops(r, kp, v, aa, bb, lw, tri_incl, lower_strict, lower_incl, eye, blockdiag):
    L = RWKV_CHUNK
    cum = _dot_exact_rhs(lw, tri_incl, dot=lambda a, b: _bdot(b, a))
    w_incl = jnp.exp(cum)
    w_inv = jnp.exp(-cum)
    cum_last = cum[L - 1:L, :]
    w_tail = jnp.exp(cum_last - cum)
    at = aa * jnp.exp(cum - lw)
    rt = r * w_incl
    bt = bb * w_inv
    kt = kp * w_inv
    bh = bb * w_tail
    kh = kp * w_tail
    ar = jnp.concatenate([at, rt], axis=0)
    bk = jnp.concatenate([bt, kt], axis=0)
    A = _dot3(ar, bk, dot=_bdot_nt)
    a_ab = jnp.where(lower_strict, A[:L, :L], 0.0)
    a_ak = jnp.where(lower_strict, A[:L, L:], 0.0)
    a_rb = jnp.where(lower_incl, A[L:, :L], 0.0)
    a_rk = jnp.where(lower_incl, A[L:, L:], 0.0)
    akv = _dot3(jnp.concatenate([a_ak, a_rk], axis=0), v)
    dblk = jnp.where(blockdiag, a_ab, 0.0)
    d2 = _dot3(dblk, dblk)
    d4 = _dot3(d2, d2)
    d8 = _dot3(d4, d4)
    tdiag = _dot3(_dot3(eye + dblk, eye + d2), _dot3(eye + d4, eye + d8))
    rhs = jnp.concatenate([at, akv[:L]], axis=1)
    xt = _dot3(tdiag, rhs)
    nt = _dot3(tdiag, a_ab - dblk)
    nt2 = _dot3(nt, nt)
    y1 = xt + _dot3(nt2, xt)
    pq = y1 + _dot3(nt, y1)
    gh = jnp.concatenate([rt, akv[L:]], axis=1) + _dot3(a_rb, pq)
    mc = _dot3(bh, pq, dot=_bdot_tn)
    n_ch = r.shape[1]
    m_op = mc[:, :n_ch] + eye * jnp.exp(cum_last)
    c_op = mc[:, n_ch:] + _dot3(kh, v, dot=_bdot_tn)
    return gh[:, :n_ch], gh[:, n_ch:], m_op, c_op


def _rwkv_scan_kernel(r_ref, k_ref, v_ref, z_ref, lw_ref, a_ref, kk_ref, ka_ref, rk_ref,
                      lnw_ref, lnb_ref, o_ref, state_ref):
    tb = pl.program_id(2)

    @pl.when(tb == 0)
    def _():
        state_ref[...] = jnp.zeros_like(state_ref)

    L = RWKV_CHUNK
    N = RWKV_HEAD
    ri = lax.broadcasted_iota(jnp.int32, (L, L), 0)
    ci = lax.broadcasted_iota(jnp.int32, (L, L), 1)
    lower_strict = ri > ci
    lower_incl = ri >= ci
    tri_incl = jnp.where(lower_incl, 1.0, 0.0).astype(BF16)
    eye = jnp.where(ri == ci, 1.0, 0.0).astype(F32)
    blockdiag = (ri // 16) == (ci // 16)
    outs = []
    for hh in range(2):
        sl = slice(hh * N, (hh + 1) * N)
        z_state = state_ref[hh]
        ys = []
        for c in range(RWKV_BLOCK // L):
            rows = slice(c * L, (c + 1) * L)
            r = r_ref[0, rows, sl]
            k = k_ref[0, rows, sl]
            v = v_ref[0, rows, sl]
            a = a_ref[0, rows, sl]
            lw = lw_ref[0, rows, sl]
            kk = k * kk_ref[:, sl]
            kk = kk / jnp.maximum(jnp.sqrt(jnp.sum(kk * kk, axis=-1, keepdims=True)), 1e-12)
            kp = k * (1.0 + (a - 1.0) * ka_ref[:, sl])
            g_op, h_op, m_op, c_op = _rwkv_chunk_ops(r, kp, v, -kk, kk * a, lw, tri_incl,
                                                      lower_strict, lower_incl, eye, blockdiag)
            upd = _dot3(jnp.concatenate([g_op, m_op], axis=0), z_state)
            y = upd[:L] + h_op
            z_state = upd[L:] + c_op
            mean = jnp.mean(y, axis=-1, keepdims=True)
            yc = y - mean
            var = jnp.mean(yc * yc, axis=-1, keepdims=True)
            yn = yc * lax.rsqrt(var + RWKV_GN_EPS) * lnw_ref[:, sl] + lnb_ref[:, sl]
            bonus = jnp.sum(r * kp * rk_ref[:, sl], axis=-1, keepdims=True) * v
            ys.append((yn + bonus) * _silu(z_ref[0, rows, sl]))
        state_ref[hh] = z_state
        outs.append(jnp.concatenate(ys, axis=0))
    o_ref[0] = jnp.concatenate(outs, axis=1)


def _rwkv_scan(r, k, v, z, lw, a, k_k, k_a, r_k, lnx_w, lnx_b):
    b, t, c = r.shape
    tb = min(RWKV_BLOCK, t)
    seq_spec = pl.BlockSpec((1, tb, LANES), lambda i, p, j: (i, j, p))
    par_spec = pl.BlockSpec((1, LANES), lambda i, p, j: (0, p))
    params = [x.reshape(1, c).astype(F32) for x in (k_k, k_a, r_k, lnx_w, lnx_b)]
    return pl.pallas_call(
        _rwkv_scan_kernel, grid=(b, c // LANES, t // tb),
        in_specs=[seq_spec] * 6 + [par_spec] * 5, out_specs=seq_spec,
        out_shape=jax.ShapeDtypeStruct((b, t, c), F32),
        scratch_shapes=[pltpu.VMEM((2, RWKV_HEAD, RWKV_HEAD), F32)],
        compiler_params=_cparams("parallel", "parallel", "arbitrary"), name="rwkv_scan",
    )(r, k, v, z, lw, a, *params)


def _rwkv_layer(x2d, g, mu, w_in, w0, w1, w2, a0, a1, a2, k_k, k_a, r_k, lnx_w, lnx_b, w_out,
                final_g, batch, seq):
    c = w0.shape[0]
    r, k, v, z, lw, a = _rwkv_pre(x2d, g, mu, w_in, w0, w1, w2, a0, a1, a2, seq)
    sh = (batch, seq, c)
    y = _rwkv_scan(r.reshape(sh), k.reshape(sh), v.reshape(sh), z.reshape(sh), lw.reshape(sh),
                   a.reshape(sh), k_k, k_a, r_k, lnx_w, lnx_b)
    return _out_proj(x2d, [y.reshape(x2d.shape[0], c)], None, w_out, final_g)


def _nsa_compress_kernel(u_ref, posa_ref, posb_ref, w1a_ref, w1b_ref, w2_ref, o_ref):
    u = u_ref[0, 0]
    ha = _bdot(u + posa_ref[...], w1a_ref[...])
    hb = _bdot(u + posb_ref[...], w1b_ref[...])
    h = ha + pltpu.roll(hb, hb.shape[0] - 1, 0)
    o_ref[0, 0] = _bdot(_silu(h), w2_ref[...])


def _nsa_compress(t_hm, pos, w1, w2):
    b, g, t, dh = t_hm.shape
    nch = t // NSA_CMP_STRIDE
    half = NSA_CMP_STRIDE * dh
    u = t_hm.reshape(b, g, nch, half)
    posf = pos.astype(F32).reshape(2, 1, half)
    hid = w1.shape[1]
    fixed = lambda i, j: (0, 0)
    return pl.pallas_call(
        _nsa_compress_kernel, grid=(b, g),
        in_specs=[pl.BlockSpec((1, 1, nch, half), lambda i, j: (i, j, 0, 0)),
                  pl.BlockSpec((1, half), fixed), pl.BlockSpec((1, half), fixed),
                  pl.BlockSpec((half, hid), fixed), pl.BlockSpec((half, hid), fixed),
                  pl.BlockSpec((hid, dh), fixed)],
        out_specs=pl.BlockSpec((1, 1, nch, dh), lambda i, j: (i, j, 0, 0)),
        out_shape=jax.ShapeDtypeStruct((b, g, nch, dh), F32),
        compiler_params=_cparams("parallel", "parallel"), name="nsa_compress",
    )(u, posf[0], posf[1], w1[:half].astype(BF16), w1[half:].astype(BF16), w2.astype(BF16))


def _nsa_cmp_attn_kernel(q_ref, kc_ref, vc_ref, gate_ref, o_ref, sel_ref, *, n_sel):
    g = pl.program_id(1)
    n = pl.program_id(2)
    tq = Q_BLOCK
    ncmp = kc_ref.shape[2]
    q = q_ref[0].reshape(GQA * tq, HEAD_DIM) * (HEAD_DIM ** -0.5)
    s = _bdot_nt(q, kc_ref[0, 0]).reshape(GQA, tq, ncmp)
    tpos = n * tq + lax.broadcasted_iota(jnp.int32, (tq, ncmp), 0)
    cend = lax.broadcasted_iota(jnp.int32, (tq, ncmp), 1) * NSA_CMP_STRIDE + (NSA_CMP_LEN - 1)
    ok = (cend <= tpos)[None]
    s = jnp.where(ok, s, NEG)
    m = jnp.max(s, axis=-1, keepdims=True)
    e = jnp.where(ok, jnp.exp(s - m), 0.0)
    l = jnp.sum(e, axis=-1, keepdims=True)
    p = e / jnp.where(l > 0.0, l, 1.0)
    o = _bdot(p.reshape(GQA * tq, ncmp), vc_ref[0, 0]).reshape(GQA, tq, HEAD_DIM)
    outs = []
    for r in range(GQA):
        outs.append(o[r] * _gate_column(gate_ref[0], GQA * g + r))
    o_ref[0] = jnp.concatenate(outs, axis=-1)

    psum = p[0] + p[1] + p[2] + p[3]
    si = lax.broadcasted_iota(jnp.int32, (n_sel, ncmp), 0)
    ni = lax.broadcasted_iota(jnp.int32, (n_sel, ncmp), 1)
    ratio = NSA_SEL_LEN // NSA_CMP_STRIDE
    overlap = ((ni < ratio * (si + 1)) & (ni * NSA_CMP_STRIDE + NSA_CMP_LEN - 1 >= si * NSA_SEL_LEN))
    overlap = jnp.where(overlap, 1.0, 0.0).astype(BF16)
    imp = _dot_exact_rhs(psum, overlap, dot=lambda a, b: _bdot_nt(b, a))
    blk = lax.broadcasted_iota(jnp.int32, (n_sel, tq), 0)
    tq_pos = n * tq + lax.broadcasted_iota(jnp.int32, (n_sel, tq), 1)
    cur = tq_pos // NSA_SEL_LEN
    forced = (blk == 0) | (blk == cur) | (blk == cur - 1)
    future = blk * NSA_SEL_LEN > tq_pos
    imp = jnp.where(forced, BIG, jnp.where(future, NEG, imp))
    rank = jnp.zeros((n_sel, tq), F32)
    for j in range(n_sel):
        row = imp[j:j + 1, :]
        ahead = (row > imp) | ((row == imp) & (blk > j))
        rank = rank + jnp.where(ahead, 1.0, 0.0)
    sel = jnp.where(rank < float(min(NSA_TOPK, n_sel)), 1.0, 0.0)
    sel_ref[0, 0] = sel.T


def _nsa_cmp_attn(q, k_cmp, v_cmp, gates):
    b, _, t, _ = q.shape
    ncmp = k_cmp.shape[2]
    n_sel = t // NSA_SEL_LEN
    d = ATTN_HEADS * HEAD_DIM
    kern = functools.partial(_nsa_cmp_attn_kernel, n_sel=n_sel)
    cmp_spec = pl.BlockSpec((1, 1, ncmp, HEAD_DIM), lambda i, g, n: (i, g, 0, 0))
    return pl.pallas_call(
        kern, grid=(b, KV_HEADS, t // Q_BLOCK),
        in_specs=[pl.BlockSpec((1, GQA, Q_BLOCK, HEAD_DIM), lambda i, g, n: (i, g, n, 0)),
                  cmp_spec, cmp_spec,
                  pl.BlockSpec((1, Q_BLOCK, LANES), lambda i, g, n: (i, n, 0))],
        out_specs=[pl.BlockSpec((1, Q_BLOCK, GQA * HEAD_DIM), lambda i, g, n: (i, n, g)),
                   pl.BlockSpec((1, 1, Q_BLOCK, n_sel), lambda i, g, n: (i, g, n, 0))],
        out_shape=[jax.ShapeDtypeStruct((b, t, d), F32),
                   jax.ShapeDtypeStruct((b, KV_HEADS, t, n_sel), F32)],
        compiler_params=_cparams("parallel", "parallel", "parallel"), name="nsa_cmp_attn",
    )(q, k_cmp, v_cmp, gates)


def _nsa_sel_attn_kernel(q_ref, k_ref, v_ref, sel_ref, bias_ref, far_ref, gate_ref, o_ref, *, n_sel):
    g = pl.program_id(1)
    n = pl.program_id(2)
    tq = Q_BLOCK
    per_tile = tq // NSA_SEL_LEN
    q = q_ref[0].reshape(GQA * tq, HEAD_DIM) * (HEAD_DIM ** -0.5)
    sel = sel_ref[0, 0].astype(BF16)
    srow = lax.broadcasted_iota(jnp.int32, (n_sel, tq), 0)
    scol = lax.broadcasted_iota(jnp.int32, (n_sel, tq), 1) // NSA_SEL_LEN
    qi = lax.broadcasted_iota(jnp.int32, (tq, tq), 0)
    kj = lax.broadcasted_iota(jnp.int32, (tq, tq), 1)

    def tile(j, bias, extra_ok):
        start = pl.multiple_of(j * tq, tq)
        kt = k_ref[0, 0, pl.ds(start, tq), :]
        vt = v_ref[0, 0, pl.ds(start, tq), :]
        expand = jnp.where(srow == j * per_tile + scol, 1.0, 0.0).astype(BF16)
        ok = (jnp.dot(sel, expand, preferred_element_type=F32) > 0.5) & extra_ok
        s = _bdot_nt(q, kt).reshape(GQA, tq, tq) + bias
        return jnp.where(ok[None], s, NEG), vt

    def update(carry, s, vt):
        m, l, acc = carry
        m_new = jnp.maximum(m, jnp.max(s, axis=-1, keepdims=True))
        alpha = jnp.exp(m - m_new)
        p = jnp.exp(s - m_new)
        l = alpha * l + jnp.sum(p, axis=-1, keepdims=True)
        pv = _bdot(p.reshape(GQA * tq, tq), vt).reshape(GQA, tq, HEAD_DIM)
        return m_new, l, alpha * acc + pv

    s, vt = tile(n, bias_ref[0, :, :, tq:], qi >= kj)
    m = jnp.max(s, axis=-1, keepdims=True)
    p = jnp.exp(s - m)
    l = jnp.sum(p, axis=-1, keepdims=True)
    acc = _bdot(p.reshape(GQA * tq, tq), vt).reshape(GQA, tq, HEAD_DIM)
    s, vt = tile(jnp.maximum(n - 1, 0), bias_ref[0, :, :, :tq], jnp.broadcast_to(n >= 1, (tq, tq)))
    carry = update((m, l, acc), s, vt)
    far = far_ref[pl.ds(GQA * g, GQA)]

    def body(j, carry):
        s, vt = tile(j, far, jnp.full((tq, tq), True))
        return update(carry, s, vt)

    m, l, acc = lax.fori_loop(0, jnp.maximum(n - 1, 0), body, carry)
    o = acc / l
    outs = []
    for r in range(GQA):
        outs.append(o[r] * _gate_column(gate_ref[0], ATTN_HEADS + GQA * g + r))
    o_ref[0] = jnp.concatenate(outs, axis=-1)


def _nsa_sel_attn(q, ks, vs, sel, bias_near, far, gates):
    b, _, t, _ = q.shape
    n_sel = t // NSA_SEL_LEN
    d = ATTN_HEADS * HEAD_DIM
    kern = functools.partial(_nsa_sel_attn_kernel, n_sel=n_sel)
    kv_spec = pl.BlockSpec((1, 1, t, HEAD_DIM), lambda i, g, n: (i, g, 0, 0))
    return pl.pallas_call(
        kern, grid=(b, KV_HEADS, t // Q_BLOCK),
        in_specs=[pl.BlockSpec((1, GQA, Q_BLOCK, HEAD_DIM), lambda i, g, n: (i, g, n, 0)),
                  kv_spec, kv_spec,
                  pl.BlockSpec((1, 1, Q_BLOCK, n_sel), lambda i, g, n: (i, g, n, 0)),
                  pl.BlockSpec((1, GQA, Q_BLOCK, 2 * Q_BLOCK), lambda i, g, n: (g, 0, 0, 0)),
                  pl.BlockSpec((ATTN_HEADS, 1, 1), lambda i, g, n: (0, 0, 0)),
                  pl.BlockSpec((1, Q_BLOCK, LANES), lambda i, g, n: (i, n, 0))],
        out_specs=pl.BlockSpec((1, Q_BLOCK, GQA * HEAD_DIM), lambda i, g, n: (i, n, g)),
        out_shape=jax.ShapeDtypeStruct((b, t, d), F32),
        compiler_params=_cparams("parallel", "parallel", "parallel"), name="nsa_sel_attn",
    )(q, ks, vs, sel, bias_near, far, gates)


def _nsa_layer(x2d, g, w_in, pos_k, k_w1, k_w2, pos_v, v_w1, v_w2, w_out, t5_table, final_g,
               batch, seq):
    nq, nkv = ATTN_HEADS * HEAD_DIM, KV_HEADS * HEAD_DIM
    offs = [0, nq] + [nq + nkv * (i + 1) for i in range(6)]
    ws = [w_in[:, offs[i]:offs[i + 1]] for i in range(7)]
    n_gate = 3 * ATTN_HEADS
    wg = jnp.pad(w_in[:, offs[7]:offs[7] + n_gate], ((0, 0), (0, LANES - n_gate)))
    wz = w_in[:, offs[7] + n_gate:]
    q, kc, vc, ks, vs, kw, vw, gates, z = _norm_proj(
        x2d, g, ws + [wg, wz], [True] * 7 + [False, False],
        [BF16, F32, F32, BF16, BF16, BF16, BF16, F32, F32], batch, seq)
    gates = gates.reshape(batch, seq, LANES)
    k_cmp = _nsa_compress(kc, pos_k, k_w1, k_w2)
    v_cmp = _nsa_compress(vc, pos_v, v_w1, v_w2)
    o_cmp, sel = _nsa_cmp_attn(q, k_cmp, v_cmp, gates)
    bias_near = _band_bias(t5_table, 1).reshape(KV_HEADS, GQA, Q_BLOCK, 2 * Q_BLOCK)
    far = t5_table[T5_BUCKETS - 1].astype(F32).reshape(ATTN_HEADS, 1, 1)
    o_sel = _nsa_sel_attn(q, ks, vs, sel, bias_near, far, gates)
    bias_win = _band_bias(t5_table, -(-NSA_WINDOW // Q_BLOCK))
    o_win = _band_attn(q, kw, vw, bias_win, NSA_WINDOW, gates=gates, gate_col=2 * ATTN_HEADS)
    sh = x2d.shape
    return _out_proj(x2d, [o_cmp.reshape(sh), o_sel.reshape(sh), o_win.reshape(sh)], z, w_out,
                     final_g)


def _lru_kernel(x_ref, g_ref, win_u_ref, win_z_ref, cw_ref, cb_ref, wa_ref, ba_ref, wx_ref, bx_ref,
                sp_ref, wout_ref, *rest, half, has_fg):
    fg_ref = rest[0] if has_fg else None
    o_ref, tail_ref, h_ref = rest[-3:]
    j = pl.program_id(1)

    @pl.when(j == 0)
    def _():
        tail_ref[...] = jnp.zeros_like(tail_ref)
        h_ref[...] = jnp.zeros_like(h_ref)

    x = x_ref[...]
    xb = _rms(x, g_ref[...]).astype(BF16)
    u = jnp.dot(xb, win_u_ref[...], preferred_element_type=F32)
    z = jnp.dot(xb, win_z_ref[...], preferred_element_type=F32)
    tm, width = u.shape
    ext = jnp.concatenate([tail_ref[...], u], axis=0)
    tail_ref[...] = u[tm - 8:, :]
    uc = cb_ref[...] + cw_ref[CONV_WIDTH - 1:CONV_WIDTH, :] * u
    for s in range(1, CONV_WIDTH):
        uc = uc + cw_ref[CONV_WIDTH - 1 - s:CONV_WIDTH - s, :] * ext[8 - s:8 - s + tm, :]
    ucb = uc.astype(BF16)
    gr, gi = [], []
    for c in range(width // half):
        blk = ucb[:, c * half:(c + 1) * half]
        gr.append(jnp.dot(blk, wa_ref[c], preferred_element_type=F32))
        gi.append(jnp.dot(blk, wx_ref[c], preferred_element_type=F32))
    rg = jax.nn.sigmoid(jnp.concatenate(gr, axis=1) + ba_ref[...])
    ig = jax.nn.sigmoid(jnp.concatenate(gi, axis=1) + bx_ref[...])
    log_a = -LRU_C * rg * sp_ref[...]
    a = jnp.exp(log_a)
    bv = jnp.sqrt(1.0 - jnp.exp(2.0 * log_a)) * (ig * uc)
    row = lax.broadcasted_iota(jnp.int32, (tm, width), 0)
    sh = 1
    while sh < tm:
        a_s = jnp.where(row >= sh, pltpu.roll(a, sh, 0), 1.0)
        b_s = jnp.where(row >= sh, pltpu.roll(bv, sh, 0), 0.0)
        bv = a * b_s + bv
        a = a * a_s
        sh *= 2
    h = bv + a * h_ref[...]
    h_ref[...] = h[tm - 1:tm, :]
    gated = (h * _silu(z)).astype(BF16)
    y = x + jnp.dot(gated, wout_ref[...], preferred_element_type=F32)
    o_ref[...] = _rms(y, fg_ref[...]) if has_fg else y


def _block_diag(w, group):
    nb, n, _ = w.shape
    w = w.reshape(nb // group, group, n, n)
    eye = jnp.eye(group, dtype=w.dtype)
    return jnp.einsum('cgij,gh->cgihj', w, eye).reshape(nb // group, group * n, group * n)


def _lru_layer(x2d, g, w_in, conv_w, conv_b, ga_w, ga_b, gx_w, gx_b, lam, w_out, final_g, batch, seq):
    n, d = x2d.shape
    width = w_in.shape[1] // 2
    blk = ga_w.shape[1]
    group = LANES // math.gcd(blk, LANES)
    group = min(group, ga_w.shape[0])
    half = group * blk
    nsup = width // half
    tm = min(LRU_TILE, seq)
    tiles = seq // tm
    row = lambda i, j: (i * tiles + j, 0)
    fixed = lambda i, j: (0, 0)
    fixed3 = lambda i, j: (0, 0, 0)
    vec = lambda a: a.astype(F32).reshape(1, -1)
    softplus_neg_lam = jax.nn.softplus(-lam.astype(F32))
    has_fg = final_g is not None
    in_specs = [pl.BlockSpec((tm, d), row), pl.BlockSpec((1, d), fixed),
                pl.BlockSpec((d, width), fixed), pl.BlockSpec((d, width), fixed),
                pl.BlockSpec((CONV_WIDTH, width), fixed), pl.BlockSpec((1, width), fixed),
                pl.BlockSpec((nsup, half, half), fixed3), pl.BlockSpec((1, width), fixed),
                pl.BlockSpec((nsup, half, half), fixed3), pl.BlockSpec((1, width), fixed),
                pl.BlockSpec((1, width), fixed), pl.BlockSpec((width, d), fixed)]
    args = [x2d, g.reshape(1, d), w_in[:, :width].astype(BF16), w_in[:, width:].astype(BF16),
            conv_w.astype(F32), vec(conv_b), _block_diag(ga_w, group).astype(BF16), vec(ga_b),
            _block_diag(gx_w, group).astype(BF16), vec(gx_b), vec(softplus_neg_lam),
            w_out.astype(BF16)]
    if has_fg:
        in_specs.append(pl.BlockSpec((1, d), fixed))
        args.append(final_g.reshape(1, d))
    kern = functools.partial(_lru_kernel, half=half, has_fg=has_fg)
    return pl.pallas_call(
        kern, grid=(batch, tiles), in_specs=in_specs, out_specs=pl.BlockSpec((tm, d), row),
        out_shape=jax.ShapeDtypeStruct((n, d), F32),
        scratch_shapes=[pltpu.VMEM((8, width), F32), pltpu.VMEM((1, width), F32)],
        compiler_params=_cparams("parallel", "arbitrary"), name="rglru_layer",
    )(*args)


def kernel(x, t5_table, norm_g, final_g, a_w_in, a_sinks, a_w_out, b_mu, b_w_in, b_w0, b_w1, b_w2, b_a0, b_a1, b_a2, b_k_k, b_k_a, b_r_k, b_lnx_w, b_lnx_b, b_w_out, c_w_in, c_cmp_pos_k, c_cmp_k_w1, c_cmp_k_w2, c_cmp_pos_v, c_cmp_v_w1, c_cmp_v_w2, c_w_out, d_w_in, d_conv_w, d_conv_b, d_gate_a_w, d_gate_a_b, d_gate_x_w, d_gate_x_b, d_lambda, d_w_out):
    batch, seq, d = x.shape
    depth = norm_g.shape[0]
    h = x.reshape(batch * seq, d)
    for layer in range(depth):
        m, j = layer % 4, layer // 4
        g = norm_g[layer]
        fg = final_g if layer == depth - 1 else None
        if m == 0:
            h = _swa_layer(h, g, a_w_in[j], a_sinks[j], a_w_out[j], t5_table, fg, batch, seq)
        elif m == 1:
            h = _rwkv_layer(h, g, b_mu[j], b_w_in[j], b_w0[j], b_w1[j], b_w2[j], b_a0[j], b_a1[j],
                            b_a2[j], b_k_k[j], b_k_a[j], b_r_k[j], b_lnx_w[j], b_lnx_b[j],
                            b_w_out[j], fg, batch, seq)
        elif m == 2:
            h = _nsa_layer(h, g, c_w_in[j], c_cmp_pos_k[j], c_cmp_k_w1[j], c_cmp_k_w2[j],
                           c_cmp_pos_v[j], c_cmp_v_w1[j], c_cmp_v_w2[j], c_w_out[j], t5_table,
                           fg, batch, seq)
        else:
            h = _lru_layer(h, g, d_w_in[j], d_conv_w[j], d_conv_b[j], d_gate_a_w[j], d_gate_a_b[j],
                           d_gate_x_w[j], d_gate_x_b[j], d_lambda[j], d_w_out[j], fg, batch, seq)
    return h.reshape(batch, seq, d)
```

```python
import functools
import math

import jax
import jax.numpy as jnp
from jax import lax
from jax.experimental import pallas as pl
from jax.experimental.pallas import tpu as pltpu

F32 = jnp.float32
BF16 = jnp.bfloat16

EPS = 1e-6
NEG = -1e30
BIG = 1e30
T5_BUCKETS = 32
T5_MAX_DIST = 128
ATTN_HEADS = 16
HEAD_DIM = 64
KV_HEADS = 4
GQA = ATTN_HEADS // KV_HEADS
Q_BLOCK = 128
SWA_WINDOW = 128
RWKV_HEAD = 64
RWKV_GN_EPS = 64e-5
NSA_CMP_LEN = 32
NSA_CMP_STRIDE = 16
NSA_SEL_LEN = 64
NSA_TOPK = 16
NSA_WINDOW = 512
LRU_BLOCKS = 16
LRU_C = 8.0
CONV_WIDTH = 4

LANES = 128
VMEM_LIMIT = 56 * 1024 * 1024
ROW_TILE = 256
RWKV_CHUNK = 64
RWKV_BLOCK = 128
RWKV_LANES = 256
LRU_TILE = 256


def _cparams(*sem):
    return pltpu.CompilerParams(dimension_semantics=sem, vmem_limit_bytes=VMEM_LIMIT)


def _bdot(a, b):
    return jnp.dot(a.astype(BF16), b.astype(BF16), preferred_element_type=F32)


def _bdot_nt(a, b):
    return lax.dot_general(a.astype(BF16), b.astype(BF16), (((1,), (1,)), ((), ())),
                           preferred_element_type=F32)


def _bdot_tn(a, b):
    return lax.dot_general(a.astype(BF16), b.astype(BF16), (((0,), (0,)), ((), ())),
                           preferred_element_type=F32)


def _split2(x):
    hi = x.astype(BF16)
    lo = (x - hi.astype(F32)).astype(BF16)
    return hi, lo


def _dot3(a, b, dot=_bdot):
    ah, al = _split2(a)
    bh, bl = _split2(b)
    return dot(ah, bh) + (dot(ah, bl) + dot(al, bh))


def _dot_exact_rhs(a, b01, dot=_bdot):
    ah = a.astype(BF16)
    r1 = a - ah.astype(F32)
    am = r1.astype(BF16)
    al = (r1 - am.astype(F32)).astype(BF16)
    return dot(ah, b01) + (dot(am, b01) + dot(al, b01))


def _rms(x, g):
    return x * lax.rsqrt(jnp.mean(x * x, axis=-1, keepdims=True) + EPS) * g


def _silu(z):
    return z * jax.nn.sigmoid(z)


def _gate_column(gates, c):
    col = lax.broadcasted_iota(jnp.int32, gates.shape, 1)
    return jax.nn.sigmoid(jnp.sum(jnp.where(col == c, gates, 0.0), axis=-1, keepdims=True))


def _norm_proj_kernel(x_ref, g_ref, *refs, n_out, head_major):
    w_refs, o_refs = refs[:n_out], refs[n_out:]
    xb = _rms(x_ref[...], g_ref[...]).astype(BF16)
    for w_ref, o_ref, hm in zip(w_refs, o_refs, head_major):
        width = w_ref.shape[1]
        for c0 in range(0, width, 512):
            cw = min(512, width - c0)
            acc = jnp.dot(xb, w_ref[:, c0:c0 + cw], preferred_element_type=F32)
            if hm:
                for j in range(cw // HEAD_DIM):
                    o_ref[0, (c0 // HEAD_DIM) + j] = acc[:, j * HEAD_DIM:(j + 1) * HEAD_DIM].astype(o_ref.dtype)
            else:
                o_ref[:, c0:c0 + cw] = acc.astype(o_ref.dtype)


def _norm_proj(x2d, g, weights, head_major, dtypes, batch, seq):
    n, d = x2d.shape
    tm = min(ROW_TILE, seq)
    tiles_per_seq = seq // tm
    in_specs = [pl.BlockSpec((tm, d), lambda i: (i, 0)), pl.BlockSpec((1, d), lambda i: (0, 0))]
    out_specs, out_shapes = [], []
    for w, hm, dt in zip(weights, head_major, dtypes):
        width = w.shape[1]
        in_specs.append(pl.BlockSpec((d, width), lambda i: (0, 0)))
        if hm:
            nh = width // HEAD_DIM
            out_shapes.append(jax.ShapeDtypeStruct((batch, nh, seq, HEAD_DIM), dt))
            out_specs.append(pl.BlockSpec((1, nh, tm, HEAD_DIM),
                                          lambda i: (i // tiles_per_seq, 0, i % tiles_per_seq, 0)))
        else:
            out_shapes.append(jax.ShapeDtypeStruct((n, width), dt))
            out_specs.append(pl.BlockSpec((tm, width), lambda i: (i, 0)))
    kern = functools.partial(_norm_proj_kernel, n_out=len(weights), head_major=tuple(head_major))
    return pl.pallas_call(
        kern, grid=(n // tm,), in_specs=in_specs, out_specs=out_specs, out_shape=out_shapes,
        compiler_params=_cparams("parallel"), name="norm_proj",
    )(x2d, g.reshape(1, d), *[w.astype(BF16) for w in weights])


def _out_proj_kernel(*refs, n_a, has_z, has_g):
    x_ref = refs[0]
    a_refs = refs[1:1 + n_a]
    pos = 1 + n_a
    z_ref = refs[pos] if has_z else None
    pos += int(has_z)
    w_ref = refs[pos]
    pos += 1
    g_ref = refs[pos] if has_g else None
    o_ref = refs[-1]
    a = a_refs[0][...].astype(F32)
    for r in a_refs[1:]:
        a = a + r[...].astype(F32)
    if has_z:
        a = a * _silu(z_ref[...])
    y = x_ref[...] + jnp.dot(a.astype(BF16), w_ref[...], preferred_element_type=F32)
    if has_g:
        y = _rms(y, g_ref[...])
    o_ref[...] = y


def _out_proj(x2d, a_list, z, w, final_g=None):
    n, d = x2d.shape
    c = w.shape[0]
    tm = min(ROW_TILE, n)
    row = lambda i: (i, 0)
    fixed = lambda i: (0, 0)
    in_specs = [pl.BlockSpec((tm, d), row)] + [pl.BlockSpec((tm, c), row) for _ in a_list]
    args = [x2d] + list(a_list)
    if z is not None:
        in_specs.append(pl.BlockSpec((tm, c), row))
        args.append(z)
    in_specs.append(pl.BlockSpec((c, d), fixed))
    args.append(w.astype(BF16))
    if final_g is not None:
        in_specs.append(pl.BlockSpec((1, d), fixed))
        args.append(final_g.reshape(1, d))
    kern = functools.partial(_out_proj_kernel, n_a=len(a_list), has_z=z is not None,
                             has_g=final_g is not None)
    return pl.pallas_call(
        kern, grid=(n // tm,), in_specs=in_specs, out_specs=pl.BlockSpec((tm, d), row),
        out_shape=jax.ShapeDtypeStruct((n, d), F32), compiler_params=_cparams("parallel"),
        name="out_proj",
    )(*args)


def _t5_bucket(dist):
    max_exact = T5_BUCKETS // 2
    d = jnp.maximum(dist, 0)
    df = jnp.maximum(d, 1).astype(F32)
    large = max_exact + (jnp.log(df / max_exact) / math.log(T5_MAX_DIST / max_exact)
                         * (T5_BUCKETS - max_exact)).astype(jnp.int32)
    large = jnp.minimum(large, T5_BUCKETS - 1)
    return jnp.where(d < max_exact, d, large)


def _band_bias(t5_table, nprev):
    kc = (nprev + 1) * Q_BLOCK
    dist = nprev * Q_BLOCK + jnp.arange(Q_BLOCK)[:, None] - jnp.arange(kc)[None, :]
    bias = jnp.take(t5_table, _t5_bucket(dist), axis=0)
    return jnp.moveaxis(bias, -1, 0).astype(F32)


def _band_attn_kernel(*refs, nprev, window, has_sink, gate_col):
    q_ref, k_ref, v_ref, bias_ref = refs[:4]
    pos = 4
    sink_ref = refs[pos] if has_sink else None
    pos += int(has_sink)
    gate_ref = refs[pos] if gate_col is not None else None
    o_ref = refs[-1]
    n = pl.program_id(1)
    tq = Q_BLOCK
    kc = (nprev + 1) * tq
    qi = lax.broadcasted_iota(jnp.int32, (tq, kc), 0)
    kj = lax.broadcasted_iota(jnp.int32, (tq, kc), 1)
    dist = nprev * tq + qi - kj
    kpos = (n - nprev) * tq + kj
    valid = ((dist >= 0) & (dist < window) & (kpos >= 0))[None]
    starts = [pl.multiple_of(jnp.maximum(n - nprev + j, 0) * tq, tq) for j in range(nprev + 1)]
    for g in range(KV_HEADS):
        kg = jnp.concatenate([k_ref[0, g, pl.ds(s, tq), :] for s in starts], axis=0)
        vg = jnp.concatenate([v_ref[0, g, pl.ds(s, tq), :] for s in starts], axis=0)
        qg = q_ref[0, GQA * g:GQA * (g + 1)].reshape(GQA * tq, HEAD_DIM) * (HEAD_DIM ** -0.5)
        s = _bdot_nt(qg, kg).reshape(GQA, tq, kc) + bias_ref[GQA * g:GQA * (g + 1)]
        s = jnp.where(valid, s, NEG)
        m = jnp.max(s, axis=-1, keepdims=True)
        if has_sink:
            sk = sink_ref[GQA * g:GQA * (g + 1)]
            m = jnp.maximum(m, sk)
        p = jnp.exp(s - m)
        l = jnp.sum(p, axis=-1, keepdims=True)
        if has_sink:
            l = l + jnp.exp(sk - m)
        o = _bdot(p.reshape(GQA * tq, kc), vg).reshape(GQA, tq, HEAD_DIM) / l
        outs = []
        for r in range(GQA):
            oh = o[r]
            if gate_col is not None:
                c = gate_col + GQA * g + r
                oh = oh * jax.nn.sigmoid(gate_ref[0, :, c:c + 1])
            outs.append(oh)
        o_ref[0, :, g * GQA * HEAD_DIM:(g + 1) * GQA * HEAD_DIM] = jnp.concatenate(outs, axis=-1)


def _band_attn(q, k, v, bias, window, sinks=None, gates=None, gate_col=None):
    b, _, t, _ = q.shape
    nprev = -(-window // Q_BLOCK)
    kc = (nprev + 1) * Q_BLOCK
    in_specs = [
        pl.BlockSpec((1, ATTN_HEADS, Q_BLOCK, HEAD_DIM), lambda i, n: (i, 0, n, 0)),
        pl.BlockSpec((1, KV_HEADS, t, HEAD_DIM), lambda i, n: (i, 0, 0, 0)),
        pl.BlockSpec((1, KV_HEADS, t, HEAD_DIM), lambda i, n: (i, 0, 0, 0)),
        pl.BlockSpec((ATTN_HEADS, Q_BLOCK, kc), lambda i, n: (0, 0, 0)),
    ]
    args = [q, k, v, bias]
    if sinks is not None:
        in_specs.append(pl.BlockSpec((ATTN_HEADS, 1, 1), lambda i, n: (0, 0, 0)))
        args.append(sinks.astype(F32).reshape(ATTN_HEADS, 1, 1))
    if gates is not None:
        in_specs.append(pl.BlockSpec((1, Q_BLOCK, LANES), lambda i, n: (i, n, 0)))
        args.append(gates)
    kern = functools.partial(_band_attn_kernel, nprev=nprev, window=window,
                             has_sink=sinks is not None, gate_col=gate_col)
    d = ATTN_HEADS * HEAD_DIM
    return pl.pallas_call(
        kern, grid=(b, t // Q_BLOCK), in_specs=in_specs,
        out_specs=pl.BlockSpec((1, Q_BLOCK, d), lambda i, n: (i, n, 0)),
        out_shape=jax.ShapeDtypeStruct((b, t, d), F32),
        compiler_params=_cparams("parallel", "parallel"), name="band_attn",
    )(*args)


def _swa_layer(x2d, g, w_in, sinks, w_out, t5_table, final_g, batch, seq):
    nq, nkv = ATTN_HEADS * HEAD_DIM, KV_HEADS * HEAD_DIM
    ws = [w_in[:, :nq], w_in[:, nq:nq + nkv], w_in[:, nq + nkv:nq + 2 * nkv], w_in[:, nq + 2 * nkv:]]
    q, k, v, z = _norm_proj(x2d, g, ws, [True, True, True, False], [BF16, BF16, BF16, F32], batch, seq)
    bias = _band_bias(t5_table, -(-SWA_WINDOW // Q_BLOCK))
    o = _band_attn(q, k, v, bias, SWA_WINDOW, sinks=sinks)
    return _out_proj(x2d, [o.reshape(x2d.shape)], z, w_out, final_g)


def _rwkv_pre_kernel(x_ref, xp_ref, g_ref, mu_ref, wr_ref, wk_ref, wv_ref, wz_ref,
                     w0_ref, w1_ref, w2_ref, a0_ref, a1_ref, a2_ref,
                     r_ref, k_ref, v_ref, z_ref, lw_ref, a_ref, *, tiles_per_seq):
    i = pl.program_id(0)
    g = g_ref[...]
    xn = _rms(x_ref[...], g)
    prev = _rms(xp_ref[...], g)[7:8]
    prev = jnp.where(i % tiles_per_seq == 0, 0.0, prev)
    row = lax.broadcasted_iota(jnp.int32, xn.shape, 0)
    xprev = jnp.where(row == 0, prev, pltpu.roll(xn, 1, 0))
    xx = xprev - xn
    lerp = lambda s: xn + xx * mu_ref[s:s + 1, :]
    r_ref[...] = _bdot(lerp(0), wr_ref[...])
    k_ref[...] = _bdot(lerp(1), wk_ref[...])
    v_ref[...] = _bdot(lerp(2), wv_ref[...])
    z_ref[...] = _bdot(lerp(3), wz_ref[...])
    wl = w0_ref[...] + _bdot(jnp.tanh(_bdot(lerp(4), w1_ref[...])), w2_ref[...])
    sp = jnp.maximum(-wl, 0.0) + jnp.log1p(jnp.exp(-jnp.abs(wl)))
    lw_ref[...] = -jnp.exp(-sp - 0.5)
    al = a0_ref[...] + _bdot(_bdot(lerp(5), a1_ref[...]), a2_ref[...])
    a_ref[...] = jax.nn.sigmoid(al)


def _rwkv_pre(x2d, g, mu, w_in, w0, w1, w2, a0, a1, a2, seq):
    n, d = x2d.shape
    c = w0.shape[0]
    tm = min(ROW_TILE, seq)
    tiles_per_seq = seq // tm
    row = lambda i: (i, 0)
    fixed = lambda i: (0, 0)
    ws = [w_in[:, s * c:(s + 1) * c].astype(BF16) for s in range(4)]
    in_specs = [
        pl.BlockSpec((tm, d), row),
        pl.BlockSpec((8, d), lambda i: (jnp.maximum(i * (tm // 8) - 1, 0), 0)),
        pl.BlockSpec((1, d), fixed), pl.BlockSpec((6, d), fixed),
    ] + [pl.BlockSpec((d, c), fixed)] * 4 + [
        pl.BlockSpec((1, c), fixed), pl.BlockSpec(w1.shape, fixed), pl.BlockSpec(w2.shape, fixed),
        pl.BlockSpec((1, c), fixed), pl.BlockSpec(a1.shape, fixed), pl.BlockSpec(a2.shape, fixed),
    ]
    out = jax.ShapeDtypeStruct((n, c), F32)
    kern = functools.partial(_rwkv_pre_kernel, tiles_per_seq=tiles_per_seq)
    return pl.pallas_call(
        kern, grid=(n // tm,), in_specs=in_specs, out_specs=[pl.BlockSpec((tm, c), row)] * 6,
        out_shape=[out] * 6, compiler_params=_cparams("parallel"), name="rwkv_pre",
    )(x2d, x2d, g.reshape(1, d), mu, *ws, w0.reshape(1, c), w1.astype(BF16), w2.astype(BF16),
      a0.reshape(1, c), a1.astype(BF16), a2.astype(BF16))


def _dot3_many(a_list, b_list, dot=_bdot):
    sa = [_split2(a) for a in a_list]
    sb = [_split2(b) for b in b_list]
    hh = [dot(x[0], y[0]) for x, y in zip(sa, sb)]
    hl = [dot(x[0], y[1]) for x, y in zip(sa, sb)]
    lh = [dot(x[1], y[0]) for x, y in zip(sa, sb)]
    return [p + (q + r) for p, q, r in zip(hh, hl, lh)]


def _rwkv_chunk_ops(rs, kps, vs, aas, bbs, lws, tri_incl, lower_strict, lower_incl, eye, blockdiag):
    L = RWKV_CHUNK
    n = len(rs)
    n_ch = rs[0].shape[1]
    pieces = []
    for lw in lws:
        hi = lw.astype(BF16)
        r1 = lw - hi.astype(F32)
        mid = r1.astype(BF16)
        pieces.append((hi, mid, (r1 - mid.astype(F32)).astype(BF16)))
    cum_p = [[jnp.dot(tri_incl, p[i], preferred_element_type=F32) for p in pieces] for i in range(3)]
    cums = [a + (b + c) for a, b, c in zip(*cum_p)]
    ats, rts, bks, bhs, khs, wls = [], [], [], [], [], []
    for r, kp, aa, bb, lw, cum in zip(rs, kps, aas, bbs, lws, cums):
        w_inv = jnp.exp(-cum)
        cum_last = cum[L - 1:L, :]
        w_tail = jnp.exp(cum_last - cum)
        ats.append(aa * jnp.exp(cum - lw))
        rts.append(r * jnp.exp(cum))
        bks.append(jnp.concatenate([bb * w_inv, kp * w_inv], axis=0))
        bhs.append(bb * w_tail)
        khs.append(kp * w_tail)
        wls.append(jnp.exp(cum_last))
    ars = [jnp.concatenate([at, rt], axis=0) for at, rt in zip(ats, rts)]
    As = _dot3_many(ars, bks, dot=_bdot_nt)
    a_ab = [jnp.where(lower_strict, A[:L, :L], 0.0) for A in As]
    a_rb = [jnp.where(lower_incl, A[L:, :L], 0.0) for A in As]
    a_k = [jnp.concatenate([jnp.where(lower_strict, A[:L, L:], 0.0),
                            jnp.where(lower_incl, A[L:, L:], 0.0)], axis=0) for A in As]
    dblk = [jnp.where(blockdiag, x, 0.0) for x in a_ab]
    res = _dot3_many(a_k + dblk, vs + dblk)
    akv, d2 = res[:n], res[n:]
    res = _dot3_many([eye + d for d in dblk] + d2, [eye + d for d in d2] + d2)
    s4, d4 = res[:n], res[n:]
    res = _dot3_many(d4 + d4, s4 + d4)
    s8 = [s + x for s, x in zip(s4, res[:n])]
    d8 = res[n:]
    tdiag = [s + x for s, x in zip(s8, _dot3_many(d8, s8))]
    rhs = [jnp.concatenate([at, x[:L]], axis=1) for at, x in zip(ats, akv)]
    res = _dot3_many(tdiag + tdiag, rhs + [x - d for x, d in zip(a_ab, dblk)])
    xt, nt = res[:n], res[n:]
    res = _dot3_many(nt + nt, xt + nt)
    u = [x + y for x, y in zip(xt, res[:n])]
    nt2 = res[n:]
    pq = [x + y for x, y in zip(u, _dot3_many(nt2, u))]
    res = _dot3_many(a_rb, pq)
    gh = [jnp.concatenate([rt, x[L:]], axis=1) + y for rt, x, y in zip(rts, akv, res)]
    res = _dot3_many(bhs + khs, pq + vs, dot=_bdot_tn)
    out = []
    for i in range(n):
        mc = res[i]
        m_op = mc[:, :n_ch] + eye * wls[i]
        c_op = mc[:, n_ch:] + res[n + i]
        out.append((gh[i][:, :n_ch], gh[i][:, n_ch:], m_op, c_op))
    return out


def _rwkv_scan_kernel(r_ref, k_ref, v_ref, z_ref, lw_ref, a_ref, kk_ref, ka_ref, rk_ref,
                      lnw_ref, lnb_ref, o_ref, state_ref):
    tb = pl.program_id(2)

    @pl.when(tb == 0)
    def _():
        state_ref[...] = jnp.zeros_like(state_ref)

    L = RWKV_CHUNK
    N = RWKV_HEAD
    ri = lax.broadcasted_iota(jnp.int32, (L, L), 0)
    ci = lax.broadcasted_iota(jnp.int32, (L, L), 1)
    lower_strict = ri > ci
    lower_incl = ri >= ci
    tri_incl = jnp.where(lower_incl, 1.0, 0.0).astype(BF16)
    eye = jnp.where(ri == ci, 1.0, 0.0).astype(F32)
    blockdiag = (ri // 16) == (ci // 16)
    n_heads = r_ref.shape[2] // N
    n_chunks = r_ref.shape[1] // L
    probs = [(hh, c) for c in range(n_chunks) for hh in range(n_heads)]
    rs, kps, vs, aas, bbs, lws = [], [], [], [], [], []
    for hh, c in probs:
        sl = slice(hh * N, (hh + 1) * N)
        rows = slice(c * L, (c + 1) * L)
        k = k_ref[0, rows, sl]
        a = a_ref[0, rows, sl]
        kk = k * kk_ref[:, sl]
        kk = kk / jnp.maximum(jnp.sqrt(jnp.sum(kk * kk, axis=-1, keepdims=True)), 1e-12)
        rs.append(r_ref[0, rows, sl])
        kps.append(k * (1.0 + (a - 1.0) * ka_ref[:, sl]))
        vs.append(v_ref[0, rows, sl])
        aas.append(-kk)
        bbs.append(kk * a)
        lws.append(lw_ref[0, rows, sl])
    ops = _rwkv_chunk_ops(rs, kps, vs, aas, bbs, lws, tri_incl, lower_strict, lower_incl, eye,
                          blockdiag)
    states = [state_ref[hh] for hh in range(n_heads)]
    ys = {}
    for c in range(n_chunks):
        idx = [c * n_heads + hh for hh in range(n_heads)]
        upd = _dot3_many([jnp.concatenate([ops[i][0], ops[i][2]], axis=0) for i in idx], states)
        for hh, i in enumerate(idx):
            ys[(hh, c)] = upd[hh][:L] + ops[i][1]
            states[hh] = upd[hh][L:] + ops[i][3]
    for hh in range(n_heads):
        state_ref[hh] = states[hh]
    cols = []
    for hh in range(n_heads):
        sl = slice(hh * N, (hh + 1) * N)
        parts = []
        for c in range(n_chunks):
            rows = slice(c * L, (c + 1) * L)
            i = c * n_heads + hh
            y = ys[(hh, c)]
            mean = jnp.mean(y, axis=-1, keepdims=True)
            yc = y - mean
            var = jnp.mean(yc * yc, axis=-1, keepdims=True)
            yn = yc * lax.rsqrt(var + RWKV_GN_EPS) * lnw_ref[:, sl] + lnb_ref[:, sl]
            bonus = jnp.sum(rs[i] * kps[i] * rk_ref[:, sl], axis=-1, keepdims=True) * vs[i]
            parts.append((yn + bonus) * _silu(z_ref[0, rows, sl]))
        cols.append(jnp.concatenate(parts, axis=0))
    o_ref[0] = jnp.concatenate(cols, axis=1)


def _rwkv_scan(r, k, v, z, lw, a, k_k, k_a, r_k, lnx_w, lnx_b):
    b, t, c = r.shape
    tb = min(RWKV_BLOCK, t)
    seq_spec = pl.BlockSpec((1, tb, RWKV_LANES), lambda i, p, j: (i, j, p))
    par_spec = pl.BlockSpec((1, RWKV_LANES), lambda i, p, j: (0, p))
    params = [x.reshape(1, c).astype(F32) for x in (k_k, k_a, r_k, lnx_w, lnx_b)]
    return pl.pallas_call(
        _rwkv_scan_kernel, grid=(b, c // RWKV_LANES, t // tb),
        in_specs=[seq_spec] * 6 + [par_spec] * 5, out_specs=seq_spec,
        out_shape=jax.ShapeDtypeStruct((b, t, c), F32),
        scratch_shapes=[pltpu.VMEM((RWKV_LANES // RWKV_HEAD, RWKV_HEAD, RWKV_HEAD), F32)],
        compiler_params=_cparams("parallel", "parallel", "arbitrary"), name="rwkv_scan",
    )(r, k, v, z, lw, a, *params)


def _rwkv_layer(x2d, g, mu, w_in, w0, w1, w2, a0, a1, a2, k_k, k_a, r_k, lnx_w, lnx_b, w_out,
                final_g, batch, seq):
    c = w0.shape[0]
    r, k, v, z, lw, a = _rwkv_pre(x2d, g, mu, w_in, w0, w1, w2, a0, a1, a2, seq)
    sh = (batch, seq, c)
    y = _rwkv_scan(r.reshape(sh), k.reshape(sh), v.reshape(sh), z.reshape(sh), lw.reshape(sh),
                   a.reshape(sh), k_k, k_a, r_k, lnx_w, lnx_b)
    return _out_proj(x2d, [y.reshape(x2d.shape[0], c)], None, w_out, final_g)


def _nsa_compress_kernel(u_ref, posa_ref, posb_ref, w1a_ref, w1b_ref, w2_ref, o_ref):
    u = u_ref[0, 0]
    ha = _bdot(u + posa_ref[...], w1a_ref[...])
    hb = _bdot(u + posb_ref[...], w1b_ref[...])
    h = ha + pltpu.roll(hb, hb.shape[0] - 1, 0)
    o_ref[0, 0] = _bdot(_silu(h), w2_ref[...])


def _nsa_compress(t_hm, pos, w1, w2):
    b, g, t, dh = t_hm.shape
    nch = t // NSA_CMP_STRIDE
    half = NSA_CMP_STRIDE * dh
    u = t_hm.reshape(b, g, nch, half)
    posf = pos.astype(F32).reshape(2, 1, half)
    hid = w1.shape[1]
    fixed = lambda i, j: (0, 0)
    return pl.pallas_call(
        _nsa_compress_kernel, grid=(b, g),
        in_specs=[pl.BlockSpec((1, 1, nch, half), lambda i, j: (i, j, 0, 0)),
                  pl.BlockSpec((1, half), fixed), pl.BlockSpec((1, half), fixed),
                  pl.BlockSpec((half, hid), fixed), pl.BlockSpec((half, hid), fixed),
                  pl.BlockSpec((hid, dh), fixed)],
        out_specs=pl.BlockSpec((1, 1, nch, dh), lambda i, j: (i, j, 0, 0)),
        out_shape=jax.ShapeDtypeStruct((b, g, nch, dh), F32),
        compiler_params=_cparams("parallel", "parallel"), name="nsa_compress",
    )(u, posf[0], posf[1], w1[:half].astype(BF16), w1[half:].astype(BF16), w2.astype(BF16))


def _nsa_cmp_attn_kernel(q_ref, kc_ref, vc_ref, gate_ref, o_ref, sel_ref, *, n_sel):
    g = pl.program_id(1)
    n = pl.program_id(2)
    tq = Q_BLOCK
    ncmp = kc_ref.shape[2]
    q = q_ref[0].reshape(GQA * tq, HEAD_DIM) * (HEAD_DIM ** -0.5)
    s = _bdot_nt(q, kc_ref[0, 0]).reshape(GQA, tq, ncmp)
    tpos = n * tq + lax.broadcasted_iota(jnp.int32, (tq, ncmp), 0)
    cend = lax.broadcasted_iota(jnp.int32, (tq, ncmp), 1) * NSA_CMP_STRIDE + (NSA_CMP_LEN - 1)
    ok = (cend <= tpos)[None]
    s = jnp.where(ok, s, NEG)
    m = jnp.max(s, axis=-1, keepdims=True)
    e = jnp.where(ok, jnp.exp(s - m), 0.0)
    l = jnp.sum(e, axis=-1, keepdims=True)
    p = e / jnp.where(l > 0.0, l, 1.0)
    o = _bdot(p.reshape(GQA * tq, ncmp), vc_ref[0, 0]).reshape(GQA, tq, HEAD_DIM)
    outs = []
    for r in range(GQA):
        outs.append(o[r] * _gate_column(gate_ref[0], GQA * g + r))
    o_ref[0] = jnp.concatenate(outs, axis=-1)

    psum = p[0] + p[1] + p[2] + p[3]
    si = lax.broadcasted_iota(jnp.int32, (n_sel, ncmp), 0)
    ni = lax.broadcasted_iota(jnp.int32, (n_sel, ncmp), 1)
    ratio = NSA_SEL_LEN // NSA_CMP_STRIDE
    overlap = ((ni < ratio * (si + 1)) & (ni * NSA_CMP_STRIDE + NSA_CMP_LEN - 1 >= si * NSA_SEL_LEN))
    overlap = jnp.where(overlap, 1.0, 0.0).astype(BF16)
    imp = _dot_exact_rhs(psum, overlap, dot=lambda a, b: _bdot_nt(b, a))
    blk = lax.broadcasted_iota(jnp.int32, (n_sel, tq), 0)
    tq_pos = n * tq + lax.broadcasted_iota(jnp.int32, (n_sel, tq), 1)
    cur = tq_pos // NSA_SEL_LEN
    forced = (blk == 0) | (blk == cur) | (blk == cur - 1)
    future = blk * NSA_SEL_LEN > tq_pos
    imp = jnp.where(forced, BIG, jnp.where(future, NEG, imp))
    rank = jnp.zeros((n_sel, tq), F32)
    for j in range(n_sel):
        row = imp[j:j + 1, :]
        ahead = (row > imp) | ((row == imp) & (blk > j))
        rank = rank + jnp.where(ahead, 1.0, 0.0)
    sel = jnp.where(rank < float(min(NSA_TOPK, n_sel)), 1.0, 0.0)
    sel_ref[0, 0] = sel.T


def _nsa_cmp_attn(q, k_cmp, v_cmp, gates):
    b, _, t, _ = q.shape
    ncmp = k_cmp.shape[2]
    n_sel = t // NSA_SEL_LEN
    d = ATTN_HEADS * HEAD_DIM
    kern = functools.partial(_nsa_cmp_attn_kernel, n_sel=n_sel)
    cmp_spec = pl.BlockSpec((1, 1, ncmp, HEAD_DIM), lambda i, g, n: (i, g, 0, 0))
    return pl.pallas_call(
        kern, grid=(b, KV_HEADS, t // Q_BLOCK),
        in_specs=[pl.BlockSpec((1, GQA, Q_BLOCK, HEAD_DIM), lambda i, g, n: (i, g, n, 0)),
                  cmp_spec, cmp_spec,
                  pl.BlockSpec((1, Q_BLOCK, LANES), lambda i, g, n: (i, n, 0))],
        out_specs=[pl.BlockSpec((1, Q_BLOCK, GQA * HEAD_DIM), lambda i, g, n: (i, n, g)),
                   pl.BlockSpec((1, 1, Q_BLOCK, n_sel), lambda i, g, n: (i, g, n, 0))],
        out_shape=[jax.ShapeDtypeStruct((b, t, d), F32),
                   jax.ShapeDtypeStruct((b, KV_HEADS, t, n_sel), F32)],
        compiler_params=_cparams("parallel", "parallel", "parallel"), name="nsa_cmp_attn",
    )(q, k_cmp, v_cmp, gates)


def _nsa_sel_attn_kernel(q_ref, k_ref, v_ref, sel_ref, bias_ref, far_ref, gate_ref, o_ref, *, n_sel):
    g = pl.program_id(1)
    n = pl.program_id(2)
    tq = Q_BLOCK
    per_tile = tq // NSA_SEL_LEN
    q = q_ref[0].reshape(GQA * tq, HEAD_DIM) * (HEAD_DIM ** -0.5)
    sel = sel_ref[0, 0].astype(BF16)
    srow = lax.broadcasted_iota(jnp.int32, (n_sel, tq), 0)
    scol = lax.broadcasted_iota(jnp.int32, (n_sel, tq), 1) // NSA_SEL_LEN
    qi = lax.broadcasted_iota(jnp.int32, (tq, tq), 0)
    kj = lax.broadcasted_iota(jnp.int32, (tq, tq), 1)

    def tile(j, bias, extra_ok):
        start = pl.multiple_of(j * tq, tq)
        kt = k_ref[0, 0, pl.ds(start, tq), :]
        vt = v_ref[0, 0, pl.ds(start, tq), :]
        expand = jnp.where(srow == j * per_tile + scol, 1.0, 0.0).astype(BF16)
        ok = (jnp.dot(sel, expand, preferred_element_type=F32) > 0.5) & extra_ok
        s = _bdot_nt(q, kt).reshape(GQA, tq, tq) + bias
        return jnp.where(ok[None], s, NEG), vt

    def update(carry, s, vt):
        m, l, acc = carry
        m_new = jnp.maximum(m, jnp.max(s, axis=-1, keepdims=True))
        alpha = jnp.exp(m - m_new)
        p = jnp.exp(s - m_new)
        l = alpha * l + jnp.sum(p, axis=-1, keepdims=True)
        pv = _bdot(p.reshape(GQA * tq, tq), vt).reshape(GQA, tq, HEAD_DIM)
        return m_new, l, alpha * acc + pv

    s, vt = tile(n, bias_ref[0, :, :, tq:], qi >= kj)
    m = jnp.max(s, axis=-1, keepdims=True)
    p = jnp.exp(s - m)
    l = jnp.sum(p, axis=-1, keepdims=True)
    acc = _bdot(p.reshape(GQA * tq, tq), vt).reshape(GQA, tq, HEAD_DIM)
    s, vt = tile(jnp.maximum(n - 1, 0), bias_ref[0, :, :, :tq], jnp.broadcast_to(n >= 1, (tq, tq)))
    carry = update((m, l, acc), s, vt)
    far = far_ref[pl.ds(GQA * g, GQA)]

    def body(j, carry):
        s, vt = tile(j, far, jnp.full((tq, tq), True))
        return update(carry, s, vt)

    m, l, acc = lax.fori_loop(0, jnp.maximum(n - 1, 0), body, carry)
    o = acc / l
    outs = []
    for r in range(GQA):
        outs.append(o[r] * _gate_column(gate_ref[0], ATTN_HEADS + GQA * g + r))
    o_ref[0] = jnp.concatenate(outs, axis=-1)


def _nsa_sel_attn(q, ks, vs, sel, bias_near, far, gates):
    b, _, t, _ = q.shape
    n_sel = t // NSA_SEL_LEN
    d = ATTN_HEADS * HEAD_DIM
    kern = functools.partial(_nsa_sel_attn_kernel, n_sel=n_sel)
    kv_spec = pl.BlockSpec((1, 1, t, HEAD_DIM), lambda i, g, n: (i, g, 0, 0))
    return pl.pallas_call(
        kern, grid=(b, KV_HEADS, t // Q_BLOCK),
        in_specs=[pl.BlockSpec((1, GQA, Q_BLOCK, HEAD_DIM), lambda i, g, n: (i, g, n, 0)),
                  kv_spec, kv_spec,
                  pl.BlockSpec((1, 1, Q_BLOCK, n_sel), lambda i, g, n: (i, g, n, 0)),
                  pl.BlockSpec((1, GQA, Q_BLOCK, 2 * Q_BLOCK), lambda i, g, n: (g, 0, 0, 0)),
                  pl.BlockSpec((ATTN_HEADS, 1, 1), lambda i, g, n: (0, 0, 0)),
                  pl.BlockSpec((1, Q_BLOCK, LANES), lambda i, g, n: (i, n, 0))],
        out_specs=pl.BlockSpec((1, Q_BLOCK, GQA * HEAD_DIM), lambda i, g, n: (i, n, g)),
        out_shape=jax.ShapeDtypeStruct((b, t, d), F32),
        compiler_params=_cparams("parallel", "parallel", "parallel"), name="nsa_sel_attn",
    )(q, ks, vs, sel, bias_near, far, gates)


def _nsa_layer(x2d, g, w_in, pos_k, k_w1, k_w2, pos_v, v_w1, v_w2, w_out, t5_table, final_g,
               batch, seq):
    nq, nkv = ATTN_HEADS * HEAD_DIM, KV_HEADS * HEAD_DIM
    offs = [0, nq] + [nq + nkv * (i + 1) for i in range(6)]
    ws = [w_in[:, offs[i]:offs[i + 1]] for i in range(7)]
    n_gate = 3 * ATTN_HEADS
    wg = jnp.pad(w_in[:, offs[7]:offs[7] + n_gate], ((0, 0), (0, LANES - n_gate)))
    wz = w_in[:, offs[7] + n_gate:]
    q, kc, vc, ks, vs, kw, vw, gates, z = _norm_proj(
        x2d, g, ws + [wg, wz], [True] * 7 + [False, False],
        [BF16, F32, F32, BF16, BF16, BF16, BF16, F32, F32], batch, seq)
    gates = gates.reshape(batch, seq, LANES)
    k_cmp = _nsa_compress(kc, pos_k, k_w1, k_w2)
    v_cmp = _nsa_compress(vc, pos_v, v_w1, v_w2)
    o_cmp, sel = _nsa_cmp_attn(q, k_cmp, v_cmp, gates)
    bias_near = _band_bias(t5_table, 1).reshape(KV_HEADS, GQA, Q_BLOCK, 2 * Q_BLOCK)
    far = t5_table[T5_BUCKETS - 1].astype(F32).reshape(ATTN_HEADS, 1, 1)
    o_sel = _nsa_sel_attn(q, ks, vs, sel, bias_near, far, gates)
    bias_win = _band_bias(t5_table, -(-NSA_WINDOW // Q_BLOCK))
    o_win = _band_attn(q, kw, vw, bias_win, NSA_WINDOW, gates=gates, gate_col=2 * ATTN_HEADS)
    sh = x2d.shape
    return _out_proj(x2d, [o_cmp.reshape(sh), o_sel.reshape(sh), o_win.reshape(sh)], z, w_out,
                     final_g)


def _lru_kernel(x_ref, g_ref, win_u_ref, win_z_ref, cw_ref, cb_ref, wa_ref, ba_ref, wx_ref, bx_ref,
                sp_ref, wout_ref, *rest, half, has_fg):
    fg_ref = rest[0] if has_fg else None
    o_ref, tail_ref, h_ref = rest[-3:]
    j = pl.program_id(1)

    @pl.when(j == 0)
    def _():
        tail_ref[...] = jnp.zeros_like(tail_ref)
        h_ref[...] = jnp.zeros_like(h_ref)

    x = x_ref[...]
    xb = _rms(x, g_ref[...]).astype(BF16)
    u = jnp.dot(xb, win_u_ref[...], preferred_element_type=F32)
    z = jnp.dot(xb, win_z_ref[...], preferred_element_type=F32)
    tm, width = u.shape
    ext = jnp.concatenate([tail_ref[...], u], axis=0)
    tail_ref[...] = u[tm - 8:, :]
    uc = cb_ref[...] + cw_ref[CONV_WIDTH - 1:CONV_WIDTH, :] * u
    for s in range(1, CONV_WIDTH):
        uc = uc + cw_ref[CONV_WIDTH - 1 - s:CONV_WIDTH - s, :] * ext[8 - s:8 - s + tm, :]
    ucb = uc.astype(BF16)
    gr, gi = [], []
    for c in range(width // half):
        blk = ucb[:, c * half:(c + 1) * half]
        gr.append(jnp.dot(blk, wa_ref[c], preferred_element_type=F32))
        gi.append(jnp.dot(blk, wx_ref[c], preferred_element_type=F32))
    rg = jax.nn.sigmoid(jnp.concatenate(gr, axis=1) + ba_ref[...])
    ig = jax.nn.sigmoid(jnp.concatenate(gi, axis=1) + bx_ref[...])
    log_a = -LRU_C * rg * sp_ref[...]
    a = jnp.exp(log_a)
    bv = jnp.sqrt(1.0 - jnp.exp(2.0 * log_a)) * (ig * uc)
    row = lax.broadcasted_iota(jnp.int32, (tm, width), 0)
    sh = 1
    while sh < tm:
        a_s = jnp.where(row >= sh, pltpu.roll(a, sh, 0), 1.0)
        b_s = jnp.where(row >= sh, pltpu.roll(bv, sh, 0), 0.0)
        bv = a * b_s + bv
        a = a * a_s
        sh *= 2
    h = bv + a * h_ref[...]
    h_ref[...] = h[tm - 1:tm, :]
    gated = (h * _silu(z)).astype(BF16)
    y = x + jnp.dot(gated, wout_ref[...], preferred_element_type=F32)
    o_ref[...] = _rms(y, fg_ref[...]) if has_fg else y


def _block_diag(w, group):
    nb, n, _ = w.shape
    w = w.reshape(nb // group, group, n, n)
    eye = jnp.eye(group, dtype=w.dtype)
    return jnp.einsum('cgij,gh->cgihj', w, eye).reshape(nb // group, group * n, group * n)


def _lru_layer(x2d, g, w_in, conv_w, conv_b, ga_w, ga_b, gx_w, gx_b, lam, w_out, final_g, batch, seq):
    n, d = x2d.shape
    width = w_in.shape[1] // 2
    blk = ga_w.shape[1]
    group = LANES // math.gcd(blk, LANES)
    group = min(group, ga_w.shape[0])
    half = group * blk
    nsup = width // half
    tm = min(LRU_TILE, seq)
    tiles = seq // tm
    row = lambda i, j: (i * tiles + j, 0)
    fixed = lambda i, j: (0, 0)
    fixed3 = lambda i, j: (0, 0, 0)
    vec = lambda a: a.astype(F32).reshape(1, -1)
    softplus_neg_lam = jax.nn.softplus(-lam.astype(F32))
    has_fg = final_g is not None
    in_specs = [pl.BlockSpec((tm, d), row), pl.BlockSpec((1, d), fixed),
                pl.BlockSpec((d, width), fixed), pl.BlockSpec((d, width), fixed),
                pl.BlockSpec((CONV_WIDTH, width), fixed), pl.BlockSpec((1, width), fixed),
                pl.BlockSpec((nsup, half, half), fixed3), pl.BlockSpec((1, width), fixed),
                pl.BlockSpec((nsup, half, half), fixed3), pl.BlockSpec((1, width), fixed),
                pl.BlockSpec((1, width), fixed), pl.BlockSpec((width, d), fixed)]
    args = [x2d, g.reshape(1, d), w_in[:, :width].astype(BF16), w_in[:, width:].astype(BF16),
            conv_w.astype(F32), vec(conv_b), _block_diag(ga_w, group).astype(BF16), vec(ga_b),
            _block_diag(gx_w, group).astype(BF16), vec(gx_b), vec(softplus_neg_lam),
            w_out.astype(BF16)]
    if has_fg:
        in_specs.append(pl.BlockSpec((1, d), fixed))
        args.append(final_g.reshape(1, d))
    kern = functools.partial(_lru_kernel, half=half, has_fg=has_fg)
    return pl.pallas_call(
        kern, grid=(batch, tiles), in_specs=in_specs, out_specs=pl.BlockSpec((tm, d), row),
        out_shape=jax.ShapeDtypeStruct((n, d), F32),
        scratch_shapes=[pltpu.VMEM((8, width), F32), pltpu.VMEM((1, width), F32)],
        compiler_params=_cparams("parallel", "arbitrary"), name="rglru_layer",
    )(*args)


def kernel(x, t5_table, norm_g, final_g, a_w_in, a_sinks, a_w_out, b_mu, b_w_in, b_w0, b_w1, b_w2, b_a0, b_a1, b_a2, b_k_k, b_k_a, b_r_k, b_lnx_w, b_lnx_b, b_w_out, c_w_in, c_cmp_pos_k, c_cmp_k_w1, c_cmp_k_w2, c_cmp_pos_v, c_cmp_v_w1, c_cmp_v_w2, c_w_out, d_w_in, d_conv_w, d_conv_b, d_gate_a_w, d_gate_a_b, d_gate_x_w, d_gate_x_b, d_lambda, d_w_out):
    batch, seq, d = x.shape
    depth = norm_g.shape[0]
    h = x.reshape(batch * seq, d)
    for layer in range(depth):
        m, j = layer % 4, layer // 4
        g = norm_g[layer]
        fg = final_g if layer == depth - 1 else None
        if m == 0:
            h = _swa_layer(h, g, a_w_in[j], a_sinks[j], a_w_out[j], t5_table, fg, batch, seq)
        elif m == 1:
            h = _rwkv_layer(h, g, b_mu[j], b_w_in[j], b_w0[j], b_w1[j], b_w2[j], b_a0[j], b_a1[j],
                            b_a2[j], b_k_k[j], b_k_a[j], b_r_k[j], b_lnx_w[j], b_lnx_b[j],
                            b_w_out[j], fg, batch, seq)
        elif m == 2:
            h = _nsa_layer(h, g, c_w_in[j], c_cmp_pos_k[j], c_cmp_k_w1[j], c_cmp_k_w2[j],
                           c_cmp_pos_v[j], c_cmp_v_w1[j], c_cmp_v_w2[j], c_w_out[j], t5_table,
                           fg, batch, seq)
        else:
            h = _lru_layer(h, g, d_w_in[j], d_conv_w[j], d_conv_b[j], d_gate_a_w[j], d_gate_a_b[j],
                           d_gate_x_w[j], d_gate_x_b[j], d_lambda[j], d_w_out[j], fg, batch, seq)
    return h.reshape(batch, seq, d)
```

```python
import functools
import math

import jax
import jax.numpy as jnp
from jax import lax
from jax.experimental import pallas as pl
from jax.experimental.pallas import tpu as pltpu

F32 = jnp.float32
BF16 = jnp.bfloat16

EPS = 1e-6
NEG = -1e30
BIG = 1e30
T5_BUCKETS = 32
T5_MAX_DIST = 128
ATTN_HEADS = 16
HEAD_DIM = 64
KV_HEADS = 4
GQA = ATTN_HEADS // KV_HEADS
Q_BLOCK = 128
SWA_WINDOW = 128
RWKV_HEAD = 64
RWKV_GN_EPS = 64e-5
NSA_CMP_LEN = 32
NSA_CMP_STRIDE = 16
NSA_SEL_LEN = 64
NSA_TOPK = 16
NSA_WINDOW = 512
LRU_BLOCKS = 16
LRU_C = 8.0
CONV_WIDTH = 4

LANES = 128
VMEM_LIMIT = 56 * 1024 * 1024
ROW_TILE = 256
RWKV_CHUNK = 64
RWKV_BLOCK = 256
RWKV_LANES = 256
RWKV_SCORE_PASSES = 1
RWKV_VALUE_PASSES = 1
RWKV_INVERSE_PASSES = 1
RWKV_STATE_PASSES = 3
NSA_FAR_KEYS = 512
LRU_TILE = 256


def _cparams(*sem):
    return pltpu.CompilerParams(dimension_semantics=sem, vmem_limit_bytes=VMEM_LIMIT)


def _bdot(a, b):
    return jnp.dot(a.astype(BF16), b.astype(BF16), preferred_element_type=F32)


def _bdot_nt(a, b):
    return lax.dot_general(a.astype(BF16), b.astype(BF16), (((1,), (1,)), ((), ())),
                           preferred_element_type=F32)


def _bdot_tn(a, b):
    return lax.dot_general(a.astype(BF16), b.astype(BF16), (((0,), (0,)), ((), ())),
                           preferred_element_type=F32)


def _split2(x):
    hi = x.astype(BF16)
    lo = (x - hi.astype(F32)).astype(BF16)
    return hi, lo


def _dot3(a, b, dot=_bdot):
    ah, al = _split2(a)
    bh, bl = _split2(b)
    return dot(ah, bh) + (dot(ah, bl) + dot(al, bh))


def _dot_exact_rhs(a, b01, dot=_bdot):
    ah = a.astype(BF16)
    r1 = a - ah.astype(F32)
    am = r1.astype(BF16)
    al = (r1 - am.astype(F32)).astype(BF16)
    return dot(ah, b01) + (dot(am, b01) + dot(al, b01))


def _rms(x, g):
    return x * lax.rsqrt(jnp.mean(x * x, axis=-1, keepdims=True) + EPS) * g


def _silu(z):
    return z * jax.nn.sigmoid(z)


def _gate_column(gates, c):
    col = lax.broadcasted_iota(jnp.int32, gates.shape, 1)
    return jax.nn.sigmoid(jnp.sum(jnp.where(col == c, gates, 0.0), axis=-1, keepdims=True))


def _norm_proj_kernel(x_ref, g_ref, *refs, n_out, head_major):
    w_refs, o_refs = refs[:n_out], refs[n_out:]
    xb = _rms(x_ref[...], g_ref[...]).astype(BF16)
    for w_ref, o_ref, hm in zip(w_refs, o_refs, head_major):
        width = w_ref.shape[1]
        for c0 in range(0, width, 512):
            cw = min(512, width - c0)
            acc = jnp.dot(xb, w_ref[:, c0:c0 + cw], preferred_element_type=F32)
            if hm:
                for j in range(cw // HEAD_DIM):
                    o_ref[0, (c0 // HEAD_DIM) + j] = acc[:, j * HEAD_DIM:(j + 1) * HEAD_DIM].astype(o_ref.dtype)
            else:
                o_ref[:, c0:c0 + cw] = acc.astype(o_ref.dtype)


def _norm_proj(x2d, g, weights, head_major, dtypes, batch, seq):
    n, d = x2d.shape
    tm = min(ROW_TILE, seq)
    tiles_per_seq = seq // tm
    in_specs = [pl.BlockSpec((tm, d), lambda i: (i, 0)), pl.BlockSpec((1, d), lambda i: (0, 0))]
    out_specs, out_shapes = [], []
    for w, hm, dt in zip(weights, head_major, dtypes):
        width = w.shape[1]
        in_specs.append(pl.BlockSpec((d, width), lambda i: (0, 0)))
        if hm:
            nh = width // HEAD_DIM
            out_shapes.append(jax.ShapeDtypeStruct((batch, nh, seq, HEAD_DIM), dt))
            out_specs.append(pl.BlockSpec((1, nh, tm, HEAD_DIM),
                                          lambda i: (i // tiles_per_seq, 0, i % tiles_per_seq, 0)))
        else:
            out_shapes.append(jax.ShapeDtypeStruct((n, width), dt))
            out_specs.append(pl.BlockSpec((tm, width), lambda i: (i, 0)))
    kern = functools.partial(_norm_proj_kernel, n_out=len(weights), head_major=tuple(head_major))
    return pl.pallas_call(
        kern, grid=(n // tm,), in_specs=in_specs, out_specs=out_specs, out_shape=out_shapes,
        compiler_params=_cparams("parallel"), name="norm_proj",
    )(x2d, g.reshape(1, d), *[w.astype(BF16) for w in weights])


def _out_proj_kernel(*refs, n_a, has_z, has_g):
    x_ref = refs[0]
    a_refs = refs[1:1 + n_a]
    pos = 1 + n_a
    z_ref = refs[pos] if has_z else None
    pos += int(has_z)
    w_ref = refs[pos]
    pos += 1
    g_ref = refs[pos] if has_g else None
    o_ref = refs[-1]
    a = a_refs[0][...].astype(F32)
    for r in a_refs[1:]:
        a = a + r[...].astype(F32)
    if has_z:
        a = a * _silu(z_ref[...])
    y = x_ref[...] + jnp.dot(a.astype(BF16), w_ref[...], preferred_element_type=F32)
    if has_g:
        y = _rms(y, g_ref[...])
    o_ref[...] = y


def _out_proj(x2d, a_list, z, w, final_g=None):
    n, d = x2d.shape
    c = w.shape[0]
    tm = min(ROW_TILE, n)
    row = lambda i: (i, 0)
    fixed = lambda i: (0, 0)
    in_specs = [pl.BlockSpec((tm, d), row)] + [pl.BlockSpec((tm, c), row) for _ in a_list]
    args = [x2d] + list(a_list)
    if z is not None:
        in_specs.append(pl.BlockSpec((tm, c), row))
        args.append(z)
    in_specs.append(pl.BlockSpec((c, d), fixed))
    args.append(w.astype(BF16))
    if final_g is not None:
        in_specs.append(pl.BlockSpec((1, d), fixed))
        args.append(final_g.reshape(1, d))
    kern = functools.partial(_out_proj_kernel, n_a=len(a_list), has_z=z is not None,
                             has_g=final_g is not None)
    return pl.pallas_call(
        kern, grid=(n // tm,), in_specs=in_specs, out_specs=pl.BlockSpec((tm, d), row),
        out_shape=jax.ShapeDtypeStruct((n, d), F32), compiler_params=_cparams("parallel"),
        name="out_proj",
    )(*args)


def _t5_bucket(dist):
    max_exact = T5_BUCKETS // 2
    d = jnp.maximum(dist, 0)
    df = jnp.maximum(d, 1).astype(F32)
    large = max_exact + (jnp.log(df / max_exact) / math.log(T5_MAX_DIST / max_exact)
                         * (T5_BUCKETS - max_exact)).astype(jnp.int32)
    large = jnp.minimum(large, T5_BUCKETS - 1)
    return jnp.where(d < max_exact, d, large)


def _band_bias(t5_table, nprev):
    kc = (nprev + 1) * Q_BLOCK
    period = kc + Q_BLOCK
    j = jnp.arange(period)
    j = jnp.where(j >= kc, j - period, j)
    vec = jnp.take(t5_table, _t5_bucket(nprev * Q_BLOCK - j), axis=0).astype(F32).T
    heads = vec.shape[0]
    flat = jnp.tile(vec, (1, Q_BLOCK))[:, :Q_BLOCK * (period - 1)]
    return flat.reshape(heads, Q_BLOCK, period - 1)[:, :, :kc]


def _band_attn_kernel(*refs, nprev, window, has_sink, gate_col):
    q_ref, k_ref, v_ref, bias_ref = refs[:4]
    pos = 4
    sink_ref = refs[pos] if has_sink else None
    pos += int(has_sink)
    gate_ref = refs[pos] if gate_col is not None else None
    o_ref = refs[-1]
    n = pl.program_id(1)
    tq = Q_BLOCK
    kc = (nprev + 1) * tq
    qi = lax.broadcasted_iota(jnp.int32, (tq, kc), 0)
    kj = lax.broadcasted_iota(jnp.int32, (tq, kc), 1)
    dist = nprev * tq + qi - kj
    kpos = (n - nprev) * tq + kj
    valid = ((dist >= 0) & (dist < window) & (kpos >= 0))[None]
    starts = [pl.multiple_of(jnp.maximum(n - nprev + j, 0) * tq, tq) for j in range(nprev + 1)]
    groups = range(KV_HEADS)
    scores = []
    for g in groups:
        kg = jnp.concatenate([k_ref[0, g, pl.ds(s, tq), :] for s in starts], axis=0)
        qg = q_ref[0, GQA * g:GQA * (g + 1)].reshape(GQA * tq, HEAD_DIM) * (HEAD_DIM ** -0.5)
        scores.append(_bdot_nt(qg, kg))
    probs, denoms = [], []
    for g in groups:
        s = scores[g].reshape(GQA, tq, kc) + bias_ref[GQA * g:GQA * (g + 1)]
        s = jnp.where(valid, s, NEG)
        m = jnp.max(s, axis=-1, keepdims=True)
        if has_sink:
            sk = sink_ref[GQA * g:GQA * (g + 1)]
            m = jnp.maximum(m, sk)
        p = jnp.exp(s - m)
        l = jnp.sum(p, axis=-1, keepdims=True)
        if has_sink:
            l = l + jnp.exp(sk - m)
        probs.append(p.reshape(GQA * tq, kc).astype(BF16))
        denoms.append(l)
    pvs = []
    for g in groups:
        vg = jnp.concatenate([v_ref[0, g, pl.ds(s, tq), :] for s in starts], axis=0)
        pvs.append(jnp.dot(probs[g], vg, preferred_element_type=F32))
    for g in groups:
        o = pvs[g].reshape(GQA, tq, HEAD_DIM) / denoms[g]
        outs = []
        for r in range(GQA):
            oh = o[r]
            if gate_col is not None:
                c = gate_col + GQA * g + r
                oh = oh * jax.nn.sigmoid(gate_ref[0, :, c:c + 1])
            outs.append(oh)
        o_ref[0, :, g * GQA * HEAD_DIM:(g + 1) * GQA * HEAD_DIM] = jnp.concatenate(outs, axis=-1)


def _band_attn(q, k, v, bias, window, sinks=None, gates=None, gate_col=None):
    b, _, t, _ = q.shape
    nprev = -(-window // Q_BLOCK)
    kc = (nprev + 1) * Q_BLOCK
    in_specs = [
        pl.BlockSpec((1, ATTN_HEADS, Q_BLOCK, HEAD_DIM), lambda i, n: (i, 0, n, 0)),
        pl.BlockSpec((1, KV_HEADS, t, HEAD_DIM), lambda i, n: (i, 0, 0, 0)),
        pl.BlockSpec((1, KV_HEADS, t, HEAD_DIM), lambda i, n: (i, 0, 0, 0)),
        pl.BlockSpec((ATTN_HEADS, Q_BLOCK, kc), lambda i, n: (0, 0, 0)),
    ]
    args = [q, k, v, bias]
    if sinks is not None:
        in_specs.append(pl.BlockSpec((ATTN_HEADS, 1, 1), lambda i, n: (0, 0, 0)))
        args.append(sinks.astype(F32).reshape(ATTN_HEADS, 1, 1))
    if gates is not None:
        in_specs.append(pl.BlockSpec((1, Q_BLOCK, LANES), lambda i, n: (i, n, 0)))
        args.append(gates)
    kern = functools.partial(_band_attn_kernel, nprev=nprev, window=window,
                             has_sink=sinks is not None, gate_col=gate_col)
    d = ATTN_HEADS * HEAD_DIM
    return pl.pallas_call(
        kern, grid=(b, t // Q_BLOCK), in_specs=in_specs,
        out_specs=pl.BlockSpec((1, Q_BLOCK, d), lambda i, n: (i, n, 0)),
        out_shape=jax.ShapeDtypeStruct((b, t, d), F32),
        compiler_params=_cparams("parallel", "parallel"), name="band_attn",
    )(*args)


def _swa_layer(x2d, g, w_in, sinks, w_out, t5_table, final_g, batch, seq):
    nq, nkv = ATTN_HEADS * HEAD_DIM, KV_HEADS * HEAD_DIM
    ws = [w_in[:, :nq], w_in[:, nq:nq + nkv], w_in[:, nq + nkv:nq + 2 * nkv], w_in[:, nq + 2 * nkv:]]
    q, k, v, z = _norm_proj(x2d, g, ws, [True, True, True, False], [BF16, BF16, BF16, F32], batch, seq)
    bias = _band_bias(t5_table, -(-SWA_WINDOW // Q_BLOCK))
    o = _band_attn(q, k, v, bias, SWA_WINDOW, sinks=sinks)
    return _out_proj(x2d, [o.reshape(x2d.shape)], z, w_out, final_g)


def _rwkv_pre_kernel(x_ref, xp_ref, g_ref, mu_ref, wr_ref, wk_ref, wv_ref, wz_ref,
                     w0_ref, w1_ref, w2_ref, a0_ref, a1_ref, a2_ref,
                     r_ref, k_ref, v_ref, z_ref, lw_ref, a_ref, *, tiles_per_seq):
    i = pl.program_id(0)
    g = g_ref[...]
    xn = _rms(x_ref[...], g)
    prev = _rms(xp_ref[...], g)[7:8]
    prev = jnp.where(i % tiles_per_seq == 0, 0.0, prev)
    row = lax.broadcasted_iota(jnp.int32, xn.shape, 0)
    xprev = jnp.where(row == 0, prev, pltpu.roll(xn, 1, 0))
    xx = xprev - xn
    lerp = lambda s: xn + xx * mu_ref[s:s + 1, :]
    r_ref[...] = _bdot(lerp(0), wr_ref[...])
    k_ref[...] = _bdot(lerp(1), wk_ref[...])
    v_ref[...] = _bdot(lerp(2), wv_ref[...])
    z_ref[...] = _bdot(lerp(3), wz_ref[...])
    wl = w0_ref[...] + _bdot(jnp.tanh(_bdot(lerp(4), w1_ref[...])), w2_ref[...])
    sp = jnp.maximum(-wl, 0.0) + jnp.log1p(jnp.exp(-jnp.abs(wl)))
    lw_ref[...] = -jnp.exp(-sp - 0.5)
    al = a0_ref[...] + _bdot(_bdot(lerp(5), a1_ref[...]), a2_ref[...])
    a_ref[...] = jax.nn.sigmoid(al)


def _rwkv_pre(x2d, g, mu, w_in, w0, w1, w2, a0, a1, a2, seq):
    n, d = x2d.shape
    c = w0.shape[0]
    tm = min(ROW_TILE, seq)
    tiles_per_seq = seq // tm
    row = lambda i: (i, 0)
    fixed = lambda i: (0, 0)
    ws = [w_in[:, s * c:(s + 1) * c].astype(BF16) for s in range(4)]
    in_specs = [
        pl.BlockSpec((tm, d), row),
        pl.BlockSpec((8, d), lambda i: (jnp.maximum(i * (tm // 8) - 1, 0), 0)),
        pl.BlockSpec((1, d), fixed), pl.BlockSpec((6, d), fixed),
    ] + [pl.BlockSpec((d, c), fixed)] * 4 + [
        pl.BlockSpec((1, c), fixed), pl.BlockSpec(w1.shape, fixed), pl.BlockSpec(w2.shape, fixed),
        pl.BlockSpec((1, c), fixed), pl.BlockSpec(a1.shape, fixed), pl.BlockSpec(a2.shape, fixed),
    ]
    out = jax.ShapeDtypeStruct((n, c), F32)
    kern = functools.partial(_rwkv_pre_kernel, tiles_per_seq=tiles_per_seq)
    return pl.pallas_call(
        kern, grid=(n // tm,), in_specs=in_specs, out_specs=[pl.BlockSpec((tm, c), row)] * 6,
        out_shape=[out] * 6, compiler_params=_cparams("parallel"), name="rwkv_pre",
    )(x2d, x2d, g.reshape(1, d), mu, *ws, w0.reshape(1, c), w1.astype(BF16), w2.astype(BF16),
      a0.reshape(1, c), a1.astype(BF16), a2.astype(BF16))


def _dot3_many(a_list, b_list, dot=_bdot, passes=3):
    if passes == 1:
        return [dot(a, b) for a, b in zip(a_list, b_list)]
    sa = [_split2(a) for a in a_list]
    sb = [_split2(b) for b in b_list]
    hh = [dot(x[0], y[0]) for x, y in zip(sa, sb)]
    hl = [dot(x[0], y[1]) for x, y in zip(sa, sb)]
    lh = [dot(x[1], y[0]) for x, y in zip(sa, sb)]
    return [p + (q + r) for p, q, r in zip(hh, hl, lh)]


def _rwkv_chunk_ops(rs, kps, vs, aas, bbs, lws, tri_incl, lower_strict, lower_incl, eye, blockdiag):
    L = RWKV_CHUNK
    n = len(rs)
    n_ch = rs[0].shape[1]
    pieces = []
    for lw in lws:
        hi = lw.astype(BF16)
        r1 = lw - hi.astype(F32)
        mid = r1.astype(BF16)
        pieces.append((hi, mid, (r1 - mid.astype(F32)).astype(BF16)))
    cum_p = [[jnp.dot(tri_incl, p[i], preferred_element_type=F32) for p in pieces] for i in range(3)]
    cums = [a + (b + c) for a, b, c in zip(*cum_p)]
    ats, rts, bks, bhs, khs, wls = [], [], [], [], [], []
    for r, kp, aa, bb, lw, cum in zip(rs, kps, aas, bbs, lws, cums):
        w_inv = jnp.exp(-cum)
        cum_last = cum[L - 1:L, :]
        w_tail = jnp.exp(cum_last - cum)
        ats.append(aa * jnp.exp(cum - lw))
        rts.append(r * jnp.exp(cum))
        bks.append(jnp.concatenate([bb * w_inv, kp * w_inv], axis=0))
        bhs.append(bb * w_tail)
        khs.append(kp * w_tail)
        wls.append(jnp.exp(cum_last))
    ars = [jnp.concatenate([at, rt], axis=0) for at, rt in zip(ats, rts)]
    As = _dot3_many(ars, bks, dot=_bdot_nt, passes=RWKV_SCORE_PASSES)
    a_ab = [jnp.where(lower_strict, A[:L, :L], 0.0) for A in As]
    a_rb = [jnp.where(lower_incl, A[L:, :L], 0.0) for A in As]
    a_k = [jnp.concatenate([jnp.where(lower_strict, A[:L, L:], 0.0),
                            jnp.where(lower_incl, A[L:, L:], 0.0)], axis=0) for A in As]
    dblk = [jnp.where(blockdiag, x, 0.0) for x in a_ab]
    akv = _dot3_many(a_k, vs, passes=RWKV_VALUE_PASSES)
    inv = functools.partial(_dot3_many, passes=RWKV_INVERSE_PASSES)
    d2 = inv(dblk, dblk)
    res = inv([eye + d for d in dblk] + d2, [eye + d for d in d2] + d2)
    s4, d4 = res[:n], res[n:]
    res = inv(d4 + d4, s4 + d4)
    s8 = [s + x for s, x in zip(s4, res[:n])]
    d8 = res[n:]
    tdiag = [s + x for s, x in zip(s8, inv(d8, s8))]
    rhs = [jnp.concatenate([at, x[:L]], axis=1) for at, x in zip(ats, akv)]
    res = inv(tdiag + tdiag, rhs + [x - d for x, d in zip(a_ab, dblk)])
    xt, nt = res[:n], res[n:]
    res = inv(nt + nt, xt + nt)
    u = [x + y for x, y in zip(xt, res[:n])]
    nt2 = res[n:]
    pq = [x + y for x, y in zip(u, inv(nt2, u))]
    res = _dot3_many(a_rb, pq, passes=RWKV_VALUE_PASSES)
    gh = [jnp.concatenate([rt, x[L:]], axis=1) + y for rt, x, y in zip(rts, akv, res)]
    res = _dot3_many(bhs + khs, pq + vs, dot=_bdot_tn, passes=RWKV_VALUE_PASSES)
    out = []
    for i in range(n):
        mc = res[i]
        m_op = mc[:, :n_ch] + eye * wls[i]
        c_op = mc[:, n_ch:] + res[n + i]
        out.append((gh[i][:, :n_ch], gh[i][:, n_ch:], m_op, c_op))
    return out


def _rwkv_scan_kernel(r_ref, k_ref, v_ref, z_ref, lw_ref, a_ref, kk_ref, ka_ref, rk_ref,
                      lnw_ref, lnb_ref, o_ref, state_ref):
    tb = pl.program_id(2)

    @pl.when(tb == 0)
    def _():
        state_ref[...] = jnp.zeros_like(state_ref)

    L = RWKV_CHUNK
    N = RWKV_HEAD
    ri = lax.broadcasted_iota(jnp.int32, (L, L), 0)
    ci = lax.broadcasted_iota(jnp.int32, (L, L), 1)
    lower_strict = ri > ci
    lower_incl = ri >= ci
    tri_incl = jnp.where(lower_incl, 1.0, 0.0).astype(BF16)
    eye = jnp.where(ri == ci, 1.0, 0.0).astype(F32)
    blockdiag = (ri // 16) == (ci // 16)
    n_heads = r_ref.shape[2] // N
    n_chunks = r_ref.shape[1] // L
    probs = [(hh, c) for c in range(n_chunks) for hh in range(n_heads)]
    rs, kps, vs, aas, bbs, lws = [], [], [], [], [], []
    for hh, c in probs:
        sl = slice(hh * N, (hh + 1) * N)
        rows = slice(c * L, (c + 1) * L)
        k = k_ref[0, rows, sl]
        a = a_ref[0, rows, sl]
        kk = k * kk_ref[:, sl]
        kk = kk / jnp.maximum(jnp.sqrt(jnp.sum(kk * kk, axis=-1, keepdims=True)), 1e-12)
        rs.append(r_ref[0, rows, sl])
        kps.append(k * (1.0 + (a - 1.0) * ka_ref[:, sl]))
        vs.append(v_ref[0, rows, sl])
        aas.append(-kk)
        bbs.append(kk * a)
        lws.append(lw_ref[0, rows, sl])
    ops = _rwkv_chunk_ops(rs, kps, vs, aas, bbs, lws, tri_incl, lower_strict, lower_incl, eye,
                          blockdiag)
    states = [state_ref[hh] for hh in range(n_heads)]
    ys = {}
    for c in range(n_chunks):
        idx = [c * n_heads + hh for hh in range(n_heads)]
        upd = _dot3_many([jnp.concatenate([ops[i][0], ops[i][2]], axis=0) for i in idx], states,
                         passes=RWKV_STATE_PASSES)
        for hh, i in enumerate(idx):
            ys[(hh, c)] = upd[hh][:L] + ops[i][1]
            states[hh] = upd[hh][L:] + ops[i][3]
    for hh in range(n_heads):
        state_ref[hh] = states[hh]
    cols = []
    for hh in range(n_heads):
        sl = slice(hh * N, (hh + 1) * N)
        parts = []
        for c in range(n_chunks):
            rows = slice(c * L, (c + 1) * L)
            i = c * n_heads + hh
            y = ys[(hh, c)]
            mean = jnp.mean(y, axis=-1, keepdims=True)
            yc = y - mean
            var = jnp.mean(yc * yc, axis=-1, keepdims=True)
            yn = yc * lax.rsqrt(var + RWKV_GN_EPS) * lnw_ref[:, sl] + lnb_ref[:, sl]
            bonus = jnp.sum(rs[i] * kps[i] * rk_ref[:, sl], axis=-1, keepdims=True) * vs[i]
            parts.append((yn + bonus) * _silu(z_ref[0, rows, sl]))
        cols.append(jnp.concatenate(parts, axis=0))
    o_ref[0] = jnp.concatenate(cols, axis=1)


def _rwkv_scan(r, k, v, z, lw, a, k_k, k_a, r_k, lnx_w, lnx_b):
    b, t, c = r.shape
    tb = min(RWKV_BLOCK, t)
    seq_spec = pl.BlockSpec((1, tb, RWKV_LANES), lambda i, p, j: (i, j, p))
    par_spec = pl.BlockSpec((1, RWKV_LANES), lambda i, p, j: (0, p))
    params = [x.reshape(1, c).astype(F32) for x in (k_k, k_a, r_k, lnx_w, lnx_b)]
    return pl.pallas_call(
        _rwkv_scan_kernel, grid=(b, c // RWKV_LANES, t // tb),
        in_specs=[seq_spec] * 6 + [par_spec] * 5, out_specs=seq_spec,
        out_shape=jax.ShapeDtypeStruct((b, t, c), F32),
        scratch_shapes=[pltpu.VMEM((RWKV_LANES // RWKV_HEAD, RWKV_HEAD, RWKV_HEAD), F32)],
        compiler_params=_cparams("parallel", "parallel", "arbitrary"), name="rwkv_scan",
    )(r, k, v, z, lw, a, *params)


def _rwkv_layer(x2d, g, mu, w_in, w0, w1, w2, a0, a1, a2, k_k, k_a, r_k, lnx_w, lnx_b, w_out,
                final_g, batch, seq):
    c = w0.shape[0]
    r, k, v, z, lw, a = _rwkv_pre(x2d, g, mu, w_in, w0, w1, w2, a0, a1, a2, seq)
    sh = (batch, seq, c)
    y = _rwkv_scan(r.reshape(sh), k.reshape(sh), v.reshape(sh), z.reshape(sh), lw.reshape(sh),
                   a.reshape(sh), k_k, k_a, r_k, lnx_w, lnx_b)
    return _out_proj(x2d, [y.reshape(x2d.shape[0], c)], None, w_out, final_g)


def _nsa_compress_kernel(u_ref, posa_ref, posb_ref, w1a_ref, w1b_ref, w2_ref, o_ref):
    u = u_ref[0, 0]
    ha = _bdot(u + posa_ref[...], w1a_ref[...])
    hb = _bdot(u + posb_ref[...], w1b_ref[...])
    h = ha + pltpu.roll(hb, hb.shape[0] - 1, 0)
    o_ref[0, 0] = _bdot(_silu(h), w2_ref[...])


def _nsa_compress(t_hm, pos, w1, w2):
    b, g, t, dh = t_hm.shape
    nch = t // NSA_CMP_STRIDE
    half = NSA_CMP_STRIDE * dh
    u = t_hm.reshape(b, g, nch, half)
    posf = pos.astype(F32).reshape(2, 1, half)
    hid = w1.shape[1]
    fixed = lambda i, j: (0, 0)
    return pl.pallas_call(
        _nsa_compress_kernel, grid=(b, g),
        in_specs=[pl.BlockSpec((1, 1, nch, half), lambda i, j: (i, j, 0, 0)),
                  pl.BlockSpec((1, half), fixed), pl.BlockSpec((1, half), fixed),
                  pl.BlockSpec((half, hid), fixed), pl.BlockSpec((half, hid), fixed),
                  pl.BlockSpec((hid, dh), fixed)],
        out_specs=pl.BlockSpec((1, 1, nch, dh), lambda i, j: (i, j, 0, 0)),
        out_shape=jax.ShapeDtypeStruct((b, g, nch, dh), F32),
        compiler_params=_cparams("parallel", "parallel"), name="nsa_compress",
    )(u, posf[0], posf[1], w1[:half].astype(BF16), w1[half:].astype(BF16), w2.astype(BF16))


def _nsa_cmp_attn_kernel(q_ref, kc_ref, vc_ref, gate_ref, o_ref, sel_ref, *, n_sel):
    g = pl.program_id(1)
    n = pl.program_id(2)
    tq = Q_BLOCK
    ncmp = kc_ref.shape[2]
    q = q_ref[0].reshape(GQA * tq, HEAD_DIM) * (HEAD_DIM ** -0.5)
    s = _bdot_nt(q, kc_ref[0, 0]).reshape(GQA, tq, ncmp)
    tpos = n * tq + lax.broadcasted_iota(jnp.int32, (tq, ncmp), 0)
    cend = lax.broadcasted_iota(jnp.int32, (tq, ncmp), 1) * NSA_CMP_STRIDE + (NSA_CMP_LEN - 1)
    ok = (cend <= tpos)[None]
    s = jnp.where(ok, s, NEG)
    m = jnp.max(s, axis=-1, keepdims=True)
    e = jnp.where(ok, jnp.exp(s - m), 0.0)
    l = jnp.sum(e, axis=-1, keepdims=True)
    p = e / jnp.where(l > 0.0, l, 1.0)
    o = _bdot(p.reshape(GQA * tq, ncmp), vc_ref[0, 0]).reshape(GQA, tq, HEAD_DIM)
    outs = []
    for r in range(GQA):
        outs.append(o[r] * _gate_column(gate_ref[0], GQA * g + r))
    o_ref[0] = jnp.concatenate(outs, axis=-1)

    psum = p[0] + p[1] + p[2] + p[3]
    si = lax.broadcasted_iota(jnp.int32, (n_sel, ncmp), 0)
    ni = lax.broadcasted_iota(jnp.int32, (n_sel, ncmp), 1)
    ratio = NSA_SEL_LEN // NSA_CMP_STRIDE
    overlap = ((ni < ratio * (si + 1)) & (ni * NSA_CMP_STRIDE + NSA_CMP_LEN - 1 >= si * NSA_SEL_LEN))
    overlap = jnp.where(overlap, 1.0, 0.0).astype(BF16)
    imp = _dot_exact_rhs(psum, overlap, dot=lambda a, b: _bdot_nt(b, a))
    blk = lax.broadcasted_iota(jnp.int32, (n_sel, tq), 0)
    tq_pos = n * tq + lax.broadcasted_iota(jnp.int32, (n_sel, tq), 1)
    cur = tq_pos // NSA_SEL_LEN
    forced = (blk == 0) | (blk == cur) | (blk == cur - 1)
    future = blk * NSA_SEL_LEN > tq_pos
    imp = jnp.where(forced, BIG, jnp.where(future, NEG, imp))
    rank = jnp.zeros((n_sel, tq), F32)
    for j in range(n_sel):
        row = imp[j:j + 1, :]
        ahead = (row > imp) | ((row == imp) & (blk > j))
        rank = rank + jnp.where(ahead, 1.0, 0.0)
    sel = jnp.where(rank < float(min(NSA_TOPK, n_sel)), 1.0, 0.0)
    sel_ref[0, 0] = sel.T


def _nsa_cmp_attn(q, k_cmp, v_cmp, gates):
    b, _, t, _ = q.shape
    ncmp = k_cmp.shape[2]
    n_sel = t // NSA_SEL_LEN
    d = ATTN_HEADS * HEAD_DIM
    kern = functools.partial(_nsa_cmp_attn_kernel, n_sel=n_sel)
    cmp_spec = pl.BlockSpec((1, 1, ncmp, HEAD_DIM), lambda i, g, n: (i, g, 0, 0))
    return pl.pallas_call(
        kern, grid=(b, KV_HEADS, t // Q_BLOCK),
        in_specs=[pl.BlockSpec((1, GQA, Q_BLOCK, HEAD_DIM), lambda i, g, n: (i, g, n, 0)),
                  cmp_spec, cmp_spec,
                  pl.BlockSpec((1, Q_BLOCK, LANES), lambda i, g, n: (i, n, 0))],
        out_specs=[pl.BlockSpec((1, Q_BLOCK, GQA * HEAD_DIM), lambda i, g, n: (i, n, g)),
                   pl.BlockSpec((1, 1, Q_BLOCK, n_sel), lambda i, g, n: (i, g, n, 0))],
        out_shape=[jax.ShapeDtypeStruct((b, t, d), F32),
                   jax.ShapeDtypeStruct((b, KV_HEADS, t, n_sel), F32)],
        compiler_params=_cparams("parallel", "parallel", "parallel"), name="nsa_cmp_attn",
    )(q, k_cmp, v_cmp, gates)


def _nsa_sel_attn_kernel(q_ref, k_ref, v_ref, sel_ref, bias_ref, far_ref, gate_ref, o_ref, *, n_sel):
    g = pl.program_id(1)
    n = pl.program_id(2)
    tq = Q_BLOCK
    per_tile = tq // NSA_SEL_LEN
    fk = NSA_FAR_KEYS
    heads = range(GQA)
    qs = [q_ref[0, r] * (HEAD_DIM ** -0.5) for r in heads]
    sel = sel_ref[0, 0].astype(BF16)

    def selected(first_block, width):
        srow = lax.broadcasted_iota(jnp.int32, (n_sel, width), 0)
        scol = lax.broadcasted_iota(jnp.int32, (n_sel, width), 1) // NSA_SEL_LEN
        expand = jnp.where(srow == first_block + scol, 1.0, 0.0).astype(BF16)
        return jnp.dot(sel, expand, preferred_element_type=F32) > 0.5

    prev = pl.multiple_of(jnp.maximum(n - 1, 0) * tq, tq)
    diag = pl.multiple_of(n * tq, tq)
    kt = jnp.concatenate([k_ref[0, 0, pl.ds(prev, tq), :], k_ref[0, 0, pl.ds(diag, tq), :]], axis=0)
    vt = jnp.concatenate([v_ref[0, 0, pl.ds(prev, tq), :], v_ref[0, 0, pl.ds(diag, tq), :]], axis=0)
    qi = lax.broadcasted_iota(jnp.int32, (tq, 2 * tq), 0)
    kj = lax.broadcasted_iota(jnp.int32, (tq, 2 * tq), 1)
    ok = selected((n - 1) * per_tile, 2 * tq) & (tq + qi - kj >= 0)
    ss = [_bdot_nt(qs[r], kt) for r in heads]
    ms, ps = [], []
    for r in heads:
        s = jnp.where(ok, ss[r] + bias_ref[0, r], NEG)
        m = jnp.max(s, axis=-1, keepdims=True)
        ps.append(jnp.exp(s - m).astype(BF16))
        ms.append(m)
    accs = [jnp.dot(ps[r], vt, preferred_element_type=F32) for r in heads]
    far = far_ref[pl.ds(GQA * g, GQA)]
    ms = [ms[r] - far[r] for r in heads]
    far_end = (n - 1) * tq
    kcol = lax.broadcasted_iota(jnp.int32, (tq, fk), 1)

    def body(c, carry):
        ms, accs = carry
        start = pl.multiple_of(c * fk, fk)
        ok = selected(c * (fk // NSA_SEL_LEN), fk) & (start + kcol < far_end)
        kt = k_ref[0, 0, pl.ds(start, fk), :]
        vt = v_ref[0, 0, pl.ds(start, fk), :]
        ss = [_bdot_nt(qs[r], kt) for r in heads]
        m_new, alphas, ps = [], [], []
        for r in heads:
            s = jnp.where(ok, ss[r], NEG)
            m = jnp.maximum(ms[r], jnp.max(s, axis=-1, keepdims=True))
            ps.append(jnp.exp(s - m).astype(BF16))
            alphas.append(jnp.exp(ms[r] - m))
            m_new.append(m)
        pvs = [jnp.dot(ps[r], vt, preferred_element_type=F32) for r in heads]
        return tuple(m_new), tuple(alphas[r] * accs[r] + pvs[r] for r in heads)

    n_far = (jnp.maximum(n - 1, 0) * tq + fk - 1) // fk
    _, accs = lax.fori_loop(0, n_far, body, (tuple(ms), tuple(accs)))
    outs = []
    for r in heads:
        o = accs[r][:, :HEAD_DIM] / accs[r][:, HEAD_DIM:HEAD_DIM + 1]
        outs.append(o * _gate_column(gate_ref[0], ATTN_HEADS + GQA * g + r))
    o_ref[0] = jnp.concatenate(outs, axis=-1)


def _nsa_sel_attn(q, ks, vs, sel, bias_near, far, gates):
    b, _, t, _ = q.shape
    n_sel = t // NSA_SEL_LEN
    d = ATTN_HEADS * HEAD_DIM
    kern = functools.partial(_nsa_sel_attn_kernel, n_sel=n_sel)
    assert t % NSA_FAR_KEYS == 0 or t <= 2 * Q_BLOCK
    vs_ones = jnp.concatenate([vs, jnp.ones_like(vs)], axis=-1)
    return pl.pallas_call(
        kern, grid=(b, KV_HEADS, t // Q_BLOCK),
        in_specs=[pl.BlockSpec((1, GQA, Q_BLOCK, HEAD_DIM), lambda i, g, n: (i, g, n, 0)),
                  pl.BlockSpec((1, 1, t, HEAD_DIM), lambda i, g, n: (i, g, 0, 0)),
                  pl.BlockSpec((1, 1, t, 2 * HEAD_DIM), lambda i, g, n: (i, g, 0, 0)),
                  pl.BlockSpec((1, 1, Q_BLOCK, n_sel), lambda i, g, n: (i, g, n, 0)),
                  pl.BlockSpec((1, GQA, Q_BLOCK, 2 * Q_BLOCK), lambda i, g, n: (g, 0, 0, 0)),
                  pl.BlockSpec((ATTN_HEADS, 1, 1), lambda i, g, n: (0, 0, 0)),
                  pl.BlockSpec((1, Q_BLOCK, LANES), lambda i, g, n: (i, n, 0))],
        out_specs=pl.BlockSpec((1, Q_BLOCK, GQA * HEAD_DIM), lambda i, g, n: (i, n, g)),
        out_shape=jax.ShapeDtypeStruct((b, t, d), F32),
        compiler_params=_cparams("parallel", "parallel", "parallel"), name="nsa_sel_attn",
    )(q, ks, vs_ones, sel, bias_near, far, gates)


def _nsa_layer(x2d, g, w_in, pos_k, k_w1, k_w2, pos_v, v_w1, v_w2, w_out, t5_table, final_g,
               batch, seq):
    nq, nkv = ATTN_HEADS * HEAD_DIM, KV_HEADS * HEAD_DIM
    offs = [0, nq] + [nq + nkv * (i + 1) for i in range(6)]
    ws = [w_in[:, offs[i]:offs[i + 1]] for i in range(7)]
    n_gate = 3 * ATTN_HEADS
    wg = jnp.pad(w_in[:, offs[7]:offs[7] + n_gate], ((0, 0), (0, LANES - n_gate)))
    wz = w_in[:, offs[7] + n_gate:]
    q, kc, vc, ks, vs, kw, vw, gates, z = _norm_proj(
        x2d, g, ws + [wg, wz], [True] * 7 + [False, False],
        [BF16, F32, F32, BF16, BF16, BF16, BF16, F32, F32], batch, seq)
    gates = gates.reshape(batch, seq, LANES)
    k_cmp = _nsa_compress(kc, pos_k, k_w1, k_w2)
    v_cmp = _nsa_compress(vc, pos_v, v_w1, v_w2)
    o_cmp, sel = _nsa_cmp_attn(q, k_cmp, v_cmp, gates)
    bias_near = _band_bias(t5_table, 1).reshape(KV_HEADS, GQA, Q_BLOCK, 2 * Q_BLOCK)
    far = t5_table[T5_BUCKETS - 1].astype(F32).reshape(ATTN_HEADS, 1, 1)
    o_sel = _nsa_sel_attn(q, ks, vs, sel, bias_near, far, gates)
    bias_win = _band_bias(t5_table, -(-NSA_WINDOW // Q_BLOCK))
    o_win = _band_attn(q, kw, vw, bias_win, NSA_WINDOW, gates=gates, gate_col=2 * ATTN_HEADS)
    sh = x2d.shape
    return _out_proj(x2d, [o_cmp.reshape(sh), o_sel.reshape(sh), o_win.reshape(sh)], z, w_out,
                     final_g)


def _lru_kernel(x_ref, g_ref, win_u_ref, win_z_ref, cw_ref, cb_ref, wa_ref, ba_ref, wx_ref, bx_ref,
                sp_ref, wout_ref, *rest, half, has_fg):
    fg_ref = rest[0] if has_fg else None
    o_ref, tail_ref, h_ref = rest[-3:]
    j = pl.program_id(1)

    @pl.when(j == 0)
    def _():
        tail_ref[...] = jnp.zeros_like(tail_ref)
        h_ref[...] = jnp.zeros_like(h_ref)

    x = x_ref[...]
    xb = _rms(x, g_ref[...]).astype(BF16)
    u = jnp.dot(xb, win_u_ref[...], preferred_element_type=F32)
    z = jnp.dot(xb, win_z_ref[...], preferred_element_type=F32)
    tm, width = u.shape
    ext = jnp.concatenate([tail_ref[...], u], axis=0)
    tail_ref[...] = u[tm - 8:, :]
    uc = cb_ref[...] + cw_ref[CONV_WIDTH - 1:CONV_WIDTH, :] * u
    for s in range(1, CONV_WIDTH):
        uc = uc + cw_ref[CONV_WIDTH - 1 - s:CONV_WIDTH - s, :] * ext[8 - s:8 - s + tm, :]
    ucb = uc.astype(BF16)
    gr, gi = [], []
    for c in range(width // half):
        blk = ucb[:, c * half:(c + 1) * half]
        gr.append(jnp.dot(blk, wa_ref[c], preferred_element_type=F32))
        gi.append(jnp.dot(blk, wx_ref[c], preferred_element_type=F32))
    rg = jax.nn.sigmoid(jnp.concatenate(gr, axis=1) + ba_ref[...])
    ig = jax.nn.sigmoid(jnp.concatenate(gi, axis=1) + bx_ref[...])
    log_a = -LRU_C * rg * sp_ref[...]
    a = jnp.exp(log_a)
    bv = jnp.sqrt(1.0 - jnp.exp(2.0 * log_a)) * (ig * uc)
    row = lax.broadcasted_iota(jnp.int32, (tm, width), 0)
    sh = 1
    while sh < tm:
        a_s = jnp.where(row >= sh, pltpu.roll(a, sh, 0), 1.0)
        b_s = jnp.where(row >= sh, pltpu.roll(bv, sh, 0), 0.0)
        bv = a * b_s + bv
        a = a * a_s
        sh *= 2
    h = bv + a * h_ref[...]
    h_ref[...] = h[tm - 1:tm, :]
    gated = (h * _silu(z)).astype(BF16)
    y = x + jnp.dot(gated, wout_ref[...], preferred_element_type=F32)
    o_ref[...] = _rms(y, fg_ref[...]) if has_fg else y


def _block_diag(w, group):
    nb, n, _ = w.shape
    w = w.reshape(nb // group, group, n, n)
    eye = jnp.eye(group, dtype=w.dtype)
    return jnp.einsum('cgij,gh->cgihj', w, eye).reshape(nb // group, group * n, group * n)


def _lru_layer(x2d, g, w_in, conv_w, conv_b, ga_w, ga_b, gx_w, gx_b, lam, w_out, final_g, batch, seq):
    n, d = x2d.shape
    width = w_in.shape[1] // 2
    blk = ga_w.shape[1]
    group = LANES // math.gcd(blk, LANES)
    group = min(group, ga_w.shape[0])
    half = group * blk
    nsup = width // half
    tm = min(LRU_TILE, seq)
    tiles = seq // tm
    row = lambda i, j: (i * tiles + j, 0)
    fixed = lambda i, j: (0, 0)
    fixed3 = lambda i, j: (0, 0, 0)
    vec = lambda a: a.astype(F32).reshape(1, -1)
    softplus_neg_lam = jax.nn.softplus(-lam.astype(F32))
    has_fg = final_g is not None
    in_specs = [pl.BlockSpec((tm, d), row), pl.BlockSpec((1, d), fixed),
                pl.BlockSpec((d, width), fixed), pl.BlockSpec((d, width), fixed),
                pl.BlockSpec((CONV_WIDTH, width), fixed), pl.BlockSpec((1, width), fixed),
                pl.BlockSpec((nsup, half, half), fixed3), pl.BlockSpec((1, width), fixed),
                pl.BlockSpec((nsup, half, half), fixed3), pl.BlockSpec((1, width), fixed),
                pl.BlockSpec((1, width), fixed), pl.BlockSpec((width, d), fixed)]
    args = [x2d, g.reshape(1, d), w_in[:, :width].astype(BF16), w_in[:, width:].astype(BF16),
            conv_w.astype(F32), vec(conv_b), _block_diag(ga_w, group).astype(BF16), vec(ga_b),
            _block_diag(gx_w, group).astype(BF16), vec(gx_b), vec(softplus_neg_lam),
            w_out.astype(BF16)]
    if has_fg:
        in_specs.append(pl.BlockSpec((1, d), fixed))
        args.append(final_g.reshape(1, d))
    kern = functools.partial(_lru_kernel, half=half, has_fg=has_fg)
    return pl.pallas_call(
        kern, grid=(batch, tiles), in_specs=in_specs, out_specs=pl.BlockSpec((tm, d), row),
        out_shape=jax.ShapeDtypeStruct((n, d), F32),
        scratch_shapes=[pltpu.VMEM((8, width), F32), pltpu.VMEM((1, width), F32)],
        compiler_params=_cparams("parallel", "arbitrary"), name="rglru_layer",
    )(*args)


def kernel(x, t5_table, norm_g, final_g, a_w_in, a_sinks, a_w_out, b_mu, b_w_in, b_w0, b_w1, b_w2, b_a0, b_a1, b_a2, b_k_k, b_k_a, b_r_k, b_lnx_w, b_lnx_b, b_w_out, c_w_in, c_cmp_pos_k, c_cmp_k_w1, c_cmp_k_w2, c_cmp_pos_v, c_cmp_v_w1, c_cmp_v_w2, c_w_out, d_w_in, d_conv_w, d_conv_b, d_gate_a_w, d_gate_a_b, d_gate_x_w, d_gate_x_b, d_lambda, d_w_out):
    batch, seq, d = x.shape
    depth = norm_g.shape[0]
    h = x.reshape(batch * seq, d)
    for layer in range(depth):
        m, j = layer % 4, layer // 4
        g = norm_g[layer]
        fg = final_g if layer == depth - 1 else None
        if m == 0:
            h = _swa_layer(h, g, a_w_in[j], a_sinks[j], a_w_out[j], t5_table, fg, batch, seq)
        elif m == 1:
            h = _rwkv_layer(h, g, b_mu[j], b_w_in[j], b_w0[j], b_w1[j], b_w2[j], b_a0[j], b_a1[j],
                            b_a2[j], b_k_k[j], b_k_a[j], b_r_k[j], b_lnx_w[j], b_lnx_b[j],
                            b_w_out[j], fg, batch, seq)
        elif m == 2:
            h = _nsa_layer(h, g, c_w_in[j], c_cmp_pos_k[j], c_cmp_k_w1[j], c_cmp_k_w2[j],
                           c_cmp_pos_v[j], c_cmp_v_w1[j], c_cmp_v_w2[j], c_w_out[j], t5_table,
                           fg, batch, seq)
        else:
            h = _lru_layer(h, g, d_w_in[j], d_conv_w[j], d_conv_b[j], d_gate_a_w[j], d_gate_a_b[j],
                           d_gate_x_w[j], d_gate_x_b[j], d_lambda[j], d_w_out[j], fg, batch, seq)
    return h.reshape(batch, seq, d)
```

```python
import functools
import math

import jax
import jax.numpy as jnp
from jax import lax
from jax.experimental import pallas as pl
from jax.experimental.pallas import tpu as pltpu

F32 = jnp.float32
BF16 = jnp.bfloat16

EPS = 1e-6
NEG = -1e30
BIG = 1e30
T5_BUCKETS = 32
T5_MAX_DIST = 128
ATTN_HEADS = 16
HEAD_DIM = 64
KV_HEADS = 4
GQA = ATTN_HEADS // KV_HEADS
Q_BLOCK = 128
SWA_WINDOW = 128
RWKV_HEAD = 64
RWKV_GN_EPS = 64e-5
NSA_CMP_LEN = 32
NSA_CMP_STRIDE = 16
NSA_SEL_LEN = 64
NSA_TOPK = 16
NSA_WINDOW = 512
LRU_BLOCKS = 16
LRU_C = 8.0
CONV_WIDTH = 4

LANES = 128
SUBLANES = 8
VMEM_LIMIT = 56 * 1024 * 1024
ROW_TILE = 256
RWKV_CHUNK = 64
RWKV_BLOCK = 256
RWKV_LANES = 256
RWKV_SCORE_PASSES = 1
RWKV_VALUE_PASSES = 1
RWKV_INVERSE_PASSES = 1
RWKV_STATE_PASSES = 3
NSA_FAR_KEYS = 512
LRU_TILE = 256


def _cparams(*sem):
    return pltpu.CompilerParams(dimension_semantics=sem, vmem_limit_bytes=VMEM_LIMIT)


def _bdot(a, b):
    return jnp.dot(a.astype(BF16), b.astype(BF16), preferred_element_type=F32)


def _bdot_nt(a, b):
    return lax.dot_general(a.astype(BF16), b.astype(BF16), (((1,), (1,)), ((), ())),
                           preferred_element_type=F32)


def _bdot_tn(a, b):
    return lax.dot_general(a.astype(BF16), b.astype(BF16), (((0,), (0,)), ((), ())),
                           preferred_element_type=F32)


def _split2(x):
    hi = x.astype(BF16)
    lo = (x - hi.astype(F32)).astype(BF16)
    return hi, lo


def _dot3(a, b, dot=_bdot):
    ah, al = _split2(a)
    bh, bl = _split2(b)
    return dot(ah, bh) + (dot(ah, bl) + dot(al, bh))


def _dot_exact_rhs(a, b01, dot=_bdot):
    ah = a.astype(BF16)
    r1 = a - ah.astype(F32)
    am = r1.astype(BF16)
    al = (r1 - am.astype(F32)).astype(BF16)
    return dot(ah, b01) + (dot(am, b01) + dot(al, b01))


def _rms(x, g):
    return x * lax.rsqrt(jnp.mean(x * x, axis=-1, keepdims=True) + EPS) * g


def _silu(z):
    return z * jax.nn.sigmoid(z)


def _gate_column(gates, c):
    col = lax.broadcasted_iota(jnp.int32, gates.shape, 1)
    return jax.nn.sigmoid(jnp.sum(jnp.where(col == c, gates, 0.0), axis=-1, keepdims=True))


def _norm_proj_kernel(x_ref, g_ref, *refs, n_out, head_major):
    w_refs, o_refs = refs[:n_out], refs[n_out:]
    xb = _rms(x_ref[...], g_ref[...]).astype(BF16)
    for w_ref, o_ref, hm in zip(w_refs, o_refs, head_major):
        width = w_ref.shape[1]
        for c0 in range(0, width, 512):
            cw = min(512, width - c0)
            acc = jnp.dot(xb, w_ref[:, c0:c0 + cw], preferred_element_type=F32)
            if hm:
                for j in range(cw // HEAD_DIM):
                    o_ref[0, (c0 // HEAD_DIM) + j] = acc[:, j * HEAD_DIM:(j + 1) * HEAD_DIM].astype(o_ref.dtype)
            else:
                o_ref[:, c0:c0 + cw] = acc.astype(o_ref.dtype)


def _norm_proj(x2d, g, weights, head_major, dtypes, batch, seq):
    n, d = x2d.shape
    tm = min(ROW_TILE, seq)
    tiles_per_seq = seq // tm
    in_specs = [pl.BlockSpec((tm, d), lambda i: (i, 0)), pl.BlockSpec((1, d), lambda i: (0, 0))]
    out_specs, out_shapes = [], []
    for w, hm, dt in zip(weights, head_major, dtypes):
        width = w.shape[1]
        in_specs.append(pl.BlockSpec((d, width), lambda i: (0, 0)))
        if hm:
            nh = width // HEAD_DIM
            out_shapes.append(jax.ShapeDtypeStruct((batch, nh, seq, HEAD_DIM), dt))
            out_specs.append(pl.BlockSpec((1, nh, tm, HEAD_DIM),
                                          lambda i: (i // tiles_per_seq, 0, i % tiles_per_seq, 0)))
        else:
            out_shapes.append(jax.ShapeDtypeStruct((n, width), dt))
            out_specs.append(pl.BlockSpec((tm, width), lambda i: (i, 0)))
    kern = functools.partial(_norm_proj_kernel, n_out=len(weights), head_major=tuple(head_major))
    return pl.pallas_call(
        kern, grid=(n // tm,), in_specs=in_specs, out_specs=out_specs, out_shape=out_shapes,
        compiler_params=_cparams("parallel"), name="norm_proj",
    )(x2d, g.reshape(1, d), *[w.astype(BF16) for w in weights])


def _out_proj_kernel(*refs, n_a, has_z, has_g):
    x_ref = refs[0]
    a_refs = refs[1:1 + n_a]
    pos = 1 + n_a
    z_ref = refs[pos] if has_z else None
    pos += int(has_z)
    w_ref = refs[pos]
    pos += 1
    g_ref = refs[pos] if has_g else None
    o_ref = refs[-1]
    a = a_refs[0][...].astype(F32)
    for r in a_refs[1:]:
        a = a + r[...].astype(F32)
    if has_z:
        a = a * _silu(z_ref[...])
    y = x_ref[...] + jnp.dot(a.astype(BF16), w_ref[...], preferred_element_type=F32)
    if has_g:
        y = _rms(y, g_ref[...])
    o_ref[...] = y


def _out_proj(x2d, a_list, z, w, final_g=None):
    n, d = x2d.shape
    c = w.shape[0]
    tm = min(ROW_TILE, n)
    row = lambda i: (i, 0)
    fixed = lambda i: (0, 0)
    in_specs = [pl.BlockSpec((tm, d), row)] + [pl.BlockSpec((tm, c), row) for _ in a_list]
    args = [x2d] + list(a_list)
    if z is not None:
        in_specs.append(pl.BlockSpec((tm, c), row))
        args.append(z)
    in_specs.append(pl.BlockSpec((c, d), fixed))
    args.append(w.astype(BF16))
    if final_g is not None:
        in_specs.append(pl.BlockSpec((1, d), fixed))
        args.append(final_g.reshape(1, d))
    kern = functools.partial(_out_proj_kernel, n_a=len(a_list), has_z=z is not None,
                             has_g=final_g is not None)
    return pl.pallas_call(
        kern, grid=(n // tm,), in_specs=in_specs, out_specs=pl.BlockSpec((tm, d), row),
        out_shape=jax.ShapeDtypeStruct((n, d), F32), compiler_params=_cparams("parallel"),
        name="out_proj",
    )(*args)


def _t5_bucket(dist):
    max_exact = T5_BUCKETS // 2
    d = jnp.maximum(dist, 0)
    df = jnp.maximum(d, 1).astype(F32)
    large = max_exact + (jnp.log(df / max_exact) / math.log(T5_MAX_DIST / max_exact)
                         * (T5_BUCKETS - max_exact)).astype(jnp.int32)
    large = jnp.minimum(large, T5_BUCKETS - 1)
    return jnp.where(d < max_exact, d, large)


def _band_bias(t5_table, nprev):
    kc = (nprev + 1) * Q_BLOCK
    period = kc + Q_BLOCK
    j = jnp.arange(period)
    j = jnp.where(j >= kc, j - period, j)
    vec = jnp.take(t5_table, _t5_bucket(nprev * Q_BLOCK - j), axis=0).astype(F32).T
    heads = vec.shape[0]
    flat = jnp.tile(vec, (1, Q_BLOCK))[:, :Q_BLOCK * (period - 1)]
    return flat.reshape(heads, Q_BLOCK, period - 1)[:, :, :kc]


def _band_attn_kernel(*refs, nprev, window, has_sink, gate_col):
    q_ref, k_ref, v_ref, bias_ref = refs[:4]
    pos = 4
    sink_ref = refs[pos] if has_sink else None
    pos += int(has_sink)
    gate_ref = refs[pos] if gate_col is not None else None
    o_ref = refs[-1]
    n = pl.program_id(1)
    tq = Q_BLOCK
    kc = (nprev + 1) * tq
    qi = lax.broadcasted_iota(jnp.int32, (tq, kc), 0)
    kj = lax.broadcasted_iota(jnp.int32, (tq, kc), 1)
    dist = nprev * tq + qi - kj
    kpos = (n - nprev) * tq + kj
    valid = ((dist >= 0) & (dist < window) & (kpos >= 0))[None]
    starts = [pl.multiple_of(jnp.maximum(n - nprev + j, 0) * tq, tq) for j in range(nprev + 1)]
    groups = range(KV_HEADS)
    scores = []
    for g in groups:
        kg = jnp.concatenate([k_ref[0, g, pl.ds(s, tq), :] for s in starts], axis=0)
        qg = q_ref[0, GQA * g:GQA * (g + 1)].reshape(GQA * tq, HEAD_DIM) * (HEAD_DIM ** -0.5)
        scores.append(_bdot_nt(qg, kg))
    probs, denoms = [], []
    for g in groups:
        s = scores[g].reshape(GQA, tq, kc) + bias_ref[GQA * g:GQA * (g + 1)]
        s = jnp.where(valid, s, NEG)
        m = jnp.max(s, axis=-1, keepdims=True)
        if has_sink:
            sk = sink_ref[GQA * g:GQA * (g + 1)]
            m = jnp.maximum(m, sk)
        p = jnp.exp(s - m)
        l = jnp.sum(p, axis=-1, keepdims=True)
        if has_sink:
            l = l + jnp.exp(sk - m)
        probs.append(p.reshape(GQA * tq, kc).astype(BF16))
        denoms.append(l)
    pvs = []
    for g in groups:
        vg = jnp.concatenate([v_ref[0, g, pl.ds(s, tq), :] for s in starts], axis=0)
        pvs.append(jnp.dot(probs[g], vg, preferred_element_type=F32))
    for g in groups:
        o = pvs[g].reshape(GQA, tq, HEAD_DIM) / denoms[g]
        outs = []
        for r in range(GQA):
            oh = o[r]
            if gate_col is not None:
                c = gate_col + GQA * g + r
                oh = oh * jax.nn.sigmoid(gate_ref[0, :, c:c + 1])
            outs.append(oh)
        o_ref[0, :, g * GQA * HEAD_DIM:(g + 1) * GQA * HEAD_DIM] = jnp.concatenate(outs, axis=-1)


def _band_attn(q, k, v, bias, window, sinks=None, gates=None, gate_col=None):
    b, _, t, _ = q.shape
    nprev = -(-window // Q_BLOCK)
    kc = (nprev + 1) * Q_BLOCK
    in_specs = [
        pl.BlockSpec((1, ATTN_HEADS, Q_BLOCK, HEAD_DIM), lambda i, n: (i, 0, n, 0)),
        pl.BlockSpec((1, KV_HEADS, t, HEAD_DIM), lambda i, n: (i, 0, 0, 0)),
        pl.BlockSpec((1, KV_HEADS, t, HEAD_DIM), lambda i, n: (i, 0, 0, 0)),
        pl.BlockSpec((ATTN_HEADS, Q_BLOCK, kc), lambda i, n: (0, 0, 0)),
    ]
    args = [q, k, v, bias]
    if sinks is not None:
        in_specs.append(pl.BlockSpec((ATTN_HEADS, 1, 1), lambda i, n: (0, 0, 0)))
        args.append(sinks.astype(F32).reshape(ATTN_HEADS, 1, 1))
    if gates is not None:
        in_specs.append(pl.BlockSpec((1, Q_BLOCK, LANES), lambda i, n: (i, n, 0)))
        args.append(gates)
    kern = functools.partial(_band_attn_kernel, nprev=nprev, window=window,
                             has_sink=sinks is not None, gate_col=gate_col)
    d = ATTN_HEADS * HEAD_DIM
    return pl.pallas_call(
        kern, grid=(b, t // Q_BLOCK), in_specs=in_specs,
        out_specs=pl.BlockSpec((1, Q_BLOCK, d), lambda i, n: (i, n, 0)),
        out_shape=jax.ShapeDtypeStruct((b, t, d), F32),
        compiler_params=_cparams("parallel", "parallel"), name="band_attn",
    )(*args)


def _swa_layer(x2d, g, w_in, sinks, w_out, t5_table, final_g, batch, seq):
    nq, nkv = ATTN_HEADS * HEAD_DIM, KV_HEADS * HEAD_DIM
    ws = [w_in[:, :nq], w_in[:, nq:nq + nkv], w_in[:, nq + nkv:nq + 2 * nkv], w_in[:, nq + 2 * nkv:]]
    q, k, v, z = _norm_proj(x2d, g, ws, [True, True, True, False], [BF16, BF16, BF16, F32], batch, seq)
    bias = _band_bias(t5_table, -(-SWA_WINDOW // Q_BLOCK))
    o = _band_attn(q, k, v, bias, SWA_WINDOW, sinks=sinks)
    return _out_proj(x2d, [o.reshape(x2d.shape)], z, w_out, final_g)


def _rwkv_pre_kernel(x_ref, xp_ref, g_ref, mu_ref, wr_ref, wk_ref, wv_ref, wz_ref,
                     w0_ref, w1_ref, w2_ref, a0_ref, a1_ref, a2_ref,
                     r_ref, k_ref, v_ref, z_ref, lw_ref, a_ref, *, tiles_per_seq):
    i = pl.program_id(0)
    g = g_ref[...]
    xn = _rms(x_ref[...], g)
    prev = _rms(xp_ref[...], g)[7:8]
    prev = jnp.where(i % tiles_per_seq == 0, 0.0, prev)
    row = lax.broadcasted_iota(jnp.int32, xn.shape, 0)
    xprev = jnp.where(row == 0, prev, pltpu.roll(xn, 1, 0))
    xx = xprev - xn
    lerp = lambda s: xn + xx * mu_ref[s:s + 1, :]
    r_ref[...] = _bdot(lerp(0), wr_ref[...])
    k_ref[...] = _bdot(lerp(1), wk_ref[...])
    v_ref[...] = _bdot(lerp(2), wv_ref[...])
    z_ref[...] = _bdot(lerp(3), wz_ref[...])
    wl = w0_ref[...] + _bdot(jnp.tanh(_bdot(lerp(4), w1_ref[...])), w2_ref[...])
    sp = jnp.maximum(-wl, 0.0) + jnp.log1p(jnp.exp(-jnp.abs(wl)))
    lw_ref[...] = -jnp.exp(-sp - 0.5)
    al = a0_ref[...] + _bdot(_bdot(lerp(5), a1_ref[...]), a2_ref[...])
    a_ref[...] = jax.nn.sigmoid(al)


def _rwkv_pre(x2d, g, mu, w_in, w0, w1, w2, a0, a1, a2, seq):
    n, d = x2d.shape
    c = w0.shape[0]
    tm = min(ROW_TILE, seq)
    tiles_per_seq = seq // tm
    row = lambda i: (i, 0)
    fixed = lambda i: (0, 0)
    ws = [w_in[:, s * c:(s + 1) * c].astype(BF16) for s in range(4)]
    in_specs = [
        pl.BlockSpec((tm, d), row),
        pl.BlockSpec((8, d), lambda i: (jnp.maximum(i * (tm // 8) - 1, 0), 0)),
        pl.BlockSpec((1, d), fixed), pl.BlockSpec((6, d), fixed),
    ] + [pl.BlockSpec((d, c), fixed)] * 4 + [
        pl.BlockSpec((1, c), fixed), pl.BlockSpec(w1.shape, fixed), pl.BlockSpec(w2.shape, fixed),
        pl.BlockSpec((1, c), fixed), pl.BlockSpec(a1.shape, fixed), pl.BlockSpec(a2.shape, fixed),
    ]
    out = jax.ShapeDtypeStruct((n, c), F32)
    kern = functools.partial(_rwkv_pre_kernel, tiles_per_seq=tiles_per_seq)
    return pl.pallas_call(
        kern, grid=(n // tm,), in_specs=in_specs, out_specs=[pl.BlockSpec((tm, c), row)] * 6,
        out_shape=[out] * 6, compiler_params=_cparams("parallel"), name="rwkv_pre",
    )(x2d, x2d, g.reshape(1, d), mu, *ws, w0.reshape(1, c), w1.astype(BF16), w2.astype(BF16),
      a0.reshape(1, c), a1.astype(BF16), a2.astype(BF16))


def _dot3_many(a_list, b_list, dot=_bdot, passes=3):
    if passes == 1:
        return [dot(a, b) for a, b in zip(a_list, b_list)]
    sa = [_split2(a) for a in a_list]
    sb = [_split2(b) for b in b_list]
    hh = [dot(x[0], y[0]) for x, y in zip(sa, sb)]
    hl = [dot(x[0], y[1]) for x, y in zip(sa, sb)]
    lh = [dot(x[1], y[0]) for x, y in zip(sa, sb)]
    return [p + (q + r) for p, q, r in zip(hh, hl, lh)]


def _rwkv_chunk_ops(rs, kps, vs, aas, bbs, lws, tri_incl, lower_strict, lower_incl, eye, blockdiag):
    L = RWKV_CHUNK
    n = len(rs)
    n_ch = rs[0].shape[1]
    pieces = []
    for lw in lws:
        hi = lw.astype(BF16)
        r1 = lw - hi.astype(F32)
        mid = r1.astype(BF16)
        pieces.append((hi, mid, (r1 - mid.astype(F32)).astype(BF16)))
    cum_p = [[jnp.dot(tri_incl, p[i], preferred_element_type=F32) for p in pieces] for i in range(3)]
    cums = [a + (b + c) for a, b, c in zip(*cum_p)]
    ats, rts, bks, bhs, khs, wls = [], [], [], [], [], []
    for r, kp, aa, bb, lw, cum in zip(rs, kps, aas, bbs, lws, cums):
        w_inv = jnp.exp(-cum)
        cum_last = cum[L - 1:L, :]
        w_tail = jnp.exp(cum_last - cum)
        ats.append(aa * jnp.exp(cum - lw))
        rts.append(r * jnp.exp(cum))
        bks.append(jnp.concatenate([bb * w_inv, kp * w_inv], axis=0))
        bhs.append(bb * w_tail)
        khs.append(kp * w_tail)
        wls.append(jnp.exp(cum_last))
    ars = [jnp.concatenate([at, rt], axis=0) for at, rt in zip(ats, rts)]
    As = _dot3_many(ars, bks, dot=_bdot_nt, passes=RWKV_SCORE_PASSES)
    a_ab = [jnp.where(lower_strict, A[:L, :L], 0.0) for A in As]
    a_rb = [jnp.where(lower_incl, A[L:, :L], 0.0) for A in As]
    a_k = [jnp.concatenate([jnp.where(lower_strict, A[:L, L:], 0.0),
                            jnp.where(lower_incl, A[L:, L:], 0.0)], axis=0) for A in As]
    dblk = [jnp.where(blockdiag, x, 0.0) for x in a_ab]
    akv = _dot3_many(a_k, vs, passes=RWKV_VALUE_PASSES)
    inv = functools.partial(_dot3_many, passes=RWKV_INVERSE_PASSES)
    d2 = inv(dblk, dblk)
    res = inv([eye + d for d in dblk] + d2, [eye + d for d in d2] + d2)
    s4, d4 = res[:n], res[n:]
    res = inv(d4 + d4, s4 + d4)
    s8 = [s + x for s, x in zip(s4, res[:n])]
    d8 = res[n:]
    tdiag = [s + x for s, x in zip(s8, inv(d8, s8))]
    rhs = [jnp.concatenate([at, x[:L]], axis=1) for at, x in zip(ats, akv)]
    res = inv(tdiag + tdiag, rhs + [x - d for x, d in zip(a_ab, dblk)])
    xt, nt = res[:n], res[n:]
    res = inv(nt + nt, xt + nt)
    u = [x + y for x, y in zip(xt, res[:n])]
    nt2 = res[n:]
    pq = [x + y for x, y in zip(u, inv(nt2, u))]
    res = _dot3_many(a_rb, pq, passes=RWKV_VALUE_PASSES)
    gh = [jnp.concatenate([rt, x[L:]], axis=1) + y for rt, x, y in zip(rts, akv, res)]
    res = _dot3_many(bhs + khs, pq + vs, dot=_bdot_tn, passes=RWKV_VALUE_PASSES)
    out = []
    for i in range(n):
        mc = res[i]
        m_op = mc[:, :n_ch] + eye * wls[i]
        c_op = mc[:, n_ch:] + res[n + i]
        out.append((gh[i][:, :n_ch], gh[i][:, n_ch:], m_op, c_op))
    return out


def _rwkv_scan_kernel(r_ref, k_ref, v_ref, z_ref, lw_ref, a_ref, kk_ref, ka_ref, rk_ref,
                      lnw_ref, lnb_ref, o_ref, state_ref):
    tb = pl.program_id(2)

    @pl.when(tb == 0)
    def _():
        state_ref[...] = jnp.zeros_like(state_ref)

    L = RWKV_CHUNK
    N = RWKV_HEAD
    ri = lax.broadcasted_iota(jnp.int32, (L, L), 0)
    ci = lax.broadcasted_iota(jnp.int32, (L, L), 1)
    lower_strict = ri > ci
    lower_incl = ri >= ci
    tri_incl = jnp.where(lower_incl, 1.0, 0.0).astype(BF16)
    eye = jnp.where(ri == ci, 1.0, 0.0).astype(F32)
    blockdiag = (ri // 16) == (ci // 16)
    n_heads = r_ref.shape[2] // N
    n_chunks = r_ref.shape[1] // L
    probs = [(hh, c) for c in range(n_chunks) for hh in range(n_heads)]
    rs, kps, vs, aas, bbs, lws = [], [], [], [], [], []
    for hh, c in probs:
        sl = slice(hh * N, (hh + 1) * N)
        rows = slice(c * L, (c + 1) * L)
        k = k_ref[0, rows, sl]
        a = a_ref[0, rows, sl]
        kk = k * kk_ref[:, sl]
        kk = kk / jnp.maximum(jnp.sqrt(jnp.sum(kk * kk, axis=-1, keepdims=True)), 1e-12)
        rs.append(r_ref[0, rows, sl])
        kps.append(k * (1.0 + (a - 1.0) * ka_ref[:, sl]))
        vs.append(v_ref[0, rows, sl])
        aas.append(-kk)
        bbs.append(kk * a)
        lws.append(lw_ref[0, rows, sl])
    ops = _rwkv_chunk_ops(rs, kps, vs, aas, bbs, lws, tri_incl, lower_strict, lower_incl, eye,
                          blockdiag)
    states = [state_ref[hh] for hh in range(n_heads)]
    ys = {}
    for c in range(n_chunks):
        idx = [c * n_heads + hh for hh in range(n_heads)]
        upd = _dot3_many([jnp.concatenate([ops[i][0], ops[i][2]], axis=0) for i in idx], states,
                         passes=RWKV_STATE_PASSES)
        for hh, i in enumerate(idx):
            ys[(hh, c)] = upd[hh][:L] + ops[i][1]
            states[hh] = upd[hh][L:] + ops[i][3]
    for hh in range(n_heads):
        state_ref[hh] = states[hh]
    cols = []
    for hh in range(n_heads):
        sl = slice(hh * N, (hh + 1) * N)
        parts = []
        for c in range(n_chunks):
            rows = slice(c * L, (c + 1) * L)
            i = c * n_heads + hh
            y = ys[(hh, c)]
            mean = jnp.mean(y, axis=-1, keepdims=True)
            yc = y - mean
            var = jnp.mean(yc * yc, axis=-1, keepdims=True)
            yn = yc * lax.rsqrt(var + RWKV_GN_EPS) * lnw_ref[:, sl] + lnb_ref[:, sl]
            bonus = jnp.sum(rs[i] * kps[i] * rk_ref[:, sl], axis=-1, keepdims=True) * vs[i]
            parts.append((yn + bonus) * _silu(z_ref[0, rows, sl]))
        cols.append(jnp.concatenate(parts, axis=0))
    o_ref[0] = jnp.concatenate(cols, axis=1)


def _rwkv_scan(r, k, v, z, lw, a, k_k, k_a, r_k, lnx_w, lnx_b):
    b, t, c = r.shape
    tb = min(RWKV_BLOCK, t)
    seq_spec = pl.BlockSpec((1, tb, RWKV_LANES), lambda i, p, j: (i, j, p))
    par_spec = pl.BlockSpec((1, RWKV_LANES), lambda i, p, j: (0, p))
    params = [x.reshape(1, c).astype(F32) for x in (k_k, k_a, r_k, lnx_w, lnx_b)]
    return pl.pallas_call(
        _rwkv_scan_kernel, grid=(b, c // RWKV_LANES, t // tb),
        in_specs=[seq_spec] * 6 + [par_spec] * 5, out_specs=seq_spec,
        out_shape=jax.ShapeDtypeStruct((b, t, c), F32),
        scratch_shapes=[pltpu.VMEM((RWKV_LANES // RWKV_HEAD, RWKV_HEAD, RWKV_HEAD), F32)],
        compiler_params=_cparams("parallel", "parallel", "arbitrary"), name="rwkv_scan",
    )(r, k, v, z, lw, a, *params)


def _rwkv_layer(x2d, g, mu, w_in, w0, w1, w2, a0, a1, a2, k_k, k_a, r_k, lnx_w, lnx_b, w_out,
                final_g, batch, seq):
    c = w0.shape[0]
    r, k, v, z, lw, a = _rwkv_pre(x2d, g, mu, w_in, w0, w1, w2, a0, a1, a2, seq)
    sh = (batch, seq, c)
    y = _rwkv_scan(r.reshape(sh), k.reshape(sh), v.reshape(sh), z.reshape(sh), lw.reshape(sh),
                   a.reshape(sh), k_k, k_a, r_k, lnx_w, lnx_b)
    return _out_proj(x2d, [y.reshape(x2d.shape[0], c)], None, w_out, final_g)


def _nsa_compress_kernel(u_ref, posa_ref, posb_ref, w1a_ref, w1b_ref, w2_ref, o_ref):
    u = u_ref[0, 0]
    ha = _bdot(u + posa_ref[...], w1a_ref[...])
    hb = _bdot(u + posb_ref[...], w1b_ref[...])
    h = ha + pltpu.roll(hb, hb.shape[0] - 1, 0)
    o_ref[0, 0] = _bdot(_silu(h), w2_ref[...])


def _nsa_compress(t_hm, pos, w1, w2):
    b, g, t, dh = t_hm.shape
    nch = t // NSA_CMP_STRIDE
    half = NSA_CMP_STRIDE * dh
    u = t_hm.reshape(b, g, nch, half)
    posf = pos.astype(F32).reshape(2, 1, half)
    hid = w1.shape[1]
    fixed = lambda i, j: (0, 0)
    return pl.pallas_call(
        _nsa_compress_kernel, grid=(b, g),
        in_specs=[pl.BlockSpec((1, 1, nch, half), lambda i, j: (i, j, 0, 0)),
                  pl.BlockSpec((1, half), fixed), pl.BlockSpec((1, half), fixed),
                  pl.BlockSpec((half, hid), fixed), pl.BlockSpec((half, hid), fixed),
                  pl.BlockSpec((hid, dh), fixed)],
        out_specs=pl.BlockSpec((1, 1, nch, dh), lambda i, j: (i, j, 0, 0)),
        out_shape=jax.ShapeDtypeStruct((b, g, nch, dh), F32),
        compiler_params=_cparams("parallel", "parallel"), name="nsa_compress",
    )(u, posf[0], posf[1], w1[:half].astype(BF16), w1[half:].astype(BF16), w2.astype(BF16))


def _nsa_cmp_attn_kernel(q_ref, kc_ref, vc_ref, gate_ref, o_ref, sel_ref, *, n_sel):
    g = pl.program_id(1)
    n = pl.program_id(2)
    tq = Q_BLOCK
    ncmp = kc_ref.shape[2]
    q = q_ref[0].reshape(GQA * tq, HEAD_DIM) * (HEAD_DIM ** -0.5)
    s = _bdot_nt(q, kc_ref[0, 0]).reshape(GQA, tq, ncmp)
    tpos = n * tq + lax.broadcasted_iota(jnp.int32, (tq, ncmp), 0)
    cend = lax.broadcasted_iota(jnp.int32, (tq, ncmp), 1) * NSA_CMP_STRIDE + (NSA_CMP_LEN - 1)
    ok = (cend <= tpos)[None]
    s = jnp.where(ok, s, NEG)
    m = jnp.max(s, axis=-1, keepdims=True)
    e = jnp.where(ok, jnp.exp(s - m), 0.0)
    l = jnp.sum(e, axis=-1, keepdims=True)
    p = e / jnp.where(l > 0.0, l, 1.0)
    o = _bdot(p.reshape(GQA * tq, ncmp), vc_ref[0, 0]).reshape(GQA, tq, HEAD_DIM)
    outs = []
    for r in range(GQA):
        outs.append(o[r] * _gate_column(gate_ref[0], GQA * g + r))
    o_ref[0] = jnp.concatenate(outs, axis=-1)

    psum = p[0] + p[1] + p[2] + p[3]
    si = lax.broadcasted_iota(jnp.int32, (n_sel, ncmp), 0)
    ni = lax.broadcasted_iota(jnp.int32, (n_sel, ncmp), 1)
    ratio = NSA_SEL_LEN // NSA_CMP_STRIDE
    overlap = ((ni < ratio * (si + 1)) & (ni * NSA_CMP_STRIDE + NSA_CMP_LEN - 1 >= si * NSA_SEL_LEN))
    overlap = jnp.where(overlap, 1.0, 0.0).astype(BF16)
    imp = _dot_exact_rhs(psum, overlap, dot=lambda a, b: _bdot_nt(b, a))
    blk = lax.broadcasted_iota(jnp.int32, (n_sel, tq), 0)
    tq_pos = n * tq + lax.broadcasted_iota(jnp.int32, (n_sel, tq), 1)
    cur = tq_pos // NSA_SEL_LEN
    forced = (blk == 0) | (blk == cur) | (blk == cur - 1)
    future = blk * NSA_SEL_LEN > tq_pos
    imp = jnp.where(forced, BIG, jnp.where(future, NEG, imp))
    rank = jnp.zeros((n_sel, tq), F32)
    for j in range(n_sel):
        row = imp[j:j + 1, :]
        ahead = (row > imp) | ((row == imp) & (blk > j))
        rank = rank + jnp.where(ahead, 1.0, 0.0)
    sel = jnp.where(rank < float(min(NSA_TOPK, n_sel)), 1.0, 0.0)
    sel_ref[0, 0] = sel


def _nsa_cmp_attn(q, k_cmp, v_cmp, gates):
    b, _, t, _ = q.shape
    ncmp = k_cmp.shape[2]
    n_sel = t // NSA_SEL_LEN
    d = ATTN_HEADS * HEAD_DIM
    kern = functools.partial(_nsa_cmp_attn_kernel, n_sel=n_sel)
    cmp_spec = pl.BlockSpec((1, 1, ncmp, HEAD_DIM), lambda i, g, n: (i, g, 0, 0))
    return pl.pallas_call(
        kern, grid=(b, KV_HEADS, t // Q_BLOCK),
        in_specs=[pl.BlockSpec((1, GQA, Q_BLOCK, HEAD_DIM), lambda i, g, n: (i, g, n, 0)),
                  cmp_spec, cmp_spec,
                  pl.BlockSpec((1, Q_BLOCK, LANES), lambda i, g, n: (i, n, 0))],
        out_specs=[pl.BlockSpec((1, Q_BLOCK, GQA * HEAD_DIM), lambda i, g, n: (i, n, g)),
                   pl.BlockSpec((1, 1, n_sel, Q_BLOCK), lambda i, g, n: (i, g, 0, n))],
        out_shape=[jax.ShapeDtypeStruct((b, t, d), F32),
                   jax.ShapeDtypeStruct((b, KV_HEADS, n_sel, t), F32)],
        compiler_params=_cparams("parallel", "parallel", "parallel"), name="nsa_cmp_attn",
    )(q, k_cmp, v_cmp, gates)


def _nsa_sel_attn_kernel(q_ref, k_ref, vt_ref, selt_ref, bias_ref, far_ref, gate_ref, o_ref):
    g = pl.program_id(1)
    n = pl.program_id(2)
    tq = Q_BLOCK
    bl = NSA_SEL_LEN
    fk = NSA_FAR_KEYS
    q_all = q_ref[0].reshape(GQA * tq, HEAD_DIM) * (HEAD_DIM ** -0.5)

    def block_rows(first_block, count, limit):
        rows = []
        for i in range(count):
            blk = first_block + i
            row = selt_ref[0, 0, pl.ds(jnp.clip(blk, 0, limit - 1), 1), :]
            rows.append(jnp.broadcast_to(jnp.where((blk >= 0) & (blk < limit), row, 0.0), (bl, tq)))
        return jnp.concatenate(rows, axis=0) > 0.5

    prev = pl.multiple_of(jnp.maximum(n - 1, 0) * tq, tq)
    diag = pl.multiple_of(n * tq, tq)
    kt = jnp.concatenate([k_ref[0, 0, pl.ds(prev, tq), :], k_ref[0, 0, pl.ds(diag, tq), :]], axis=0)
    vt = jnp.concatenate([vt_ref[0, 0, :, pl.ds(prev, tq)], vt_ref[0, 0, :, pl.ds(diag, tq)]], axis=1)
    kj = lax.broadcasted_iota(jnp.int32, (2 * tq, tq), 0)
    qi = lax.broadcasted_iota(jnp.int32, (2 * tq, tq), 1)
    n_blocks = (n + 1) * (tq // bl)
    ok = block_rows((n - 1) * (tq // bl), 2 * tq // bl, n_blocks) & (tq + qi - kj >= 0)
    ok = jnp.concatenate([ok] * GQA, axis=1)
    s = jnp.where(ok, _bdot_nt(kt, q_all) + bias_ref[0], NEG)
    m = jnp.max(s, axis=0, keepdims=True)
    p = jnp.exp(s - m).astype(BF16)
    acc = jnp.dot(vt, p, preferred_element_type=F32)
    m = m - far_ref[0]
    far_blocks = jnp.maximum(n - 1, 0) * (tq // bl)

    def body(c, carry):
        m, acc = carry
        start = pl.multiple_of(c * fk, fk)
        ok = block_rows(c * (fk // bl), fk // bl, far_blocks)
        ok = jnp.concatenate([ok] * GQA, axis=1)
        s = jnp.where(ok, _bdot_nt(k_ref[0, 0, pl.ds(start, fk), :], q_all), NEG)
        m_new = jnp.maximum(m, jnp.max(s, axis=0, keepdims=True))
        p = jnp.exp(s - m_new).astype(BF16)
        pv = jnp.dot(vt_ref[0, 0, :, pl.ds(start, fk)], p, preferred_element_type=F32)
        return m_new, jnp.exp(m - m_new) * acc + pv

    n_far = (far_blocks * bl + fk - 1) // fk
    _, acc = lax.fori_loop(0, n_far, body, (m, acc))
    o_t = acc[:HEAD_DIM] / acc[HEAD_DIM:HEAD_DIM + 1]
    outs = []
    for r in range(GQA):
        o = o_t[:, r * tq:(r + 1) * tq].T
        outs.append(o * _gate_column(gate_ref[0], ATTN_HEADS + GQA * g + r))
    o_ref[0] = jnp.concatenate(outs, axis=-1)


def _nsa_sel_attn(q, ks, vs, sel_t, bias_near, far, gates):
    b, _, t, _ = q.shape
    n_sel = t // NSA_SEL_LEN
    d = ATTN_HEADS * HEAD_DIM
    assert t % NSA_FAR_KEYS == 0 or t <= 2 * Q_BLOCK
    vt_ones = jnp.swapaxes(jnp.concatenate([vs, jnp.ones_like(vs)], axis=-1), 2, 3)
    return pl.pallas_call(
        _nsa_sel_attn_kernel, grid=(b, KV_HEADS, t // Q_BLOCK),
        in_specs=[pl.BlockSpec((1, GQA, Q_BLOCK, HEAD_DIM), lambda i, g, n: (i, g, n, 0)),
                  pl.BlockSpec((1, 1, t, HEAD_DIM), lambda i, g, n: (i, g, 0, 0)),
                  pl.BlockSpec((1, 1, 2 * HEAD_DIM, t), lambda i, g, n: (i, g, 0, 0)),
                  pl.BlockSpec((1, 1, n_sel, Q_BLOCK), lambda i, g, n: (i, g, 0, n)),
                  pl.BlockSpec((1, 2 * Q_BLOCK, GQA * Q_BLOCK), lambda i, g, n: (g, 0, 0)),
                  pl.BlockSpec((1, 1, GQA * Q_BLOCK), lambda i, g, n: (g, 0, 0)),
                  pl.BlockSpec((1, Q_BLOCK, LANES), lambda i, g, n: (i, n, 0))],
        out_specs=pl.BlockSpec((1, Q_BLOCK, GQA * HEAD_DIM), lambda i, g, n: (i, n, g)),
        out_shape=jax.ShapeDtypeStruct((b, t, d), F32),
        compiler_params=_cparams("parallel", "parallel", "parallel"), name="nsa_sel_attn",
    )(q, ks, vt_ones, sel_t, bias_near, far, gates)


def _nsa_layer(x2d, g, w_in, pos_k, k_w1, k_w2, pos_v, v_w1, v_w2, w_out, t5_table, final_g,
               batch, seq):
    nq, nkv = ATTN_HEADS * HEAD_DIM, KV_HEADS * HEAD_DIM
    offs = [0, nq] + [nq + nkv * (i + 1) for i in range(6)]
    ws = [w_in[:, offs[i]:offs[i + 1]] for i in range(7)]
    n_gate = 3 * ATTN_HEADS
    wg = jnp.pad(w_in[:, offs[7]:offs[7] + n_gate], ((0, 0), (0, LANES - n_gate)))
    wz = w_in[:, offs[7] + n_gate:]
    q, kc, vc, ks, vs, kw, vw, gates, z = _norm_proj(
        x2d, g, ws + [wg, wz], [True] * 7 + [False, False],
        [BF16, F32, F32, BF16, BF16, BF16, BF16, F32, F32], batch, seq)
    gates = gates.reshape(batch, seq, LANES)
    k_cmp = _nsa_compress(kc, pos_k, k_w1, k_w2)
    v_cmp = _nsa_compress(vc, pos_v, v_w1, v_w2)
    o_cmp, sel_t = _nsa_cmp_attn(q, k_cmp, v_cmp, gates)
    bias_near = _band_bias(t5_table, 1).reshape(KV_HEADS, GQA, Q_BLOCK, 2 * Q_BLOCK)
    bias_near = bias_near.transpose(0, 3, 1, 2).reshape(KV_HEADS, 2 * Q_BLOCK, GQA * Q_BLOCK)
    far = t5_table[T5_BUCKETS - 1].astype(F32).reshape(KV_HEADS, GQA, 1)
    far = jnp.broadcast_to(far, (KV_HEADS, GQA, Q_BLOCK)).reshape(KV_HEADS, 1, GQA * Q_BLOCK)
    o_sel = _nsa_sel_attn(q, ks, vs, sel_t, bias_near, far, gates)
    bias_win = _band_bias(t5_table, -(-NSA_WINDOW // Q_BLOCK))
    o_win = _band_attn(q, kw, vw, bias_win, NSA_WINDOW, gates=gates, gate_col=2 * ATTN_HEADS)
    sh = x2d.shape
    return _out_proj(x2d, [o_cmp.reshape(sh), o_sel.reshape(sh), o_win.reshape(sh)], z, w_out,
                     final_g)


def _lru_kernel(x_ref, g_ref, win_u_ref, win_z_ref, cw_ref, cb_ref, wa_ref, ba_ref, wx_ref, bx_ref,
                sp_ref, wout_ref, *rest, half, has_fg):
    fg_ref = rest[0] if has_fg else None
    o_ref, tail_ref, h_ref = rest[-3:]
    j = pl.program_id(1)

    @pl.when(j == 0)
    def _():
        tail_ref[...] = jnp.zeros_like(tail_ref)
        h_ref[...] = jnp.zeros_like(h_ref)

    x = x_ref[...]
    xb = _rms(x, g_ref[...]).astype(BF16)
    u = jnp.dot(xb, win_u_ref[...], preferred_element_type=F32)
    z = jnp.dot(xb, win_z_ref[...], preferred_element_type=F32)
    tm, width = u.shape
    ext = jnp.concatenate([tail_ref[...], u], axis=0)
    tail_ref[...] = u[tm - 8:, :]
    uc = cb_ref[...] + cw_ref[CONV_WIDTH - 1:CONV_WIDTH, :] * u
    for s in range(1, CONV_WIDTH):
        uc = uc + cw_ref[CONV_WIDTH - 1 - s:CONV_WIDTH - s, :] * ext[8 - s:8 - s + tm, :]
    ucb = uc.astype(BF16)
    gr, gi = [], []
    for c in range(width // half):
        blk = ucb[:, c * half:(c + 1) * half]
        gr.append(jnp.dot(blk, wa_ref[c], preferred_element_type=F32))
        gi.append(jnp.dot(blk, wx_ref[c], preferred_element_type=F32))
    rg = jax.nn.sigmoid(jnp.concatenate(gr, axis=1) + ba_ref[...])
    ig = jax.nn.sigmoid(jnp.concatenate(gi, axis=1) + bx_ref[...])
    log_a = -LRU_C * rg * sp_ref[...]
    a = jnp.exp(log_a)
    bv = jnp.sqrt(1.0 - a * a) * (ig * uc)
    row = lax.broadcasted_iota(jnp.int32, (tm, width), 0) % SUBLANES
    sh = 1
    while sh < SUBLANES:
        a_s = jnp.where(row >= sh, pltpu.roll(a, sh, 0), 1.0)
        b_s = jnp.where(row >= sh, pltpu.roll(bv, sh, 0), 0.0)
        bv = a * b_s + bv
        a = a * a_s
        sh *= 2
    carry = h_ref[...]
    groups = []
    for i in range(tm // SUBLANES):
        rows = slice(i * SUBLANES, (i + 1) * SUBLANES)
        hg = bv[rows] + a[rows] * carry
        carry = hg[SUBLANES - 1:SUBLANES, :]
        groups.append(hg)
    h_ref[...] = carry
    gated = (jnp.concatenate(groups, axis=0) * _silu(z)).astype(BF16)
    y = x + jnp.dot(gated, wout_ref[...], preferred_element_type=F32)
    o_ref[...] = _rms(y, fg_ref[...]) if has_fg else y


def _block_diag(w, group):
    nb, n, _ = w.shape
    w = w.reshape(nb // group, group, n, n)
    eye = jnp.eye(group, dtype=w.dtype)
    return jnp.einsum('cgij,gh->cgihj', w, eye).reshape(nb // group, group * n, group * n)


def _lru_layer(x2d, g, w_in, conv_w, conv_b, ga_w, ga_b, gx_w, gx_b, lam, w_out, final_g, batch, seq):
    n, d = x2d.shape
    width = w_in.shape[1] // 2
    blk = ga_w.shape[1]
    group = LANES // math.gcd(blk, LANES)
    group = min(group, ga_w.shape[0])
    half = group * blk
    nsup = width // half
    tm = min(LRU_TILE, seq)
    tiles = seq // tm
    row = lambda i, j: (i * tiles + j, 0)
    fixed = lambda i, j: (0, 0)
    fixed3 = lambda i, j: (0, 0, 0)
    vec = lambda a: a.astype(F32).reshape(1, -1)
    softplus_neg_lam = jax.nn.softplus(-lam.astype(F32))
    has_fg = final_g is not None
    in_specs = [pl.BlockSpec((tm, d), row), pl.BlockSpec((1, d), fixed),
                pl.BlockSpec((d, width), fixed), pl.BlockSpec((d, width), fixed),
                pl.BlockSpec((CONV_WIDTH, width), fixed), pl.BlockSpec((1, width), fixed),
                pl.BlockSpec((nsup, half, half), fixed3), pl.BlockSpec((1, width), fixed),
                pl.BlockSpec((nsup, half, half), fixed3), pl.BlockSpec((1, width), fixed),
                pl.BlockSpec((1, width), fixed), pl.BlockSpec((width, d), fixed)]
    args = [x2d, g.reshape(1, d), w_in[:, :width].astype(BF16), w_in[:, width:].astype(BF16),
            conv_w.astype(F32), vec(conv_b), _block_diag(ga_w, group).astype(BF16), vec(ga_b),
            _block_diag(gx_w, group).astype(BF16), vec(gx_b), vec(softplus_neg_lam),
            w_out.astype(BF16)]
    if has_fg:
        in_specs.append(pl.BlockSpec((1, d), fixed))
        args.append(final_g.reshape(1, d))
    kern = functools.partial(_lru_kernel, half=half, has_fg=has_fg)
    return pl.pallas_call(
        kern, grid=(batch, tiles), in_specs=in_specs, out_specs=pl.BlockSpec((tm, d), row),
        out_shape=jax.ShapeDtypeStruct((n, d), F32),
        scratch_shapes=[pltpu.VMEM((8, width), F32), pltpu.VMEM((1, width), F32)],
        compiler_params=_cparams("parallel", "arbitrary"), name="rglru_layer",
    )(*args)


def kernel(x, t5_table, norm_g, final_g, a_w_in, a_sinks, a_w_out, b_mu, b_w_in, b_w0, b_w1, b_w2, b_a0, b_a1, b_a2, b_k_k, b_k_a, b_r_k, b_lnx_w, b_lnx_b, b_w_out, c_w_in, c_cmp_pos_k, c_cmp_k_w1, c_cmp_k_w2, c_cmp_pos_v, c_cmp_v_w1, c_cmp_v_w2, c_w_out, d_w_in, d_conv_w, d_conv_b, d_gate_a_w, d_gate_a_b, d_gate_x_w, d_gate_x_b, d_lambda, d_w_out):
    batch, seq, d = x.shape
    depth = norm_g.shape[0]
    h = x.reshape(batch * seq, d)
    for layer in range(depth):
        m, j = layer % 4, layer // 4
        g = norm_g[layer]
        fg = final_g if layer == depth - 1 else None
        if m == 0:
            h = _swa_layer(h, g, a_w_in[j], a_sinks[j], a_w_out[j], t5_table, fg, batch, seq)
        elif m == 1:
            h = _rwkv_layer(h, g, b_mu[j], b_w_in[j], b_w0[j], b_w1[j], b_w2[j], b_a0[j], b_a1[j],
                            b_a2[j], b_k_k[j], b_k_a[j], b_r_k[j], b_lnx_w[j], b_lnx_b[j],
                            b_w_out[j], fg, batch, seq)
        elif m == 2:
            h = _nsa_layer(h, g, c_w_in[j], c_cmp_pos_k[j], c_cmp_k_w1[j], c_cmp_k_w2[j],
                           c_cmp_pos_v[j], c_cmp_v_w1[j], c_cmp_v_w2[j], c_w_out[j], t5_table,
                           fg, batch, seq)
        else:
            h = _lru_layer(h, g, d_w_in[j], d_conv_w[j], d_conv_b[j], d_gate_a_w[j], d_gate_a_b[j],
                           d_gate_x_w[j], d_gate_x_b[j], d_lambda[j], d_w_out[j], fg, batch, seq)
    return h.reshape(batch, seq, d)
```

```python
import functools
import math

import jax
import jax.numpy as jnp
from jax import lax
from jax.experimental import pallas as pl
from jax.experimental.pallas import tpu as pltpu

F32 = jnp.float32
BF16 = jnp.bfloat16

EPS = 1e-6
NEG = -1e30
BIG = 1e30
T5_BUCKETS = 32
T5_MAX_DIST = 128
ATTN_HEADS = 16
HEAD_DIM = 64
KV_HEADS = 4
GQA = ATTN_HEADS // KV_HEADS
Q_BLOCK = 128
SWA_WINDOW = 128
RWKV_HEAD = 64
RWKV_GN_EPS = 64e-5
NSA_CMP_LEN = 32
NSA_CMP_STRIDE = 16
NSA_SEL_LEN = 64
NSA_TOPK = 16
NSA_WINDOW = 512
LRU_BLOCKS = 16
LRU_C = 8.0
CONV_WIDTH = 4

LANES = 128
SUBLANES = 8
VMEM_LIMIT = 56 * 1024 * 1024
ROW_TILE = 256
RWKV_CHUNK = 64
RWKV_BLOCK = 256
RWKV_LANES = 256
RWKV_SCORE_PASSES = 1
RWKV_VALUE_PASSES = 1
RWKV_INVERSE_PASSES = 1
RWKV_STATE_PASSES = 3
NSA_FAR_KEYS = 512
LRU_TILE = 256


def _cparams(*sem):
    return pltpu.CompilerParams(dimension_semantics=sem, vmem_limit_bytes=VMEM_LIMIT)


def _bdot(a, b):
    return jnp.dot(a.astype(BF16), b.astype(BF16), preferred_element_type=F32)


def _bdot_nt(a, b):
    return lax.dot_general(a.astype(BF16), b.astype(BF16), (((1,), (1,)), ((), ())),
                           preferred_element_type=F32)


def _bdot_tn(a, b):
    return lax.dot_general(a.astype(BF16), b.astype(BF16), (((0,), (0,)), ((), ())),
                           preferred_element_type=F32)


def _split2(x):
    hi = x.astype(BF16)
    lo = (x - hi.astype(F32)).astype(BF16)
    return hi, lo


def _dot3(a, b, dot=_bdot):
    ah, al = _split2(a)
    bh, bl = _split2(b)
    return dot(ah, bh) + (dot(ah, bl) + dot(al, bh))


def _dot_exact_rhs(a, b01, dot=_bdot):
    ah = a.astype(BF16)
    r1 = a - ah.astype(F32)
    am = r1.astype(BF16)
    al = (r1 - am.astype(F32)).astype(BF16)
    return dot(ah, b01) + (dot(am, b01) + dot(al, b01))


def _rms(x, g):
    return x * lax.rsqrt(jnp.mean(x * x, axis=-1, keepdims=True) + EPS) * g


def _silu(z):
    return z * jax.nn.sigmoid(z)


def _gate_row(gate_ref, c):
    return jax.nn.sigmoid(gate_ref[0, pl.ds(c, 1), :])


def _norm_proj_kernel(x_ref, g_ref, *refs, n_out, layouts):
    w_refs, o_refs = refs[:n_out], refs[n_out:]
    xb = _rms(x_ref[...], g_ref[...]).astype(BF16)
    for w_ref, o_ref, layout in zip(w_refs, o_refs, layouts):
        width = w_ref.shape[1]
        for c0 in range(0, width, 512):
            cw = min(512, width - c0)
            acc = jnp.dot(xb, w_ref[:, c0:c0 + cw], preferred_element_type=F32)
            if layout == "heads":
                for j in range(cw // HEAD_DIM):
                    o_ref[0, (c0 // HEAD_DIM) + j] = acc[:, j * HEAD_DIM:(j + 1) * HEAD_DIM].astype(o_ref.dtype)
            elif layout == "cols":
                o_ref[0, c0:c0 + cw, :] = acc.T.astype(o_ref.dtype)
            else:
                o_ref[:, c0:c0 + cw] = acc.astype(o_ref.dtype)


def _norm_proj(x2d, g, weights, layouts, dtypes, batch, seq):
    n, d = x2d.shape
    tm = min(ROW_TILE, seq)
    tiles_per_seq = seq // tm
    in_specs = [pl.BlockSpec((tm, d), lambda i: (i, 0)), pl.BlockSpec((1, d), lambda i: (0, 0))]
    out_specs, out_shapes = [], []
    for w, layout, dt in zip(weights, layouts, dtypes):
        width = w.shape[1]
        in_specs.append(pl.BlockSpec((d, width), lambda i: (0, 0)))
        if layout == "heads":
            nh = width // HEAD_DIM
            out_shapes.append(jax.ShapeDtypeStruct((batch, nh, seq, HEAD_DIM), dt))
            out_specs.append(pl.BlockSpec((1, nh, tm, HEAD_DIM),
                                          lambda i: (i // tiles_per_seq, 0, i % tiles_per_seq, 0)))
        elif layout == "cols":
            out_shapes.append(jax.ShapeDtypeStruct((batch, width, seq), dt))
            out_specs.append(pl.BlockSpec((1, width, tm),
                                          lambda i: (i // tiles_per_seq, 0, i % tiles_per_seq)))
        else:
            out_shapes.append(jax.ShapeDtypeStruct((n, width), dt))
            out_specs.append(pl.BlockSpec((tm, width), lambda i: (i, 0)))
    kern = functools.partial(_norm_proj_kernel, n_out=len(weights), layouts=tuple(layouts))
    return pl.pallas_call(
        kern, grid=(n // tm,), in_specs=in_specs, out_specs=out_specs, out_shape=out_shapes,
        compiler_params=_cparams("parallel"), name="norm_proj",
    )(x2d, g.reshape(1, d), *[w.astype(BF16) for w in weights])


def _out_proj_kernel(*refs, n_a, has_z, has_g):
    x_ref = refs[0]
    a_refs = refs[1:1 + n_a]
    pos = 1 + n_a
    z_ref = refs[pos] if has_z else None
    pos += int(has_z)
    w_ref = refs[pos]
    pos += 1
    g_ref = refs[pos] if has_g else None
    o_ref = refs[-1]
    a = a_refs[0][...].astype(F32)
    for r in a_refs[1:]:
        a = a + r[...].astype(F32)
    if has_z:
        a = a * _silu(z_ref[...])
    y = x_ref[...] + jnp.dot(a.astype(BF16), w_ref[...], preferred_element_type=F32)
    if has_g:
        y = _rms(y, g_ref[...])
    o_ref[...] = y


def _out_proj(x2d, a_list, z, w, final_g=None):
    n, d = x2d.shape
    c = w.shape[0]
    tm = min(ROW_TILE, n)
    row = lambda i: (i, 0)
    fixed = lambda i: (0, 0)
    in_specs = [pl.BlockSpec((tm, d), row)] + [pl.BlockSpec((tm, c), row) for _ in a_list]
    args = [x2d] + list(a_list)
    if z is not None:
        in_specs.append(pl.BlockSpec((tm, c), row))
        args.append(z)
    in_specs.append(pl.BlockSpec((c, d), fixed))
    args.append(w.astype(BF16))
    if final_g is not None:
        in_specs.append(pl.BlockSpec((1, d), fixed))
        args.append(final_g.reshape(1, d))
    kern = functools.partial(_out_proj_kernel, n_a=len(a_list), has_z=z is not None,
                             has_g=final_g is not None)
    return pl.pallas_call(
        kern, grid=(n // tm,), in_specs=in_specs, out_specs=pl.BlockSpec((tm, d), row),
        out_shape=jax.ShapeDtypeStruct((n, d), F32), compiler_params=_cparams("parallel"),
        name="out_proj",
    )(*args)


def _t5_bucket(dist):
    max_exact = T5_BUCKETS // 2
    d = jnp.maximum(dist, 0)
    df = jnp.maximum(d, 1).astype(F32)
    large = max_exact + (jnp.log(df / max_exact) / math.log(T5_MAX_DIST / max_exact)
                         * (T5_BUCKETS - max_exact)).astype(jnp.int32)
    large = jnp.minimum(large, T5_BUCKETS - 1)
    return jnp.where(d < max_exact, d, large)


def _band_bias(t5_table, nprev):
    kc = (nprev + 1) * Q_BLOCK
    period = kc + Q_BLOCK
    j = jnp.arange(period)
    j = jnp.where(j >= kc, j - period, j)
    vec = jnp.take(t5_table, _t5_bucket(nprev * Q_BLOCK - j), axis=0).astype(F32).T
    heads = vec.shape[0]
    flat = jnp.tile(vec, (1, Q_BLOCK))[:, :Q_BLOCK * (period - 1)]
    return flat.reshape(heads, Q_BLOCK, period - 1)[:, :, :kc]


def _band_attn_kernel(*refs, nprev, window, has_sink, gate_col):
    q_ref, k_ref, vt_ref, bias_ref = refs[:4]
    pos = 4
    sink_ref = refs[pos] if has_sink else None
    pos += int(has_sink)
    gate_ref = refs[pos] if gate_col is not None else None
    o_ref = refs[-1]
    n = pl.program_id(1)
    tq = Q_BLOCK
    kc = (nprev + 1) * tq
    kj = lax.broadcasted_iota(jnp.int32, (kc, tq), 0)
    qi = lax.broadcasted_iota(jnp.int32, (kc, tq), 1)
    dist = nprev * tq + qi - kj
    kpos = (n - nprev) * tq + kj
    valid = (dist >= 0) & (dist < window) & (kpos >= 0)
    valid = jnp.concatenate([valid] * GQA, axis=1)
    starts = [pl.multiple_of(jnp.maximum(n - nprev + j, 0) * tq, tq) for j in range(nprev + 1)]
    groups = range(KV_HEADS)
    scores = []
    for g in groups:
        kg = jnp.concatenate([k_ref[0, g, pl.ds(s, tq), :] for s in starts], axis=0)
        qg = q_ref[0, GQA * g:GQA * (g + 1)].reshape(GQA * tq, HEAD_DIM) * (HEAD_DIM ** -0.5)
        scores.append(_bdot_nt(kg, qg))
    probs, maxes = [], []
    for g in groups:
        s = jnp.where(valid, scores[g] + bias_ref[g], NEG)
        m = jnp.max(s, axis=0, keepdims=True)
        if has_sink:
            m = jnp.maximum(m, sink_ref[g])
        probs.append(jnp.exp(s - m).astype(BF16))
        maxes.append(m)
    accs = []
    ones = jnp.ones((HEAD_DIM, kc), BF16)
    for g in groups:
        vtg = jnp.concatenate([vt_ref[0, g * HEAD_DIM:(g + 1) * HEAD_DIM, pl.ds(s, tq)] for s in starts],
                              axis=1)
        vtg = jnp.concatenate([vtg, ones], axis=0)
        accs.append(jnp.dot(vtg, probs[g], preferred_element_type=F32))
    for g in groups:
        l = accs[g][HEAD_DIM:HEAD_DIM + 1]
        if has_sink:
            l = l + jnp.exp(sink_ref[g] - maxes[g])
        o_t = accs[g][:HEAD_DIM] / l
        outs = []
        for r in range(GQA):
            oh = o_t[:, r * tq:(r + 1) * tq]
            if gate_col is not None:
                oh = oh * _gate_row(gate_ref, gate_col + GQA * g + r)
            outs.append(oh.T)
        o_ref[0, :, g * GQA * HEAD_DIM:(g + 1) * GQA * HEAD_DIM] = jnp.concatenate(outs, axis=-1)


def _band_attn(q, k, vt, bias, window, sinks=None, gates_t=None, gate_col=None):
    b, _, t, _ = q.shape
    nprev = -(-window // Q_BLOCK)
    kc = (nprev + 1) * Q_BLOCK
    width = GQA * Q_BLOCK
    in_specs = [
        pl.BlockSpec((1, ATTN_HEADS, Q_BLOCK, HEAD_DIM), lambda i, n: (i, 0, n, 0)),
        pl.BlockSpec((1, KV_HEADS, t, HEAD_DIM), lambda i, n: (i, 0, 0, 0)),
        pl.BlockSpec((1, KV_HEADS * HEAD_DIM, t), lambda i, n: (i, 0, 0)),
        pl.BlockSpec((KV_HEADS, kc, width), lambda i, n: (0, 0, 0)),
    ]
    bias_t = bias.reshape(KV_HEADS, GQA, Q_BLOCK, kc).transpose(0, 3, 1, 2).reshape(KV_HEADS, kc, width)
    args = [q, k, vt, bias_t]
    if sinks is not None:
        in_specs.append(pl.BlockSpec((KV_HEADS, 1, width), lambda i, n: (0, 0, 0)))
        sk = jnp.broadcast_to(sinks.astype(F32).reshape(KV_HEADS, GQA, 1), (KV_HEADS, GQA, Q_BLOCK))
        args.append(sk.reshape(KV_HEADS, 1, width))
    if gates_t is not None:
        in_specs.append(pl.BlockSpec((1, LANES, Q_BLOCK), lambda i, n: (i, 0, n)))
        args.append(gates_t)
    kern = functools.partial(_band_attn_kernel, nprev=nprev, window=window,
                             has_sink=sinks is not None, gate_col=gate_col)
    d = ATTN_HEADS * HEAD_DIM
    return pl.pallas_call(
        kern, grid=(b, t // Q_BLOCK), in_specs=in_specs,
        out_specs=pl.BlockSpec((1, Q_BLOCK, d), lambda i, n: (i, n, 0)),
        out_shape=jax.ShapeDtypeStruct((b, t, d), F32),
        compiler_params=_cparams("parallel", "parallel"), name="band_attn",
    )(*args)


def _swa_layer(x2d, g, w_in, sinks, w_out, t5_table, final_g, batch, seq):
    nq, nkv = ATTN_HEADS * HEAD_DIM, KV_HEADS * HEAD_DIM
    ws = [w_in[:, :nq], w_in[:, nq:nq + nkv], w_in[:, nq + nkv:nq + 2 * nkv], w_in[:, nq + 2 * nkv:]]
    q, k, vt, z = _norm_proj(x2d, g, ws, ["heads", "heads", "cols", "rows"], [BF16, BF16, BF16, F32],
                             batch, seq)
    bias = _band_bias(t5_table, -(-SWA_WINDOW // Q_BLOCK))
    o = _band_attn(q, k, vt, bias, SWA_WINDOW, sinks=sinks)
    return _out_proj(x2d, [o.reshape(x2d.shape)], z, w_out, final_g)


def _rwkv_pre_kernel(x_ref, xp_ref, g_ref, mu_ref, wr_ref, wk_ref, wv_ref, wz_ref,
                     w0_ref, w1_ref, w2_ref, a0_ref, a1_ref, a2_ref,
                     r_ref, k_ref, v_ref, z_ref, lw_ref, a_ref, *, tiles_per_seq):
    i = pl.program_id(0)
    g = g_ref[...]
    xn = _rms(x_ref[...], g)
    prev = _rms(xp_ref[...], g)[7:8]
    prev = jnp.where(i % tiles_per_seq == 0, 0.0, prev)
    row = lax.broadcasted_iota(jnp.int32, xn.shape, 0)
    xprev = jnp.where(row == 0, prev, pltpu.roll(xn, 1, 0))
    xx = xprev - xn
    lerp = lambda s: xn + xx * mu_ref[s:s + 1, :]
    r_ref[...] = _bdot(lerp(0), wr_ref[...])
    k_ref[...] = _bdot(lerp(1), wk_ref[...])
    v_ref[...] = _bdot(lerp(2), wv_ref[...])
    z_ref[...] = _bdot(lerp(3), wz_ref[...])
    wl = w0_ref[...] + _bdot(jnp.tanh(_bdot(lerp(4), w1_ref[...])), w2_ref[...])
    sp = jnp.maximum(-wl, 0.0) + jnp.log1p(jnp.exp(-jnp.abs(wl)))
    lw_ref[...] = -jnp.exp(-sp - 0.5)
    al = a0_ref[...] + _bdot(_bdot(lerp(5), a1_ref[...]), a2_ref[...])
    a_ref[...] = jax.nn.sigmoid(al)


def _rwkv_pre(x2d, g, mu, w_in, w0, w1, w2, a0, a1, a2, seq):
    n, d = x2d.shape
    c = w0.shape[0]
    tm = min(ROW_TILE, seq)
    tiles_per_seq = seq // tm
    row = lambda i: (i, 0)
    fixed = lambda i: (0, 0)
    ws = [w_in[:, s * c:(s + 1) * c].astype(BF16) for s in range(4)]
    in_specs = [
        pl.BlockSpec((tm, d), row),
        pl.BlockSpec((8, d), lambda i: (jnp.maximum(i * (tm // 8) - 1, 0), 0)),
        pl.BlockSpec((1, d), fixed), pl.BlockSpec((6, d), fixed),
    ] + [pl.BlockSpec((d, c), fixed)] * 4 + [
        pl.BlockSpec((1, c), fixed), pl.BlockSpec(w1.shape, fixed), pl.BlockSpec(w2.shape, fixed),
        pl.BlockSpec((1, c), fixed), pl.BlockSpec(a1.shape, fixed), pl.BlockSpec(a2.shape, fixed),
    ]
    out = jax.ShapeDtypeStruct((n, c), F32)
    kern = functools.partial(_rwkv_pre_kernel, tiles_per_seq=tiles_per_seq)
    return pl.pallas_call(
        kern, grid=(n // tm,), in_specs=in_specs, out_specs=[pl.BlockSpec((tm, c), row)] * 6,
        out_shape=[out] * 6, compiler_params=_cparams("parallel"), name="rwkv_pre",
    )(x2d, x2d, g.reshape(1, d), mu, *ws, w0.reshape(1, c), w1.astype(BF16), w2.astype(BF16),
      a0.reshape(1, c), a1.astype(BF16), a2.astype(BF16))


def _dot3_many(a_list, b_list, dot=_bdot, passes=3):
    if passes == 1:
        return [dot(a, b) for a, b in zip(a_list, b_list)]
    sa = [_split2(a) for a in a_list]
    sb = [_split2(b) for b in b_list]
    hh = [dot(x[0], y[0]) for x, y in zip(sa, sb)]
    hl = [dot(x[0], y[1]) for x, y in zip(sa, sb)]
    lh = [dot(x[1], y[0]) for x, y in zip(sa, sb)]
    return [p + (q + r) for p, q, r in zip(hh, hl, lh)]


def _lane_sums(x_list):
    n = x_list[0].shape[1]
    ones = jnp.ones((n, n), BF16)
    parts = [_split2(x) for x in x_list]
    hi = [jnp.dot(p[0], ones, preferred_element_type=F32) for p in parts]
    lo = [jnp.dot(p[1], ones, preferred_element_type=F32) for p in parts]
    return [a + b for a, b in zip(hi, lo)]


def _rwkv_chunk_ops(rs, kps, vs, aas, bbs, lws, cums, lower_strict, lower_incl, eye, blockdiag):
    L = RWKV_CHUNK
    n = len(rs)
    n_ch = rs[0].shape[1]
    ats, rts, bks, bhs, khs, wls = [], [], [], [], [], []
    for r, kp, aa, bb, lw, cum in zip(rs, kps, aas, bbs, lws, cums):
        w_inv = jnp.exp(-cum)
        cum_last = cum[L - 1:L, :]
        w_tail = jnp.exp(cum_last - cum)
        ats.append(aa * jnp.exp(cum - lw))
        rts.append(r * jnp.exp(cum))
        bks.append(jnp.concatenate([bb * w_inv, kp * w_inv], axis=0))
        bhs.append(bb * w_tail)
        khs.append(kp * w_tail)
        wls.append(jnp.exp(cum_last))
    ars = [jnp.concatenate([at, rt], axis=0) for at, rt in zip(ats, rts)]
    As = _dot3_many(ars, bks, dot=_bdot_nt, passes=RWKV_SCORE_PASSES)
    a_ab = [jnp.where(lower_strict, A[:L, :L], 0.0) for A in As]
    a_rb = [jnp.where(lower_incl, A[L:, :L], 0.0) for A in As]
    a_k = [jnp.concatenate([jnp.where(lower_strict, A[:L, L:], 0.0),
                            jnp.where(lower_incl, A[L:, L:], 0.0)], axis=0) for A in As]
    dblk = [jnp.where(blockdiag, x, 0.0) for x in a_ab]
    akv = _dot3_many(a_k, vs, passes=RWKV_VALUE_PASSES)
    inv = functools.partial(_dot3_many, passes=RWKV_INVERSE_PASSES)
    d2 = inv(dblk, dblk)
    res = inv([eye + d for d in dblk] + d2, [eye + d for d in d2] + d2)
    s4, d4 = res[:n], res[n:]
    res = inv(d4 + d4, s4 + d4)
    s8 = [s + x for s, x in zip(s4, res[:n])]
    d8 = res[n:]
    tdiag = [s + x for s, x in zip(s8, inv(d8, s8))]
    rhs = [jnp.concatenate([at, x[:L]], axis=1) for at, x in zip(ats, akv)]
    res = inv(tdiag + tdiag, rhs + [x - d for x, d in zip(a_ab, dblk)])
    xt, nt = res[:n], res[n:]
    res = inv(nt + nt, xt + nt)
    u = [x + y for x, y in zip(xt, res[:n])]
    nt2 = res[n:]
    pq = [x + y for x, y in zip(u, inv(nt2, u))]
    res = _dot3_many(a_rb, pq, passes=RWKV_VALUE_PASSES)
    gh = [jnp.concatenate([rt, x[L:]], axis=1) + y for rt, x, y in zip(rts, akv, res)]
    res = _dot3_many(bhs + khs, pq + vs, dot=_bdot_tn, passes=RWKV_VALUE_PASSES)
    out = []
    for i in range(n):
        mc = res[i]
        m_op = mc[:, :n_ch] + eye * wls[i]
        c_op = mc[:, n_ch:] + res[n + i]
        out.append((gh[i][:, :n_ch], gh[i][:, n_ch:], m_op, c_op))
    return out


def _rwkv_scan_kernel(r_ref, k_ref, v_ref, z_ref, lw_ref, a_ref, kk_ref, ka_ref, rk_ref,
                      lnw_ref, lnb_ref, o_ref, state_ref):
    tb = pl.program_id(2)

    @pl.when(tb == 0)
    def _():
        state_ref[...] = jnp.zeros_like(state_ref)

    L = RWKV_CHUNK
    N = RWKV_HEAD
    ri = lax.broadcasted_iota(jnp.int32, (L, L), 0)
    ci = lax.broadcasted_iota(jnp.int32, (L, L), 1)
    lower_strict = ri > ci
    lower_incl = ri >= ci
    tri_incl = jnp.where(lower_incl, 1.0, 0.0).astype(BF16)
    eye = jnp.where(ri == ci, 1.0, 0.0).astype(F32)
    blockdiag = (ri // 16) == (ci // 16)
    n_heads = r_ref.shape[2] // N
    n_chunks = r_ref.shape[1] // L
    probs = [(hh, c) for c in range(n_chunks) for hh in range(n_heads)]
    rs, kps, vs, avs, kks, lws, rkr = [], [], [], [], [], [], []
    for hh, c in probs:
        sl = slice(hh * N, (hh + 1) * N)
        rows = slice(c * L, (c + 1) * L)
        k = k_ref[0, rows, sl]
        a = a_ref[0, rows, sl]
        kks.append(k * kk_ref[:, sl])
        rs.append(r_ref[0, rows, sl])
        kps.append(k * (1.0 + (a - 1.0) * ka_ref[:, sl]))
        vs.append(v_ref[0, rows, sl])
        avs.append(a)
        lws.append(lw_ref[0, rows, sl])
        rkr.append(rs[-1] * kps[-1] * rk_ref[:, sl])
    n_prob = len(probs)
    sums = _lane_sums([x * x for x in kks] + rkr)
    aas, bbs = [], []
    for i in range(n_prob):
        kk = kks[i] / jnp.maximum(jnp.sqrt(sums[i]), 1e-12)
        aas.append(-kk)
        bbs.append(kk * avs[i])
    bonus = [sums[n_prob + i] * vs[i] for i in range(n_prob)]
    cum_all = []
    for c in range(n_chunks):
        lw_c = lw_ref[0, c * L:(c + 1) * L, :]
        hi = lw_c.astype(BF16)
        r1 = lw_c - hi.astype(F32)
        mid = r1.astype(BF16)
        lo = (r1 - mid.astype(F32)).astype(BF16)
        cum_all.append([jnp.dot(tri_incl, p, preferred_element_type=F32) for p in (hi, mid, lo)])
    cum_all = [a + (b + c) for a, b, c in cum_all]
    cums = [cum_all[c][:, hh * N:(hh + 1) * N] for hh, c in probs]
    ops = _rwkv_chunk_ops(rs, kps, vs, aas, bbs, lws, cums, lower_strict, lower_incl, eye,
                          blockdiag)
    states = [state_ref[hh] for hh in range(n_heads)]
    ys = {}
    for c in range(n_chunks):
        idx = [c * n_heads + hh for hh in range(n_heads)]
        upd = _dot3_many([jnp.concatenate([ops[i][0], ops[i][2]], axis=0) for i in idx], states,
                         passes=RWKV_STATE_PASSES)
        for hh, i in enumerate(idx):
            ys[(hh, c)] = upd[hh][:L] + ops[i][1]
            states[hh] = upd[hh][L:] + ops[i][3]
    for hh in range(n_heads):
        state_ref[hh] = states[hh]
    y_list = [ys[p] for p in probs]
    ycs = [y - s * (1.0 / N) for y, s in zip(y_list, _lane_sums(y_list))]
    var = [s * (1.0 / N) for s in _lane_sums([yc * yc for yc in ycs])]
    cols = []
    for hh in range(n_heads):
        sl = slice(hh * N, (hh + 1) * N)
        parts = []
        for c in range(n_chunks):
            rows = slice(c * L, (c + 1) * L)
            i = c * n_heads + hh
            yn = ycs[i] * lax.rsqrt(var[i] + RWKV_GN_EPS) * lnw_ref[:, sl] + lnb_ref[:, sl]
            parts.append((yn + bonus[i]) * _silu(z_ref[0, rows, sl]))
        cols.append(jnp.concatenate(parts, axis=0))
    o_ref[0] = jnp.concatenate(cols, axis=1)


def _rwkv_scan(r, k, v, z, lw, a, k_k, k_a, r_k, lnx_w, lnx_b):
    b, t, c = r.shape
    tb = min(RWKV_BLOCK, t)
    seq_spec = pl.BlockSpec((1, tb, RWKV_LANES), lambda i, p, j: (i, j, p))
    par_spec = pl.BlockSpec((1, RWKV_LANES), lambda i, p, j: (0, p))
    params = [x.reshape(1, c).astype(F32) for x in (k_k, k_a, r_k, lnx_w, lnx_b)]
    return pl.pallas_call(
        _rwkv_scan_kernel, grid=(b, c // RWKV_LANES, t // tb),
        in_specs=[seq_spec] * 6 + [par_spec] * 5, out_specs=seq_spec,
        out_shape=jax.ShapeDtypeStruct((b, t, c), F32),
        scratch_shapes=[pltpu.VMEM((RWKV_LANES // RWKV_HEAD, RWKV_HEAD, RWKV_HEAD), F32)],
        compiler_params=_cparams("parallel", "parallel", "arbitrary"), name="rwkv_scan",
    )(r, k, v, z, lw, a, *params)


def _rwkv_layer(x2d, g, mu, w_in, w0, w1, w2, a0, a1, a2, k_k, k_a, r_k, lnx_w, lnx_b, w_out,
                final_g, batch, seq):
    c = w0.shape[0]
    r, k, v, z, lw, a = _rwkv_pre(x2d, g, mu, w_in, w0, w1, w2, a0, a1, a2, seq)
    sh = (batch, seq, c)
    y = _rwkv_scan(r.reshape(sh), k.reshape(sh), v.reshape(sh), z.reshape(sh), lw.reshape(sh),
                   a.reshape(sh), k_k, k_a, r_k, lnx_w, lnx_b)
    return _out_proj(x2d, [y.reshape(x2d.shape[0], c)], None, w_out, final_g)


def _nsa_compress_kernel(u_ref, posa_ref, posb_ref, w1a_ref, w1b_ref, w2_ref, o_ref):
    u = u_ref[0, 0]
    ha = _bdot(u + posa_ref[...], w1a_ref[...])
    hb = _bdot(u + posb_ref[...], w1b_ref[...])
    h = ha + pltpu.roll(hb, hb.shape[0] - 1, 0)
    o_ref[0, 0] = _bdot(_silu(h), w2_ref[...])


def _nsa_compress(t_hm, pos, w1, w2):
    b, g, t, dh = t_hm.shape
    nch = t // NSA_CMP_STRIDE
    half = NSA_CMP_STRIDE * dh
    u = t_hm.reshape(b, g, nch, half)
    posf = pos.astype(F32).reshape(2, 1, half)
    hid = w1.shape[1]
    fixed = lambda i, j: (0, 0)
    return pl.pallas_call(
        _nsa_compress_kernel, grid=(b, g),
        in_specs=[pl.BlockSpec((1, 1, nch, half), lambda i, j: (i, j, 0, 0)),
                  pl.BlockSpec((1, half), fixed), pl.BlockSpec((1, half), fixed),
                  pl.BlockSpec((half, hid), fixed), pl.BlockSpec((half, hid), fixed),
                  pl.BlockSpec((hid, dh), fixed)],
        out_specs=pl.BlockSpec((1, 1, nch, dh), lambda i, j: (i, j, 0, 0)),
        out_shape=jax.ShapeDtypeStruct((b, g, nch, dh), F32),
        compiler_params=_cparams("parallel", "parallel"), name="nsa_compress",
    )(u, posf[0], posf[1], w1[:half].astype(BF16), w1[half:].astype(BF16), w2.astype(BF16))


def _nsa_cmp_attn_kernel(q_ref, kc_ref, vct_ref, gate_ref, o_ref, sel_ref, *, n_sel):
    g = pl.program_id(1)
    n = pl.program_id(2)
    tq = Q_BLOCK
    ncmp = kc_ref.shape[2]
    q = q_ref[0].reshape(GQA * tq, HEAD_DIM) * (HEAD_DIM ** -0.5)
    s = _bdot_nt(kc_ref[0, 0], q)
    cend = lax.broadcasted_iota(jnp.int32, (ncmp, tq), 0) * NSA_CMP_STRIDE + (NSA_CMP_LEN - 1)
    tpos = n * tq + lax.broadcasted_iota(jnp.int32, (ncmp, tq), 1)
    ok = jnp.concatenate([cend <= tpos] * GQA, axis=1)
    s = jnp.where(ok, s, NEG)
    m = jnp.max(s, axis=0, keepdims=True)
    e = jnp.where(ok, jnp.exp(s - m), 0.0)
    l = jnp.sum(e, axis=0, keepdims=True)
    p = e / jnp.where(l > 0.0, l, 1.0)
    o_t = jnp.dot(vct_ref[0, 0].astype(BF16), p.astype(BF16), preferred_element_type=F32)
    outs = []
    for r in range(GQA):
        outs.append((o_t[:, r * tq:(r + 1) * tq] * _gate_row(gate_ref, GQA * g + r)).T)
    o_ref[0] = jnp.concatenate(outs, axis=-1)

    psum = p[:, :tq]
    for r in range(1, GQA):
        psum = psum + p[:, r * tq:(r + 1) * tq]
    si = lax.broadcasted_iota(jnp.int32, (n_sel, ncmp), 0)
    ni = lax.broadcasted_iota(jnp.int32, (n_sel, ncmp), 1)
    ratio = NSA_SEL_LEN // NSA_CMP_STRIDE
    overlap = ((ni < ratio * (si + 1)) & (ni * NSA_CMP_STRIDE + NSA_CMP_LEN - 1 >= si * NSA_SEL_LEN))
    overlap = jnp.where(overlap, 1.0, 0.0).astype(BF16)
    imp = _dot_exact_rhs(psum, overlap, dot=lambda a, b: _bdot(b, a))
    blk = lax.broadcasted_iota(jnp.int32, (n_sel, tq), 0)
    tq_pos = n * tq + lax.broadcasted_iota(jnp.int32, (n_sel, tq), 1)
    cur = tq_pos // NSA_SEL_LEN
    forced = (blk == 0) | (blk == cur) | (blk == cur - 1)
    future = blk * NSA_SEL_LEN > tq_pos
    imp = jnp.where(forced, BIG, jnp.where(future, NEG, imp))
    n_grp = n_sel // SUBLANES
    imp_g = [imp[SUBLANES * b:SUBLANES * (b + 1)] for b in range(n_grp)]
    blk_g = blk[:SUBLANES]
    ranks = [jnp.zeros((SUBLANES, tq), F32) for _ in range(n_grp)]
    for j in range(n_sel):
        row = imp[j:j + 1, :]
        for b in range(n_grp):
            if b > j // SUBLANES:
                ahead = row >= imp_g[b]
            elif b < j // SUBLANES:
                ahead = row > imp_g[b]
            else:
                ahead = (row > imp_g[b]) | ((row == imp_g[b]) & (blk_g > j % SUBLANES))
            ranks[b] = ranks[b] + jnp.where(ahead, 1.0, 0.0)
    rank = jnp.concatenate(ranks, axis=0)
    sel = jnp.where(rank < float(min(NSA_TOPK, n_sel)), 1.0, 0.0)
    sel_ref[0, 0] = sel


def _nsa_cmp_attn(q, k_cmp, v_cmp, gates_t):
    b, _, t, _ = q.shape
    ncmp = k_cmp.shape[2]
    n_sel = t // NSA_SEL_LEN
    d = ATTN_HEADS * HEAD_DIM
    kern = functools.partial(_nsa_cmp_attn_kernel, n_sel=n_sel)
    cmp_spec = pl.BlockSpec((1, 1, ncmp, HEAD_DIM), lambda i, g, n: (i, g, 0, 0))
    return pl.pallas_call(
        kern, grid=(b, KV_HEADS, t // Q_BLOCK),
        in_specs=[pl.BlockSpec((1, GQA, Q_BLOCK, HEAD_DIM), lambda i, g, n: (i, g, n, 0)),
                  cmp_spec, pl.BlockSpec((1, 1, HEAD_DIM, ncmp), lambda i, g, n: (i, g, 0, 0)),
                  pl.BlockSpec((1, LANES, Q_BLOCK), lambda i, g, n: (i, 0, n))],
        out_specs=[pl.BlockSpec((1, Q_BLOCK, GQA * HEAD_DIM), lambda i, g, n: (i, n, g)),
                   pl.BlockSpec((1, 1, n_sel, Q_BLOCK), lambda i, g, n: (i, g, 0, n))],
        out_shape=[jax.ShapeDtypeStruct((b, t, d), F32),
                   jax.ShapeDtypeStruct((b, KV_HEADS, n_sel, t), F32)],
        compiler_params=_cparams("parallel", "parallel", "parallel"), name="nsa_cmp_attn",
    )(q, k_cmp, jnp.swapaxes(v_cmp, 2, 3), gates_t)


def _nsa_sel_attn_kernel(q_ref, k_ref, vt_ref, selt_ref, bias_ref, far_ref, gate_ref, o_ref):
    g = pl.program_id(1)
    n = pl.program_id(2)
    tq = Q_BLOCK
    bl = NSA_SEL_LEN
    fk = NSA_FAR_KEYS
    q_all = q_ref[0].reshape(GQA * tq, HEAD_DIM) * (HEAD_DIM ** -0.5)

    def block_rows(first_block, count, limit):
        rows = []
        for i in range(count):
            blk = first_block + i
            row = selt_ref[0, 0, pl.ds(jnp.clip(blk, 0, jnp.maximum(limit - 1, 0)), 1), :]
            rows.append(jnp.broadcast_to(jnp.where((blk >= 0) & (blk < limit), row, 0.0), (bl, tq)))
        return jnp.concatenate(rows, axis=0) > 0.5

    def values_t(*starts_widths):
        vt = jnp.concatenate([vt_ref[0, :, pl.ds(s, w)] for s, w in starts_widths], axis=1)
        return jnp.concatenate([vt, jnp.ones_like(vt)], axis=0)

    prev = pl.multiple_of(jnp.maximum(n - 1, 0) * tq, tq)
    diag = pl.multiple_of(n * tq, tq)
    kt = jnp.concatenate([k_ref[0, 0, pl.ds(prev, tq), :], k_ref[0, 0, pl.ds(diag, tq), :]], axis=0)
    vt = values_t((prev, tq), (diag, tq))
    kj = lax.broadcasted_iota(jnp.int32, (2 * tq, tq), 0)
    qi = lax.broadcasted_iota(jnp.int32, (2 * tq, tq), 1)
    n_blocks = (n + 1) * (tq // bl)
    ok = block_rows((n - 1) * (tq // bl), 2 * tq // bl, n_blocks) & (tq + qi - kj >= 0)
    ok = jnp.concatenate([ok] * GQA, axis=1)
    s = jnp.where(ok, _bdot_nt(kt, q_all) + bias_ref[0], NEG)
    m = jnp.max(s, axis=0, keepdims=True)
    p = jnp.exp(s - m).astype(BF16)
    acc = jnp.dot(vt, p, preferred_element_type=F32)
    m = m - far_ref[0]
    far_blocks = jnp.maximum(n - 1, 0) * (tq // bl)

    def body(c, carry):
        m, acc = carry
        start = pl.multiple_of(c * fk, fk)
        ok = block_rows(c * (fk // bl), fk // bl, far_blocks)
        ok = jnp.concatenate([ok] * GQA, axis=1)
        s = jnp.where(ok, _bdot_nt(k_ref[0, 0, pl.ds(start, fk), :], q_all), NEG)
        m_new = jnp.maximum(m, jnp.max(s, axis=0, keepdims=True))
        p = jnp.exp(s - m_new).astype(BF16)
        pv = jnp.dot(values_t((start, fk)), p, preferred_element_type=F32)
        return m_new, jnp.exp(m - m_new) * acc + pv

    n_far = (far_blocks * bl + fk - 1) // fk
    _, acc = lax.fori_loop(0, n_far, body, (m, acc))
    o_t = acc[:HEAD_DIM] / acc[HEAD_DIM:HEAD_DIM + 1]
    outs = []
    for r in range(GQA):
        o = o_t[:, r * tq:(r + 1) * tq] * _gate_row(gate_ref, ATTN_HEADS + GQA * g + r)
        outs.append(o.T)
    o_ref[0] = jnp.concatenate(outs, axis=-1)


def _nsa_sel_attn(q, ks, vst, sel_t, bias_near, far, gates_t):
    b, _, t, _ = q.shape
    n_sel = t // NSA_SEL_LEN
    d = ATTN_HEADS * HEAD_DIM
    assert t % NSA_FAR_KEYS == 0 or t <= 2 * Q_BLOCK
    return pl.pallas_call(
        _nsa_sel_attn_kernel, grid=(b, KV_HEADS, t // Q_BLOCK),
        in_specs=[pl.BlockSpec((1, GQA, Q_BLOCK, HEAD_DIM), lambda i, g, n: (i, g, n, 0)),
                  pl.BlockSpec((1, 1, t, HEAD_DIM), lambda i, g, n: (i, g, 0, 0)),
                  pl.BlockSpec((1, HEAD_DIM, t), lambda i, g, n: (i, g, 0)),
                  pl.BlockSpec((1, 1, n_sel, Q_BLOCK), lambda i, g, n: (i, g, 0, n)),
                  pl.BlockSpec((1, 2 * Q_BLOCK, GQA * Q_BLOCK), lambda i, g, n: (g, 0, 0)),
                  pl.BlockSpec((1, 1, GQA * Q_BLOCK), lambda i, g, n: (g, 0, 0)),
                  pl.BlockSpec((1, LANES, Q_BLOCK), lambda i, g, n: (i, 0, n))],
        out_specs=pl.BlockSpec((1, Q_BLOCK, GQA * HEAD_DIM), lambda i, g, n: (i, n, g)),
        out_shape=jax.ShapeDtypeStruct((b, t, d), F32),
        compiler_params=_cparams("parallel", "parallel", "parallel"), name="nsa_sel_attn",
    )(q, ks, vst, sel_t, bias_near, far, gates_t)


def _nsa_layer(x2d, g, w_in, pos_k, k_w1, k_w2, pos_v, v_w1, v_w2, w_out, t5_table, final_g,
               batch, seq):
    nq, nkv = ATTN_HEADS * HEAD_DIM, KV_HEADS * HEAD_DIM
    offs = [0, nq] + [nq + nkv * (i + 1) for i in range(6)]
    ws = [w_in[:, offs[i]:offs[i + 1]] for i in range(7)]
    n_gate = 3 * ATTN_HEADS
    wg = jnp.pad(w_in[:, offs[7]:offs[7] + n_gate], ((0, 0), (0, LANES - n_gate)))
    wz = w_in[:, offs[7] + n_gate:]
    q, kc, vc, ks, vst, kw, vwt, gates_t, z = _norm_proj(
        x2d, g, ws + [wg, wz],
        ["heads", "heads", "heads", "heads", "cols", "heads", "cols", "cols", "rows"],
        [BF16, F32, F32, BF16, BF16, BF16, BF16, F32, F32], batch, seq)
    k_cmp = _nsa_compress(kc, pos_k, k_w1, k_w2)
    v_cmp = _nsa_compress(vc, pos_v, v_w1, v_w2)
    o_cmp, sel_t = _nsa_cmp_attn(q, k_cmp, v_cmp, gates_t)
    bias_near = _band_bias(t5_table, 1).reshape(KV_HEADS, GQA, Q_BLOCK, 2 * Q_BLOCK)
    bias_near = bias_near.transpose(0, 3, 1, 2).reshape(KV_HEADS, 2 * Q_BLOCK, GQA * Q_BLOCK)
    far = t5_table[T5_BUCKETS - 1].astype(F32).reshape(KV_HEADS, GQA, 1)
    far = jnp.broadcast_to(far, (KV_HEADS, GQA, Q_BLOCK)).reshape(KV_HEADS, 1, GQA * Q_BLOCK)
    o_sel = _nsa_sel_attn(q, ks, vst, sel_t, bias_near, far, gates_t)
    bias_win = _band_bias(t5_table, -(-NSA_WINDOW // Q_BLOCK))
    o_win = _band_attn(q, kw, vwt, bias_win, NSA_WINDOW, gates_t=gates_t, gate_col=2 * ATTN_HEADS)
    sh = x2d.shape
    return _out_proj(x2d, [o_cmp.reshape(sh), o_sel.reshape(sh), o_win.reshape(sh)], z, w_out,
                     final_g)


def _lru_kernel(x_ref, g_ref, win_u_ref, win_z_ref, cw_ref, cb_ref, wa_ref, ba_ref, wx_ref, bx_ref,
                sp_ref, wout_ref, *rest, half, has_fg):
    fg_ref = rest[0] if has_fg else None
    o_ref, tail_ref, h_ref = rest[-3:]
    j = pl.program_id(1)

    @pl.when(j == 0)
    def _():
        tail_ref[...] = jnp.zeros_like(tail_ref)
        h_ref[...] = jnp.zeros_like(h_ref)

    x = x_ref[...]
    xb = _rms(x, g_ref[...]).astype(BF16)
    u = jnp.dot(xb, win_u_ref[...], preferred_element_type=F32)
    z = jnp.dot(xb, win_z_ref[...], preferred_element_type=F32)
    tm, width = u.shape
    ext = jnp.concatenate([tail_ref[...], u], axis=0)
    tail_ref[...] = u[tm - 8:, :]
    uc = cb_ref[...] + cw_ref[CONV_WIDTH - 1:CONV_WIDTH, :] * u
    for s in range(1, CONV_WIDTH):
        uc = uc + cw_ref[CONV_WIDTH - 1 - s:CONV_WIDTH - s, :] * ext[8 - s:8 - s + tm, :]
    ucb = uc.astype(BF16)
    gr, gi = [], []
    for c in range(width // half):
        blk = ucb[:, c * half:(c + 1) * half]
        gr.append(jnp.dot(blk, wa_ref[c], preferred_element_type=F32))
        gi.append(jnp.dot(blk, wx_ref[c], preferred_element_type=F32))
    rg = jax.nn.sigmoid(jnp.concatenate(gr, axis=1) + ba_ref[...])
    ig = jax.nn.sigmoid(jnp.concatenate(gi, axis=1) + bx_ref[...])
    log_a = -LRU_C * rg * sp_ref[...]
    a = jnp.exp(log_a)
    bv = jnp.sqrt(1.0 - a * a) * (ig * uc)
    row = lax.broadcasted_iota(jnp.int32, (tm, width), 0) % SUBLANES
    sh = 1
    while sh < SUBLANES:
        a_s = jnp.where(row >= sh, pltpu.roll(a, sh, 0), 1.0)
        b_s = jnp.where(row >= sh, pltpu.roll(bv, sh, 0), 0.0)
        bv = a * b_s + bv
        a = a * a_s
        sh *= 2
    carry = h_ref[...]
    groups = []
    for i in range(tm // SUBLANES):
        rows = slice(i * SUBLANES, (i + 1) * SUBLANES)
        hg = bv[rows] + a[rows] * carry
        carry = hg[SUBLANES - 1:SUBLANES, :]
        groups.append(hg)
    h_ref[...] = carry
    gated = (jnp.concatenate(groups, axis=0) * _silu(z)).astype(BF16)
    y = x + jnp.dot(gated, wout_ref[...], preferred_element_type=F32)
    o_ref[...] = _rms(y, fg_ref[...]) if has_fg else y


def _block_diag(w, group):
    nb, n, _ = w.shape
    w = w.reshape(nb // group, group, n, n)
    eye = jnp.eye(group, dtype=w.dtype)
    return jnp.einsum('cgij,gh->cgihj', w, eye).reshape(nb // group, group * n, group * n)


def _lru_layer(x2d, g, w_in, conv_w, conv_b, ga_w, ga_b, gx_w, gx_b, lam, w_out, final_g, batch, seq):
    n, d = x2d.shape
    width = w_in.shape[1] // 2
    blk = ga_w.shape[1]
    group = LANES // math.gcd(blk, LANES)
    group = min(group, ga_w.shape[0])
    half = group * blk
    nsup = width // half
    tm = min(LRU_TILE, seq)
    tiles = seq // tm
    row = lambda i, j: (i * tiles + j, 0)
    fixed = lambda i, j: (0, 0)
    fixed3 = lambda i, j: (0, 0, 0)
    vec = lambda a: a.astype(F32).reshape(1, -1)
    softplus_neg_lam = jax.nn.softplus(-lam.astype(F32))
    has_fg = final_g is not None
    in_specs = [pl.BlockSpec((tm, d), row), pl.BlockSpec((1, d), fixed),
                pl.BlockSpec((d, width), fixed), pl.BlockSpec((d, width), fixed),
                pl.BlockSpec((CONV_WIDTH, width), fixed), pl.BlockSpec((1, width), fixed),
                pl.BlockSpec((nsup, half, half), fixed3), pl.BlockSpec((1, width), fixed),
                pl.BlockSpec((nsup, half, half), fixed3), pl.BlockSpec((1, width), fixed),
                pl.BlockSpec((1, width), fixed), pl.BlockSpec((width, d), fixed)]
    args = [x2d, g.reshape(1, d), w_in[:, :width].astype(BF16), w_in[:, width:].astype(BF16),
            conv_w.astype(F32), vec(conv_b), _block_diag(ga_w, group).astype(BF16), vec(ga_b),
            _block_diag(gx_w, group).astype(BF16), vec(gx_b), vec(softplus_neg_lam),
            w_out.astype(BF16)]
    if has_fg:
        in_specs.append(pl.BlockSpec((1, d), fixed))
        args.append(final_g.reshape(1, d))
    kern = functools.partial(_lru_kernel, half=half, has_fg=has_fg)
    return pl.pallas_call(
        kern, grid=(batch, tiles), in_specs=in_specs, out_specs=pl.BlockSpec((tm, d), row),
        out_shape=jax.ShapeDtypeStruct((n, d), F32),
        scratch_shapes=[pltpu.VMEM((8, width), F32), pltpu.VMEM((1, width), F32)],
        compiler_params=_cparams("parallel", "arbitrary"), name="rglru_layer",
    )(*args)


def kernel(x, t5_table, norm_g, final_g, a_w_in, a_sinks, a_w_out, b_mu, b_w_in, b_w0, b_w1, b_w2, b_a0, b_a1, b_a2, b_k_k, b_k_a, b_r_k, b_lnx_w, b_lnx_b, b_w_out, c_w_in, c_cmp_pos_k, c_cmp_k_w1, c_cmp_k_w2, c_cmp_pos_v, c_cmp_v_w1, c_cmp_v_w2, c_w_out, d_w_in, d_conv_w, d_conv_b, d_gate_a_w, d_gate_a_b, d_gate_x_w, d_gate_x_b, d_lambda, d_w_out):
    batch, seq, d = x.shape
    depth = norm_g.shape[0]
    h = x.reshape(batch * seq, d)
    for layer in range(depth):
        m, j = layer % 4, layer // 4
        g = norm_g[layer]
        fg = final_g if layer == depth - 1 else None
        if m == 0:
            h = _swa_layer(h, g, a_w_in[j], a_sinks[j], a_w_out[j], t5_table, fg, batch, seq)
        elif m == 1:
            h = _rwkv_layer(h, g, b_mu[j], b_w_in[j], b_w0[j], b_w1[j], b_w2[j], b_a0[j], b_a1[j],
                            b_a2[j], b_k_k[j], b_k_a[j], b_r_k[j], b_lnx_w[j], b_lnx_b[j],
                            b_w_out[j], fg, batch, seq)
        elif m == 2:
            h = _nsa_layer(h, g, c_w_in[j], c_cmp_pos_k[j], c_cmp_k_w1[j], c_cmp_k_w2[j],
                           c_cmp_pos_v[j], c_cmp_v_w1[j], c_cmp_v_w2[j], c_w_out[j], t5_table,
                           fg, batch, seq)
        else:
            h = _lru_layer(h, g, d_w_in[j], d_conv_w[j], d_conv_b[j], d_gate_a_w[j], d_gate_a_b[j],
                           d_gate_x_w[j], d_gate_x_b[j], d_lambda[j], d_w_out[j], fg, batch, seq)
    return h.reshape(batch, seq, d)
```

```python
import functools
import math

import jax
import jax.numpy as jnp
from jax import lax
from jax.experimental import pallas as pl
from jax.experimental.pallas import tpu as pltpu

F32 = jnp.float32
BF16 = jnp.bfloat16

EPS = 1e-6
NEG = -1e30
BIG = 1e30
T5_BUCKETS = 32
T5_MAX_DIST = 128
ATTN_HEADS = 16
HEAD_DIM = 64
KV_HEADS = 4
GQA = ATTN_HEADS // KV_HEADS
Q_BLOCK = 128
SWA_WINDOW = 128
RWKV_HEAD = 64
RWKV_GN_EPS = 64e-5
NSA_CMP_LEN = 32
NSA_CMP_STRIDE = 16
NSA_SEL_LEN = 64
NSA_TOPK = 16
NSA_WINDOW = 512
LRU_BLOCKS = 16
LRU_C = 8.0
CONV_WIDTH = 4

LANES = 128
SUBLANES = 8
VMEM_LIMIT = 56 * 1024 * 1024
ROW_TILE = 512
RWKV_CHUNK = 64
RWKV_BLOCK = 256
RWKV_LANES = 256
RWKV_SCORE_PASSES = 1
RWKV_VALUE_PASSES = 1
RWKV_INVERSE_PASSES = 1
RWKV_STATE_PASSES = 3
NSA_FAR_KEYS = 512
LRU_TILE = 256


def _cparams(*sem):
    return pltpu.CompilerParams(dimension_semantics=sem, vmem_limit_bytes=VMEM_LIMIT)


def _bdot(a, b):
    return jnp.dot(a.astype(BF16), b.astype(BF16), preferred_element_type=F32)


def _bdot_nt(a, b):
    return lax.dot_general(a.astype(BF16), b.astype(BF16), (((1,), (1,)), ((), ())),
                           preferred_element_type=F32)


def _bdot_tn(a, b):
    return lax.dot_general(a.astype(BF16), b.astype(BF16), (((0,), (0,)), ((), ())),
                           preferred_element_type=F32)


def _split2(x):
    hi = x.astype(BF16)
    lo = (x - hi.astype(F32)).astype(BF16)
    return hi, lo


def _dot3(a, b, dot=_bdot):
    ah, al = _split2(a)
    bh, bl = _split2(b)
    return dot(ah, bh) + (dot(ah, bl) + dot(al, bh))


def _dot_exact_rhs(a, b01, dot=_bdot):
    ah = a.astype(BF16)
    r1 = a - ah.astype(F32)
    am = r1.astype(BF16)
    al = (r1 - am.astype(F32)).astype(BF16)
    return dot(ah, b01) + (dot(am, b01) + dot(al, b01))


def _rms(x, g):
    return x * lax.rsqrt(jnp.mean(x * x, axis=-1, keepdims=True) + EPS) * g


def _silu(z):
    return z * jax.nn.sigmoid(z)


def _gate_row(gate_ref, c):
    return jax.nn.sigmoid(gate_ref[0, pl.ds(c, 1), :])


def _norm_proj_kernel(x_ref, g_ref, *refs, n_out, layouts):
    w_refs, o_refs = refs[:n_out], refs[n_out:]
    xb = _rms(x_ref[...], g_ref[...]).astype(BF16)
    for w_ref, o_ref, layout in zip(w_refs, o_refs, layouts):
        width = w_ref.shape[1]
        for c0 in range(0, width, 512):
            cw = min(512, width - c0)
            acc = jnp.dot(xb, w_ref[:, c0:c0 + cw], preferred_element_type=F32)
            if layout == "heads":
                for j in range(cw // HEAD_DIM):
                    o_ref[0, (c0 // HEAD_DIM) + j] = acc[:, j * HEAD_DIM:(j + 1) * HEAD_DIM].astype(o_ref.dtype)
            elif layout == "cols":
                o_ref[0, c0:c0 + cw, :] = acc.T.astype(o_ref.dtype)
            else:
                o_ref[:, c0:c0 + cw] = acc.astype(o_ref.dtype)


def _norm_proj(x2d, g, weights, layouts, dtypes, batch, seq):
    n, d = x2d.shape
    tm = min(ROW_TILE, seq)
    tiles_per_seq = seq // tm
    in_specs = [pl.BlockSpec((tm, d), lambda i: (i, 0)), pl.BlockSpec((1, d), lambda i: (0, 0))]
    out_specs, out_shapes = [], []
    for w, layout, dt in zip(weights, layouts, dtypes):
        width = w.shape[1]
        in_specs.append(pl.BlockSpec((d, width), lambda i: (0, 0)))
        if layout == "heads":
            nh = width // HEAD_DIM
            out_shapes.append(jax.ShapeDtypeStruct((batch, nh, seq, HEAD_DIM), dt))
            out_specs.append(pl.BlockSpec((1, nh, tm, HEAD_DIM),
                                          lambda i: (i // tiles_per_seq, 0, i % tiles_per_seq, 0)))
        elif layout == "cols":
            out_shapes.append(jax.ShapeDtypeStruct((batch, width, seq), dt))
            out_specs.append(pl.BlockSpec((1, width, tm),
                                          lambda i: (i // tiles_per_seq, 0, i % tiles_per_seq)))
        else:
            out_shapes.append(jax.ShapeDtypeStruct((n, width), dt))
            out_specs.append(pl.BlockSpec((tm, width), lambda i: (i, 0)))
    kern = functools.partial(_norm_proj_kernel, n_out=len(weights), layouts=tuple(layouts))
    return pl.pallas_call(
        kern, grid=(n // tm,), in_specs=in_specs, out_specs=out_specs, out_shape=out_shapes,
        compiler_params=_cparams("parallel"), name="norm_proj",
    )(x2d, g.reshape(1, d), *[w.astype(BF16) for w in weights])


def _out_proj_kernel(*refs, n_a, has_z, has_g):
    x_ref = refs[0]
    a_refs = refs[1:1 + n_a]
    pos = 1 + n_a
    z_ref = refs[pos] if has_z else None
    pos += int(has_z)
    w_ref = refs[pos]
    pos += 1
    g_ref = refs[pos] if has_g else None
    o_ref = refs[-1]
    a = a_refs[0][...].astype(F32)
    for r in a_refs[1:]:
        a = a + r[...].astype(F32)
    if has_z:
        a = a * _silu(z_ref[...])
    y = x_ref[...] + jnp.dot(a.astype(BF16), w_ref[...], preferred_element_type=F32)
    if has_g:
        y = _rms(y, g_ref[...])
    o_ref[...] = y


def _out_proj(x2d, a_list, z, w, final_g=None):
    n, d = x2d.shape
    c = w.shape[0]
    tm = min(ROW_TILE, n)
    row = lambda i: (i, 0)
    fixed = lambda i: (0, 0)
    in_specs = [pl.BlockSpec((tm, d), row)] + [pl.BlockSpec((tm, c), row) for _ in a_list]
    args = [x2d] + list(a_list)
    if z is not None:
        in_specs.append(pl.BlockSpec((tm, c), row))
        args.append(z)
    in_specs.append(pl.BlockSpec((c, d), fixed))
    args.append(w.astype(BF16))
    if final_g is not None:
        in_specs.append(pl.BlockSpec((1, d), fixed))
        args.append(final_g.reshape(1, d))
    kern = functools.partial(_out_proj_kernel, n_a=len(a_list), has_z=z is not None,
                             has_g=final_g is not None)
    return pl.pallas_call(
        kern, grid=(n // tm,), in_specs=in_specs, out_specs=pl.BlockSpec((tm, d), row),
        out_shape=jax.ShapeDtypeStruct((n, d), F32), compiler_params=_cparams("parallel"),
        name="out_proj",
    )(*args)


def _t5_bucket(dist):
    max_exact = T5_BUCKETS // 2
    d = jnp.maximum(dist, 0)
    df = jnp.maximum(d, 1).astype(F32)
    large = max_exact + (jnp.log(df / max_exact) / math.log(T5_MAX_DIST / max_exact)
                         * (T5_BUCKETS - max_exact)).astype(jnp.int32)
    large = jnp.minimum(large, T5_BUCKETS - 1)
    return jnp.where(d < max_exact, d, large)


def _band_bias(t5_table, nprev):
    kc = (nprev + 1) * Q_BLOCK
    period = kc + Q_BLOCK
    j = jnp.arange(period)
    j = jnp.where(j >= kc, j - period, j)
    vec = jnp.take(t5_table, _t5_bucket(nprev * Q_BLOCK - j), axis=0).astype(F32).T
    heads = vec.shape[0]
    flat = jnp.tile(vec, (1, Q_BLOCK))[:, :Q_BLOCK * (period - 1)]
    return flat.reshape(heads, Q_BLOCK, period - 1)[:, :, :kc]


def _band_attn_kernel(*refs, nprev, window, has_sink, gate_col):
    q_ref, k_ref, vt_ref, bias_ref = refs[:4]
    pos = 4
    sink_ref = refs[pos] if has_sink else None
    pos += int(has_sink)
    gate_ref = refs[pos] if gate_col is not None else None
    o_ref = refs[-1]
    n = pl.program_id(1)
    tq = Q_BLOCK
    kc = (nprev + 1) * tq
    kj = lax.broadcasted_iota(jnp.int32, (kc, tq), 0)
    qi = lax.broadcasted_iota(jnp.int32, (kc, tq), 1)
    dist = nprev * tq + qi - kj
    kpos = (n - nprev) * tq + kj
    valid = (dist >= 0) & (dist < window) & (kpos >= 0)
    valid = jnp.concatenate([valid] * GQA, axis=1)
    starts = [pl.multiple_of(jnp.maximum(n - nprev + j, 0) * tq, tq) for j in range(nprev + 1)]
    groups = range(KV_HEADS)
    scores = []
    for g in groups:
        kg = jnp.concatenate([k_ref[0, g, pl.ds(s, tq), :] for s in starts], axis=0)
        qg = q_ref[0, GQA * g:GQA * (g + 1)].reshape(GQA * tq, HEAD_DIM) * (HEAD_DIM ** -0.5)
        scores.append(_bdot_nt(kg, qg))
    probs, maxes = [], []
    for g in groups:
        s = jnp.where(valid, scores[g] + bias_ref[g], NEG)
        m = jnp.max(s, axis=0, keepdims=True)
        if has_sink:
            m = jnp.maximum(m, sink_ref[g])
        probs.append(jnp.exp(s - m).astype(BF16))
        maxes.append(m)
    accs = []
    ones = jnp.ones((HEAD_DIM, kc), BF16)
    for g in groups:
        vtg = jnp.concatenate([vt_ref[0, g * HEAD_DIM:(g + 1) * HEAD_DIM, pl.ds(s, tq)] for s in starts],
                              axis=1)
        vtg = jnp.concatenate([vtg, ones], axis=0)
        accs.append(jnp.dot(vtg, probs[g], preferred_element_type=F32))
    for g in groups:
        l = accs[g][HEAD_DIM:HEAD_DIM + 1]
        if has_sink:
            l = l + jnp.exp(sink_ref[g] - maxes[g])
        o_t = accs[g][:HEAD_DIM] / l
        outs = []
        for r in range(GQA):
            oh = o_t[:, r * tq:(r + 1) * tq]
            if gate_col is not None:
                oh = oh * _gate_row(gate_ref, gate_col + GQA * g + r)
            outs.append(oh.T)
        o_ref[0, :, g * GQA * HEAD_DIM:(g + 1) * GQA * HEAD_DIM] = jnp.concatenate(outs, axis=-1)


def _band_attn(q, k, vt, bias, window, sinks=None, gates_t=None, gate_col=None):
    b, _, t, _ = q.shape
    nprev = -(-window // Q_BLOCK)
    kc = (nprev + 1) * Q_BLOCK
    width = GQA * Q_BLOCK
    in_specs = [
        pl.BlockSpec((1, ATTN_HEADS, Q_BLOCK, HEAD_DIM), lambda i, n: (i, 0, n, 0)),
        pl.BlockSpec((1, KV_HEADS, t, HEAD_DIM), lambda i, n: (i, 0, 0, 0)),
        pl.BlockSpec((1, KV_HEADS * HEAD_DIM, t), lambda i, n: (i, 0, 0)),
        pl.BlockSpec((KV_HEADS, kc, width), lambda i, n: (0, 0, 0)),
    ]
    bias_t = bias.reshape(KV_HEADS, GQA, Q_BLOCK, kc).transpose(0, 3, 1, 2).reshape(KV_HEADS, kc, width)
    args = [q, k, vt, bias_t]
    if sinks is not None:
        in_specs.append(pl.BlockSpec((KV_HEADS, 1, width), lambda i, n: (0, 0, 0)))
        sk = jnp.broadcast_to(sinks.astype(F32).reshape(KV_HEADS, GQA, 1), (KV_HEADS, GQA, Q_BLOCK))
        args.append(sk.reshape(KV_HEADS, 1, width))
    if gates_t is not None:
        in_specs.append(pl.BlockSpec((1, LANES, Q_BLOCK), lambda i, n: (i, 0, n)))
        args.append(gates_t)
    kern = functools.partial(_band_attn_kernel, nprev=nprev, window=window,
                             has_sink=sinks is not None, gate_col=gate_col)
    d = ATTN_HEADS * HEAD_DIM
    return pl.pallas_call(
        kern, grid=(b, t // Q_BLOCK), in_specs=in_specs,
        out_specs=pl.BlockSpec((1, Q_BLOCK, d), lambda i, n: (i, n, 0)),
        out_shape=jax.ShapeDtypeStruct((b, t, d), F32),
        compiler_params=_cparams("parallel", "parallel"), name="band_attn",
    )(*args)


def _swa_layer(x2d, g, w_in, sinks, w_out, t5_table, final_g, batch, seq):
    nq, nkv = ATTN_HEADS * HEAD_DIM, KV_HEADS * HEAD_DIM
    ws = [w_in[:, :nq], w_in[:, nq:nq + nkv], w_in[:, nq + nkv:nq + 2 * nkv], w_in[:, nq + 2 * nkv:]]
    q, k, vt, z = _norm_proj(x2d, g, ws, ["heads", "heads", "cols", "rows"], [BF16, BF16, BF16, F32],
                             batch, seq)
    bias = _band_bias(t5_table, -(-SWA_WINDOW // Q_BLOCK))
    o = _band_attn(q, k, vt, bias, SWA_WINDOW, sinks=sinks)
    return _out_proj(x2d, [o.reshape(x2d.shape)], z, w_out, final_g)


def _rwkv_pre_kernel(x_ref, xp_ref, g_ref, mu_ref, wr_ref, wk_ref, wv_ref, wz_ref,
                     w0_ref, w1_ref, w2_ref, a0_ref, a1_ref, a2_ref,
                     r_ref, k_ref, v_ref, z_ref, lw_ref, a_ref, *, tiles_per_seq):
    i = pl.program_id(0)
    g = g_ref[...]
    xn = _rms(x_ref[...], g)
    prev = _rms(xp_ref[...], g)[7:8]
    prev = jnp.where(i % tiles_per_seq == 0, 0.0, prev)
    row = lax.broadcasted_iota(jnp.int32, xn.shape, 0)
    xprev = jnp.where(row == 0, prev, pltpu.roll(xn, 1, 0))
    xx = xprev - xn
    lerp = lambda s: xn + xx * mu_ref[s:s + 1, :]
    r_ref[...] = _bdot(lerp(0), wr_ref[...])
    k_ref[...] = _bdot(lerp(1), wk_ref[...])
    v_ref[...] = _bdot(lerp(2), wv_ref[...])
    z_ref[...] = _bdot(lerp(3), wz_ref[...])
    wl = w0_ref[...] + _bdot(jnp.tanh(_bdot(lerp(4), w1_ref[...])), w2_ref[...])
    sp = jnp.maximum(-wl, 0.0) + jnp.log1p(jnp.exp(-jnp.abs(wl)))
    lw_ref[...] = -jnp.exp(-sp - 0.5)
    al = a0_ref[...] + _bdot(_bdot(lerp(5), a1_ref[...]), a2_ref[...])
    a_ref[...] = jax.nn.sigmoid(al)


def _rwkv_pre(x2d, g, mu, w_in, w0, w1, w2, a0, a1, a2, seq):
    n, d = x2d.shape
    c = w0.shape[0]
    tm = min(ROW_TILE, seq)
    tiles_per_seq = seq // tm
    row = lambda i: (i, 0)
    fixed = lambda i: (0, 0)
    ws = [w_in[:, s * c:(s + 1) * c].astype(BF16) for s in range(4)]
    in_specs = [
        pl.BlockSpec((tm, d), row),
        pl.BlockSpec((8, d), lambda i: (jnp.maximum(i * (tm // 8) - 1, 0), 0)),
        pl.BlockSpec((1, d), fixed), pl.BlockSpec((6, d), fixed),
    ] + [pl.BlockSpec((d, c), fixed)] * 4 + [
        pl.BlockSpec((1, c), fixed), pl.BlockSpec(w1.shape, fixed), pl.BlockSpec(w2.shape, fixed),
        pl.BlockSpec((1, c), fixed), pl.BlockSpec(a1.shape, fixed), pl.BlockSpec(a2.shape, fixed),
    ]
    out = jax.ShapeDtypeStruct((n, c), F32)
    kern = functools.partial(_rwkv_pre_kernel, tiles_per_seq=tiles_per_seq)
    return pl.pallas_call(
        kern, grid=(n // tm,), in_specs=in_specs, out_specs=[pl.BlockSpec((tm, c), row)] * 6,
        out_shape=[out] * 6, compiler_params=_cparams("parallel"), name="rwkv_pre",
    )(x2d, x2d, g.reshape(1, d), mu, *ws, w0.reshape(1, c), w1.astype(BF16), w2.astype(BF16),
      a0.reshape(1, c), a1.astype(BF16), a2.astype(BF16))


def _dot3_many(a_list, b_list, dot=_bdot, passes=3):
    if passes == 1:
        return [dot(a, b) for a, b in zip(a_list, b_list)]
    sa = [_split2(a) for a in a_list]
    sb = [_split2(b) for b in b_list]
    hh = [dot(x[0], y[0]) for x, y in zip(sa, sb)]
    hl = [dot(x[0], y[1]) for x, y in zip(sa, sb)]
    lh = [dot(x[1], y[0]) for x, y in zip(sa, sb)]
    return [p + (q + r) for p, q, r in zip(hh, hl, lh)]


def _lane_sums(x_list, ones):
    parts = [_split2(x) for x in x_list]
    hi = [jnp.dot(p[0], ones, preferred_element_type=F32) for p in parts]
    lo = [jnp.dot(p[1], ones, preferred_element_type=F32) for p in parts]
    return [a + b for a, b in zip(hi, lo)]


def _rwkv_chunk_ops(ats, rts, bts, kts, bhs, khs, vs, wls, lower_strict, lower_incl, eye, blockdiag):
    L = RWKV_CHUNK
    n = len(ats)
    n_ch = ats[0].shape[1]
    bks = [jnp.concatenate([bt, kt], axis=0) for bt, kt in zip(bts, kts)]
    ars = [jnp.concatenate([at, rt], axis=0) for at, rt in zip(ats, rts)]
    As = _dot3_many(ars, bks, dot=_bdot_nt, passes=RWKV_SCORE_PASSES)
    a_ab = [jnp.where(lower_strict, A[:L, :L], 0.0) for A in As]
    a_rb = [jnp.where(lower_incl, A[L:, :L], 0.0) for A in As]
    a_k = [jnp.concatenate([jnp.where(lower_strict, A[:L, L:], 0.0),
                            jnp.where(lower_incl, A[L:, L:], 0.0)], axis=0) for A in As]
    dblk = [jnp.where(blockdiag, x, 0.0) for x in a_ab]
    akv = _dot3_many(a_k, vs, passes=RWKV_VALUE_PASSES)
    inv = functools.partial(_dot3_many, passes=RWKV_INVERSE_PASSES)
    d2 = inv(dblk, dblk)
    res = inv([eye + d for d in dblk] + d2, [eye + d for d in d2] + d2)
    s4, d4 = res[:n], res[n:]
    res = inv(d4 + d4, s4 + d4)
    s8 = [s + x for s, x in zip(s4, res[:n])]
    d8 = res[n:]
    tdiag = [s + x for s, x in zip(s8, inv(d8, s8))]
    rhs = [jnp.concatenate([at, x[:L]], axis=1) for at, x in zip(ats, akv)]
    res = inv(tdiag + tdiag, rhs + [x - d for x, d in zip(a_ab, dblk)])
    xt, nt = res[:n], res[n:]
    res = inv(nt + nt, xt + nt)
    u = [x + y for x, y in zip(xt, res[:n])]
    nt2 = res[n:]
    pq = [x + y for x, y in zip(u, inv(nt2, u))]
    res = _dot3_many(a_rb, pq, passes=RWKV_VALUE_PASSES)
    gh = [jnp.concatenate([rt, x[L:]], axis=1) + y for rt, x, y in zip(rts, akv, res)]
    res = _dot3_many(bhs + khs, pq + vs, dot=_bdot_tn, passes=RWKV_VALUE_PASSES)
    out = []
    for i in range(n):
        mc = res[i]
        m_op = mc[:, :n_ch] + eye * wls[i]
        c_op = mc[:, n_ch:] + res[n + i]
        out.append((gh[i][:, :n_ch], gh[i][:, n_ch:], m_op, c_op))
    return out


def _rwkv_scan_kernel(r_ref, k_ref, v_ref, z_ref, lw_ref, a_ref, kk_ref, ka_ref, rk_ref,
                      lnw_ref, lnb_ref, o_ref, state_ref):
    tb = pl.program_id(2)

    @pl.when(tb == 0)
    def _():
        state_ref[...] = jnp.zeros_like(state_ref)

    L = RWKV_CHUNK
    N = RWKV_HEAD
    ri = lax.broadcasted_iota(jnp.int32, (L, L), 0)
    ci = lax.broadcasted_iota(jnp.int32, (L, L), 1)
    lower_strict = ri > ci
    lower_incl = ri >= ci
    tri_incl = jnp.where(lower_incl, 1.0, 0.0).astype(BF16)
    eye = jnp.where(ri == ci, 1.0, 0.0).astype(F32)
    blockdiag = (ri // 16) == (ci // 16)
    n_heads = r_ref.shape[2] // N
    n_chunks = r_ref.shape[1] // L
    width = n_heads * N
    hi_ = lax.broadcasted_iota(jnp.int32, (width, width), 0) // N
    hj_ = lax.broadcasted_iota(jnp.int32, (width, width), 1) // N
    head_ones = jnp.where(hi_ == hj_, 1.0, 0.0).astype(BF16)
    chunk_rows = [slice(c * L, (c + 1) * L) for c in range(n_chunks)]
    r_c = [r_ref[0, rows, :] for rows in chunk_rows]
    v_c = [v_ref[0, rows, :] for rows in chunk_rows]
    lw_c = [lw_ref[0, rows, :] for rows in chunk_rows]
    kk_c, kp_c, a_c = [], [], []
    for rows in chunk_rows:
        k = k_ref[0, rows, :]
        a = a_ref[0, rows, :]
        kk_c.append(k * kk_ref[...])
        kp_c.append(k * (1.0 + (a - 1.0) * ka_ref[...]))
        a_c.append(a)
    sums = _lane_sums([x * x for x in kk_c] + [r * kp * rk_ref[...] for r, kp in zip(r_c, kp_c)],
                      head_ones)
    bonus_c = [sums[n_chunks + c] * v_c[c] for c in range(n_chunks)]
    cum_c = []
    for c in range(n_chunks):
        hi = lw_c[c].astype(BF16)
        r1 = lw_c[c] - hi.astype(F32)
        mid = r1.astype(BF16)
        lo = (r1 - mid.astype(F32)).astype(BF16)
        cum_c.append([jnp.dot(tri_incl, p, preferred_element_type=F32) for p in (hi, mid, lo)])
    cum_c = [a + (b + c) for a, b, c in cum_c]
    slabs = []
    for c in range(n_chunks):
        kk = kk_c[c] / jnp.maximum(jnp.sqrt(sums[c]), 1e-12)
        cum = cum_c[c]
        w_inv = jnp.exp(-cum)
        cum_last = cum[L - 1:L, :]
        w_tail = jnp.exp(cum_last - cum)
        bb = kk * a_c[c]
        slabs.append((-kk * jnp.exp(cum - lw_c[c]), r_c[c] * jnp.exp(cum), bb * w_inv,
                      kp_c[c] * w_inv, bb * w_tail, kp_c[c] * w_tail, v_c[c], jnp.exp(cum_last)))
    probs = [(hh, c) for c in range(n_chunks) for hh in range(n_heads)]
    per_head = [[slabs[c][j][:, hh * N:(hh + 1) * N] for hh, c in probs] for j in range(8)]
    ops = _rwkv_chunk_ops(*per_head, lower_strict, lower_incl, eye, blockdiag)
    states = [state_ref[hh] for hh in range(n_heads)]
    ys = {}
    for c in range(n_chunks):
        idx = [c * n_heads + hh for hh in range(n_heads)]
        upd = _dot3_many([jnp.concatenate([ops[i][0], ops[i][2]], axis=0) for i in idx], states,
                         passes=RWKV_STATE_PASSES)
        for hh, i in enumerate(idx):
            ys[(hh, c)] = upd[hh][:L] + ops[i][1]
            states[hh] = upd[hh][L:] + ops[i][3]
    for hh in range(n_heads):
        state_ref[hh] = states[hh]
    y_c = [jnp.concatenate([ys[(hh, c)] for hh in range(n_heads)], axis=1) for c in range(n_chunks)]
    yc_c = [y - s * (1.0 / N) for y, s in zip(y_c, _lane_sums(y_c, head_ones))]
    var_c = [s * (1.0 / N) for s in _lane_sums([yc * yc for yc in yc_c], head_ones)]
    for c, rows in enumerate(chunk_rows):
        yn = yc_c[c] * lax.rsqrt(var_c[c] + RWKV_GN_EPS) * lnw_ref[...] + lnb_ref[...]
        o_ref[0, rows, :] = (yn + bonus_c[c]) * _silu(z_ref[0, rows, :])


def _rwkv_scan(r, k, v, z, lw, a, k_k, k_a, r_k, lnx_w, lnx_b):
    b, t, c = r.shape
    tb = min(RWKV_BLOCK, t)
    seq_spec = pl.BlockSpec((1, tb, RWKV_LANES), lambda i, p, j: (i, j, p))
    par_spec = pl.BlockSpec((1, RWKV_LANES), lambda i, p, j: (0, p))
    params = [x.reshape(1, c).astype(F32) for x in (k_k, k_a, r_k, lnx_w, lnx_b)]
    return pl.pallas_call(
        _rwkv_scan_kernel, grid=(b, c // RWKV_LANES, t // tb),
        in_specs=[seq_spec] * 6 + [par_spec] * 5, out_specs=seq_spec,
        out_shape=jax.ShapeDtypeStruct((b, t, c), F32),
        scratch_shapes=[pltpu.VMEM((RWKV_LANES // RWKV_HEAD, RWKV_HEAD, RWKV_HEAD), F32)],
        compiler_params=_cparams("parallel", "parallel", "arbitrary"), name="rwkv_scan",
    )(r, k, v, z, lw, a, *params)


def _rwkv_layer(x2d, g, mu, w_in, w0, w1, w2, a0, a1, a2, k_k, k_a, r_k, lnx_w, lnx_b, w_out,
                final_g, batch, seq):
    c = w0.shape[0]
    r, k, v, z, lw, a = _rwkv_pre(x2d, g, mu, w_in, w0, w1, w2, a0, a1, a2, seq)
    sh = (batch, seq, c)
    y = _rwkv_scan(r.reshape(sh), k.reshape(sh), v.reshape(sh), z.reshape(sh), lw.reshape(sh),
                   a.reshape(sh), k_k, k_a, r_k, lnx_w, lnx_b)
    return _out_proj(x2d, [y.reshape(x2d.shape[0], c)], None, w_out, final_g)


def _nsa_compress_kernel(u_ref, posa_ref, posb_ref, w1a_ref, w1b_ref, w2_ref, o_ref):
    u = u_ref[0, 0]
    ha = _bdot(u + posa_ref[...], w1a_ref[...])
    hb = _bdot(u + posb_ref[...], w1b_ref[...])
    h = ha + pltpu.roll(hb, hb.shape[0] - 1, 0)
    o_ref[0, 0] = _bdot(_silu(h), w2_ref[...])


def _nsa_compress(t_hm, pos, w1, w2):
    b, g, t, dh = t_hm.shape
    nch = t // NSA_CMP_STRIDE
    half = NSA_CMP_STRIDE * dh
    u = t_hm.reshape(b, g, nch, half)
    posf = pos.astype(F32).reshape(2, 1, half)
    hid = w1.shape[1]
    fixed = lambda i, j: (0, 0)
    return pl.pallas_call(
        _nsa_compress_kernel, grid=(b, g),
        in_specs=[pl.BlockSpec((1, 1, nch, half), lambda i, j: (i, j, 0, 0)),
                  pl.BlockSpec((1, half), fixed), pl.BlockSpec((1, half), fixed),
                  pl.BlockSpec((half, hid), fixed), pl.BlockSpec((half, hid), fixed),
                  pl.BlockSpec((hid, dh), fixed)],
        out_specs=pl.BlockSpec((1, 1, nch, dh), lambda i, j: (i, j, 0, 0)),
        out_shape=jax.ShapeDtypeStruct((b, g, nch, dh), F32),
        compiler_params=_cparams("parallel", "parallel"), name="nsa_compress",
    )(u, posf[0], posf[1], w1[:half].astype(BF16), w1[half:].astype(BF16), w2.astype(BF16))


def _nsa_cmp_attn_kernel(q_ref, kc_ref, vct_ref, gate_ref, o_ref, sel_ref, *, n_sel):
    g = pl.program_id(1)
    n = pl.program_id(2)
    tq = Q_BLOCK
    ncmp = kc_ref.shape[2]
    q = q_ref[0].reshape(GQA * tq, HEAD_DIM) * (HEAD_DIM ** -0.5)
    s = _bdot_nt(kc_ref[0, 0], q)
    cend = lax.broadcasted_iota(jnp.int32, (ncmp, tq), 0) * NSA_CMP_STRIDE + (NSA_CMP_LEN - 1)
    tpos = n * tq + lax.broadcasted_iota(jnp.int32, (ncmp, tq), 1)
    ok = jnp.concatenate([cend <= tpos] * GQA, axis=1)
    s = jnp.where(ok, s, NEG)
    m = jnp.max(s, axis=0, keepdims=True)
    e = jnp.where(ok, jnp.exp(s - m), 0.0)
    l = jnp.sum(e, axis=0, keepdims=True)
    p = e / jnp.where(l > 0.0, l, 1.0)
    o_t = jnp.dot(vct_ref[0, 0].astype(BF16), p.astype(BF16), preferred_element_type=F32)
    outs = []
    for r in range(GQA):
        outs.append((o_t[:, r * tq:(r + 1) * tq] * _gate_row(gate_ref, GQA * g + r)).T)
    o_ref[0] = jnp.concatenate(outs, axis=-1)

    psum = p[:, :tq]
    for r in range(1, GQA):
        psum = psum + p[:, r * tq:(r + 1) * tq]
    si = lax.broadcasted_iota(jnp.int32, (n_sel, ncmp), 0)
    ni = lax.broadcasted_iota(jnp.int32, (n_sel, ncmp), 1)
    ratio = NSA_SEL_LEN // NSA_CMP_STRIDE
    overlap = ((ni < ratio * (si + 1)) & (ni * NSA_CMP_STRIDE + NSA_CMP_LEN - 1 >= si * NSA_SEL_LEN))
    overlap = jnp.where(overlap, 1.0, 0.0).astype(BF16)
    imp = _dot_exact_rhs(psum, overlap, dot=lambda a, b: _bdot(b, a))
    blk = lax.broadcasted_iota(jnp.int32, (n_sel, tq), 0)
    tq_pos = n * tq + lax.broadcasted_iota(jnp.int32, (n_sel, tq), 1)
    cur = tq_pos // NSA_SEL_LEN
    forced = (blk == 0) | (blk == cur) | (blk == cur - 1)
    future = blk * NSA_SEL_LEN > tq_pos
    imp = jnp.where(forced, BIG, jnp.where(future, NEG, imp))
    n_grp = n_sel // SUBLANES
    imp_g = [imp[SUBLANES * b:SUBLANES * (b + 1)] for b in range(n_grp)]
    blk_g = blk[:SUBLANES]
    ranks = [jnp.zeros((SUBLANES, tq), F32) for _ in range(n_grp)]
    for j in range(n_sel):
        row = imp[j:j + 1, :]
        for b in range(n_grp):
            if b > j // SUBLANES:
                ahead = row >= imp_g[b]
            elif b < j // SUBLANES:
                ahead = row > imp_g[b]
            else:
                ahead = (row > imp_g[b]) | ((row == imp_g[b]) & (blk_g > j % SUBLANES))
            ranks[b] = ranks[b] + jnp.where(ahead, 1.0, 0.0)
    rank = jnp.concatenate(ranks, axis=0)
    sel = jnp.where(rank < float(min(NSA_TOPK, n_sel)), 1.0, 0.0)
    sel_ref[0, 0] = sel


def _nsa_cmp_attn(q, k_cmp, v_cmp, gates_t):
    b, _, t, _ = q.shape
    ncmp = k_cmp.shape[2]
    n_sel = t // NSA_SEL_LEN
    d = ATTN_HEADS * HEAD_DIM
    kern = functools.partial(_nsa_cmp_attn_kernel, n_sel=n_sel)
    cmp_spec = pl.BlockSpec((1, 1, ncmp, HEAD_DIM), lambda i, g, n: (i, g, 0, 0))
    return pl.pallas_call(
        kern, grid=(b, KV_HEADS, t // Q_BLOCK),
        in_specs=[pl.BlockSpec((1, GQA, Q_BLOCK, HEAD_DIM), lambda i, g, n: (i, g, n, 0)),
                  cmp_spec, pl.BlockSpec((1, 1, HEAD_DIM, ncmp), lambda i, g, n: (i, g, 0, 0)),
                  pl.BlockSpec((1, LANES, Q_BLOCK), lambda i, g, n: (i, 0, n))],
        out_specs=[pl.BlockSpec((1, Q_BLOCK, GQA * HEAD_DIM), lambda i, g, n: (i, n, g)),
                   pl.BlockSpec((1, 1, n_sel, Q_BLOCK), lambda i, g, n: (i, g, 0, n))],
        out_shape=[jax.ShapeDtypeStruct((b, t, d), F32),
                   jax.ShapeDtypeStruct((b, KV_HEADS, n_sel, t), F32)],
        compiler_params=_cparams("parallel", "parallel", "parallel"), name="nsa_cmp_attn",
    )(q, k_cmp, jnp.swapaxes(v_cmp, 2, 3), gates_t)


def _nsa_sel_attn_kernel(q_ref, k_ref, vt_ref, selt_ref, bias_ref, far_ref, gate_ref, o_ref):
    g = pl.program_id(1)
    n = pl.program_id(2)
    tq = Q_BLOCK
    bl = NSA_SEL_LEN
    fk = NSA_FAR_KEYS
    q_all = q_ref[0].reshape(GQA * tq, HEAD_DIM) * (HEAD_DIM ** -0.5)

    def block_rows(first_block, count, limit):
        rows = []
        for i in range(count):
            blk = first_block + i
            row = selt_ref[0, 0, pl.ds(jnp.clip(blk, 0, jnp.maximum(limit - 1, 0)), 1), :]
            rows.append(jnp.broadcast_to(jnp.where((blk >= 0) & (blk < limit), row, 0.0), (bl, tq)))
        return jnp.concatenate(rows, axis=0) > 0.5

    def values_t(*starts_widths):
        vt = jnp.concatenate([vt_ref[0, :, pl.ds(s, w)] for s, w in starts_widths], axis=1)
        return jnp.concatenate([vt, jnp.ones_like(vt)], axis=0)

    prev = pl.multiple_of(jnp.maximum(n - 1, 0) * tq, tq)
    diag = pl.multiple_of(n * tq, tq)
    kt = jnp.concatenate([k_ref[0, 0, pl.ds(prev, tq), :], k_ref[0, 0, pl.ds(diag, tq), :]], axis=0)
    vt = values_t((prev, tq), (diag, tq))
    kj = lax.broadcasted_iota(jnp.int32, (2 * tq, tq), 0)
    qi = lax.broadcasted_iota(jnp.int32, (2 * tq, tq), 1)
    n_blocks = (n + 1) * (tq // bl)
    ok = block_rows((n - 1) * (tq // bl), 2 * tq // bl, n_blocks) & (tq + qi - kj >= 0)
    ok = jnp.concatenate([ok] * GQA, axis=1)
    s = jnp.where(ok, _bdot_nt(kt, q_all) + bias_ref[0], NEG)
    m = jnp.max(s, axis=0, keepdims=True)
    p = jnp.exp(s - m).astype(BF16)
    acc = jnp.dot(vt, p, preferred_element_type=F32)
    m = m - far_ref[0]
    far_blocks = jnp.maximum(n - 1, 0) * (tq // bl)

    def body(c, carry):
        m, acc = carry
        start = pl.multiple_of(c * fk, fk)
        ok = block_rows(c * (fk // bl), fk // bl, far_blocks)
        ok = jnp.concatenate([ok] * GQA, axis=1)
        s = jnp.where(ok, _bdot_nt(k_ref[0, 0, pl.ds(start, fk), :], q_all), NEG)
        m_new = jnp.maximum(m, jnp.max(s, axis=0, keepdims=True))
        p = jnp.exp(s - m_new).astype(BF16)
        pv = jnp.dot(values_t((start, fk)), p, preferred_element_type=F32)
        return m_new, jnp.exp(m - m_new) * acc + pv

    n_far = (far_blocks * bl + fk - 1) // fk
    _, acc = lax.fori_loop(0, n_far, body, (m, acc))
    o_t = acc[:HEAD_DIM] / acc[HEAD_DIM:HEAD_DIM + 1]
    outs = []
    for r in range(GQA):
        o = o_t[:, r * tq:(r + 1) * tq] * _gate_row(gate_ref, ATTN_HEADS + GQA * g + r)
        outs.append(o.T)
    o_ref[0] = jnp.concatenate(outs, axis=-1)


def _nsa_sel_attn(q, ks, vst, sel_t, bias_near, far, gates_t):
    b, _, t, _ = q.shape
    n_sel = t // NSA_SEL_LEN
    d = ATTN_HEADS * HEAD_DIM
    assert t % NSA_FAR_KEYS == 0 or t <= 2 * Q_BLOCK
    return pl.pallas_call(
        _nsa_sel_attn_kernel, grid=(b, KV_HEADS, t // Q_BLOCK),
        in_specs=[pl.BlockSpec((1, GQA, Q_BLOCK, HEAD_DIM), lambda i, g, n: (i, g, n, 0)),
                  pl.BlockSpec((1, 1, t, HEAD_DIM), lambda i, g, n: (i, g, 0, 0)),
                  pl.BlockSpec((1, HEAD_DIM, t), lambda i, g, n: (i, g, 0)),
                  pl.BlockSpec((1, 1, n_sel, Q_BLOCK), lambda i, g, n: (i, g, 0, n)),
                  pl.BlockSpec((1, 2 * Q_BLOCK, GQA * Q_BLOCK), lambda i, g, n: (g, 0, 0)),
                  pl.BlockSpec((1, 1, GQA * Q_BLOCK), lambda i, g, n: (g, 0, 0)),
                  pl.BlockSpec((1, LANES, Q_BLOCK), lambda i, g, n: (i, 0, n))],
        out_specs=pl.BlockSpec((1, Q_BLOCK, GQA * HEAD_DIM), lambda i, g, n: (i, n, g)),
        out_shape=jax.ShapeDtypeStruct((b, t, d), F32),
        compiler_params=_cparams("parallel", "parallel", "parallel"), name="nsa_sel_attn",
    )(q, ks, vst, sel_t, bias_near, far, gates_t)


def _nsa_layer(x2d, g, w_in, pos_k, k_w1, k_w2, pos_v, v_w1, v_w2, w_out, t5_table, final_g,
               batch, seq):
    nq, nkv = ATTN_HEADS * HEAD_DIM, KV_HEADS * HEAD_DIM
    offs = [0, nq] + [nq + nkv * (i + 1) for i in range(6)]
    ws = [w_in[:, offs[i]:offs[i + 1]] for i in range(7)]
    n_gate = 3 * ATTN_HEADS
    wg = jnp.pad(w_in[:, offs[7]:offs[7] + n_gate], ((0, 0), (0, LANES - n_gate)))
    wz = w_in[:, offs[7] + n_gate:]
    q, kc, vc, ks, vst, kw, vwt, gates_t, z = _norm_proj(
        x2d, g, ws + [wg, wz],
        ["heads", "heads", "heads", "heads", "cols", "heads", "cols", "cols", "rows"],
        [BF16, F32, F32, BF16, BF16, BF16, BF16, F32, F32], batch, seq)
    k_cmp = _nsa_compress(kc, pos_k, k_w1, k_w2)
    v_cmp = _nsa_compress(vc, pos_v, v_w1, v_w2)
    o_cmp, sel_t = _nsa_cmp_attn(q, k_cmp, v_cmp, gates_t)
    bias_near = _band_bias(t5_table, 1).reshape(KV_HEADS, GQA, Q_BLOCK, 2 * Q_BLOCK)
    bias_near = bias_near.transpose(0, 3, 1, 2).reshape(KV_HEADS, 2 * Q_BLOCK, GQA * Q_BLOCK)
    far = t5_table[T5_BUCKETS - 1].astype(F32).reshape(KV_HEADS, GQA, 1)
    far = jnp.broadcast_to(far, (KV_HEADS, GQA, Q_BLOCK)).reshape(KV_HEADS, 1, GQA * Q_BLOCK)
    o_sel = _nsa_sel_attn(q, ks, vst, sel_t, bias_near, far, gates_t)
    bias_win = _band_bias(t5_table, -(-NSA_WINDOW // Q_BLOCK))
    o_win = _band_attn(q, kw, vwt, bias_win, NSA_WINDOW, gates_t=gates_t, gate_col=2 * ATTN_HEADS)
    sh = x2d.shape
    return _out_proj(x2d, [o_cmp.reshape(sh), o_sel.reshape(sh), o_win.reshape(sh)], z, w_out,
                     final_g)


def _lru_kernel(x_ref, g_ref, win_u_ref, win_z_ref, cw_ref, cb_ref, wa_ref, ba_ref, wx_ref, bx_ref,
                sp_ref, wout_ref, *rest, half, has_fg):
    fg_ref = rest[0] if has_fg else None
    o_ref, tail_ref, h_ref = rest[-3:]
    j = pl.program_id(1)

    @pl.when(j == 0)
    def _():
        tail_ref[...] = jnp.zeros_like(tail_ref)
        h_ref[...] = jnp.zeros_like(h_ref)

    x = x_ref[...]
    xb = _rms(x, g_ref[...]).astype(BF16)
    u = jnp.dot(xb, win_u_ref[...], preferred_element_type=F32)
    z = jnp.dot(xb, win_z_ref[...], preferred_element_type=F32)
    tm, width = u.shape
    ext = jnp.concatenate([tail_ref[...], u], axis=0)
    tail_ref[...] = u[tm - 8:, :]
    uc = cb_ref[...] + cw_ref[CONV_WIDTH - 1:CONV_WIDTH, :] * u
    for s in range(1, CONV_WIDTH):
        uc = uc + cw_ref[CONV_WIDTH - 1 - s:CONV_WIDTH - s, :] * ext[8 - s:8 - s + tm, :]
    ucb = uc.astype(BF16)
    gr, gi = [], []
    for c in range(width // half):
        blk = ucb[:, c * half:(c + 1) * half]
        gr.append(jnp.dot(blk, wa_ref[c], preferred_element_type=F32))
        gi.append(jnp.dot(blk, wx_ref[c], preferred_element_type=F32))
    rg = jax.nn.sigmoid(jnp.concatenate(gr, axis=1) + ba_ref[...])
    ig = jax.nn.sigmoid(jnp.concatenate(gi, axis=1) + bx_ref[...])
    log_a = -LRU_C * rg * sp_ref[...]
    a = jnp.exp(log_a)
    bv = jnp.sqrt(1.0 - a * a) * (ig * uc)
    row = lax.broadcasted_iota(jnp.int32, (tm, width), 0) % SUBLANES
    sh = 1
    while sh < SUBLANES:
        a_s = jnp.where(row >= sh, pltpu.roll(a, sh, 0), 1.0)
        b_s = jnp.where(row >= sh, pltpu.roll(bv, sh, 0), 0.0)
        bv = a * b_s + bv
        a = a * a_s
        sh *= 2
    carry = h_ref[...]
    groups = []
    for i in range(tm // SUBLANES):
        rows = slice(i * SUBLANES, (i + 1) * SUBLANES)
        hg = bv[rows] + a[rows] * carry
        carry = hg[SUBLANES - 1:SUBLANES, :]
        groups.append(hg)
    h_ref[...] = carry
    gated = (jnp.concatenate(groups, axis=0) * _silu(z)).astype(BF16)
    y = x + jnp.dot(gated, wout_ref[...], preferred_element_type=F32)
    o_ref[...] = _rms(y, fg_ref[...]) if has_fg else y


def _block_diag(w, group):
    nb, n, _ = w.shape
    w = w.reshape(nb // group, group, n, n)
    eye = jnp.eye(group, dtype=w.dtype)
    return jnp.einsum('cgij,gh->cgihj', w, eye).reshape(nb // group, group * n, group * n)


def _lru_layer(x2d, g, w_in, conv_w, conv_b, ga_w, ga_b, gx_w, gx_b, lam, w_out, final_g, batch, seq):
    n, d = x2d.shape
    width = w_in.shape[1] // 2
    blk = ga_w.shape[1]
    group = LANES // math.gcd(blk, LANES)
    group = min(group, ga_w.shape[0])
    half = group * blk
    nsup = width // half
    tm = min(LRU_TILE, seq)
    tiles = seq // tm
    row = lambda i, j: (i * tiles + j, 0)
    fixed = lambda i, j: (0, 0)
    fixed3 = lambda i, j: (0, 0, 0)
    vec = lambda a: a.astype(F32).reshape(1, -1)
    softplus_neg_lam = jax.nn.softplus(-lam.astype(F32))
    has_fg = final_g is not None
    in_specs = [pl.BlockSpec((tm, d), row), pl.BlockSpec((1, d), fixed),
                pl.BlockSpec((d, width), fixed), pl.BlockSpec((d, width), fixed),
                pl.BlockSpec((CONV_WIDTH, width), fixed), pl.BlockSpec((1, width), fixed),
                pl.BlockSpec((nsup, half, half), fixed3), pl.BlockSpec((1, width), fixed),
                pl.BlockSpec((nsup, half, half), fixed3), pl.BlockSpec((1, width), fixed),
                pl.BlockSpec((1, width), fixed), pl.BlockSpec((width, d), fixed)]
    args = [x2d, g.reshape(1, d), w_in[:, :width].astype(BF16), w_in[:, width:].astype(BF16),
            conv_w.astype(F32), vec(conv_b), _block_diag(ga_w, group).astype(BF16), vec(ga_b),
            _block_diag(gx_w, group).astype(BF16), vec(gx_b), vec(softplus_neg_lam),
            w_out.astype(BF16)]
    if has_fg:
        in_specs.append(pl.BlockSpec((1, d), fixed))
        args.append(final_g.reshape(1, d))
    kern = functools.partial(_lru_kernel, half=half, has_fg=has_fg)
    return pl.pallas_call(
        kern, grid=(batch, tiles), in_specs=in_specs, out_specs=pl.BlockSpec((tm, d), row),
        out_shape=jax.ShapeDtypeStruct((n, d), F32),
        scratch_shapes=[pltpu.VMEM((8, width), F32), pltpu.VMEM((1, width), F32)],
        compiler_params=_cparams("parallel", "arbitrary"), name="rglru_layer",
    )(*args)


def kernel(x, t5_table, norm_g, final_g, a_w_in, a_sinks, a_w_out, b_mu, b_w_in, b_w0, b_w1, b_w2, b_a0, b_a1, b_a2, b_k_k, b_k_a, b_r_k, b_lnx_w, b_lnx_b, b_w_out, c_w_in, c_cmp_pos_k, c_cmp_k_w1, c_cmp_k_w2, c_cmp_pos_v, c_cmp_v_w1, c_cmp_v_w2, c_w_out, d_w_in, d_conv_w, d_conv_b, d_gate_a_w, d_gate_a_b, d_gate_x_w, d_gate_x_b, d_lambda, d_w_out):
    batch, seq, d = x.shape
    depth = norm_g.shape[0]
    h = x.reshape(batch * seq, d)
    for layer in range(depth):
        m, j = layer % 4, layer // 4
        g = norm_g[layer]
        fg = final_g if layer == depth - 1 else None
        if m == 0:
            h = _swa_layer(h, g, a_w_in[j], a_sinks[j], a_w_out[j], t5_table, fg, batch, seq)
        elif m == 1:
            h = _rwkv_layer(h, g, b_mu[j], b_w_in[j], b_w0[j], b_w1[j], b_w2[j], b_a0[j], b_a1[j],
                            b_a2[j], b_k_k[j], b_k_a[j], b_r_k[j], b_lnx_w[j], b_lnx_b[j],
                            b_w_out[j], fg, batch, seq)
        elif m == 2:
            h = _nsa_layer(h, g, c_w_in[j], c_cmp_pos_k[j], c_cmp_k_w1[j], c_cmp_k_w2[j],
                           c_cmp_pos_v[j], c_cmp_v_w1[j], c_cmp_v_w2[j], c_w_out[j], t5_table,
                           fg, batch, seq)
        else:
            h = _lru_layer(h, g, d_w_in[j], d_conv_w[j], d_conv_b[j], d_gate_a_w[j], d_gate_a_b[j],
                           d_gate_x_w[j], d_gate_x_b[j], d_lambda[j], d_w_out[j], fg, batch, seq)
    return h.reshape(batch, seq, d)
```

```python
import functools
import math

import jax
import jax.numpy as jnp
from jax import lax
from jax.experimental import pallas as pl
from jax.experimental.pallas import tpu as pltpu

F32 = jnp.float32
BF16 = jnp.bfloat16

EPS = 1e-6
NEG = -1e30
BIG = 1e30
T5_BUCKETS = 32
T5_MAX_DIST = 128
ATTN_HEADS = 16
HEAD_DIM = 64
KV_HEADS = 4
GQA = ATTN_HEADS // KV_HEADS
Q_BLOCK = 128
SWA_WINDOW = 128
RWKV_HEAD = 64
RWKV_GN_EPS = 64e-5
NSA_CMP_LEN = 32
NSA_CMP_STRIDE = 16
NSA_SEL_LEN = 64
NSA_TOPK = 16
NSA_WINDOW = 512
LRU_BLOCKS = 16
LRU_C = 8.0
CONV_WIDTH = 4

LANES = 128
SUBLANES = 8
VMEM_LIMIT = 56 * 1024 * 1024
ROW_TILE = 512
RWKV_CHUNK = 64
RWKV_BLOCK = 256
RWKV_LANES = 256
RWKV_SCORE_PASSES = 1
RWKV_VALUE_PASSES = 1
RWKV_INVERSE_PASSES = 1
RWKV_STATE_PASSES = 3
NSA_FAR_KEYS = 512
NSA_SEL_QUERIES = 512
LRU_TILE = 256


def _cparams(*sem):
    return pltpu.CompilerParams(dimension_semantics=sem, vmem_limit_bytes=VMEM_LIMIT)


def _bdot(a, b):
    return jnp.dot(a.astype(BF16), b.astype(BF16), preferred_element_type=F32)


def _bdot_nt(a, b):
    return lax.dot_general(a.astype(BF16), b.astype(BF16), (((1,), (1,)), ((), ())),
                           preferred_element_type=F32)


def _bdot_tn(a, b):
    return lax.dot_general(a.astype(BF16), b.astype(BF16), (((0,), (0,)), ((), ())),
                           preferred_element_type=F32)


def _split2(x):
    hi = x.astype(BF16)
    lo = (x - hi.astype(F32)).astype(BF16)
    return hi, lo


def _dot3(a, b, dot=_bdot):
    ah, al = _split2(a)
    bh, bl = _split2(b)
    return dot(ah, bh) + (dot(ah, bl) + dot(al, bh))


def _dot_exact_rhs(a, b01, dot=_bdot):
    ah = a.astype(BF16)
    r1 = a - ah.astype(F32)
    am = r1.astype(BF16)
    al = (r1 - am.astype(F32)).astype(BF16)
    return dot(ah, b01) + (dot(am, b01) + dot(al, b01))


def _rms(x, g):
    return x * lax.rsqrt(jnp.mean(x * x, axis=-1, keepdims=True) + EPS) * g


def _silu(z):
    return z * jax.nn.sigmoid(z)


def _gate_row(gate_ref, c):
    return jax.nn.sigmoid(gate_ref[0, pl.ds(c, 1), :])


def _norm_proj_kernel(x_ref, g_ref, *refs, n_out, layouts):
    w_refs, o_refs = refs[:n_out], refs[n_out:]
    xb = _rms(x_ref[...], g_ref[...]).astype(BF16)
    for w_ref, o_ref, layout in zip(w_refs, o_refs, layouts):
        width = w_ref.shape[1]
        for c0 in range(0, width, 512):
            cw = min(512, width - c0)
            acc = jnp.dot(xb, w_ref[:, c0:c0 + cw], preferred_element_type=F32)
            if layout == "heads":
                for j in range(cw // HEAD_DIM):
                    o_ref[0, (c0 // HEAD_DIM) + j] = acc[:, j * HEAD_DIM:(j + 1) * HEAD_DIM].astype(o_ref.dtype)
            elif layout == "cols":
                o_ref[0, c0:c0 + cw, :] = acc.T.astype(o_ref.dtype)
            else:
                o_ref[:, c0:c0 + cw] = acc.astype(o_ref.dtype)


def _norm_proj(x2d, g, weights, layouts, dtypes, batch, seq):
    n, d = x2d.shape
    tm = min(ROW_TILE, seq)
    tiles_per_seq = seq // tm
    in_specs = [pl.BlockSpec((tm, d), lambda i: (i, 0)), pl.BlockSpec((1, d), lambda i: (0, 0))]
    out_specs, out_shapes = [], []
    for w, layout, dt in zip(weights, layouts, dtypes):
        width = w.shape[1]
        in_specs.append(pl.BlockSpec((d, width), lambda i: (0, 0)))
        if layout == "heads":
            nh = width // HEAD_DIM
            out_shapes.append(jax.ShapeDtypeStruct((batch, nh, seq, HEAD_DIM), dt))
            out_specs.append(pl.BlockSpec((1, nh, tm, HEAD_DIM),
                                          lambda i: (i // tiles_per_seq, 0, i % tiles_per_seq, 0)))
        elif layout == "cols":
            out_shapes.append(jax.ShapeDtypeStruct((batch, width, seq), dt))
            out_specs.append(pl.BlockSpec((1, width, tm),
                                          lambda i: (i // tiles_per_seq, 0, i % tiles_per_seq)))
        else:
            out_shapes.append(jax.ShapeDtypeStruct((n, width), dt))
            out_specs.append(pl.BlockSpec((tm, width), lambda i: (i, 0)))
    kern = functools.partial(_norm_proj_kernel, n_out=len(weights), layouts=tuple(layouts))
    return pl.pallas_call(
        kern, grid=(n // tm,), in_specs=in_specs, out_specs=out_specs, out_shape=out_shapes,
        compiler_params=_cparams("parallel"), name="norm_proj",
    )(x2d, g.reshape(1, d), *[w.astype(BF16) for w in weights])


def _out_proj_kernel(*refs, n_a, has_z, has_g):
    x_ref = refs[0]
    a_refs = refs[1:1 + n_a]
    pos = 1 + n_a
    z_ref = refs[pos] if has_z else None
    pos += int(has_z)
    w_ref = refs[pos]
    pos += 1
    g_ref = refs[pos] if has_g else None
    o_ref = refs[-1]
    a = a_refs[0][...].astype(F32)
    for r in a_refs[1:]:
        a = a + r[...].astype(F32)
    if has_z:
        a = a * _silu(z_ref[...])
    y = x_ref[...] + jnp.dot(a.astype(BF16), w_ref[...], preferred_element_type=F32)
    if has_g:
        y = _rms(y, g_ref[...])
    o_ref[...] = y


def _out_proj(x2d, a_list, z, w, final_g=None):
    n, d = x2d.shape
    c = w.shape[0]
    tm = min(ROW_TILE, n)
    row = lambda i: (i, 0)
    fixed = lambda i: (0, 0)
    in_specs = [pl.BlockSpec((tm, d), row)] + [pl.BlockSpec((tm, c), row) for _ in a_list]
    args = [x2d] + list(a_list)
    if z is not None:
        in_specs.append(pl.BlockSpec((tm, c), row))
        args.append(z)
    in_specs.append(pl.BlockSpec((c, d), fixed))
    args.append(w.astype(BF16))
    if final_g is not None:
        in_specs.append(pl.BlockSpec((1, d), fixed))
        args.append(final_g.reshape(1, d))
    kern = functools.partial(_out_proj_kernel, n_a=len(a_list), has_z=z is not None,
                             has_g=final_g is not None)
    return pl.pallas_call(
        kern, grid=(n // tm,), in_specs=in_specs, out_specs=pl.BlockSpec((tm, d), row),
        out_shape=jax.ShapeDtypeStruct((n, d), F32), compiler_params=_cparams("parallel"),
        name="out_proj",
    )(*args)


def _t5_bucket(dist):
    max_exact = T5_BUCKETS // 2
    d = jnp.maximum(dist, 0)
    df = jnp.maximum(d, 1).astype(F32)
    large = max_exact + (jnp.log(df / max_exact) / math.log(T5_MAX_DIST / max_exact)
                         * (T5_BUCKETS - max_exact)).astype(jnp.int32)
    large = jnp.minimum(large, T5_BUCKETS - 1)
    return jnp.where(d < max_exact, d, large)


def _band_bias(t5_table, q_rows, kc, offset):
    period = kc + q_rows
    j = jnp.arange(period)
    j = jnp.where(j >= kc, j - period, j)
    vec = jnp.take(t5_table, _t5_bucket(offset - j), axis=0).astype(F32).T
    heads = vec.shape[0]
    flat = jnp.tile(vec, (1, q_rows))[:, :q_rows * (period - 1)]
    return flat.reshape(heads, q_rows, period - 1)[:, :, :kc]


def _band_attn_kernel(*refs, nprev, window, has_sink, gate_col):
    q_ref, k_ref, vt_ref, bias_ref = refs[:4]
    pos = 4
    sink_ref = refs[pos] if has_sink else None
    pos += int(has_sink)
    gate_ref = refs[pos] if gate_col is not None else None
    o_ref = refs[-1]
    n = pl.program_id(1)
    tq = Q_BLOCK
    kc = (nprev + 1) * tq
    kj = lax.broadcasted_iota(jnp.int32, (kc, tq), 0)
    qi = lax.broadcasted_iota(jnp.int32, (kc, tq), 1)
    dist = nprev * tq + qi - kj
    kpos = (n - nprev) * tq + kj
    valid = (dist >= 0) & (dist < window) & (kpos >= 0)
    valid = jnp.concatenate([valid] * GQA, axis=1)
    starts = [pl.multiple_of(jnp.maximum(n - nprev + j, 0) * tq, tq) for j in range(nprev + 1)]
    groups = range(KV_HEADS)
    scores = []
    for g in groups:
        kg = jnp.concatenate([k_ref[0, g, pl.ds(s, tq), :] for s in starts], axis=0)
        qg = q_ref[0, GQA * g:GQA * (g + 1)].reshape(GQA * tq, HEAD_DIM) * (HEAD_DIM ** -0.5)
        scores.append(_bdot_nt(kg, qg))
    probs, maxes = [], []
    for g in groups:
        s = jnp.where(valid, scores[g] + bias_ref[g], NEG)
        m = jnp.max(s, axis=0, keepdims=True)
        if has_sink:
            m = jnp.maximum(m, sink_ref[g])
        probs.append(jnp.exp(s - m).astype(BF16))
        maxes.append(m)
    accs = []
    ones = jnp.ones((HEAD_DIM, kc), BF16)
    for g in groups:
        vtg = jnp.concatenate([vt_ref[0, g * HEAD_DIM:(g + 1) * HEAD_DIM, pl.ds(s, tq)] for s in starts],
                              axis=1)
        vtg = jnp.concatenate([vtg, ones], axis=0)
        accs.append(jnp.dot(vtg, probs[g], preferred_element_type=F32))
    for g in groups:
        l = accs[g][HEAD_DIM:HEAD_DIM + 1]
        if has_sink:
            l = l + jnp.exp(sink_ref[g] - maxes[g])
        o_t = accs[g][:HEAD_DIM] / l
        outs = []
        for r in range(GQA):
            oh = o_t[:, r * tq:(r + 1) * tq]
            if gate_col is not None:
                oh = oh * _gate_row(gate_ref, gate_col + GQA * g + r)
            outs.append(oh.T)
        o_ref[0, :, g * GQA * HEAD_DIM:(g + 1) * GQA * HEAD_DIM] = jnp.concatenate(outs, axis=-1)


def _band_attn(q, k, vt, t5_table, window, sinks=None, gates_t=None, gate_col=None):
    b, _, t, _ = q.shape
    nprev = -(-window // Q_BLOCK)
    kc = (nprev + 1) * Q_BLOCK
    width = GQA * Q_BLOCK
    bias = _band_bias(t5_table, Q_BLOCK, kc, nprev * Q_BLOCK)
    in_specs = [
        pl.BlockSpec((1, ATTN_HEADS, Q_BLOCK, HEAD_DIM), lambda i, n: (i, 0, n, 0)),
        pl.BlockSpec((1, KV_HEADS, t, HEAD_DIM), lambda i, n: (i, 0, 0, 0)),
        pl.BlockSpec((1, KV_HEADS * HEAD_DIM, t), lambda i, n: (i, 0, 0)),
        pl.BlockSpec((KV_HEADS, kc, width), lambda i, n: (0, 0, 0)),
    ]
    bias_t = bias.reshape(KV_HEADS, GQA, Q_BLOCK, kc).transpose(0, 3, 1, 2).reshape(KV_HEADS, kc, width)
    args = [q, k, vt, bias_t]
    if sinks is not None:
        in_specs.append(pl.BlockSpec((KV_HEADS, 1, width), lambda i, n: (0, 0, 0)))
        sk = jnp.broadcast_to(sinks.astype(F32).reshape(KV_HEADS, GQA, 1), (KV_HEADS, GQA, Q_BLOCK))
        args.append(sk.reshape(KV_HEADS, 1, width))
    if gates_t is not None:
        in_specs.append(pl.BlockSpec((1, LANES, Q_BLOCK), lambda i, n: (i, 0, n)))
        args.append(gates_t)
    kern = functools.partial(_band_attn_kernel, nprev=nprev, window=window,
                             has_sink=sinks is not None, gate_col=gate_col)
    d = ATTN_HEADS * HEAD_DIM
    return pl.pallas_call(
        kern, grid=(b, t // Q_BLOCK), in_specs=in_specs,
        out_specs=pl.BlockSpec((1, Q_BLOCK, d), lambda i, n: (i, n, 0)),
        out_shape=jax.ShapeDtypeStruct((b, t, d), F32),
        compiler_params=_cparams("parallel", "parallel"), name="band_attn",
    )(*args)


def _swa_layer(x2d, g, w_in, sinks, w_out, t5_table, final_g, batch, seq):
    nq, nkv = ATTN_HEADS * HEAD_DIM, KV_HEADS * HEAD_DIM
    ws = [w_in[:, :nq], w_in[:, nq:nq + nkv], w_in[:, nq + nkv:nq + 2 * nkv], w_in[:, nq + 2 * nkv:]]
    q, k, vt, z = _norm_proj(x2d, g, ws, ["heads", "heads", "cols", "rows"], [BF16, BF16, BF16, F32],
                             batch, seq)
    o = _band_attn(q, k, vt, t5_table, SWA_WINDOW, sinks=sinks)
    return _out_proj(x2d, [o.reshape(x2d.shape)], z, w_out, final_g)


def _rwkv_pre_kernel(x_ref, xp_ref, g_ref, mu_ref, wr_ref, wk_ref, wv_ref, wz_ref,
                     w0_ref, w1_ref, w2_ref, a0_ref, a1_ref, a2_ref,
                     r_ref, k_ref, v_ref, z_ref, lw_ref, a_ref, *, tiles_per_seq):
    i = pl.program_id(0)
    g = g_ref[...]
    xn = _rms(x_ref[...], g)
    prev = _rms(xp_ref[...], g)[7:8]
    prev = jnp.where(i % tiles_per_seq == 0, 0.0, prev)
    row = lax.broadcasted_iota(jnp.int32, xn.shape, 0)
    xprev = jnp.where(row == 0, prev, pltpu.roll(xn, 1, 0))
    xx = xprev - xn
    lerp = lambda s: xn + xx * mu_ref[s:s + 1, :]
    r_ref[...] = _bdot(lerp(0), wr_ref[...])
    k_ref[...] = _bdot(lerp(1), wk_ref[...])
    v_ref[...] = _bdot(lerp(2), wv_ref[...])
    z_ref[...] = _bdot(lerp(3), wz_ref[...])
    wl = w0_ref[...] + _bdot(jnp.tanh(_bdot(lerp(4), w1_ref[...])), w2_ref[...])
    sp = jnp.maximum(-wl, 0.0) + jnp.log1p(jnp.exp(-jnp.abs(wl)))
    lw_ref[...] = -jnp.exp(-sp - 0.5)
    al = a0_ref[...] + _bdot(_bdot(lerp(5), a1_ref[...]), a2_ref[...])
    a_ref[...] = jax.nn.sigmoid(al)


def _rwkv_pre(x2d, g, mu, w_in, w0, w1, w2, a0, a1, a2, seq):
    n, d = x2d.shape
    c = w0.shape[0]
    tm = min(ROW_TILE, seq)
    tiles_per_seq = seq // tm
    row = lambda i: (i, 0)
    fixed = lambda i: (0, 0)
    ws = [w_in[:, s * c:(s + 1) * c].astype(BF16) for s in range(4)]
    in_specs = [
        pl.BlockSpec((tm, d), row),
        pl.BlockSpec((8, d), lambda i: (jnp.maximum(i * (tm // 8) - 1, 0), 0)),
        pl.BlockSpec((1, d), fixed), pl.BlockSpec((6, d), fixed),
    ] + [pl.BlockSpec((d, c), fixed)] * 4 + [
        pl.BlockSpec((1, c), fixed), pl.BlockSpec(w1.shape, fixed), pl.BlockSpec(w2.shape, fixed),
        pl.BlockSpec((1, c), fixed), pl.BlockSpec(a1.shape, fixed), pl.BlockSpec(a2.shape, fixed),
    ]
    out = jax.ShapeDtypeStruct((n, c), F32)
    kern = functools.partial(_rwkv_pre_kernel, tiles_per_seq=tiles_per_seq)
    return pl.pallas_call(
        kern, grid=(n // tm,), in_specs=in_specs, out_specs=[pl.BlockSpec((tm, c), row)] * 6,
        out_shape=[out] * 6, compiler_params=_cparams("parallel"), name="rwkv_pre",
    )(x2d, x2d, g.reshape(1, d), mu, *ws, w0.reshape(1, c), w1.astype(BF16), w2.astype(BF16),
      a0.reshape(1, c), a1.astype(BF16), a2.astype(BF16))


def _dot3_many(a_list, b_list, dot=_bdot, passes=3):
    if passes == 1:
        return [dot(a, b) for a, b in zip(a_list, b_list)]
    sa = [_split2(a) for a in a_list]
    sb = [_split2(b) for b in b_list]
    hh = [dot(x[0], y[0]) for x, y in zip(sa, sb)]
    hl = [dot(x[0], y[1]) for x, y in zip(sa, sb)]
    lh = [dot(x[1], y[0]) for x, y in zip(sa, sb)]
    return [p + (q + r) for p, q, r in zip(hh, hl, lh)]


def _lane_sums(x_list, ones):
    parts = [_split2(x) for x in x_list]
    hi = [jnp.dot(p[0], ones, preferred_element_type=F32) for p in parts]
    lo = [jnp.dot(p[1], ones, preferred_element_type=F32) for p in parts]
    return [a + b for a, b in zip(hi, lo)]


def _rwkv_chunk_ops(ats, rts, bts, kts, bhs, khs, vs, wls, lower_strict, lower_incl, eye, blockdiag):
    L = RWKV_CHUNK
    n = len(ats)
    n_ch = ats[0].shape[1]
    bks = [jnp.concatenate([bt, kt], axis=0) for bt, kt in zip(bts, kts)]
    ars = [jnp.concatenate([at, rt], axis=0) for at, rt in zip(ats, rts)]
    As = _dot3_many(ars, bks, dot=_bdot_nt, passes=RWKV_SCORE_PASSES)
    a_ab = [jnp.where(lower_strict, A[:L, :L], 0.0) for A in As]
    a_rb = [jnp.where(lower_incl, A[L:, :L], 0.0) for A in As]
    a_k = [jnp.concatenate([jnp.where(lower_strict, A[:L, L:], 0.0),
                            jnp.where(lower_incl, A[L:, L:], 0.0)], axis=0) for A in As]
    dblk = [jnp.where(blockdiag, x, 0.0) for x in a_ab]
    akv = _dot3_many(a_k, vs, passes=RWKV_VALUE_PASSES)
    inv = functools.partial(_dot3_many, passes=RWKV_INVERSE_PASSES)
    d2 = inv(dblk, dblk)
    res = inv([eye + d for d in dblk] + d2, [eye + d for d in d2] + d2)
    s4, d4 = res[:n], res[n:]
    res = inv(d4 + d4, s4 + d4)
    s8 = [s + x for s, x in zip(s4, res[:n])]
    d8 = res[n:]
    tdiag = [s + x for s, x in zip(s8, inv(d8, s8))]
    rhs = [jnp.concatenate([at, x[:L]], axis=1) for at, x in zip(ats, akv)]
    res = inv(tdiag + tdiag, rhs + [x - d for x, d in zip(a_ab, dblk)])
    xt, nt = res[:n], res[n:]
    res = inv(nt + nt, xt + nt)
    u = [x + y for x, y in zip(xt, res[:n])]
    nt2 = res[n:]
    pq = [x + y for x, y in zip(u, inv(nt2, u))]
    res = _dot3_many(a_rb, pq, passes=RWKV_VALUE_PASSES)
    gh = [jnp.concatenate([rt, x[L:]], axis=1) + y for rt, x, y in zip(rts, akv, res)]
    res = _dot3_many(bhs + khs, pq + vs, dot=_bdot_tn, passes=RWKV_VALUE_PASSES)
    out = []
    for i in range(n):
        mc = res[i]
        m_op = mc[:, :n_ch] + eye * wls[i]
        c_op = mc[:, n_ch:] + res[n + i]
        out.append((gh[i][:, :n_ch], gh[i][:, n_ch:], m_op, c_op))
    return out


def _rwkv_scan_kernel(r_ref, k_ref, v_ref, z_ref, lw_ref, a_ref, kk_ref, ka_ref, rk_ref,
                      lnw_ref, lnb_ref, o_ref, state_ref):
    tb = pl.program_id(2)

    @pl.when(tb == 0)
    def _():
        state_ref[...] = jnp.zeros_like(state_ref)

    L = RWKV_CHUNK
    N = RWKV_HEAD
    ri = lax.broadcasted_iota(jnp.int32, (L, L), 0)
    ci = lax.broadcasted_iota(jnp.int32, (L, L), 1)
    lower_strict = ri > ci
    lower_incl = ri >= ci
    tri_incl = jnp.where(lower_incl, 1.0, 0.0).astype(BF16)
    eye = jnp.where(ri == ci, 1.0, 0.0).astype(F32)
    blockdiag = (ri // 16) == (ci // 16)
    n_heads = r_ref.shape[2] // N
    n_chunks = r_ref.shape[1] // L
    width = n_heads * N
    hi_ = lax.broadcasted_iota(jnp.int32, (width, width), 0) // N
    hj_ = lax.broadcasted_iota(jnp.int32, (width, width), 1) // N
    head_ones = jnp.where(hi_ == hj_, 1.0, 0.0).astype(BF16)
    chunk_rows = [slice(c * L, (c + 1) * L) for c in range(n_chunks)]
    r_c = [r_ref[0, rows, :] for rows in chunk_rows]
    v_c = [v_ref[0, rows, :] for rows in chunk_rows]
    lw_c = [lw_ref[0, rows, :] for rows in chunk_rows]
    kk_c, kp_c, a_c = [], [], []
    for rows in chunk_rows:
        k = k_ref[0, rows, :]
        a = a_ref[0, rows, :]
        kk_c.append(k * kk_ref[...])
        kp_c.append(k * (1.0 + (a - 1.0) * ka_ref[...]))
        a_c.append(a)
    sums = _lane_sums([x * x for x in kk_c] + [r * kp * rk_ref[...] for r, kp in zip(r_c, kp_c)],
                      head_ones)
    bonus_c = [sums[n_chunks + c] * v_c[c] for c in range(n_chunks)]
    cum_c = []
    for c in range(n_chunks):
        hi = lw_c[c].astype(BF16)
        r1 = lw_c[c] - hi.astype(F32)
        mid = r1.astype(BF16)
        lo = (r1 - mid.astype(F32)).astype(BF16)
        cum_c.append([jnp.dot(tri_incl, p, preferred_element_type=F32) for p in (hi, mid, lo)])
    cum_c = [a + (b + c) for a, b, c in cum_c]
    slabs = []
    for c in range(n_chunks):
        kk = kk_c[c] / jnp.maximum(jnp.sqrt(sums[c]), 1e-12)
        cum = cum_c[c]
        w_inv = jnp.exp(-cum)
        cum_last = cum[L - 1:L, :]
        w_tail = jnp.exp(cum_last - cum)
        bb = kk * a_c[c]
        slabs.append((-kk * jnp.exp(cum - lw_c[c]), r_c[c] * jnp.exp(cum), bb * w_inv,
                      kp_c[c] * w_inv, bb * w_tail, kp_c[c] * w_tail, v_c[c], jnp.exp(cum_last)))
    probs = [(hh, c) for c in range(n_chunks) for hh in range(n_heads)]
    per_head = [[slabs[c][j][:, hh * N:(hh + 1) * N] for hh, c in probs] for j in range(8)]
    ops = _rwkv_chunk_ops(*per_head, lower_strict, lower_incl, eye, blockdiag)
    states = [state_ref[hh] for hh in range(n_heads)]
    ys = {}
    for c in range(n_chunks):
        idx = [c * n_heads + hh for hh in range(n_heads)]
        upd = _dot3_many([jnp.concatenate([ops[i][0], ops[i][2]], axis=0) for i in idx], states,
                         passes=RWKV_STATE_PASSES)
        for hh, i in enumerate(idx):
            ys[(hh, c)] = upd[hh][:L] + ops[i][1]
            states[hh] = upd[hh][L:] + ops[i][3]
    for hh in range(n_heads):
        state_ref[hh] = states[hh]
    y_c = [jnp.concatenate([ys[(hh, c)] for hh in range(n_heads)], axis=1) for c in range(n_chunks)]
    yc_c = [y - s * (1.0 / N) for y, s in zip(y_c, _lane_sums(y_c, head_ones))]
    var_c = [s * (1.0 / N) for s in _lane_sums([yc * yc for yc in yc_c], head_ones)]
    for c, rows in enumerate(chunk_rows):
        yn = yc_c[c] * lax.rsqrt(var_c[c] + RWKV_GN_EPS) * lnw_ref[...] + lnb_ref[...]
        o_ref[0, rows, :] = (yn + bonus_c[c]) * _silu(z_ref[0, rows, :])


def _rwkv_scan(r, k, v, z, lw, a, k_k, k_a, r_k, lnx_w, lnx_b):
    b, t, c = r.shape
    tb = min(RWKV_BLOCK, t)
    seq_spec = pl.BlockSpec((1, tb, RWKV_LANES), lambda i, p, j: (i, j, p))
    par_spec = pl.BlockSpec((1, RWKV_LANES), lambda i, p, j: (0, p))
    params = [x.reshape(1, c).astype(F32) for x in (k_k, k_a, r_k, lnx_w, lnx_b)]
    return pl.pallas_call(
        _rwkv_scan_kernel, grid=(b, c // RWKV_LANES, t // tb),
        in_specs=[seq_spec] * 6 + [par_spec] * 5, out_specs=seq_spec,
        out_shape=jax.ShapeDtypeStruct((b, t, c), F32),
        scratch_shapes=[pltpu.VMEM((RWKV_LANES // RWKV_HEAD, RWKV_HEAD, RWKV_HEAD), F32)],
        compiler_params=_cparams("parallel", "parallel", "arbitrary"), name="rwkv_scan",
    )(r, k, v, z, lw, a, *params)


def _rwkv_layer(x2d, g, mu, w_in, w0, w1, w2, a0, a1, a2, k_k, k_a, r_k, lnx_w, lnx_b, w_out,
                final_g, batch, seq):
    c = w0.shape[0]
    r, k, v, z, lw, a = _rwkv_pre(x2d, g, mu, w_in, w0, w1, w2, a0, a1, a2, seq)
    sh = (batch, seq, c)
    y = _rwkv_scan(r.reshape(sh), k.reshape(sh), v.reshape(sh), z.reshape(sh), lw.reshape(sh),
                   a.reshape(sh), k_k, k_a, r_k, lnx_w, lnx_b)
    return _out_proj(x2d, [y.reshape(x2d.shape[0], c)], None, w_out, final_g)


def _nsa_compress_kernel(u_ref, posa_ref, posb_ref, w1a_ref, w1b_ref, w2_ref, o_ref):
    u = u_ref[0, 0]
    ha = _bdot(u + posa_ref[...], w1a_ref[...])
    hb = _bdot(u + posb_ref[...], w1b_ref[...])
    h = ha + pltpu.roll(hb, hb.shape[0] - 1, 0)
    o_ref[0, 0] = _bdot(_silu(h), w2_ref[...])


def _nsa_compress(t_hm, pos, w1, w2):
    b, g, t, dh = t_hm.shape
    nch = t // NSA_CMP_STRIDE
    half = NSA_CMP_STRIDE * dh
    u = t_hm.reshape(b, g, nch, half)
    posf = pos.astype(F32).reshape(2, 1, half)
    hid = w1.shape[1]
    fixed = lambda i, j: (0, 0)
    return pl.pallas_call(
        _nsa_compress_kernel, grid=(b, g),
        in_specs=[pl.BlockSpec((1, 1, nch, half), lambda i, j: (i, j, 0, 0)),
                  pl.BlockSpec((1, half), fixed), pl.BlockSpec((1, half), fixed),
                  pl.BlockSpec((half, hid), fixed), pl.BlockSpec((half, hid), fixed),
                  pl.BlockSpec((hid, dh), fixed)],
        out_specs=pl.BlockSpec((1, 1, nch, dh), lambda i, j: (i, j, 0, 0)),
        out_shape=jax.ShapeDtypeStruct((b, g, nch, dh), F32),
        compiler_params=_cparams("parallel", "parallel"), name="nsa_compress",
    )(u, posf[0], posf[1], w1[:half].astype(BF16), w1[half:].astype(BF16), w2.astype(BF16))


def _nsa_cmp_attn_kernel(q_ref, kc_ref, vct_ref, gate_ref, o_ref, sel_ref, *, n_sel):
    g = pl.program_id(1)
    n = pl.program_id(2)
    tq = Q_BLOCK
    ncmp = kc_ref.shape[2]
    q = q_ref[0].reshape(GQA * tq, HEAD_DIM) * (HEAD_DIM ** -0.5)
    s = _bdot_nt(kc_ref[0, 0], q)
    cend = lax.broadcasted_iota(jnp.int32, (ncmp, tq), 0) * NSA_CMP_STRIDE + (NSA_CMP_LEN - 1)
    tpos = n * tq + lax.broadcasted_iota(jnp.int32, (ncmp, tq), 1)
    ok = jnp.concatenate([cend <= tpos] * GQA, axis=1)
    s = jnp.where(ok, s, NEG)
    m = jnp.max(s, axis=0, keepdims=True)
    e = jnp.where(ok, jnp.exp(s - m), 0.0)
    l = jnp.sum(e, axis=0, keepdims=True)
    p = e / jnp.where(l > 0.0, l, 1.0)
    o_t = jnp.dot(vct_ref[0, 0].astype(BF16), p.astype(BF16), preferred_element_type=F32)
    outs = []
    for r in range(GQA):
        outs.append((o_t[:, r * tq:(r + 1) * tq] * _gate_row(gate_ref, GQA * g + r)).T)
    o_ref[0] = jnp.concatenate(outs, axis=-1)

    psum = p[:, :tq]
    for r in range(1, GQA):
        psum = psum + p[:, r * tq:(r + 1) * tq]
    si = lax.broadcasted_iota(jnp.int32, (n_sel, ncmp), 0)
    ni = lax.broadcasted_iota(jnp.int32, (n_sel, ncmp), 1)
    ratio = NSA_SEL_LEN // NSA_CMP_STRIDE
    overlap = ((ni < ratio * (si + 1)) & (ni * NSA_CMP_STRIDE + NSA_CMP_LEN - 1 >= si * NSA_SEL_LEN))
    overlap = jnp.where(overlap, 1.0, 0.0).astype(BF16)
    imp = _dot_exact_rhs(psum, overlap, dot=lambda a, b: _bdot(b, a))
    blk = lax.broadcasted_iota(jnp.int32, (n_sel, tq), 0)
    tq_pos = n * tq + lax.broadcasted_iota(jnp.int32, (n_sel, tq), 1)
    cur = tq_pos // NSA_SEL_LEN
    forced = (blk == 0) | (blk == cur) | (blk == cur - 1)
    future = blk * NSA_SEL_LEN > tq_pos
    imp = jnp.where(forced, BIG, jnp.where(future, NEG, imp))
    n_grp = n_sel // SUBLANES
    imp_g = [imp[SUBLANES * b:SUBLANES * (b + 1)] for b in range(n_grp)]
    blk_g = blk[:SUBLANES]
    ranks = [jnp.zeros((SUBLANES, tq), F32) for _ in range(n_grp)]
    for j in range(n_sel):
        row = imp[j:j + 1, :]
        for b in range(n_grp):
            if b > j // SUBLANES:
                ahead = row >= imp_g[b]
            elif b < j // SUBLANES:
                ahead = row > imp_g[b]
            else:
                ahead = (row > imp_g[b]) | ((row == imp_g[b]) & (blk_g > j % SUBLANES))
            ranks[b] = ranks[b] + jnp.where(ahead, 1.0, 0.0)
    rank = jnp.concatenate(ranks, axis=0)
    sel = jnp.where(rank < float(min(NSA_TOPK, n_sel)), 1.0, 0.0)
    sel_ref[0, 0] = sel


def _nsa_cmp_attn(q, k_cmp, v_cmp, gates_t):
    b, _, t, _ = q.shape
    ncmp = k_cmp.shape[2]
    n_sel = t // NSA_SEL_LEN
    d = ATTN_HEADS * HEAD_DIM
    kern = functools.partial(_nsa_cmp_attn_kernel, n_sel=n_sel)
    cmp_spec = pl.BlockSpec((1, 1, ncmp, HEAD_DIM), lambda i, g, n: (i, g, 0, 0))
    return pl.pallas_call(
        kern, grid=(b, KV_HEADS, t // Q_BLOCK),
        in_specs=[pl.BlockSpec((1, GQA, Q_BLOCK, HEAD_DIM), lambda i, g, n: (i, g, n, 0)),
                  cmp_spec, pl.BlockSpec((1, 1, HEAD_DIM, ncmp), lambda i, g, n: (i, g, 0, 0)),
                  pl.BlockSpec((1, LANES, Q_BLOCK), lambda i, g, n: (i, 0, n))],
        out_specs=[pl.BlockSpec((1, Q_BLOCK, GQA * HEAD_DIM), lambda i, g, n: (i, n, g)),
                   pl.BlockSpec((1, 1, n_sel, Q_BLOCK), lambda i, g, n: (i, g, 0, n))],
        out_shape=[jax.ShapeDtypeStruct((b, t, d), F32),
                   jax.ShapeDtypeStruct((b, KV_HEADS, n_sel, t), F32)],
        compiler_params=_cparams("parallel", "parallel", "parallel"), name="nsa_cmp_attn",
    )(q, k_cmp, jnp.swapaxes(v_cmp, 2, 3), gates_t)


def _nsa_sel_attn_kernel(q_ref, k_ref, vt_ref, selt_ref, bias_ref, far_ref, gate_ref, o_ref):
    g = pl.program_id(1)
    n = pl.program_id(2)
    tq = q_ref.shape[2]
    pw = Q_BLOCK
    bl = NSA_SEL_LEN
    fk = NSA_FAR_KEYS
    q_all = q_ref[0].reshape(GQA * tq, HEAD_DIM) * (HEAD_DIM ** -0.5)

    def block_rows(first_block, count, limit):
        rows = []
        for i in range(count):
            blk = first_block + i
            row = selt_ref[0, 0, pl.ds(jnp.clip(blk, 0, jnp.maximum(limit - 1, 0)), 1), :]
            rows.append(jnp.broadcast_to(jnp.where((blk >= 0) & (blk < limit), row, 0.0), (bl, tq)))
        return jnp.concatenate(rows, axis=0) > 0.5

    def values_t(*starts_widths):
        vt = jnp.concatenate([vt_ref[0, :, pl.ds(s, w)] for s, w in starts_widths], axis=1)
        return jnp.concatenate([vt, jnp.ones_like(vt)], axis=0)

    prev = pl.multiple_of(jnp.maximum(n * tq - pw, 0), pw)
    diag = pl.multiple_of(n * tq, tq)
    kt = jnp.concatenate([k_ref[0, 0, pl.ds(prev, pw), :], k_ref[0, 0, pl.ds(diag, tq), :]], axis=0)
    vt = values_t((prev, pw), (diag, tq))
    kj = lax.broadcasted_iota(jnp.int32, (pw + tq, tq), 0)
    qi = lax.broadcasted_iota(jnp.int32, (pw + tq, tq), 1)
    n_blocks = (n + 1) * (tq // bl)
    first_near = n * (tq // bl) - pw // bl
    ok = block_rows(first_near, (pw + tq) // bl, n_blocks) & (pw + qi - kj >= 0)
    ok = jnp.concatenate([ok] * GQA, axis=1)
    s = jnp.where(ok, _bdot_nt(kt, q_all) + bias_ref[0], NEG)
    m = jnp.max(s, axis=0, keepdims=True)
    p = jnp.exp(s - m).astype(BF16)
    acc = jnp.dot(vt, p, preferred_element_type=F32)
    m = m - far_ref[0]
    far_blocks = jnp.maximum(first_near, 0)

    def body(c, carry):
        m, acc = carry
        start = pl.multiple_of(c * fk, fk)
        ok = block_rows(c * (fk // bl), fk // bl, far_blocks)
        ok = jnp.concatenate([ok] * GQA, axis=1)
        s = jnp.where(ok, _bdot_nt(k_ref[0, 0, pl.ds(start, fk), :], q_all), NEG)
        m_new = jnp.maximum(m, jnp.max(s, axis=0, keepdims=True))
        p = jnp.exp(s - m_new).astype(BF16)
        pv = jnp.dot(values_t((start, fk)), p, preferred_element_type=F32)
        return m_new, jnp.exp(m - m_new) * acc + pv

    n_far = (far_blocks * bl + fk - 1) // fk
    _, acc = lax.fori_loop(0, n_far, body, (m, acc))
    o_t = acc[:HEAD_DIM] / acc[HEAD_DIM:HEAD_DIM + 1]
    outs = []
    for r in range(GQA):
        o = o_t[:, r * tq:(r + 1) * tq] * _gate_row(gate_ref, ATTN_HEADS + GQA * g + r)
        outs.append(o.T)
    o_ref[0] = jnp.concatenate(outs, axis=-1)


def _nsa_sel_attn(q, ks, vst, sel_t, t5_table, gates_t):
    b, _, t, _ = q.shape
    n_sel = t // NSA_SEL_LEN
    d = ATTN_HEADS * HEAD_DIM
    tq = min(NSA_SEL_QUERIES, t)
    near = Q_BLOCK + tq
    width = GQA * tq
    assert t % NSA_FAR_KEYS == 0 or t <= near
    bias_near = _band_bias(t5_table, tq, near, Q_BLOCK).reshape(KV_HEADS, GQA, tq, near)
    bias_near = bias_near.transpose(0, 3, 1, 2).reshape(KV_HEADS, near, width)
    far = t5_table[T5_BUCKETS - 1].astype(F32).reshape(KV_HEADS, GQA, 1)
    far = jnp.broadcast_to(far, (KV_HEADS, GQA, tq)).reshape(KV_HEADS, 1, width)
    return pl.pallas_call(
        _nsa_sel_attn_kernel, grid=(b, KV_HEADS, t // tq),
        in_specs=[pl.BlockSpec((1, GQA, tq, HEAD_DIM), lambda i, g, n: (i, g, n, 0)),
                  pl.BlockSpec((1, 1, t, HEAD_DIM), lambda i, g, n: (i, g, 0, 0)),
                  pl.BlockSpec((1, HEAD_DIM, t), lambda i, g, n: (i, g, 0)),
                  pl.BlockSpec((1, 1, n_sel, tq), lambda i, g, n: (i, g, 0, n)),
                  pl.BlockSpec((1, near, width), lambda i, g, n: (g, 0, 0)),
                  pl.BlockSpec((1, 1, width), lambda i, g, n: (g, 0, 0)),
                  pl.BlockSpec((1, LANES, tq), lambda i, g, n: (i, 0, n))],
        out_specs=pl.BlockSpec((1, tq, GQA * HEAD_DIM), lambda i, g, n: (i, n, g)),
        out_shape=jax.ShapeDtypeStruct((b, t, d), F32),
        compiler_params=_cparams("parallel", "parallel", "parallel"), name="nsa_sel_attn",
    )(q, ks, vst, sel_t, bias_near, far, gates_t)


def _nsa_layer(x2d, g, w_in, pos_k, k_w1, k_w2, pos_v, v_w1, v_w2, w_out, t5_table, final_g,
               batch, seq):
    nq, nkv = ATTN_HEADS * HEAD_DIM, KV_HEADS * HEAD_DIM
    offs = [0, nq] + [nq + nkv * (i + 1) for i in range(6)]
    ws = [w_in[:, offs[i]:offs[i + 1]] for i in range(7)]
    n_gate = 3 * ATTN_HEADS
    wg = jnp.pad(w_in[:, offs[7]:offs[7] + n_gate], ((0, 0), (0, LANES - n_gate)))
    wz = w_in[:, offs[7] + n_gate:]
    q, kc, vc, ks, vst, kw, vwt, gates_t, z = _norm_proj(
        x2d, g, ws + [wg, wz],
        ["heads", "heads", "heads", "heads", "cols", "heads", "cols", "cols", "rows"],
        [BF16, F32, F32, BF16, BF16, BF16, BF16, F32, F32], batch, seq)
    k_cmp = _nsa_compress(kc, pos_k, k_w1, k_w2)
    v_cmp = _nsa_compress(vc, pos_v, v_w1, v_w2)
    o_cmp, sel_t = _nsa_cmp_attn(q, k_cmp, v_cmp, gates_t)
    o_sel = _nsa_sel_attn(q, ks, vst, sel_t, t5_table, gates_t)
    o_win = _band_attn(q, kw, vwt, t5_table, NSA_WINDOW, gates_t=gates_t, gate_col=2 * ATTN_HEADS)
    sh = x2d.shape
    return _out_proj(x2d, [o_cmp.reshape(sh), o_sel.reshape(sh), o_win.reshape(sh)], z, w_out,
                     final_g)


def _lru_kernel(x_ref, g_ref, win_u_ref, win_z_ref, cw_ref, cb_ref, wa_ref, ba_ref, wx_ref, bx_ref,
                sp_ref, wout_ref, *rest, half, has_fg):
    fg_ref = rest[0] if has_fg else None
    o_ref, tail_ref, h_ref = rest[-3:]
    j = pl.program_id(1)

    @pl.when(j == 0)
    def _():
        tail_ref[...] = jnp.zeros_like(tail_ref)
        h_ref[...] = jnp.zeros_like(h_ref)

    x = x_ref[...]
    xb = _rms(x, g_ref[...]).astype(BF16)
    u = jnp.dot(xb, win_u_ref[...], preferred_element_type=F32)
    z = jnp.dot(xb, win_z_ref[...], preferred_element_type=F32)
    tm, width = u.shape
    ext = jnp.concatenate([tail_ref[...], u], axis=0)
    tail_ref[...] = u[tm - 8:, :]
    uc = cb_ref[...] + cw_ref[CONV_WIDTH - 1:CONV_WIDTH, :] * u
    for s in range(1, CONV_WIDTH):
        uc = uc + cw_ref[CONV_WIDTH - 1 - s:CONV_WIDTH - s, :] * ext[8 - s:8 - s + tm, :]
    ucb = uc.astype(BF16)
    gr, gi = [], []
    for c in range(width // half):
        blk = ucb[:, c * half:(c + 1) * half]
        gr.append(jnp.dot(blk, wa_ref[c], preferred_element_type=F32))
        gi.append(jnp.dot(blk, wx_ref[c], preferred_element_type=F32))
    rg = jax.nn.sigmoid(jnp.concatenate(gr, axis=1) + ba_ref[...])
    ig = jax.nn.sigmoid(jnp.concatenate(gi, axis=1) + bx_ref[...])
    log_a = -LRU_C * rg * sp_ref[...]
    a = jnp.exp(log_a)
    bv = jnp.sqrt(1.0 - a * a) * (ig * uc)
    row = lax.broadcasted_iota(jnp.int32, (tm, width), 0) % SUBLANES
    sh = 1
    while sh < SUBLANES:
        a_s = jnp.where(row >= sh, pltpu.roll(a, sh, 0), 1.0)
        b_s = jnp.where(row >= sh, pltpu.roll(bv, sh, 0), 0.0)
        bv = a * b_s + bv
        a = a * a_s
        sh *= 2
    carry = h_ref[...]
    groups = []
    for i in range(tm // SUBLANES):
        rows = slice(i * SUBLANES, (i + 1) * SUBLANES)
        hg = bv[rows] + a[rows] * carry
        carry = hg[SUBLANES - 1:SUBLANES, :]
        groups.append(hg)
    h_ref[...] = carry
    gated = (jnp.concatenate(groups, axis=0) * _silu(z)).astype(BF16)
    y = x + jnp.dot(gated, wout_ref[...], preferred_element_type=F32)
    o_ref[...] = _rms(y, fg_ref[...]) if has_fg else y


def _block_diag(w, group):
    nb, n, _ = w.shape
    w = w.reshape(nb // group, group, n, n)
    eye = jnp.eye(group, dtype=w.dtype)
    return jnp.einsum('cgij,gh->cgihj', w, eye).reshape(nb // group, group * n, group * n)


def _lru_layer(x2d, g, w_in, conv_w, conv_b, ga_w, ga_b, gx_w, gx_b, lam, w_out, final_g, batch, seq):
    n, d = x2d.shape
    width = w_in.shape[1] // 2
    blk = ga_w.shape[1]
    group = LANES // math.gcd(blk, LANES)
    group = min(group, ga_w.shape[0])
    half = group * blk
    nsup = width // half
    tm = min(LRU_TILE, seq)
    tiles = seq // tm
    row = lambda i, j: (i * tiles + j, 0)
    fixed = lambda i, j: (0, 0)
    fixed3 = lambda i, j: (0, 0, 0)
    vec = lambda a: a.astype(F32).reshape(1, -1)
    softplus_neg_lam = jax.nn.softplus(-lam.astype(F32))
    has_fg = final_g is not None
    in_specs = [pl.BlockSpec((tm, d), row), pl.BlockSpec((1, d), fixed),
                pl.BlockSpec((d, width), fixed), pl.BlockSpec((d, width), fixed),
                pl.BlockSpec((CONV_WIDTH, width), fixed), pl.BlockSpec((1, width), fixed),
                pl.BlockSpec((nsup, half, half), fixed3), pl.BlockSpec((1, width), fixed),
                pl.BlockSpec((nsup, half, half), fixed3), pl.BlockSpec((1, width), fixed),
                pl.BlockSpec((1, width), fixed), pl.BlockSpec((width, d), fixed)]
    args = [x2d, g.reshape(1, d), w_in[:, :width].astype(BF16), w_in[:, width:].astype(BF16),
            conv_w.astype(F32), vec(conv_b), _block_diag(ga_w, group).astype(BF16), vec(ga_b),
            _block_diag(gx_w, group).astype(BF16), vec(gx_b), vec(softplus_neg_lam),
            w_out.astype(BF16)]
    if has_fg:
        in_specs.append(pl.BlockSpec((1, d), fixed))
        args.append(final_g.reshape(1, d))
    kern = functools.partial(_lru_kernel, half=half, has_fg=has_fg)
    return pl.pallas_call(
        kern, grid=(batch, tiles), in_specs=in_specs, out_specs=pl.BlockSpec((tm, d), row),
        out_shape=jax.ShapeDtypeStruct((n, d), F32),
        scratch_shapes=[pltpu.VMEM((8, width), F32), pltpu.VMEM((1, width), F32)],
        compiler_params=_cparams("parallel", "arbitrary"), name="rglru_layer",
    )(*args)


def kernel(x, t5_table, norm_g, final_g, a_w_in, a_sinks, a_w_out, b_mu, b_w_in, b_w0, b_w1, b_w2, b_a0, b_a1, b_a2, b_k_k, b_k_a, b_r_k, b_lnx_w, b_lnx_b, b_w_out, c_w_in, c_cmp_pos_k, c_cmp_k_w1, c_cmp_k_w2, c_cmp_pos_v, c_cmp_v_w1, c_cmp_v_w2, c_w_out, d_w_in, d_conv_w, d_conv_b, d_gate_a_w, d_gate_a_b, d_gate_x_w, d_gate_x_b, d_lambda, d_w_out):
    batch, seq, d = x.shape
    depth = norm_g.shape[0]
    h = x.reshape(batch * seq, d)
    for layer in range(depth):
        m, j = layer % 4, layer // 4
        g = norm_g[layer]
        fg = final_g if layer == depth - 1 else None
        if m == 0:
            h = _swa_layer(h, g, a_w_in[j], a_sinks[j], a_w_out[j], t5_table, fg, batch, seq)
        elif m == 1:
            h = _rwkv_layer(h, g, b_mu[j], b_w_in[j], b_w0[j], b_w1[j], b_w2[j], b_a0[j], b_a1[j],
                            b_a2[j], b_k_k[j], b_k_a[j], b_r_k[j], b_lnx_w[j], b_lnx_b[j],
                            b_w_out[j], fg, batch, seq)
        elif m == 2:
            h = _nsa_layer(h, g, c_w_in[j], c_cmp_pos_k[j], c_cmp_k_w1[j], c_cmp_k_w2[j],
                           c_cmp_pos_v[j], c_cmp_v_w1[j], c_cmp_v_w2[j], c_w_out[j], t5_table,
                           fg, batch, seq)
        else:
            h = _lru_layer(h, g, d_w_in[j], d_conv_w[j], d_conv_b[j], d_gate_a_w[j], d_gate_a_b[j],
                           d_gate_x_w[j], d_gate_x_b[j], d_lambda[j], d_w_out[j], fg, batch, seq)
    return h.reshape(batch, seq, d)
```

```python
import functools
import math

import jax
import jax.numpy as jnp
from jax import lax
from jax.experimental import pallas as pl
from jax.experimental.pallas import tpu as pltpu

F32 = jnp.float32
BF16 = jnp.bfloat16

EPS = 1e-6
NEG = -1e30
BIG = 1e30
T5_BUCKETS = 32
T5_MAX_DIST = 128
ATTN_HEADS = 16
HEAD_DIM = 64
KV_HEADS = 4
GQA = ATTN_HEADS // KV_HEADS
Q_BLOCK = 128
SWA_WINDOW = 128
RWKV_HEAD = 64
RWKV_GN_EPS = 64e-5
NSA_CMP_LEN = 32
NSA_CMP_STRIDE = 16
NSA_SEL_LEN = 64
NSA_TOPK = 16
NSA_WINDOW = 512
LRU_BLOCKS = 16
LRU_C = 8.0
CONV_WIDTH = 4

LANES = 128
SUBLANES = 8
VMEM_LIMIT = 56 * 1024 * 1024
ROW_TILE = 512
RWKV_CHUNK = 64
RWKV_BLOCK = 512
RWKV_LANES = 256
RWKV_SCORE_PASSES = 1
RWKV_VALUE_PASSES = 1
RWKV_INVERSE_PASSES = 1
RWKV_STATE_PASSES = 3
NSA_FAR_KEYS = 512
NSA_CMP_QUERIES = 128
NSA_SEL_QUERIES = 512
LRU_TILE = 256


def _cparams(*sem):
    return pltpu.CompilerParams(dimension_semantics=sem, vmem_limit_bytes=VMEM_LIMIT)


def _bdot(a, b):
    return jnp.dot(a.astype(BF16), b.astype(BF16), preferred_element_type=F32)


def _bdot_nt(a, b):
    return lax.dot_general(a.astype(BF16), b.astype(BF16), (((1,), (1,)), ((), ())),
                           preferred_element_type=F32)


def _bdot_tn(a, b):
    return lax.dot_general(a.astype(BF16), b.astype(BF16), (((0,), (0,)), ((), ())),
                           preferred_element_type=F32)


def _split2(x):
    hi = x.astype(BF16)
    lo = (x - hi.astype(F32)).astype(BF16)
    return hi, lo


def _dot3(a, b, dot=_bdot):
    ah, al = _split2(a)
    bh, bl = _split2(b)
    return dot(ah, bh) + (dot(ah, bl) + dot(al, bh))


def _dot_exact_rhs(a, b01, dot=_bdot):
    ah = a.astype(BF16)
    r1 = a - ah.astype(F32)
    am = r1.astype(BF16)
    al = (r1 - am.astype(F32)).astype(BF16)
    return dot(ah, b01) + (dot(am, b01) + dot(al, b01))


def _rms(x, g):
    return x * lax.rsqrt(jnp.mean(x * x, axis=-1, keepdims=True) + EPS) * g


def _silu(z):
    return z * jax.nn.sigmoid(z)


def _gate_row(gate_ref, c):
    return jax.nn.sigmoid(gate_ref[0, pl.ds(c, 1), :])


def _norm_proj_kernel(x_ref, g_ref, *refs, n_out, layouts):
    w_refs, o_refs = refs[:n_out], refs[n_out:]
    xb = _rms(x_ref[...], g_ref[...]).astype(BF16)
    for w_ref, o_ref, layout in zip(w_refs, o_refs, layouts):
        width = w_ref.shape[1]
        for c0 in range(0, width, 512):
            cw = min(512, width - c0)
            acc = jnp.dot(xb, w_ref[:, c0:c0 + cw], preferred_element_type=F32)
            if layout == "heads":
                for j in range(cw // HEAD_DIM):
                    o_ref[0, (c0 // HEAD_DIM) + j] = acc[:, j * HEAD_DIM:(j + 1) * HEAD_DIM].astype(o_ref.dtype)
            elif layout == "cols":
                o_ref[0, c0:c0 + cw, :] = acc.T.astype(o_ref.dtype)
            else:
                o_ref[:, c0:c0 + cw] = acc.astype(o_ref.dtype)


def _norm_proj(x2d, g, weights, layouts, dtypes, batch, seq):
    n, d = x2d.shape
    tm = min(ROW_TILE, seq)
    tiles_per_seq = seq // tm
    in_specs = [pl.BlockSpec((tm, d), lambda i: (i, 0)), pl.BlockSpec((1, d), lambda i: (0, 0))]
    out_specs, out_shapes = [], []
    for w, layout, dt in zip(weights, layouts, dtypes):
        width = w.shape[1]
        in_specs.append(pl.BlockSpec((d, width), lambda i: (0, 0)))
        if layout == "heads":
            nh = width // HEAD_DIM
            out_shapes.append(jax.ShapeDtypeStruct((batch, nh, seq, HEAD_DIM), dt))
            out_specs.append(pl.BlockSpec((1, nh, tm, HEAD_DIM),
                                          lambda i: (i // tiles_per_seq, 0, i % tiles_per_seq, 0)))
        elif layout == "cols":
            out_shapes.append(jax.ShapeDtypeStruct((batch, width, seq), dt))
            out_specs.append(pl.BlockSpec((1, width, tm),
                                          lambda i: (i // tiles_per_seq, 0, i % tiles_per_seq)))
        else:
            out_shapes.append(jax.ShapeDtypeStruct((n, width), dt))
            out_specs.append(pl.BlockSpec((tm, width), lambda i: (i, 0)))
    kern = functools.partial(_norm_proj_kernel, n_out=len(weights), layouts=tuple(layouts))
    return pl.pallas_call(
        kern, grid=(n // tm,), in_specs=in_specs, out_specs=out_specs, out_shape=out_shapes,
        compiler_params=_cparams("parallel"), name="norm_proj",
    )(x2d, g.reshape(1, d), *[w.astype(BF16) for w in weights])


def _out_proj_kernel(*refs, n_a, has_z, has_g):
    x_ref = refs[0]
    a_refs = refs[1:1 + n_a]
    pos = 1 + n_a
    z_ref = refs[pos] if has_z else None
    pos += int(has_z)
    w_ref = refs[pos]
    pos += 1
    g_ref = refs[pos] if has_g else None
    o_ref = refs[-1]
    a = a_refs[0][...].astype(F32)
    for r in a_refs[1:]:
        a = a + r[...].astype(F32)
    if has_z:
        a = a * _silu(z_ref[...])
    y = x_ref[...] + jnp.dot(a.astype(BF16), w_ref[...], preferred_element_type=F32)
    if has_g:
        y = _rms(y, g_ref[...])
    o_ref[...] = y


def _out_proj(x2d, a_list, z, w, final_g=None):
    n, d = x2d.shape
    c = w.shape[0]
    tm = min(ROW_TILE, n)
    row = lambda i: (i, 0)
    fixed = lambda i: (0, 0)
    in_specs = [pl.BlockSpec((tm, d), row)] + [pl.BlockSpec((tm, c), row) for _ in a_list]
    args = [x2d] + list(a_list)
    if z is not None:
        in_specs.append(pl.BlockSpec((tm, c), row))
        args.append(z)
    in_specs.append(pl.BlockSpec((c, d), fixed))
    args.append(w.astype(BF16))
    if final_g is not None:
        in_specs.append(pl.BlockSpec((1, d), fixed))
        args.append(final_g.reshape(1, d))
    kern = functools.partial(_out_proj_kernel, n_a=len(a_list), has_z=z is not None,
                             has_g=final_g is not None)
    return pl.pallas_call(
        kern, grid=(n // tm,), in_specs=in_specs, out_specs=pl.BlockSpec((tm, d), row),
        out_shape=jax.ShapeDtypeStruct((n, d), F32), compiler_params=_cparams("parallel"),
        name="out_proj",
    )(*args)


def _t5_bucket(dist):
    max_exact = T5_BUCKETS // 2
    d = jnp.maximum(dist, 0)
    df = jnp.maximum(d, 1).astype(F32)
    large = max_exact + (jnp.log(df / max_exact) / math.log(T5_MAX_DIST / max_exact)
                         * (T5_BUCKETS - max_exact)).astype(jnp.int32)
    large = jnp.minimum(large, T5_BUCKETS - 1)
    return jnp.where(d < max_exact, d, large)


def _band_bias(t5_table, q_rows, kc, offset):
    period = kc + q_rows
    j = jnp.arange(period)
    j = jnp.where(j >= kc, j - period, j)
    vec = jnp.take(t5_table, _t5_bucket(offset - j), axis=0).astype(F32).T
    heads = vec.shape[0]
    flat = jnp.tile(vec, (1, q_rows))[:, :q_rows * (period - 1)]
    return flat.reshape(heads, q_rows, period - 1)[:, :, :kc]


def _band_attn_kernel(*refs, nprev, window, has_sink, gate_col):
    q_ref, k_ref, vt_ref, bias_ref = refs[:4]
    pos = 4
    sink_ref = refs[pos] if has_sink else None
    pos += int(has_sink)
    gate_ref = refs[pos] if gate_col is not None else None
    o_ref = refs[-1]
    n = pl.program_id(1)
    tq = Q_BLOCK
    kc = (nprev + 1) * tq
    kj = lax.broadcasted_iota(jnp.int32, (kc, tq), 0)
    qi = lax.broadcasted_iota(jnp.int32, (kc, tq), 1)
    dist = nprev * tq + qi - kj
    kpos = (n - nprev) * tq + kj
    valid = (dist >= 0) & (dist < window) & (kpos >= 0)
    valid = jnp.concatenate([valid] * GQA, axis=1)
    starts = [pl.multiple_of(jnp.maximum(n - nprev + j, 0) * tq, tq) for j in range(nprev + 1)]
    groups = range(KV_HEADS)
    scores = []
    for g in groups:
        kg = jnp.concatenate([k_ref[0, g, pl.ds(s, tq), :] for s in starts], axis=0)
        qg = q_ref[0, GQA * g:GQA * (g + 1)].reshape(GQA * tq, HEAD_DIM) * (HEAD_DIM ** -0.5)
        scores.append(_bdot_nt(kg, qg))
    probs, maxes = [], []
    for g in groups:
        s = jnp.where(valid, scores[g] + bias_ref[g], NEG)
        m = jnp.max(s, axis=0, keepdims=True)
        if has_sink:
            m = jnp.maximum(m, sink_ref[g])
        probs.append(jnp.exp(s - m).astype(BF16))
        maxes.append(m)
    accs = []
    ones = jnp.ones((HEAD_DIM, kc), BF16)
    for g in groups:
        vtg = jnp.concatenate([vt_ref[0, g * HEAD_DIM:(g + 1) * HEAD_DIM, pl.ds(s, tq)] for s in starts],
                              axis=1)
        vtg = jnp.concatenate([vtg, ones], axis=0)
        accs.append(jnp.dot(vtg, probs[g], preferred_element_type=F32))
    for g in groups:
        l = accs[g][HEAD_DIM:HEAD_DIM + 1]
        if has_sink:
            l = l + jnp.exp(sink_ref[g] - maxes[g])
        o_t = accs[g][:HEAD_DIM] / l
        outs = []
        for r in range(GQA):
            oh = o_t[:, r * tq:(r + 1) * tq]
            if gate_col is not None:
                oh = oh * _gate_row(gate_ref, gate_col + GQA * g + r)
            outs.append(oh.T)
        o_ref[0, :, g * GQA * HEAD_DIM:(g + 1) * GQA * HEAD_DIM] = jnp.concatenate(outs, axis=-1)


def _band_attn(q, k, vt, t5_table, window, sinks=None, gates_t=None, gate_col=None):
    b, _, t, _ = q.shape
    nprev = -(-window // Q_BLOCK)
    kc = (nprev + 1) * Q_BLOCK
    width = GQA * Q_BLOCK
    bias = _band_bias(t5_table, Q_BLOCK, kc, nprev * Q_BLOCK)
    in_specs = [
        pl.BlockSpec((1, ATTN_HEADS, Q_BLOCK, HEAD_DIM), lambda i, n: (i, 0, n, 0)),
        pl.BlockSpec((1, KV_HEADS, t, HEAD_DIM), lambda i, n: (i, 0, 0, 0)),
        pl.BlockSpec((1, KV_HEADS * HEAD_DIM, t), lambda i, n: (i, 0, 0)),
        pl.BlockSpec((KV_HEADS, kc, width), lambda i, n: (0, 0, 0)),
    ]
    bias_t = bias.reshape(KV_HEADS, GQA, Q_BLOCK, kc).transpose(0, 3, 1, 2).reshape(KV_HEADS, kc, width)
    args = [q, k, vt, bias_t]
    if sinks is not None:
        in_specs.append(pl.BlockSpec((KV_HEADS, 1, width), lambda i, n: (0, 0, 0)))
        sk = jnp.broadcast_to(sinks.astype(F32).reshape(KV_HEADS, GQA, 1), (KV_HEADS, GQA, Q_BLOCK))
        args.append(sk.reshape(KV_HEADS, 1, width))
    if gates_t is not None:
        in_specs.append(pl.BlockSpec((1, LANES, Q_BLOCK), lambda i, n: (i, 0, n)))
        args.append(gates_t)
    kern = functools.partial(_band_attn_kernel, nprev=nprev, window=window,
                             has_sink=sinks is not None, gate_col=gate_col)
    d = ATTN_HEADS * HEAD_DIM
    return pl.pallas_call(
        kern, grid=(b, t // Q_BLOCK), in_specs=in_specs,
        out_specs=pl.BlockSpec((1, Q_BLOCK, d), lambda i, n: (i, n, 0)),
        out_shape=jax.ShapeDtypeStruct((b, t, d), F32),
        compiler_params=_cparams("parallel", "parallel"), name="band_attn",
    )(*args)


def _swa_layer(x2d, g, w_in, sinks, w_out, t5_table, final_g, batch, seq):
    nq, nkv = ATTN_HEADS * HEAD_DIM, KV_HEADS * HEAD_DIM
    ws = [w_in[:, :nq], w_in[:, nq:nq + nkv], w_in[:, nq + nkv:nq + 2 * nkv], w_in[:, nq + 2 * nkv:]]
    q, k, vt, z = _norm_proj(x2d, g, ws, ["heads", "heads", "cols", "rows"], [BF16, BF16, BF16, F32],
                             batch, seq)
    o = _band_attn(q, k, vt, t5_table, SWA_WINDOW, sinks=sinks)
    return _out_proj(x2d, [o.reshape(x2d.shape)], z, w_out, final_g)


def _rwkv_pre_kernel(x_ref, xp_ref, g_ref, mu_ref, wr_ref, wk_ref, wv_ref, wz_ref,
                     w0_ref, w1_ref, w2_ref, a0_ref, a1_ref, a2_ref,
                     r_ref, k_ref, v_ref, z_ref, lw_ref, a_ref, *, tiles_per_seq):
    i = pl.program_id(0)
    g = g_ref[...]
    xn = _rms(x_ref[...], g)
    prev = _rms(xp_ref[...], g)[7:8]
    prev = jnp.where(i % tiles_per_seq == 0, 0.0, prev)
    row = lax.broadcasted_iota(jnp.int32, xn.shape, 0)
    xprev = jnp.where(row == 0, prev, pltpu.roll(xn, 1, 0))
    xx = xprev - xn
    lerp = lambda s: xn + xx * mu_ref[s:s + 1, :]
    r_ref[...] = _bdot(lerp(0), wr_ref[...])
    k_ref[...] = _bdot(lerp(1), wk_ref[...])
    v_ref[...] = _bdot(lerp(2), wv_ref[...])
    z_ref[...] = _bdot(lerp(3), wz_ref[...])
    wl = w0_ref[...] + _bdot(jnp.tanh(_bdot(lerp(4), w1_ref[...])), w2_ref[...])
    sp = jnp.maximum(-wl, 0.0) + jnp.log1p(jnp.exp(-jnp.abs(wl)))
    lw_ref[...] = -jnp.exp(-sp - 0.5)
    al = a0_ref[...] + _bdot(_bdot(lerp(5), a1_ref[...]), a2_ref[...])
    a_ref[...] = jax.nn.sigmoid(al)


def _rwkv_pre(x2d, g, mu, w_in, w0, w1, w2, a0, a1, a2, seq):
    n, d = x2d.shape
    c = w0.shape[0]
    tm = min(ROW_TILE, seq)
    tiles_per_seq = seq // tm
    row = lambda i: (i, 0)
    fixed = lambda i: (0, 0)
    ws = [w_in[:, s * c:(s + 1) * c].astype(BF16) for s in range(4)]
    in_specs = [
        pl.BlockSpec((tm, d), row),
        pl.BlockSpec((8, d), lambda i: (jnp.maximum(i * (tm // 8) - 1, 0), 0)),
        pl.BlockSpec((1, d), fixed), pl.BlockSpec((6, d), fixed),
    ] + [pl.BlockSpec((d, c), fixed)] * 4 + [
        pl.BlockSpec((1, c), fixed), pl.BlockSpec(w1.shape, fixed), pl.BlockSpec(w2.shape, fixed),
        pl.BlockSpec((1, c), fixed), pl.BlockSpec(a1.shape, fixed), pl.BlockSpec(a2.shape, fixed),
    ]
    out = jax.ShapeDtypeStruct((n, c), F32)
    kern = functools.partial(_rwkv_pre_kernel, tiles_per_seq=tiles_per_seq)
    return pl.pallas_call(
        kern, grid=(n // tm,), in_specs=in_specs, out_specs=[pl.BlockSpec((tm, c), row)] * 6,
        out_shape=[out] * 6, compiler_params=_cparams("parallel"), name="rwkv_pre",
    )(x2d, x2d, g.reshape(1, d), mu, *ws, w0.reshape(1, c), w1.astype(BF16), w2.astype(BF16),
      a0.reshape(1, c), a1.astype(BF16), a2.astype(BF16))


def _dot3_many(a_list, b_list, dot=_bdot, passes=3):
    if passes == 1:
        return [dot(a, b) for a, b in zip(a_list, b_list)]
    sa = [_split2(a) for a in a_list]
    sb = [_split2(b) for b in b_list]
    hh = [dot(x[0], y[0]) for x, y in zip(sa, sb)]
    hl = [dot(x[0], y[1]) for x, y in zip(sa, sb)]
    lh = [dot(x[1], y[0]) for x, y in zip(sa, sb)]
    return [p + (q + r) for p, q, r in zip(hh, hl, lh)]


def _lane_sums(x_list, ones):
    parts = [_split2(x) for x in x_list]
    hi = [jnp.dot(p[0], ones, preferred_element_type=F32) for p in parts]
    lo = [jnp.dot(p[1], ones, preferred_element_type=F32) for p in parts]
    return [a + b for a, b in zip(hi, lo)]


def _rwkv_chunk_ops(ats, rts, bts, kts, bhs, khs, vs, wls, lower_strict, lower_incl, eye, blockdiag):
    L = RWKV_CHUNK
    n = len(ats)
    n_ch = ats[0].shape[1]
    bks = [jnp.concatenate([bt, kt], axis=0) for bt, kt in zip(bts, kts)]
    ars = [jnp.concatenate([at, rt], axis=0) for at, rt in zip(ats, rts)]
    As = _dot3_many(ars, bks, dot=_bdot_nt, passes=RWKV_SCORE_PASSES)
    a_ab = [jnp.where(lower_strict, A[:L, :L], 0.0) for A in As]
    a_rb = [jnp.where(lower_incl, A[L:, :L], 0.0) for A in As]
    a_k = [jnp.concatenate([jnp.where(lower_strict, A[:L, L:], 0.0),
                            jnp.where(lower_incl, A[L:, L:], 0.0)], axis=0) for A in As]
    dblk = [jnp.where(blockdiag, x, 0.0) for x in a_ab]
    akv = _dot3_many(a_k, vs, passes=RWKV_VALUE_PASSES)
    inv = functools.partial(_dot3_many, passes=RWKV_INVERSE_PASSES)
    d2 = inv(dblk, dblk)
    res = inv([eye + d for d in dblk] + d2, [eye + d for d in d2] + d2)
    s4, d4 = res[:n], res[n:]
    res = inv(d4 + d4, s4 + d4)
    s8 = [s + x for s, x in zip(s4, res[:n])]
    d8 = res[n:]
    tdiag = [s + x for s, x in zip(s8, inv(d8, s8))]
    rhs = [jnp.concatenate([at, x[:L]], axis=1) for at, x in zip(ats, akv)]
    res = inv(tdiag + tdiag, rhs + [x - d for x, d in zip(a_ab, dblk)])
    xt, nt = res[:n], res[n:]
    res = inv(nt + nt, xt + nt)
    u = [x + y for x, y in zip(xt, res[:n])]
    nt2 = res[n:]
    pq = [x + y for x, y in zip(u, inv(nt2, u))]
    res = _dot3_many(a_rb, pq, passes=RWKV_VALUE_PASSES)
    gh = [jnp.concatenate([rt, x[L:]], axis=1) + y for rt, x, y in zip(rts, akv, res)]
    res = _dot3_many(bhs + khs, pq + vs, dot=_bdot_tn, passes=RWKV_VALUE_PASSES)
    out = []
    for i in range(n):
        mc = res[i]
        m_op = mc[:, :n_ch] + eye * wls[i]
        c_op = mc[:, n_ch:] + res[n + i]
        out.append((gh[i][:, :n_ch], gh[i][:, n_ch:], m_op, c_op))
    return out


def _rwkv_scan_kernel(r_ref, k_ref, v_ref, z_ref, lw_ref, a_ref, kk_ref, ka_ref, rk_ref,
                      lnw_ref, lnb_ref, o_ref, state_ref):
    tb = pl.program_id(2)

    @pl.when(tb == 0)
    def _():
        state_ref[...] = jnp.zeros_like(state_ref)

    L = RWKV_CHUNK
    N = RWKV_HEAD
    ri = lax.broadcasted_iota(jnp.int32, (L, L), 0)
    ci = lax.broadcasted_iota(jnp.int32, (L, L), 1)
    lower_strict = ri > ci
    lower_incl = ri >= ci
    tri_incl = jnp.where(lower_incl, 1.0, 0.0).astype(BF16)
    eye = jnp.where(ri == ci, 1.0, 0.0).astype(F32)
    blockdiag = (ri // 16) == (ci // 16)
    n_heads = r_ref.shape[2] // N
    n_chunks = r_ref.shape[1] // L
    width = n_heads * N
    hi_ = lax.broadcasted_iota(jnp.int32, (width, width), 0) // N
    hj_ = lax.broadcasted_iota(jnp.int32, (width, width), 1) // N
    head_ones = jnp.where(hi_ == hj_, 1.0, 0.0).astype(BF16)
    chunk_rows = [slice(c * L, (c + 1) * L) for c in range(n_chunks)]
    r_c = [r_ref[0, rows, :] for rows in chunk_rows]
    v_c = [v_ref[0, rows, :] for rows in chunk_rows]
    lw_c = [lw_ref[0, rows, :] for rows in chunk_rows]
    kk_c, kp_c, a_c = [], [], []
    for rows in chunk_rows:
        k = k_ref[0, rows, :]
        a = a_ref[0, rows, :]
        kk_c.append(k * kk_ref[...])
        kp_c.append(k * (1.0 + (a - 1.0) * ka_ref[...]))
        a_c.append(a)
    sums = _lane_sums([x * x for x in kk_c] + [r * kp * rk_ref[...] for r, kp in zip(r_c, kp_c)],
                      head_ones)
    bonus_c = [sums[n_chunks + c] * v_c[c] for c in range(n_chunks)]
    cum_c = []
    for c in range(n_chunks):
        hi = lw_c[c].astype(BF16)
        r1 = lw_c[c] - hi.astype(F32)
        mid = r1.astype(BF16)
        lo = (r1 - mid.astype(F32)).astype(BF16)
        cum_c.append([jnp.dot(tri_incl, p, preferred_element_type=F32) for p in (hi, mid, lo)])
    cum_c = [a + (b + c) for a, b, c in cum_c]
    slabs = []
    for c in range(n_chunks):
        kk = kk_c[c] / jnp.maximum(jnp.sqrt(sums[c]), 1e-12)
        cum = cum_c[c]
        w_inv = jnp.exp(-cum)
        cum_last = cum[L - 1:L, :]
        w_tail = jnp.exp(cum_last - cum)
        bb = kk * a_c[c]
        slabs.append((-kk * jnp.exp(cum - lw_c[c]), r_c[c] * jnp.exp(cum), bb * w_inv,
                      kp_c[c] * w_inv, bb * w_tail, kp_c[c] * w_tail, v_c[c], jnp.exp(cum_last)))
    probs = [(hh, c) for c in range(n_chunks) for hh in range(n_heads)]
    per_head = [[slabs[c][j][:, hh * N:(hh + 1) * N] for hh, c in probs] for j in range(8)]
    ops = _rwkv_chunk_ops(*per_head, lower_strict, lower_incl, eye, blockdiag)
    states = [state_ref[hh] for hh in range(n_heads)]
    ys = {}
    for c in range(n_chunks):
        idx = [c * n_heads + hh for hh in range(n_heads)]
        upd = _dot3_many([jnp.concatenate([ops[i][0], ops[i][2]], axis=0) for i in idx], states,
                         passes=RWKV_STATE_PASSES)
        for hh, i in enumerate(idx):
            ys[(hh, c)] = upd[hh][:L] + ops[i][1]
            states[hh] = upd[hh][L:] + ops[i][3]
    for hh in range(n_heads):
        state_ref[hh] = states[hh]
    y_c = [jnp.concatenate([ys[(hh, c)] for hh in range(n_heads)], axis=1) for c in range(n_chunks)]
    yc_c = [y - s * (1.0 / N) for y, s in zip(y_c, _lane_sums(y_c, head_ones))]
    var_c = [s * (1.0 / N) for s in _lane_sums([yc * yc for yc in yc_c], head_ones)]
    for c, rows in enumerate(chunk_rows):
        yn = yc_c[c] * lax.rsqrt(var_c[c] + RWKV_GN_EPS) * lnw_ref[...] + lnb_ref[...]
        o_ref[0, rows, :] = (yn + bonus_c[c]) * _silu(z_ref[0, rows, :])


def _rwkv_scan(r, k, v, z, lw, a, k_k, k_a, r_k, lnx_w, lnx_b):
    b, t, c = r.shape
    tb = min(RWKV_BLOCK, t)
    seq_spec = pl.BlockSpec((1, tb, RWKV_LANES), lambda i, p, j: (i, j, p))
    par_spec = pl.BlockSpec((1, RWKV_LANES), lambda i, p, j: (0, p))
    params = [x.reshape(1, c).astype(F32) for x in (k_k, k_a, r_k, lnx_w, lnx_b)]
    return pl.pallas_call(
        _rwkv_scan_kernel, grid=(b, c // RWKV_LANES, t // tb),
        in_specs=[seq_spec] * 6 + [par_spec] * 5, out_specs=seq_spec,
        out_shape=jax.ShapeDtypeStruct((b, t, c), F32),
        scratch_shapes=[pltpu.VMEM((RWKV_LANES // RWKV_HEAD, RWKV_HEAD, RWKV_HEAD), F32)],
        compiler_params=_cparams("parallel", "parallel", "arbitrary"), name="rwkv_scan",
    )(r, k, v, z, lw, a, *params)


def _rwkv_layer(x2d, g, mu, w_in, w0, w1, w2, a0, a1, a2, k_k, k_a, r_k, lnx_w, lnx_b, w_out,
                final_g, batch, seq):
    c = w0.shape[0]
    r, k, v, z, lw, a = _rwkv_pre(x2d, g, mu, w_in, w0, w1, w2, a0, a1, a2, seq)
    sh = (batch, seq, c)
    y = _rwkv_scan(r.reshape(sh), k.reshape(sh), v.reshape(sh), z.reshape(sh), lw.reshape(sh),
                   a.reshape(sh), k_k, k_a, r_k, lnx_w, lnx_b)
    return _out_proj(x2d, [y.reshape(x2d.shape[0], c)], None, w_out, final_g)


def _nsa_compress_kernel(u_ref, posa_ref, posb_ref, w1a_ref, w1b_ref, w2_ref, o_ref):
    u = u_ref[0, 0]
    ha = _bdot(u + posa_ref[...], w1a_ref[...])
    hb = _bdot(u + posb_ref[...], w1b_ref[...])
    h = ha + pltpu.roll(hb, hb.shape[0] - 1, 0)
    o_ref[0, 0] = _bdot(_silu(h), w2_ref[...])


def _nsa_compress(t_hm, pos, w1, w2):
    b, g, t, dh = t_hm.shape
    nch = t // NSA_CMP_STRIDE
    half = NSA_CMP_STRIDE * dh
    u = t_hm.reshape(b, g, nch, half)
    posf = pos.astype(F32).reshape(2, 1, half)
    hid = w1.shape[1]
    fixed = lambda i, j: (0, 0)
    return pl.pallas_call(
        _nsa_compress_kernel, grid=(b, g),
        in_specs=[pl.BlockSpec((1, 1, nch, half), lambda i, j: (i, j, 0, 0)),
                  pl.BlockSpec((1, half), fixed), pl.BlockSpec((1, half), fixed),
                  pl.BlockSpec((half, hid), fixed), pl.BlockSpec((half, hid), fixed),
                  pl.BlockSpec((hid, dh), fixed)],
        out_specs=pl.BlockSpec((1, 1, nch, dh), lambda i, j: (i, j, 0, 0)),
        out_shape=jax.ShapeDtypeStruct((b, g, nch, dh), F32),
        compiler_params=_cparams("parallel", "parallel"), name="nsa_compress",
    )(u, posf[0], posf[1], w1[:half].astype(BF16), w1[half:].astype(BF16), w2.astype(BF16))


def _nsa_cmp_attn_kernel(q_ref, kc_ref, vct_ref, gate_ref, o_ref, sel_ref, *, n_sel):
    g = pl.program_id(1)
    n = pl.program_id(2)
    tq = q_ref.shape[2]
    ncmp = kc_ref.shape[2]
    q = q_ref[0].reshape(GQA * tq, HEAD_DIM) * (HEAD_DIM ** -0.5)
    s = _bdot_nt(kc_ref[0, 0], q)
    cend = lax.broadcasted_iota(jnp.int32, (ncmp, tq), 0) * NSA_CMP_STRIDE + (NSA_CMP_LEN - 1)
    tpos = n * tq + lax.broadcasted_iota(jnp.int32, (ncmp, tq), 1)
    ok = jnp.concatenate([cend <= tpos] * GQA, axis=1)
    s = jnp.where(ok, s, NEG)
    m = jnp.max(s, axis=0, keepdims=True)
    e = jnp.where(ok, jnp.exp(s - m), 0.0)
    l = jnp.sum(e, axis=0, keepdims=True)
    p = e / jnp.where(l > 0.0, l, 1.0)
    o_t = jnp.dot(vct_ref[0, 0].astype(BF16), p.astype(BF16), preferred_element_type=F32)
    outs = []
    for r in range(GQA):
        outs.append((o_t[:, r * tq:(r + 1) * tq] * _gate_row(gate_ref, GQA * g + r)).T)
    o_ref[0] = jnp.concatenate(outs, axis=-1)

    psum = p[:, :tq]
    for r in range(1, GQA):
        psum = psum + p[:, r * tq:(r + 1) * tq]
    si = lax.broadcasted_iota(jnp.int32, (n_sel, ncmp), 0)
    ni = lax.broadcasted_iota(jnp.int32, (n_sel, ncmp), 1)
    ratio = NSA_SEL_LEN // NSA_CMP_STRIDE
    overlap = ((ni < ratio * (si + 1)) & (ni * NSA_CMP_STRIDE + NSA_CMP_LEN - 1 >= si * NSA_SEL_LEN))
    overlap = jnp.where(overlap, 1.0, 0.0).astype(BF16)
    imp = _dot_exact_rhs(psum, overlap, dot=lambda a, b: _bdot(b, a))
    blk = lax.broadcasted_iota(jnp.int32, (n_sel, tq), 0)
    tq_pos = n * tq + lax.broadcasted_iota(jnp.int32, (n_sel, tq), 1)
    cur = tq_pos // NSA_SEL_LEN
    forced = (blk == 0) | (blk == cur) | (blk == cur - 1)
    future = blk * NSA_SEL_LEN > tq_pos
    imp = jnp.where(forced, BIG, jnp.where(future, NEG, imp))
    n_grp = n_sel // SUBLANES
    imp_g = [imp[SUBLANES * b:SUBLANES * (b + 1)] for b in range(n_grp)]
    blk_g = blk[:SUBLANES]
    ranks = [jnp.zeros((SUBLANES, tq), F32) for _ in range(n_grp)]
    for j in range(n_sel):
        row = imp[j:j + 1, :]
        for b in range(n_grp):
            if b > j // SUBLANES:
                ahead = row >= imp_g[b]
            elif b < j // SUBLANES:
                ahead = row > imp_g[b]
            else:
                ahead = (row > imp_g[b]) | ((row == imp_g[b]) & (blk_g > j % SUBLANES))
            ranks[b] = ranks[b] + jnp.where(ahead, 1.0, 0.0)
    rank = jnp.concatenate(ranks, axis=0)
    sel = jnp.where(rank < float(min(NSA_TOPK, n_sel)), 1.0, 0.0)
    sel_ref[0, 0] = sel


def _nsa_cmp_attn(q, k_cmp, v_cmp, gates_t):
    b, _, t, _ = q.shape
    ncmp = k_cmp.shape[2]
    n_sel = t // NSA_SEL_LEN
    d = ATTN_HEADS * HEAD_DIM
    kern = functools.partial(_nsa_cmp_attn_kernel, n_sel=n_sel)
    tq = min(NSA_CMP_QUERIES, t)
    cmp_spec = pl.BlockSpec((1, 1, ncmp, HEAD_DIM), lambda i, g, n: (i, g, 0, 0))
    return pl.pallas_call(
        kern, grid=(b, KV_HEADS, t // tq),
        in_specs=[pl.BlockSpec((1, GQA, tq, HEAD_DIM), lambda i, g, n: (i, g, n, 0)),
                  cmp_spec, pl.BlockSpec((1, 1, HEAD_DIM, ncmp), lambda i, g, n: (i, g, 0, 0)),
                  pl.BlockSpec((1, LANES, tq), lambda i, g, n: (i, 0, n))],
        out_specs=[pl.BlockSpec((1, tq, GQA * HEAD_DIM), lambda i, g, n: (i, n, g)),
                   pl.BlockSpec((1, 1, n_sel, tq), lambda i, g, n: (i, g, 0, n))],
        out_shape=[jax.ShapeDtypeStruct((b, t, d), F32),
                   jax.ShapeDtypeStruct((b, KV_HEADS, n_sel, t), F32)],
        compiler_params=_cparams("parallel", "parallel", "parallel"), name="nsa_cmp_attn",
    )(q, k_cmp, jnp.swapaxes(v_cmp, 2, 3), gates_t)


def _nsa_sel_attn_kernel(q_ref, k_ref, vt_ref, selt_ref, bias_ref, far_ref, gate_ref, o_ref):
    g = pl.program_id(1)
    n = pl.program_id(2)
    tq = q_ref.shape[2]
    pw = Q_BLOCK
    bl = NSA_SEL_LEN
    fk = NSA_FAR_KEYS
    q_all = q_ref[0].reshape(GQA * tq, HEAD_DIM) * (HEAD_DIM ** -0.5)

    def block_rows(first_block, count, limit):
        rows = []
        for i in range(count):
            blk = first_block + i
            row = selt_ref[0, 0, pl.ds(jnp.clip(blk, 0, jnp.maximum(limit - 1, 0)), 1), :]
            rows.append(jnp.broadcast_to(jnp.where((blk >= 0) & (blk < limit), row, 0.0), (bl, tq)))
        return jnp.concatenate(rows, axis=0) > 0.5

    def values_t(*starts_widths):
        vt = jnp.concatenate([vt_ref[0, :, pl.ds(s, w)] for s, w in starts_widths], axis=1)
        return jnp.concatenate([vt, jnp.ones_like(vt)], axis=0)

    prev = pl.multiple_of(jnp.maximum(n * tq - pw, 0), pw)
    diag = pl.multiple_of(n * tq, tq)
    kt = jnp.concatenate([k_ref[0, 0, pl.ds(prev, pw), :], k_ref[0, 0, pl.ds(diag, tq), :]], axis=0)
    vt = values_t((prev, pw), (diag, tq))
    kj = lax.broadcasted_iota(jnp.int32, (pw + tq, tq), 0)
    qi = lax.broadcasted_iota(jnp.int32, (pw + tq, tq), 1)
    n_blocks = (n + 1) * (tq // bl)
    first_near = n * (tq // bl) - pw // bl
    ok = block_rows(first_near, (pw + tq) // bl, n_blocks) & (pw + qi - kj >= 0)
    ok = jnp.concatenate([ok] * GQA, axis=1)
    s = jnp.where(ok, _bdot_nt(kt, q_all) + bias_ref[0], NEG)
    m = jnp.max(s, axis=0, keepdims=True)
    p = jnp.exp(s - m).astype(BF16)
    acc = jnp.dot(vt, p, preferred_element_type=F32)
    m = m - far_ref[0]
    far_blocks = jnp.maximum(first_near, 0)

    def body(c, carry):
        m, acc = carry
        start = pl.multiple_of(c * fk, fk)
        ok = block_rows(c * (fk // bl), fk // bl, far_blocks)
        ok = jnp.concatenate([ok] * GQA, axis=1)
        s = jnp.where(ok, _bdot_nt(k_ref[0, 0, pl.ds(start, fk), :], q_all), NEG)
        m_new = jnp.maximum(m, jnp.max(s, axis=0, keepdims=True))
        p = jnp.exp(s - m_new).astype(BF16)
        pv = jnp.dot(values_t((start, fk)), p, preferred_element_type=F32)
        return m_new, jnp.exp(m - m_new) * acc + pv

    n_far = (far_blocks * bl + fk - 1) // fk
    _, acc = lax.fori_loop(0, n_far, body, (m, acc))
    o_t = acc[:HEAD_DIM] / acc[HEAD_DIM:HEAD_DIM + 1]
    outs = []
    for r in range(GQA):
        o = o_t[:, r * tq:(r + 1) * tq] * _gate_row(gate_ref, ATTN_HEADS + GQA * g + r)
        outs.append(o.T)
    o_ref[0] = jnp.concatenate(outs, axis=-1)


def _nsa_sel_attn(q, ks, vst, sel_t, t5_table, gates_t):
    b, _, t, _ = q.shape
    n_sel = t // NSA_SEL_LEN
    d = ATTN_HEADS * HEAD_DIM
    tq = min(NSA_SEL_QUERIES, t)
    near = Q_BLOCK + tq
    width = GQA * tq
    assert t % NSA_FAR_KEYS == 0 or t <= near
    bias_near = _band_bias(t5_table, tq, near, Q_BLOCK).reshape(KV_HEADS, GQA, tq, near)
    bias_near = bias_near.transpose(0, 3, 1, 2).reshape(KV_HEADS, near, width)
    far = t5_table[T5_BUCKETS - 1].astype(F32).reshape(KV_HEADS, GQA, 1)
    far = jnp.broadcast_to(far, (KV_HEADS, GQA, tq)).reshape(KV_HEADS, 1, width)
    return pl.pallas_call(
        _nsa_sel_attn_kernel, grid=(b, KV_HEADS, t // tq),
        in_specs=[pl.BlockSpec((1, GQA, tq, HEAD_DIM), lambda i, g, n: (i, g, n, 0)),
                  pl.BlockSpec((1, 1, t, HEAD_DIM), lambda i, g, n: (i, g, 0, 0)),
                  pl.BlockSpec((1, HEAD_DIM, t), lambda i, g, n: (i, g, 0)),
                  pl.BlockSpec((1, 1, n_sel, tq), lambda i, g, n: (i, g, 0, n)),
                  pl.BlockSpec((1, near, width), lambda i, g, n: (g, 0, 0)),
                  pl.BlockSpec((1, 1, width), lambda i, g, n: (g, 0, 0)),
                  pl.BlockSpec((1, LANES, tq), lambda i, g, n: (i, 0, n))],
        out_specs=pl.BlockSpec((1, tq, GQA * HEAD_DIM), lambda i, g, n: (i, n, g)),
        out_shape=jax.ShapeDtypeStruct((b, t, d), F32),
        compiler_params=_cparams("parallel", "parallel", "parallel"), name="nsa_sel_attn",
    )(q, ks, vst, sel_t, bias_near, far, gates_t)


def _nsa_layer(x2d, g, w_in, pos_k, k_w1, k_w2, pos_v, v_w1, v_w2, w_out, t5_table, final_g,
               batch, seq):
    nq, nkv = ATTN_HEADS * HEAD_DIM, KV_HEADS * HEAD_DIM
    offs = [0, nq] + [nq + nkv * (i + 1) for i in range(6)]
    ws = [w_in[:, offs[i]:offs[i + 1]] for i in range(7)]
    n_gate = 3 * ATTN_HEADS
    wg = jnp.pad(w_in[:, offs[7]:offs[7] + n_gate], ((0, 0), (0, LANES - n_gate)))
    wz = w_in[:, offs[7] + n_gate:]
    q, kc, vc, ks, vst, kw, vwt, gates_t, z = _norm_proj(
        x2d, g, ws + [wg, wz],
        ["heads", "heads", "heads", "heads", "cols", "heads", "cols", "cols", "rows"],
        [BF16, F32, F32, BF16, BF16, BF16, BF16, F32, F32], batch, seq)
    k_cmp = _nsa_compress(kc, pos_k, k_w1, k_w2)
    v_cmp = _nsa_compress(vc, pos_v, v_w1, v_w2)
    o_cmp, sel_t = _nsa_cmp_attn(q, k_cmp, v_cmp, gates_t)
    o_sel = _nsa_sel_attn(q, ks, vst, sel_t, t5_table, gates_t)
    o_win = _band_attn(q, kw, vwt, t5_table, NSA_WINDOW, gates_t=gates_t, gate_col=2 * ATTN_HEADS)
    sh = x2d.shape
    return _out_proj(x2d, [o_cmp.reshape(sh), o_sel.reshape(sh), o_win.reshape(sh)], z, w_out,
                     final_g)


def _lru_kernel(x_ref, g_ref, win_u_ref, win_z_ref, cw_ref, cb_ref, wa_ref, ba_ref, wx_ref, bx_ref,
                sp_ref, wout_ref, *rest, half, has_fg):
    fg_ref = rest[0] if has_fg else None
    o_ref, tail_ref, h_ref = rest[-3:]
    j = pl.program_id(1)

    @pl.when(j == 0)
    def _():
        tail_ref[...] = jnp.zeros_like(tail_ref)
        h_ref[...] = jnp.zeros_like(h_ref)

    x = x_ref[...]
    xb = _rms(x, g_ref[...]).astype(BF16)
    u = jnp.dot(xb, win_u_ref[...], preferred_element_type=F32)
    z = jnp.dot(xb, win_z_ref[...], preferred_element_type=F32)
    tm, width = u.shape
    ext = jnp.concatenate([tail_ref[...], u], axis=0)
    tail_ref[...] = u[tm - 8:, :]
    uc = cb_ref[...] + cw_ref[CONV_WIDTH - 1:CONV_WIDTH, :] * u
    for s in range(1, CONV_WIDTH):
        uc = uc + cw_ref[CONV_WIDTH - 1 - s:CONV_WIDTH - s, :] * ext[8 - s:8 - s + tm, :]
    ucb = uc.astype(BF16)
    gr, gi = [], []
    for c in range(width // half):
        blk = ucb[:, c * half:(c + 1) * half]
        gr.append(jnp.dot(blk, wa_ref[c], preferred_element_type=F32))
        gi.append(jnp.dot(blk, wx_ref[c], preferred_element_type=F32))
    rg = jax.nn.sigmoid(jnp.concatenate(gr, axis=1) + ba_ref[...])
    ig = jax.nn.sigmoid(jnp.concatenate(gi, axis=1) + bx_ref[...])
    log_a = -LRU_C * rg * sp_ref[...]
    a = jnp.exp(log_a)
    bv = jnp.sqrt(1.0 - a * a) * (ig * uc)
    n_grp = tm // SUBLANES
    a = a.reshape(n_grp, SUBLANES, width)
    bv = bv.reshape(n_grp, SUBLANES, width)
    row = lax.broadcasted_iota(jnp.int32, (n_grp, SUBLANES, width), 1)
    sh = 1
    while sh < SUBLANES:
        a_s = jnp.where(row >= sh, pltpu.roll(a, sh, 1), 1.0)
        b_s = jnp.where(row >= sh, pltpu.roll(bv, sh, 1), 0.0)
        bv = a * b_s + bv
        a = a * a_s
        sh *= 2
    carry = h_ref[...]
    groups = []
    for i in range(n_grp):
        hg = bv[i] + a[i] * carry
        carry = hg[SUBLANES - 1:SUBLANES, :]
        groups.append(hg)
    h_ref[...] = carry
    gated = (jnp.concatenate(groups, axis=0) * _silu(z)).astype(BF16)
    y = x + jnp.dot(gated, wout_ref[...], preferred_element_type=F32)
    o_ref[...] = _rms(y, fg_ref[...]) if has_fg else y


def _block_diag(w, group):
    nb, n, _ = w.shape
    w = w.reshape(nb // group, group, n, n)
    eye = jnp.eye(group, dtype=w.dtype)
    return jnp.einsum('cgij,gh->cgihj', w, eye).reshape(nb // group, group * n, group * n)


def _lru_layer(x2d, g, w_in, conv_w, conv_b, ga_w, ga_b, gx_w, gx_b, lam, w_out, final_g, batch, seq):
    n, d = x2d.shape
    width = w_in.shape[1] // 2
    blk = ga_w.shape[1]
    group = LANES // math.gcd(blk, LANES)
    group = min(group, ga_w.shape[0])
    half = group * blk
    nsup = width // half
    tm = min(LRU_TILE, seq)
    tiles = seq // tm
    row = lambda i, j: (i * tiles + j, 0)
    fixed = lambda i, j: (0, 0)
    fixed3 = lambda i, j: (0, 0, 0)
    vec = lambda a: a.astype(F32).reshape(1, -1)
    softplus_neg_lam = jax.nn.softplus(-lam.astype(F32))
    has_fg = final_g is not None
    in_specs = [pl.BlockSpec((tm, d), row), pl.BlockSpec((1, d), fixed),
                pl.BlockSpec((d, width), fixed), pl.BlockSpec((d, width), fixed),
                pl.BlockSpec((CONV_WIDTH, width), fixed), pl.BlockSpec((1, width), fixed),
                pl.BlockSpec((nsup, half, half), fixed3), pl.BlockSpec((1, width), fixed),
                pl.BlockSpec((nsup, half, half), fixed3), pl.BlockSpec((1, width), fixed),
                pl.BlockSpec((1, width), fixed), pl.BlockSpec((width, d), fixed)]
    args = [x2d, g.reshape(1, d), w_in[:, :width].astype(BF16), w_in[:, width:].astype(BF16),
            conv_w.astype(F32), vec(conv_b), _block_diag(ga_w, group).astype(BF16), vec(ga_b),
            _block_diag(gx_w, group).astype(BF16), vec(gx_b), vec(softplus_neg_lam),
            w_out.astype(BF16)]
    if has_fg:
        in_specs.append(pl.BlockSpec((1, d), fixed))
        args.append(final_g.reshape(1, d))
    kern = functools.partial(_lru_kernel, half=half, has_fg=has_fg)
    return pl.pallas_call(
        kern, grid=(batch, tiles), in_specs=in_specs, out_specs=pl.BlockSpec((tm, d), row),
        out_shape=jax.ShapeDtypeStruct((n, d), F32),
        scratch_shapes=[pltpu.VMEM((8, width), F32), pltpu.VMEM((1, width), F32)],
        compiler_params=_cparams("parallel", "arbitrary"), name="rglru_layer",
    )(*args)


def kernel(x, t5_table, norm_g, final_g, a_w_in, a_sinks, a_w_out, b_mu, b_w_in, b_w0, b_w1, b_w2, b_a0, b_a1, b_a2, b_k_k, b_k_a, b_r_k, b_lnx_w, b_lnx_b, b_w_out, c_w_in, c_cmp_pos_k, c_cmp_k_w1, c_cmp_k_w2, c_cmp_pos_v, c_cmp_v_w1, c_cmp_v_w2, c_w_out, d_w_in, d_conv_w, d_conv_b, d_gate_a_w, d_gate_a_b, d_gate_x_w, d_gate_x_b, d_lambda, d_w_out):
    batch, seq, d = x.shape
    depth = norm_g.shape[0]
    h = x.reshape(batch * seq, d)
    for layer in range(depth):
        m, j = layer % 4, layer // 4
        g = norm_g[layer]
        fg = final_g if layer == depth - 1 else None
        if m == 0:
            h = _swa_layer(h, g, a_w_in[j], a_sinks[j], a_w_out[j], t5_table, fg, batch, seq)
        elif m == 1:
            h = _rwkv_layer(h, g, b_mu[j], b_w_in[j], b_w0[j], b_w1[j], b_w2[j], b_a0[j], b_a1[j],
                            b_a2[j], b_k_k[j], b_k_a[j], b_r_k[j], b_lnx_w[j], b_lnx_b[j],
                            b_w_out[j], fg, batch, seq)
        elif m == 2:
            h = _nsa_layer(h, g, c_w_in[j], c_cmp_pos_k[j], c_cmp_k_w1[j], c_cmp_k_w2[j],
                           c_cmp_pos_v[j], c_cmp_v_w1[j], c_cmp_v_w2[j], c_w_out[j], t5_table,
                           fg, batch, seq)
        else:
            h = _lru_layer(h, g, d_w_in[j], d_conv_w[j], d_conv_b[j], d_gate_a_w[j], d_gate_a_b[j],
                           d_gate_x_w[j], d_gate_x_b[j], d_lambda[j], d_w_out[j], fg, batch, seq)
    return h.reshape(batch, seq, d)
```

```python
import functools
import math

import jax
import jax.numpy as jnp
from jax import lax
from jax.experimental import pallas as pl
from jax.experimental.pallas import tpu as pltpu

F32 = jnp.float32
BF16 = jnp.bfloat16

EPS = 1e-6
NEG = -1e30
BIG = 1e30
T5_BUCKETS = 32
T5_MAX_DIST = 128
ATTN_HEADS = 16
HEAD_DIM = 64
KV_HEADS = 4
GQA = ATTN_HEADS // KV_HEADS
Q_BLOCK = 128
SWA_WINDOW = 128
RWKV_HEAD = 64
RWKV_GN_EPS = 64e-5
NSA_CMP_LEN = 32
NSA_CMP_STRIDE = 16
NSA_SEL_LEN = 64
NSA_TOPK = 16
NSA_WINDOW = 512
LRU_BLOCKS = 16
LRU_C = 8.0
CONV_WIDTH = 4

LANES = 128
SUBLANES = 8
VMEM_LIMIT = 56 * 1024 * 1024
ROW_TILE = 512
RWKV_CHUNK = 64
RWKV_BLOCK = 512
RWKV_LANES = 256
RWKV_SCORE_PASSES = 1
RWKV_VALUE_PASSES = 1
RWKV_INVERSE_PASSES = 1
RWKV_STATE_PASSES = 3
NSA_FAR_KEYS = 512
NSA_CMP_QUERIES = 128
NSA_SEL_QUERIES = 512
LRU_TILE = 256


def _cparams(*sem):
    return pltpu.CompilerParams(dimension_semantics=sem, vmem_limit_bytes=VMEM_LIMIT)


def _bdot(a, b):
    return jnp.dot(a.astype(BF16), b.astype(BF16), preferred_element_type=F32)


def _bdot_nt(a, b):
    return lax.dot_general(a.astype(BF16), b.astype(BF16), (((1,), (1,)), ((), ())),
                           preferred_element_type=F32)


def _bdot_tn(a, b):
    return lax.dot_general(a.astype(BF16), b.astype(BF16), (((0,), (0,)), ((), ())),
                           preferred_element_type=F32)


def _split2(x):
    hi = x.astype(BF16)
    lo = (x - hi.astype(F32)).astype(BF16)
    return hi, lo


def _dot3(a, b, dot=_bdot):
    ah, al = _split2(a)
    bh, bl = _split2(b)
    return dot(ah, bh) + (dot(ah, bl) + dot(al, bh))


def _dot_exact_rhs(a, b01, dot=_bdot):
    ah = a.astype(BF16)
    r1 = a - ah.astype(F32)
    am = r1.astype(BF16)
    al = (r1 - am.astype(F32)).astype(BF16)
    return dot(ah, b01) + (dot(am, b01) + dot(al, b01))


def _rms(x, g):
    return x * lax.rsqrt(jnp.mean(x * x, axis=-1, keepdims=True) + EPS) * g


def _silu(z):
    return z * jax.nn.sigmoid(z)


def _gate_row(gate_ref, c):
    return jax.nn.sigmoid(gate_ref[0, pl.ds(c, 1), :])


def _norm_proj_kernel(x_ref, g_ref, *refs, n_out, layouts):
    w_refs, o_refs = refs[:n_out], refs[n_out:]
    xb = _rms(x_ref[...], g_ref[...]).astype(BF16)
    for w_ref, o_ref, layout in zip(w_refs, o_refs, layouts):
        width = w_ref.shape[1]
        for c0 in range(0, width, 512):
            cw = min(512, width - c0)
            acc = jnp.dot(xb, w_ref[:, c0:c0 + cw], preferred_element_type=F32)
            if layout == "heads":
                for j in range(cw // HEAD_DIM):
                    o_ref[0, (c0 // HEAD_DIM) + j] = acc[:, j * HEAD_DIM:(j + 1) * HEAD_DIM].astype(o_ref.dtype)
            elif layout == "cols":
                o_ref[0, c0:c0 + cw, :] = acc.T.astype(o_ref.dtype)
            else:
                o_ref[:, c0:c0 + cw] = acc.astype(o_ref.dtype)


def _norm_proj(x2d, g, weights, layouts, dtypes, batch, seq):
    n, d = x2d.shape
    tm = min(ROW_TILE, seq)
    tiles_per_seq = seq // tm
    in_specs = [pl.BlockSpec((tm, d), lambda i: (i, 0)), pl.BlockSpec((1, d), lambda i: (0, 0))]
    out_specs, out_shapes = [], []
    for w, layout, dt in zip(weights, layouts, dtypes):
        width = w.shape[1]
        in_specs.append(pl.BlockSpec((d, width), lambda i: (0, 0)))
        if layout == "heads":
            nh = width // HEAD_DIM
            out_shapes.append(jax.ShapeDtypeStruct((batch, nh, seq, HEAD_DIM), dt))
            out_specs.append(pl.BlockSpec((1, nh, tm, HEAD_DIM),
                                          lambda i: (i // tiles_per_seq, 0, i % tiles_per_seq, 0)))
        elif layout == "cols":
            out_shapes.append(jax.ShapeDtypeStruct((batch, width, seq), dt))
            out_specs.append(pl.BlockSpec((1, width, tm),
                                          lambda i: (i // tiles_per_seq, 0, i % tiles_per_seq)))
        else:
            out_shapes.append(jax.ShapeDtypeStruct((n, width), dt))
            out_specs.append(pl.BlockSpec((tm, width), lambda i: (i, 0)))
    kern = functools.partial(_norm_proj_kernel, n_out=len(weights), layouts=tuple(layouts))
    return pl.pallas_call(
        kern, grid=(n // tm,), in_specs=in_specs, out_specs=out_specs, out_shape=out_shapes,
        compiler_params=_cparams("parallel"), name="norm_proj",
    )(x2d, g.reshape(1, d), *[w.astype(BF16) for w in weights])


def _out_proj_kernel(*refs, n_a, has_z, has_g):
    x_ref = refs[0]
    a_refs = refs[1:1 + n_a]
    pos = 1 + n_a
    z_ref = refs[pos] if has_z else None
    pos += int(has_z)
    w_ref = refs[pos]
    pos += 1
    g_ref = refs[pos] if has_g else None
    o_ref = refs[-1]
    a = a_refs[0][...].astype(F32)
    for r in a_refs[1:]:
        a = a + r[...].astype(F32)
    if has_z:
        a = a * _silu(z_ref[...])
    y = x_ref[...] + jnp.dot(a.astype(BF16), w_ref[...], preferred_element_type=F32)
    if has_g:
        y = _rms(y, g_ref[...])
    o_ref[...] = y


def _out_proj(x2d, a_list, z, w, final_g=None):
    n, d = x2d.shape
    c = w.shape[0]
    tm = min(ROW_TILE, n)
    row = lambda i: (i, 0)
    fixed = lambda i: (0, 0)
    in_specs = [pl.BlockSpec((tm, d), row)] + [pl.BlockSpec((tm, c), row) for _ in a_list]
    args = [x2d] + list(a_list)
    if z is not None:
        in_specs.append(pl.BlockSpec((tm, c), row))
        args.append(z)
    in_specs.append(pl.BlockSpec((c, d), fixed))
    args.append(w.astype(BF16))
    if final_g is not None:
        in_specs.append(pl.BlockSpec((1, d), fixed))
        args.append(final_g.reshape(1, d))
    kern = functools.partial(_out_proj_kernel, n_a=len(a_list), has_z=z is not None,
                             has_g=final_g is not None)
    return pl.pallas_call(
        kern, grid=(n // tm,), in_specs=in_specs, out_specs=pl.BlockSpec((tm, d), row),
        out_shape=jax.ShapeDtypeStruct((n, d), F32), compiler_params=_cparams("parallel"),
        name="out_proj",
    )(*args)


def _t5_bucket(dist):
    max_exact = T5_BUCKETS // 2
    d = jnp.maximum(dist, 0)
    df = jnp.maximum(d, 1).astype(F32)
    large = max_exact + (jnp.log(df / max_exact) / math.log(T5_MAX_DIST / max_exact)
                         * (T5_BUCKETS - max_exact)).astype(jnp.int32)
    large = jnp.minimum(large, T5_BUCKETS - 1)
    return jnp.where(d < max_exact, d, large)


def _band_bias(t5_table, q_rows, kc, offset):
    period = kc + q_rows
    j = jnp.arange(period)
    j = jnp.where(j >= kc, j - period, j)
    vec = jnp.take(t5_table, _t5_bucket(offset - j), axis=0).astype(F32).T
    heads = vec.shape[0]
    flat = jnp.tile(vec, (1, q_rows))[:, :q_rows * (period - 1)]
    return flat.reshape(heads, q_rows, period - 1)[:, :, :kc]


def _band_attn_kernel(*refs, nprev, window, has_sink, gate_col):
    q_ref, k_ref, vt_ref, bias_ref = refs[:4]
    pos = 4
    sink_ref = refs[pos] if has_sink else None
    pos += int(has_sink)
    gate_ref = refs[pos] if gate_col is not None else None
    o_ref = refs[-1]
    n = pl.program_id(1)
    tq = Q_BLOCK
    kc = (nprev + 1) * tq
    kj = lax.broadcasted_iota(jnp.int32, (kc, tq), 0)
    qi = lax.broadcasted_iota(jnp.int32, (kc, tq), 1)
    dist = nprev * tq + qi - kj
    kpos = (n - nprev) * tq + kj
    valid = (dist >= 0) & (dist < window) & (kpos >= 0)
    valid = jnp.concatenate([valid] * GQA, axis=1)
    starts = [pl.multiple_of(jnp.maximum(n - nprev + j, 0) * tq, tq) for j in range(nprev + 1)]
    groups = range(KV_HEADS)
    scores = []
    for g in groups:
        kg = jnp.concatenate([k_ref[0, g, pl.ds(s, tq), :] for s in starts], axis=0)
        qg = q_ref[0, GQA * g:GQA * (g + 1)].reshape(GQA * tq, HEAD_DIM) * (HEAD_DIM ** -0.5)
        scores.append(_bdot_nt(kg, qg))
    probs, maxes = [], []
    for g in groups:
        s = jnp.where(valid, scores[g] + bias_ref[g], NEG)
        m = jnp.max(s, axis=0, keepdims=True)
        if has_sink:
            m = jnp.maximum(m, sink_ref[g])
        probs.append(jnp.exp(s - m).astype(BF16))
        maxes.append(m)
    accs = []
    ones = jnp.ones((HEAD_DIM, kc), BF16)
    for g in groups:
        vtg = jnp.concatenate([vt_ref[0, g * HEAD_DIM:(g + 1) * HEAD_DIM, pl.ds(s, tq)] for s in starts],
                              axis=1)
        vtg = jnp.concatenate([vtg, ones], axis=0)
        accs.append(jnp.dot(vtg, probs[g], preferred_element_type=F32))
    for g in groups:
        l = accs[g][HEAD_DIM:HEAD_DIM + 1]
        if has_sink:
            l = l + jnp.exp(sink_ref[g] - maxes[g])
        o_t = accs[g][:HEAD_DIM] / l
        outs = []
        for r in range(GQA):
            oh = o_t[:, r * tq:(r + 1) * tq]
            if gate_col is not None:
                oh = oh * _gate_row(gate_ref, gate_col + GQA * g + r)
            outs.append(oh.T)
        o_ref[0, :, g * GQA * HEAD_DIM:(g + 1) * GQA * HEAD_DIM] = jnp.concatenate(outs, axis=-1)


def _band_attn(q, k, vt, t5_table, window, sinks=None, gates_t=None, gate_col=None):
    b, _, t, _ = q.shape
    nprev = -(-window // Q_BLOCK)
    kc = (nprev + 1) * Q_BLOCK
    width = GQA * Q_BLOCK
    bias = _band_bias(t5_table, Q_BLOCK, kc, nprev * Q_BLOCK)
    in_specs = [
        pl.BlockSpec((1, ATTN_HEADS, Q_BLOCK, HEAD_DIM), lambda i, n: (i, 0, n, 0)),
        pl.BlockSpec((1, KV_HEADS, t, HEAD_DIM), lambda i, n: (i, 0, 0, 0)),
        pl.BlockSpec((1, KV_HEADS * HEAD_DIM, t), lambda i, n: (i, 0, 0)),
        pl.BlockSpec((KV_HEADS, kc, width), lambda i, n: (0, 0, 0)),
    ]
    bias_t = bias.reshape(KV_HEADS, GQA, Q_BLOCK, kc).transpose(0, 3, 1, 2).reshape(KV_HEADS, kc, width)
    args = [q, k, vt, bias_t]
    if sinks is not None:
        in_specs.append(pl.BlockSpec((KV_HEADS, 1, width), lambda i, n: (0, 0, 0)))
        sk = jnp.broadcast_to(sinks.astype(F32).reshape(KV_HEADS, GQA, 1), (KV_HEADS, GQA, Q_BLOCK))
        args.append(sk.reshape(KV_HEADS, 1, width))
    if gates_t is not None:
        in_specs.append(pl.BlockSpec((1, LANES, Q_BLOCK), lambda i, n: (i, 0, n)))
        args.append(gates_t)
    kern = functools.partial(_band_attn_kernel, nprev=nprev, window=window,
                             has_sink=sinks is not None, gate_col=gate_col)
    d = ATTN_HEADS * HEAD_DIM
    return pl.pallas_call(
        kern, grid=(b, t // Q_BLOCK), in_specs=in_specs,
        out_specs=pl.BlockSpec((1, Q_BLOCK, d), lambda i, n: (i, n, 0)),
        out_shape=jax.ShapeDtypeStruct((b, t, d), F32),
        compiler_params=_cparams("parallel", "parallel"), name="band_attn",
    )(*args)


def _swa_layer(x2d, g, w_in, sinks, w_out, t5_table, final_g, batch, seq):
    nq, nkv = ATTN_HEADS * HEAD_DIM, KV_HEADS * HEAD_DIM
    ws = [w_in[:, :nq], w_in[:, nq:nq + nkv], w_in[:, nq + nkv:nq + 2 * nkv], w_in[:, nq + 2 * nkv:]]
    q, k, vt, z = _norm_proj(x2d, g, ws, ["heads", "heads", "cols", "rows"], [BF16, BF16, BF16, F32],
                             batch, seq)
    o = _band_attn(q, k, vt, t5_table, SWA_WINDOW, sinks=sinks)
    return _out_proj(x2d, [o.reshape(x2d.shape)], z, w_out, final_g)


def _rwkv_pre_kernel(x_ref, xp_ref, g_ref, mu_ref, wr_ref, wk_ref, wv_ref, wz_ref,
                     w0_ref, w1_ref, w2_ref, a0_ref, a1_ref, a2_ref,
                     r_ref, k_ref, v_ref, z_ref, lw_ref, a_ref, *, tiles_per_seq):
    i = pl.program_id(0)
    g = g_ref[...]
    xn = _rms(x_ref[...], g)
    prev = _rms(xp_ref[...], g)[7:8]
    prev = jnp.where(i % tiles_per_seq == 0, 0.0, prev)
    row = lax.broadcasted_iota(jnp.int32, xn.shape, 0)
    xprev = jnp.where(row == 0, prev, pltpu.roll(xn, 1, 0))
    xx = xprev - xn
    lerp = lambda s: xn + xx * mu_ref[s:s + 1, :]
    r_ref[...] = _bdot(lerp(0), wr_ref[...])
    k_ref[...] = _bdot(lerp(1), wk_ref[...])
    v_ref[...] = _bdot(lerp(2), wv_ref[...])
    z_ref[...] = _bdot(lerp(3), wz_ref[...])
    wl = w0_ref[...] + _bdot(jnp.tanh(_bdot(lerp(4), w1_ref[...])), w2_ref[...])
    sp = jnp.maximum(-wl, 0.0) + jnp.log1p(jnp.exp(-jnp.abs(wl)))
    lw_ref[...] = -jnp.exp(-sp - 0.5)
    al = a0_ref[...] + _bdot(_bdot(lerp(5), a1_ref[...]), a2_ref[...])
    a_ref[...] = jax.nn.sigmoid(al)


def _rwkv_pre(x2d, g, mu, w_in, w0, w1, w2, a0, a1, a2, seq):
    n, d = x2d.shape
    c = w0.shape[0]
    tm = min(ROW_TILE, seq)
    tiles_per_seq = seq // tm
    row = lambda i: (i, 0)
    fixed = lambda i: (0, 0)
    ws = [w_in[:, s * c:(s + 1) * c].astype(BF16) for s in range(4)]
    in_specs = [
        pl.BlockSpec((tm, d), row),
        pl.BlockSpec((8, d), lambda i: (jnp.maximum(i * (tm // 8) - 1, 0), 0)),
        pl.BlockSpec((1, d), fixed), pl.BlockSpec((6, d), fixed),
    ] + [pl.BlockSpec((d, c), fixed)] * 4 + [
        pl.BlockSpec((1, c), fixed), pl.BlockSpec(w1.shape, fixed), pl.BlockSpec(w2.shape, fixed),
        pl.BlockSpec((1, c), fixed), pl.BlockSpec(a1.shape, fixed), pl.BlockSpec(a2.shape, fixed),
    ]
    out = jax.ShapeDtypeStruct((n, c), F32)
    kern = functools.partial(_rwkv_pre_kernel, tiles_per_seq=tiles_per_seq)
    return pl.pallas_call(
        kern, grid=(n // tm,), in_specs=in_specs, out_specs=[pl.BlockSpec((tm, c), row)] * 6,
        out_shape=[out] * 6, compiler_params=_cparams("parallel"), name="rwkv_pre",
    )(x2d, x2d, g.reshape(1, d), mu, *ws, w0.reshape(1, c), w1.astype(BF16), w2.astype(BF16),
      a0.reshape(1, c), a1.astype(BF16), a2.astype(BF16))


def _dot3_many(a_list, b_list, dot=_bdot, passes=3):
    if passes == 1:
        return [dot(a, b) for a, b in zip(a_list, b_list)]
    sa = [_split2(a) for a in a_list]
    sb = [_split2(b) for b in b_list]
    hh = [dot(x[0], y[0]) for x, y in zip(sa, sb)]
    hl = [dot(x[0], y[1]) for x, y in zip(sa, sb)]
    lh = [dot(x[1], y[0]) for x, y in zip(sa, sb)]
    return [p + (q + r) for p, q, r in zip(hh, hl, lh)]


def _lane_sums(x_list, ones):
    parts = [_split2(x) for x in x_list]
    hi = [jnp.dot(p[0], ones, preferred_element_type=F32) for p in parts]
    lo = [jnp.dot(p[1], ones, preferred_element_type=F32) for p in parts]
    return [a + b for a, b in zip(hi, lo)]


def _rwkv_chunk_ops(ats, rts, bts, kts, bhs, khs, vs, wls, lower_strict, lower_incl, eye, blockdiag):
    L = RWKV_CHUNK
    n = len(ats)
    n_ch = ats[0].shape[1]
    bks = [jnp.concatenate([bt, kt], axis=0) for bt, kt in zip(bts, kts)]
    ars = [jnp.concatenate([at, rt], axis=0) for at, rt in zip(ats, rts)]
    As = _dot3_many(ars, bks, dot=_bdot_nt, passes=RWKV_SCORE_PASSES)
    a_ab = [jnp.where(lower_strict, A[:L, :L], 0.0) for A in As]
    a_rb = [jnp.where(lower_incl, A[L:, :L], 0.0) for A in As]
    a_k = [jnp.concatenate([jnp.where(lower_strict, A[:L, L:], 0.0),
                            jnp.where(lower_incl, A[L:, L:], 0.0)], axis=0) for A in As]
    dblk = [jnp.where(blockdiag, x, 0.0) for x in a_ab]
    akv = _dot3_many(a_k, vs, passes=RWKV_VALUE_PASSES)
    inv = functools.partial(_dot3_many, passes=RWKV_INVERSE_PASSES)
    d2 = inv(dblk, dblk)
    res = inv([eye + d for d in dblk] + d2, [eye + d for d in d2] + d2)
    s4, d4 = res[:n], res[n:]
    res = inv(d4 + d4, s4 + d4)
    s8 = [s + x for s, x in zip(s4, res[:n])]
    d8 = res[n:]
    tdiag = [s + x for s, x in zip(s8, inv(d8, s8))]
    rhs = [jnp.concatenate([at, x[:L]], axis=1) for at, x in zip(ats, akv)]
    res = inv(tdiag + tdiag, rhs + [x - d for x, d in zip(a_ab, dblk)])
    xt, nt = res[:n], res[n:]
    res = inv(nt + nt, xt + nt)
    u = [x + y for x, y in zip(xt, res[:n])]
    nt2 = res[n:]
    pq = [x + y for x, y in zip(u, inv(nt2, u))]
    res = _dot3_many(a_rb, pq, passes=RWKV_VALUE_PASSES)
    gh = [jnp.concatenate([rt, x[L:]], axis=1) + y for rt, x, y in zip(rts, akv, res)]
    res = _dot3_many(bhs + khs, pq + vs, dot=_bdot_tn, passes=RWKV_VALUE_PASSES)
    out = []
    for i in range(n):
        mc = res[i]
        m_op = mc[:, :n_ch] + eye * wls[i]
        c_op = mc[:, n_ch:] + res[n + i]
        out.append((gh[i][:, :n_ch], gh[i][:, n_ch:], m_op, c_op))
    return out


def _rwkv_scan_kernel(r_ref, k_ref, v_ref, z_ref, lw_ref, a_ref, kk_ref, ka_ref, rk_ref,
                      lnw_ref, lnb_ref, o_ref, state_ref):
    tb = pl.program_id(2)

    @pl.when(tb == 0)
    def _():
        state_ref[...] = jnp.zeros_like(state_ref)

    L = RWKV_CHUNK
    N = RWKV_HEAD
    ri = lax.broadcasted_iota(jnp.int32, (L, L), 0)
    ci = lax.broadcasted_iota(jnp.int32, (L, L), 1)
    lower_strict = ri > ci
    lower_incl = ri >= ci
    tri_incl = jnp.where(lower_incl, 1.0, 0.0).astype(BF16)
    eye = jnp.where(ri == ci, 1.0, 0.0).astype(F32)
    blockdiag = (ri // 16) == (ci // 16)
    n_heads = r_ref.shape[2] // N
    n_chunks = r_ref.shape[1] // L
    width = n_heads * N
    hi_ = lax.broadcasted_iota(jnp.int32, (width, width), 0) // N
    hj_ = lax.broadcasted_iota(jnp.int32, (width, width), 1) // N
    head_ones = jnp.where(hi_ == hj_, 1.0, 0.0).astype(BF16)
    chunk_rows = [slice(c * L, (c + 1) * L) for c in range(n_chunks)]
    r_c = [r_ref[0, rows, :] for rows in chunk_rows]
    v_c = [v_ref[0, rows, :] for rows in chunk_rows]
    lw_c = [lw_ref[0, rows, :] for rows in chunk_rows]
    kk_c, kp_c, a_c = [], [], []
    for rows in chunk_rows:
        k = k_ref[0, rows, :]
        a = a_ref[0, rows, :]
        kk_c.append(k * kk_ref[...])
        kp_c.append(k * (1.0 + (a - 1.0) * ka_ref[...]))
        a_c.append(a)
    sums = _lane_sums([x * x for x in kk_c] + [r * kp * rk_ref[...] for r, kp in zip(r_c, kp_c)],
                      head_ones)
    bonus_c = [sums[n_chunks + c] * v_c[c] for c in range(n_chunks)]
    cum_c = []
    for c in range(n_chunks):
        hi = lw_c[c].astype(BF16)
        r1 = lw_c[c] - hi.astype(F32)
        mid = r1.astype(BF16)
        lo = (r1 - mid.astype(F32)).astype(BF16)
        cum_c.append([jnp.dot(tri_incl, p, preferred_element_type=F32) for p in (hi, mid, lo)])
    cum_c = [a + (b + c) for a, b, c in cum_c]
    slabs = []
    for c in range(n_chunks):
        kk = kk_c[c] / jnp.maximum(jnp.sqrt(sums[c]), 1e-12)
        cum = cum_c[c]
        w_inv = jnp.exp(-cum)
        cum_last = cum[L - 1:L, :]
        w_tail = jnp.exp(cum_last - cum)
        bb = kk * a_c[c]
        slabs.append((-kk * jnp.exp(cum - lw_c[c]), r_c[c] * jnp.exp(cum), bb * w_inv,
                      kp_c[c] * w_inv, bb * w_tail, kp_c[c] * w_tail, v_c[c], jnp.exp(cum_last)))
    probs = [(hh, c) for c in range(n_chunks) for hh in range(n_heads)]
    per_head = [[slabs[c][j][:, hh * N:(hh + 1) * N] for hh, c in probs] for j in range(8)]
    ops = _rwkv_chunk_ops(*per_head, lower_strict, lower_incl, eye, blockdiag)
    states = [state_ref[hh] for hh in range(n_heads)]
    ys = {}
    for c in range(n_chunks):
        idx = [c * n_heads + hh for hh in range(n_heads)]
        upd = _dot3_many([jnp.concatenate([ops[i][0], ops[i][2]], axis=0) for i in idx], states,
                         passes=RWKV_STATE_PASSES)
        for hh, i in enumerate(idx):
            ys[(hh, c)] = upd[hh][:L] + ops[i][1]
            states[hh] = upd[hh][L:] + ops[i][3]
    for hh in range(n_heads):
        state_ref[hh] = states[hh]
    y_c = [jnp.concatenate([ys[(hh, c)] for hh in range(n_heads)], axis=1) for c in range(n_chunks)]
    yc_c = [y - s * (1.0 / N) for y, s in zip(y_c, _lane_sums(y_c, head_ones))]
    var_c = [s * (1.0 / N) for s in _lane_sums([yc * yc for yc in yc_c], head_ones)]
    for c, rows in enumerate(chunk_rows):
        yn = yc_c[c] * lax.rsqrt(var_c[c] + RWKV_GN_EPS) * lnw_ref[...] + lnb_ref[...]
        o_ref[0, rows, :] = (yn + bonus_c[c]) * _silu(z_ref[0, rows, :])


def _rwkv_scan(r, k, v, z, lw, a, k_k, k_a, r_k, lnx_w, lnx_b):
    b, t, c = r.shape
    tb = min(RWKV_BLOCK, t)
    seq_spec = pl.BlockSpec((1, tb, RWKV_LANES), lambda i, p, j: (i, j, p))
    par_spec = pl.BlockSpec((1, RWKV_LANES), lambda i, p, j: (0, p))
    params = [x.reshape(1, c).astype(F32) for x in (k_k, k_a, r_k, lnx_w, lnx_b)]
    return pl.pallas_call(
        _rwkv_scan_kernel, grid=(b, c // RWKV_LANES, t // tb),
        in_specs=[seq_spec] * 6 + [par_spec] * 5, out_specs=seq_spec,
        out_shape=jax.ShapeDtypeStruct((b, t, c), F32),
        scratch_shapes=[pltpu.VMEM((RWKV_LANES // RWKV_HEAD, RWKV_HEAD, RWKV_HEAD), F32)],
        compiler_params=_cparams("parallel", "parallel", "arbitrary"), name="rwkv_scan",
    )(r, k, v, z, lw, a, *params)


def _rwkv_layer(x2d, g, mu, w_in, w0, w1, w2, a0, a1, a2, k_k, k_a, r_k, lnx_w, lnx_b, w_out,
                final_g, batch, seq):
    c = w0.shape[0]
    r, k, v, z, lw, a = _rwkv_pre(x2d, g, mu, w_in, w0, w1, w2, a0, a1, a2, seq)
    sh = (batch, seq, c)
    y = _rwkv_scan(r.reshape(sh), k.reshape(sh), v.reshape(sh), z.reshape(sh), lw.reshape(sh),
                   a.reshape(sh), k_k, k_a, r_k, lnx_w, lnx_b)
    return _out_proj(x2d, [y.reshape(x2d.shape[0], c)], None, w_out, final_g)


def _nsa_compress_kernel(u_ref, posa_ref, posb_ref, w1a_ref, w1b_ref, w2_ref, o_ref):
    u = u_ref[0, 0]
    ha = _bdot(u + posa_ref[...], w1a_ref[...])
    hb = _bdot(u + posb_ref[...], w1b_ref[...])
    h = ha + pltpu.roll(hb, hb.shape[0] - 1, 0)
    o_ref[0, 0] = _bdot(_silu(h), w2_ref[...])


def _nsa_compress(t_hm, pos, w1, w2):
    b, g, t, dh = t_hm.shape
    nch = t // NSA_CMP_STRIDE
    half = NSA_CMP_STRIDE * dh
    u = t_hm.reshape(b, g, nch, half)
    posf = pos.astype(F32).reshape(2, 1, half)
    hid = w1.shape[1]
    fixed = lambda i, j: (0, 0)
    return pl.pallas_call(
        _nsa_compress_kernel, grid=(b, g),
        in_specs=[pl.BlockSpec((1, 1, nch, half), lambda i, j: (i, j, 0, 0)),
                  pl.BlockSpec((1, half), fixed), pl.BlockSpec((1, half), fixed),
                  pl.BlockSpec((half, hid), fixed), pl.BlockSpec((half, hid), fixed),
                  pl.BlockSpec((hid, dh), fixed)],
        out_specs=pl.BlockSpec((1, 1, nch, dh), lambda i, j: (i, j, 0, 0)),
        out_shape=jax.ShapeDtypeStruct((b, g, nch, dh), F32),
        compiler_params=_cparams("parallel", "parallel"), name="nsa_compress",
    )(u, posf[0], posf[1], w1[:half].astype(BF16), w1[half:].astype(BF16), w2.astype(BF16))


def _nsa_cmp_attn_kernel(q_ref, kc_ref, vct_ref, gate_ref, o_ref, sel_ref, *, n_sel):
    g = pl.program_id(1)
    n = pl.program_id(2)
    tq = q_ref.shape[2]
    ncmp = kc_ref.shape[2]
    q = q_ref[0].reshape(GQA * tq, HEAD_DIM) * (HEAD_DIM ** -0.5)
    s = _bdot_nt(kc_ref[0, 0], q)
    cend = lax.broadcasted_iota(jnp.int32, (ncmp, tq), 0) * NSA_CMP_STRIDE + (NSA_CMP_LEN - 1)
    tpos = n * tq + lax.broadcasted_iota(jnp.int32, (ncmp, tq), 1)
    ok = jnp.concatenate([cend <= tpos] * GQA, axis=1)
    s = jnp.where(ok, s, NEG)
    m = jnp.max(s, axis=0, keepdims=True)
    e = jnp.where(ok, jnp.exp(s - m), 0.0)
    l = jnp.sum(e, axis=0, keepdims=True)
    p = e / jnp.where(l > 0.0, l, 1.0)
    o_t = jnp.dot(vct_ref[0, 0].astype(BF16), p.astype(BF16), preferred_element_type=F32)
    outs = []
    for r in range(GQA):
        outs.append((o_t[:, r * tq:(r + 1) * tq] * _gate_row(gate_ref, GQA * g + r)).T)
    o_ref[0] = jnp.concatenate(outs, axis=-1)

    psum = p[:, :tq]
    for r in range(1, GQA):
        psum = psum + p[:, r * tq:(r + 1) * tq]
    si = lax.broadcasted_iota(jnp.int32, (n_sel, ncmp), 0)
    ni = lax.broadcasted_iota(jnp.int32, (n_sel, ncmp), 1)
    ratio = NSA_SEL_LEN // NSA_CMP_STRIDE
    overlap = ((ni < ratio * (si + 1)) & (ni * NSA_CMP_STRIDE + NSA_CMP_LEN - 1 >= si * NSA_SEL_LEN))
    overlap = jnp.where(overlap, 1.0, 0.0).astype(BF16)
    imp = _dot_exact_rhs(psum, overlap, dot=lambda a, b: _bdot(b, a))
    blk = lax.broadcasted_iota(jnp.int32, (n_sel, tq), 0)
    tq_pos = n * tq + lax.broadcasted_iota(jnp.int32, (n_sel, tq), 1)
    cur = tq_pos // NSA_SEL_LEN
    forced = (blk == 0) | (blk == cur) | (blk == cur - 1)
    future = blk * NSA_SEL_LEN > tq_pos
    imp = jnp.where(forced, BIG, jnp.where(future, NEG, imp))
    n_grp = n_sel // SUBLANES
    imp_g = [imp[SUBLANES * b:SUBLANES * (b + 1)] for b in range(n_grp)]
    blk_g = blk[:SUBLANES]
    ranks = [jnp.zeros((SUBLANES, tq), F32) for _ in range(n_grp)]
    for j in range(n_sel):
        row = imp[j:j + 1, :]
        for b in range(n_grp):
            if b > j // SUBLANES:
                ahead = row >= imp_g[b]
            elif b < j // SUBLANES:
                ahead = row > imp_g[b]
            else:
                ahead = (row > imp_g[b]) | ((row == imp_g[b]) & (blk_g > j % SUBLANES))
            ranks[b] = ranks[b] + jnp.where(ahead, 1.0, 0.0)
    rank = jnp.concatenate(ranks, axis=0)
    sel = jnp.where(rank < float(min(NSA_TOPK, n_sel)), 1.0, 0.0)
    sel_ref[0, 0] = sel


def _nsa_cmp_attn(q, k_cmp, v_cmp, gates_t):
    b, _, t, _ = q.shape
    ncmp = k_cmp.shape[2]
    n_sel = t // NSA_SEL_LEN
    d = ATTN_HEADS * HEAD_DIM
    kern = functools.partial(_nsa_cmp_attn_kernel, n_sel=n_sel)
    tq = min(NSA_CMP_QUERIES, t)
    cmp_spec = pl.BlockSpec((1, 1, ncmp, HEAD_DIM), lambda i, g, n: (i, g, 0, 0))
    return pl.pallas_call(
        kern, grid=(b, KV_HEADS, t // tq),
        in_specs=[pl.BlockSpec((1, GQA, tq, HEAD_DIM), lambda i, g, n: (i, g, n, 0)),
                  cmp_spec, pl.BlockSpec((1, 1, HEAD_DIM, ncmp), lambda i, g, n: (i, g, 0, 0)),
                  pl.BlockSpec((1, LANES, tq), lambda i, g, n: (i, 0, n))],
        out_specs=[pl.BlockSpec((1, tq, GQA * HEAD_DIM), lambda i, g, n: (i, n, g)),
                   pl.BlockSpec((1, 1, n_sel, tq), lambda i, g, n: (i, g, 0, n))],
        out_shape=[jax.ShapeDtypeStruct((b, t, d), F32),
                   jax.ShapeDtypeStruct((b, KV_HEADS, n_sel, t), F32)],
        compiler_params=_cparams("parallel", "parallel", "parallel"), name="nsa_cmp_attn",
    )(q, k_cmp, jnp.swapaxes(v_cmp, 2, 3), gates_t)


def _nsa_sel_attn_kernel(q_ref, k_ref, vt_ref, selt_ref, bias_ref, far_ref, gate_ref, o_ref):
    g = pl.program_id(1)
    n = pl.program_id(2)
    tq = q_ref.shape[2]
    pw = Q_BLOCK
    bl = NSA_SEL_LEN
    fk = NSA_FAR_KEYS
    q_all = q_ref[0].reshape(GQA * tq, HEAD_DIM) * (HEAD_DIM ** -0.5)

    def block_rows(first_block, count, limit):
        rows = []
        for i in range(count):
            blk = first_block + i
            row = selt_ref[0, 0, pl.ds(jnp.clip(blk, 0, jnp.maximum(limit - 1, 0)), 1), :]
            rows.append(jnp.broadcast_to(jnp.where((blk >= 0) & (blk < limit), row, 0.0), (bl, tq)))
        return jnp.concatenate(rows, axis=0) > 0.5

    def values_t(*starts_widths):
        return jnp.concatenate([vt_ref[0, :, pl.ds(s, w)] for s, w in starts_widths], axis=1)

    prev = pl.multiple_of(jnp.maximum(n * tq - pw, 0), pw)
    diag = pl.multiple_of(n * tq, tq)
    kt = jnp.concatenate([k_ref[0, 0, pl.ds(prev, pw), :], k_ref[0, 0, pl.ds(diag, tq), :]], axis=0)
    vt = values_t((prev, pw), (diag, tq))
    kj = lax.broadcasted_iota(jnp.int32, (pw + tq, tq), 0)
    qi = lax.broadcasted_iota(jnp.int32, (pw + tq, tq), 1)
    n_blocks = (n + 1) * (tq // bl)
    first_near = n * (tq // bl) - pw // bl
    ok = block_rows(first_near, (pw + tq) // bl, n_blocks) & (pw + qi - kj >= 0)
    ok = jnp.concatenate([ok] * GQA, axis=1)
    tbl = bias_ref[0]
    far_row = far_ref[0]
    cols = []
    for r in range(GQA):
        prev_p = tbl[:pw, r * pw:(r + 1) * pw]
        diag_p = tbl[pw:, r * pw:(r + 1) * pw]
        for a in range(tq // pw):
            far_p = jnp.broadcast_to(far_row[:, r * tq + a * pw:r * tq + (a + 1) * pw], (pw, pw))
            below = [jnp.zeros((pw, pw), F32)] * (tq // pw - 1 - a)
            cols.append(jnp.concatenate([far_p] * a + [prev_p, diag_p] + below, axis=0))
    bias = jnp.concatenate(cols, axis=1)
    s = jnp.where(ok, _bdot_nt(kt, q_all) + bias, NEG)
    m = jnp.max(s, axis=0, keepdims=True)
    p = jnp.exp(s - m).astype(BF16)
    acc = jnp.dot(vt, p, preferred_element_type=F32)
    m = m - far_ref[0]
    far_blocks = jnp.maximum(first_near, 0)

    def body(c, carry):
        m, acc = carry
        start = pl.multiple_of(c * fk, fk)
        qk = _bdot_nt(k_ref[0, 0, pl.ds(start, fk), :], q_all)
        parts = []
        for i in range(fk // bl):
            blk = c * (fk // bl) + i
            row = selt_ref[0, 0, pl.ds(jnp.minimum(blk, jnp.maximum(far_blocks - 1, 0)), 1), :]
            row = jnp.where((blk < far_blocks) & (row > 0.5), 0.0, NEG)
            parts.append(qk[i * bl:(i + 1) * bl] + jnp.concatenate([row] * GQA, axis=1))
        s = jnp.concatenate(parts, axis=0)
        m_new = jnp.maximum(m, jnp.max(s, axis=0, keepdims=True))
        p = jnp.exp(s - m_new).astype(BF16)
        pv = jnp.dot(values_t((start, fk)), p, preferred_element_type=F32)
        return m_new, jnp.exp(m - m_new) * acc + pv

    n_far = (far_blocks * bl + fk - 1) // fk
    _, acc = lax.fori_loop(0, n_far, body, (m, acc))
    o_t = acc[:HEAD_DIM] / acc[HEAD_DIM:HEAD_DIM + 1]
    outs = []
    for r in range(GQA):
        o = o_t[:, r * tq:(r + 1) * tq] * _gate_row(gate_ref, ATTN_HEADS + GQA * g + r)
        outs.append(o.T)
    o_ref[0] = jnp.concatenate(outs, axis=-1)


def _nsa_sel_attn(q, ks, vst, sel_t, t5_table, gates_t):
    b, _, t, _ = q.shape
    n_sel = t // NSA_SEL_LEN
    d = ATTN_HEADS * HEAD_DIM
    tq = min(NSA_SEL_QUERIES, t)
    near = Q_BLOCK + tq
    width = GQA * tq
    assert t % NSA_FAR_KEYS == 0 or t <= near
    bias_near = _band_bias(t5_table, Q_BLOCK, 2 * Q_BLOCK, Q_BLOCK)
    bias_near = bias_near.reshape(KV_HEADS, GQA, Q_BLOCK, 2 * Q_BLOCK).transpose(0, 3, 1, 2)
    bias_near = bias_near.reshape(KV_HEADS, 2 * Q_BLOCK, GQA * Q_BLOCK)
    far = t5_table[T5_BUCKETS - 1].astype(F32).reshape(KV_HEADS, GQA, 1)
    far = jnp.broadcast_to(far, (KV_HEADS, GQA, tq)).reshape(KV_HEADS, 1, width)
    vst_ones = vst.reshape(b, KV_HEADS, HEAD_DIM, t)
    vst_ones = jnp.concatenate([vst_ones, jnp.ones_like(vst_ones)], axis=2)
    vst_ones = vst_ones.reshape(b, 2 * KV_HEADS * HEAD_DIM, t)
    return pl.pallas_call(
        _nsa_sel_attn_kernel, grid=(b, KV_HEADS, t // tq),
        in_specs=[pl.BlockSpec((1, GQA, tq, HEAD_DIM), lambda i, g, n: (i, g, n, 0)),
                  pl.BlockSpec((1, 1, t, HEAD_DIM), lambda i, g, n: (i, g, 0, 0)),
                  pl.BlockSpec((1, 2 * HEAD_DIM, t), lambda i, g, n: (i, g, 0)),
                  pl.BlockSpec((1, 1, n_sel, tq), lambda i, g, n: (i, g, 0, n)),
                  pl.BlockSpec((1, 2 * Q_BLOCK, GQA * Q_BLOCK), lambda i, g, n: (g, 0, 0)),
                  pl.BlockSpec((1, 1, width), lambda i, g, n: (g, 0, 0)),
                  pl.BlockSpec((1, LANES, tq), lambda i, g, n: (i, 0, n))],
        out_specs=pl.BlockSpec((1, tq, GQA * HEAD_DIM), lambda i, g, n: (i, n, g)),
        out_shape=jax.ShapeDtypeStruct((b, t, d), F32),
        compiler_params=_cparams("parallel", "parallel", "parallel"), name="nsa_sel_attn",
    )(q, ks, vst_ones, sel_t, bias_near, far, gates_t)


def _nsa_layer(x2d, g, w_in, pos_k, k_w1, k_w2, pos_v, v_w1, v_w2, w_out, t5_table, final_g,
               batch, seq):
    nq, nkv = ATTN_HEADS * HEAD_DIM, KV_HEADS * HEAD_DIM
    offs = [0, nq] + [nq + nkv * (i + 1) for i in range(6)]
    ws = [w_in[:, offs[i]:offs[i + 1]] for i in range(7)]
    n_gate = 3 * ATTN_HEADS
    wg = jnp.pad(w_in[:, offs[7]:offs[7] + n_gate], ((0, 0), (0, LANES - n_gate)))
    wz = w_in[:, offs[7] + n_gate:]
    q, kc, vc, ks, vst, kw, vwt, gates_t, z = _norm_proj(
        x2d, g, ws + [wg, wz],
        ["heads", "heads", "heads", "heads", "cols", "heads", "cols", "cols", "rows"],
        [BF16, F32, F32, BF16, BF16, BF16, BF16, F32, F32], batch, seq)
    k_cmp = _nsa_compress(kc, pos_k, k_w1, k_w2)
    v_cmp = _nsa_compress(vc, pos_v, v_w1, v_w2)
    o_cmp, sel_t = _nsa_cmp_attn(q, k_cmp, v_cmp, gates_t)
    o_sel = _nsa_sel_attn(q, ks, vst, sel_t, t5_table, gates_t)
    o_win = _band_attn(q, kw, vwt, t5_table, NSA_WINDOW, gates_t=gates_t, gate_col=2 * ATTN_HEADS)
    sh = x2d.shape
    return _out_proj(x2d, [o_cmp.reshape(sh), o_sel.reshape(sh), o_win.reshape(sh)], z, w_out,
                     final_g)


def _lru_kernel(x_ref, g_ref, win_u_ref, win_z_ref, cw_ref, cb_ref, wa_ref, ba_ref, wx_ref, bx_ref,
                sp_ref, wout_ref, *rest, half, has_fg):
    fg_ref = rest[0] if has_fg else None
    o_ref, tail_ref, h_ref = rest[-3:]
    j = pl.program_id(1)

    @pl.when(j == 0)
    def _():
        tail_ref[...] = jnp.zeros_like(tail_ref)
        h_ref[...] = jnp.zeros_like(h_ref)

    x = x_ref[...]
    xb = _rms(x, g_ref[...]).astype(BF16)
    u = jnp.dot(xb, win_u_ref[...], preferred_element_type=F32)
    z = jnp.dot(xb, win_z_ref[...], preferred_element_type=F32)
    tm, width = u.shape
    ext = jnp.concatenate([tail_ref[...], u], axis=0)
    tail_ref[...] = u[tm - 8:, :]
    uc = cb_ref[...] + cw_ref[CONV_WIDTH - 1:CONV_WIDTH, :] * u
    for s in range(1, CONV_WIDTH):
        uc = uc + cw_ref[CONV_WIDTH - 1 - s:CONV_WIDTH - s, :] * ext[8 - s:8 - s + tm, :]
    ucb = uc.astype(BF16)
    gr, gi = [], []
    for c in range(width // half):
        blk = ucb[:, c * half:(c + 1) * half]
        gr.append(jnp.dot(blk, wa_ref[c], preferred_element_type=F32))
        gi.append(jnp.dot(blk, wx_ref[c], preferred_element_type=F32))
    rg = jax.nn.sigmoid(jnp.concatenate(gr, axis=1) + ba_ref[...])
    ig = jax.nn.sigmoid(jnp.concatenate(gi, axis=1) + bx_ref[...])
    log_a = -LRU_C * rg * sp_ref[...]
    a = jnp.exp(log_a)
    bv = jnp.sqrt(1.0 - a * a) * (ig * uc)
    n_grp = tm // SUBLANES
    a = a.reshape(n_grp, SUBLANES, width)
    bv = bv.reshape(n_grp, SUBLANES, width)
    row = lax.broadcasted_iota(jnp.int32, (n_grp, SUBLANES, width), 1)
    sh = 1
    while sh < SUBLANES:
        a_s = jnp.where(row >= sh, pltpu.roll(a, sh, 1), 1.0)
        b_s = jnp.where(row >= sh, pltpu.roll(bv, sh, 1), 0.0)
        bv = a * b_s + bv
        a = a * a_s
        sh *= 2
    carry = h_ref[...]
    groups = []
    for i in range(n_grp):
        hg = bv[i] + a[i] * carry
        carry = hg[SUBLANES - 1:SUBLANES, :]
        groups.append(hg)
    h_ref[...] = carry
    gated = (jnp.concatenate(groups, axis=0) * _silu(z)).astype(BF16)
    y = x + jnp.dot(gated, wout_ref[...], preferred_element_type=F32)
    o_ref[...] = _rms(y, fg_ref[...]) if has_fg else y


def _block_diag(w, group):
    nb, n, _ = w.shape
    w = w.reshape(nb // group, group, n, n)
    eye = jnp.eye(group, dtype=w.dtype)
    return jnp.einsum('cgij,gh->cgihj', w, eye).reshape(nb // group, group * n, group * n)


def _lru_layer(x2d, g, w_in, conv_w, conv_b, ga_w, ga_b, gx_w, gx_b, lam, w_out, final_g, batch, seq):
    n, d = x2d.shape
    width = w_in.shape[1] // 2
    blk = ga_w.shape[1]
    group = LANES // math.gcd(blk, LANES)
    group = min(group, ga_w.shape[0])
    half = group * blk
    nsup = width // half
    tm = min(LRU_TILE, seq)
    tiles = seq // tm
    row = lambda i, j: (i * tiles + j, 0)
    fixed = lambda i, j: (0, 0)
    fixed3 = lambda i, j: (0, 0, 0)
    vec = lambda a: a.astype(F32).reshape(1, -1)
    softplus_neg_lam = jax.nn.softplus(-lam.astype(F32))
    has_fg = final_g is not None
    in_specs = [pl.BlockSpec((tm, d), row), pl.BlockSpec((1, d), fixed),
                pl.BlockSpec((d, width), fixed), pl.BlockSpec((d, width), fixed),
                pl.BlockSpec((CONV_WIDTH, width), fixed), pl.BlockSpec((1, width), fixed),
                pl.BlockSpec((nsup, half, half), fixed3), pl.BlockSpec((1, width), fixed),
                pl.BlockSpec((nsup, half, half), fixed3), pl.BlockSpec((1, width), fixed),
                pl.BlockSpec((1, width), fixed), pl.BlockSpec((width, d), fixed)]
    args = [x2d, g.reshape(1, d), w_in[:, :width].astype(BF16), w_in[:, width:].astype(BF16),
            conv_w.astype(F32), vec(conv_b), _block_diag(ga_w, group).astype(BF16), vec(ga_b),
            _block_diag(gx_w, group).astype(BF16), vec(gx_b), vec(softplus_neg_lam),
            w_out.astype(BF16)]
    if has_fg:
        in_specs.append(pl.BlockSpec((1, d), fixed))
        args.append(final_g.reshape(1, d))
    kern = functools.partial(_lru_kernel, half=half, has_fg=has_fg)
    return pl.pallas_call(
        kern, grid=(batch, tiles), in_specs=in_specs, out_specs=pl.BlockSpec((tm, d), row),
        out_shape=jax.ShapeDtypeStruct((n, d), F32),
        scratch_shapes=[pltpu.VMEM((8, width), F32), pltpu.VMEM((1, width), F32)],
        compiler_params=_cparams("parallel", "arbitrary"), name="rglru_layer",
    )(*args)


def kernel(x, t5_table, norm_g, final_g, a_w_in, a_sinks, a_w_out, b_mu, b_w_in, b_w0, b_w1, b_w2, b_a0, b_a1, b_a2, b_k_k, b_k_a, b_r_k, b_lnx_w, b_lnx_b, b_w_out, c_w_in, c_cmp_pos_k, c_cmp_k_w1, c_cmp_k_w2, c_cmp_pos_v, c_cmp_v_w1, c_cmp_v_w2, c_w_out, d_w_in, d_conv_w, d_conv_b, d_gate_a_w, d_gate_a_b, d_gate_x_w, d_gate_x_b, d_lambda, d_w_out):
    batch, seq, d = x.shape
    depth = norm_g.shape[0]
    h = x.reshape(batch * seq, d)
    for layer in range(depth):
        m, j = layer % 4, layer // 4
        g = norm_g[layer]
        fg = final_g if layer == depth - 1 else None
        if m == 0:
            h = _swa_layer(h, g, a_w_in[j], a_sinks[j], a_w_out[j], t5_table, fg, batch, seq)
        elif m == 1:
            h = _rwkv_layer(h, g, b_mu[j], b_w_in[j], b_w0[j], b_w1[j], b_w2[j], b_a0[j], b_a1[j],
                            b_a2[j], b_k_k[j], b_k_a[j], b_r_k[j], b_lnx_w[j], b_lnx_b[j],
                            b_w_out[j], fg, batch, seq)
        elif m == 2:
            h = _nsa_layer(h, g, c_w_in[j], c_cmp_pos_k[j], c_cmp_k_w1[j], c_cmp_k_w2[j],
                           c_cmp_pos_v[j], c_cmp_v_w1[j], c_cmp_v_w2[j], c_w_out[j], t5_table,
                           fg, batch, seq)
        else:
            h = _lru_layer(h, g, d_w_in[j], d_conv_w[j], d_conv_b[j], d_gate_a_w[j], d_gate_a_b[j],
                           d_gate_x_w[j], d_gate_x_b[j], d_lambda[j], d_w_out[j], fg, batch, seq)
    return h.reshape(batch, seq, d)
```

```python
import functools
import math

import jax
import jax.numpy as jnp
from jax import lax
from jax.experimental import pallas as pl
from jax.experimental.pallas import tpu as pltpu

F32 = jnp.float32
BF16 = jnp.bfloat16

EPS = 1e-6
NEG = -1e30
BIG = 1e30
T5_BUCKETS = 32
T5_MAX_DIST = 128
ATTN_HEADS = 16
HEAD_DIM = 64
KV_HEADS = 4
GQA = ATTN_HEADS // KV_HEADS
Q_BLOCK = 128
SWA_WINDOW = 128
RWKV_HEAD = 64
RWKV_GN_EPS = 64e-5
NSA_CMP_LEN = 32
NSA_CMP_STRIDE = 16
NSA_SEL_LEN = 64
NSA_TOPK = 16
NSA_WINDOW = 512
LRU_BLOCKS = 16
LRU_C = 8.0
CONV_WIDTH = 4

LANES = 128
SUBLANES = 8
VMEM_LIMIT = 56 * 1024 * 1024
ROW_TILE = 512
RWKV_CHUNK = 64
RWKV_BLOCK = 512
RWKV_LANES = 256
RWKV_SCORE_PASSES = 1
RWKV_VALUE_PASSES = 1
RWKV_INVERSE_PASSES = 1
RWKV_STATE_PASSES = 3
NSA_FAR_KEYS = 512
NSA_CMP_QUERIES = 512
NSA_SEL_QUERIES = 512
LRU_TILE = 512


def _cparams(*sem):
    return pltpu.CompilerParams(dimension_semantics=sem, vmem_limit_bytes=VMEM_LIMIT)


def _bdot(a, b):
    return jnp.dot(a.astype(BF16), b.astype(BF16), preferred_element_type=F32)


def _bdot_nt(a, b):
    return lax.dot_general(a.astype(BF16), b.astype(BF16), (((1,), (1,)), ((), ())),
                           preferred_element_type=F32)


def _bdot_tn(a, b):
    return lax.dot_general(a.astype(BF16), b.astype(BF16), (((0,), (0,)), ((), ())),
                           preferred_element_type=F32)


def _split2(x):
    hi = x.astype(BF16)
    lo = (x - hi.astype(F32)).astype(BF16)
    return hi, lo


def _dot3(a, b, dot=_bdot):
    ah, al = _split2(a)
    bh, bl = _split2(b)
    return dot(ah, bh) + (dot(ah, bl) + dot(al, bh))


def _dot_exact_rhs(a, b01, dot=_bdot):
    ah = a.astype(BF16)
    r1 = a - ah.astype(F32)
    am = r1.astype(BF16)
    al = (r1 - am.astype(F32)).astype(BF16)
    return dot(ah, b01) + (dot(am, b01) + dot(al, b01))


def _rms(x, g):
    return x * lax.rsqrt(jnp.mean(x * x, axis=-1, keepdims=True) + EPS) * g


def _silu(z):
    return z * jax.nn.sigmoid(z)


def _gate_row(gate_ref, c):
    return jax.nn.sigmoid(gate_ref[0, pl.ds(c, 1), :])


def _norm_proj_kernel(x_ref, g_ref, *refs, n_out, layouts):
    w_refs, o_refs = refs[:n_out], refs[n_out:]
    xb = _rms(x_ref[...], g_ref[...]).astype(BF16)
    for w_ref, o_ref, layout in zip(w_refs, o_refs, layouts):
        width = w_ref.shape[1]
        for c0 in range(0, width, 512):
            cw = min(512, width - c0)
            acc = jnp.dot(xb, w_ref[:, c0:c0 + cw], preferred_element_type=F32)
            if layout == "heads":
                for j in range(cw // HEAD_DIM):
                    o_ref[0, (c0 // HEAD_DIM) + j] = acc[:, j * HEAD_DIM:(j + 1) * HEAD_DIM].astype(o_ref.dtype)
            elif layout == "cols":
                o_ref[0, c0:c0 + cw, :] = acc.T.astype(o_ref.dtype)
            else:
                o_ref[:, c0:c0 + cw] = acc.astype(o_ref.dtype)


def _norm_proj(x2d, g, weights, layouts, dtypes, batch, seq):
    n, d = x2d.shape
    tm = min(ROW_TILE, seq)
    tiles_per_seq = seq // tm
    in_specs = [pl.BlockSpec((tm, d), lambda i: (i, 0)), pl.BlockSpec((1, d), lambda i: (0, 0))]
    out_specs, out_shapes = [], []
    for w, layout, dt in zip(weights, layouts, dtypes):
        width = w.shape[1]
        in_specs.append(pl.BlockSpec((d, width), lambda i: (0, 0)))
        if layout == "heads":
            nh = width // HEAD_DIM
            out_shapes.append(jax.ShapeDtypeStruct((batch, nh, seq, HEAD_DIM), dt))
            out_specs.append(pl.BlockSpec((1, nh, tm, HEAD_DIM),
                                          lambda i: (i // tiles_per_seq, 0, i % tiles_per_seq, 0)))
        elif layout == "cols":
            out_shapes.append(jax.ShapeDtypeStruct((batch, width, seq), dt))
            out_specs.append(pl.BlockSpec((1, width, tm),
                                          lambda i: (i // tiles_per_seq, 0, i % tiles_per_seq)))
        else:
            out_shapes.append(jax.ShapeDtypeStruct((n, width), dt))
            out_specs.append(pl.BlockSpec((tm, width), lambda i: (i, 0)))
    kern = functools.partial(_norm_proj_kernel, n_out=len(weights), layouts=tuple(layouts))
    return pl.pallas_call(
        kern, grid=(n // tm,), in_specs=in_specs, out_specs=out_specs, out_shape=out_shapes,
        compiler_params=_cparams("parallel"), name="norm_proj",
    )(x2d, g.reshape(1, d), *[w.astype(BF16) for w in weights])


def _out_proj_kernel(*refs, n_a, has_z, has_g):
    x_ref = refs[0]
    a_refs = refs[1:1 + n_a]
    pos = 1 + n_a
    z_ref = refs[pos] if has_z else None
    pos += int(has_z)
    w_ref = refs[pos]
    pos += 1
    g_ref = refs[pos] if has_g else None
    o_ref = refs[-1]
    a = a_refs[0][...].astype(F32)
    for r in a_refs[1:]:
        a = a + r[...].astype(F32)
    if has_z:
        a = a * _silu(z_ref[...])
    y = x_ref[...] + jnp.dot(a.astype(BF16), w_ref[...], preferred_element_type=F32)
    if has_g:
        y = _rms(y, g_ref[...])
    o_ref[...] = y


def _out_proj(x2d, a_list, z, w, final_g=None):
    n, d = x2d.shape
    c = w.shape[0]
    tm = min(ROW_TILE, n)
    row = lambda i: (i, 0)
    fixed = lambda i: (0, 0)
    in_specs = [pl.BlockSpec((tm, d), row)] + [pl.BlockSpec((tm, c), row) for _ in a_list]
    args = [x2d] + list(a_list)
    if z is not None:
        in_specs.append(pl.BlockSpec((tm, c), row))
        args.append(z)
    in_specs.append(pl.BlockSpec((c, d), fixed))
    args.append(w.astype(BF16))
    if final_g is not None:
        in_specs.append(pl.BlockSpec((1, d), fixed))
        args.append(final_g.reshape(1, d))
    kern = functools.partial(_out_proj_kernel, n_a=len(a_list), has_z=z is not None,
                             has_g=final_g is not None)
    return pl.pallas_call(
        kern, grid=(n // tm,), in_specs=in_specs, out_specs=pl.BlockSpec((tm, d), row),
        out_shape=jax.ShapeDtypeStruct((n, d), F32), compiler_params=_cparams("parallel"),
        name="out_proj",
    )(*args)


def _t5_bucket(dist):
    max_exact = T5_BUCKETS // 2
    d = jnp.maximum(dist, 0)
    df = jnp.maximum(d, 1).astype(F32)
    large = max_exact + (jnp.log(df / max_exact) / math.log(T5_MAX_DIST / max_exact)
                         * (T5_BUCKETS - max_exact)).astype(jnp.int32)
    large = jnp.minimum(large, T5_BUCKETS - 1)
    return jnp.where(d < max_exact, d, large)


def _band_bias(t5_table, q_rows, kc, offset):
    period = kc + q_rows
    j = jnp.arange(period)
    j = jnp.where(j >= kc, j - period, j)
    vec = jnp.take(t5_table, _t5_bucket(offset - j), axis=0).astype(F32).T
    heads = vec.shape[0]
    flat = jnp.tile(vec, (1, q_rows))[:, :q_rows * (period - 1)]
    return flat.reshape(heads, q_rows, period - 1)[:, :, :kc]


def _band_attn_kernel(*refs, nprev, window, has_sink, gate_col):
    q_ref, k_ref, vt_ref, bias_ref = refs[:4]
    pos = 4
    sink_ref = refs[pos] if has_sink else None
    pos += int(has_sink)
    gate_ref = refs[pos] if gate_col is not None else None
    o_ref = refs[-1]
    n = pl.program_id(1)
    tq = Q_BLOCK
    kc = (nprev + 1) * tq
    kj = lax.broadcasted_iota(jnp.int32, (kc, tq), 0)
    qi = lax.broadcasted_iota(jnp.int32, (kc, tq), 1)
    dist = nprev * tq + qi - kj
    kpos = (n - nprev) * tq + kj
    valid = (dist >= 0) & (dist < window) & (kpos >= 0)
    valid = jnp.concatenate([valid] * GQA, axis=1)
    starts = [pl.multiple_of(jnp.maximum(n - nprev + j, 0) * tq, tq) for j in range(nprev + 1)]
    groups = range(KV_HEADS)
    scores = []
    for g in groups:
        kg = jnp.concatenate([k_ref[0, g, pl.ds(s, tq), :] for s in starts], axis=0)
        qg = q_ref[0, GQA * g:GQA * (g + 1)].reshape(GQA * tq, HEAD_DIM) * (HEAD_DIM ** -0.5)
        scores.append(_bdot_nt(kg, qg))
    probs, maxes = [], []
    for g in groups:
        s = jnp.where(valid, scores[g] + bias_ref[g], NEG)
        m = jnp.max(s, axis=0, keepdims=True)
        if has_sink:
            m = jnp.maximum(m, sink_ref[g])
        probs.append(jnp.exp(s - m).astype(BF16))
        maxes.append(m)
    accs = []
    ones = jnp.ones((HEAD_DIM, kc), BF16)
    for g in groups:
        vtg = jnp.concatenate([vt_ref[0, g * HEAD_DIM:(g + 1) * HEAD_DIM, pl.ds(s, tq)] for s in starts],
                              axis=1)
        vtg = jnp.concatenate([vtg, ones], axis=0)
        accs.append(jnp.dot(vtg, probs[g], preferred_element_type=F32))
    for g in groups:
        l = accs[g][HEAD_DIM:HEAD_DIM + 1]
        if has_sink:
            l = l + jnp.exp(sink_ref[g] - maxes[g])
        o_t = accs[g][:HEAD_DIM] / l
        outs = []
        for r in range(GQA):
            oh = o_t[:, r * tq:(r + 1) * tq]
            if gate_col is not None:
                oh = oh * _gate_row(gate_ref, gate_col + GQA * g + r)
            outs.append(oh.T)
        o_ref[0, :, g * GQA * HEAD_DIM:(g + 1) * GQA * HEAD_DIM] = jnp.concatenate(outs, axis=-1)


def _band_attn(q, k, vt, t5_table, window, sinks=None, gates_t=None, gate_col=None):
    b, _, t, _ = q.shape
    nprev = -(-window // Q_BLOCK)
    kc = (nprev + 1) * Q_BLOCK
    width = GQA * Q_BLOCK
    bias = _band_bias(t5_table, Q_BLOCK, kc, nprev * Q_BLOCK)
    in_specs = [
        pl.BlockSpec((1, ATTN_HEADS, Q_BLOCK, HEAD_DIM), lambda i, n: (i, 0, n, 0)),
        pl.BlockSpec((1, KV_HEADS, t, HEAD_DIM), lambda i, n: (i, 0, 0, 0)),
        pl.BlockSpec((1, KV_HEADS * HEAD_DIM, t), lambda i, n: (i, 0, 0)),
        pl.BlockSpec((KV_HEADS, kc, width), lambda i, n: (0, 0, 0)),
    ]
    bias_t = bias.reshape(KV_HEADS, GQA, Q_BLOCK, kc).transpose(0, 3, 1, 2).reshape(KV_HEADS, kc, width)
    args = [q, k, vt, bias_t]
    if sinks is not None:
        in_specs.append(pl.BlockSpec((KV_HEADS, 1, width), lambda i, n: (0, 0, 0)))
        sk = jnp.broadcast_to(sinks.astype(F32).reshape(KV_HEADS, GQA, 1), (KV_HEADS, GQA, Q_BLOCK))
        args.append(sk.reshape(KV_HEADS, 1, width))
    if gates_t is not None:
        in_specs.append(pl.BlockSpec((1, LANES, Q_BLOCK), lambda i, n: (i, 0, n)))
        args.append(gates_t)
    kern = functools.partial(_band_attn_kernel, nprev=nprev, window=window,
                             has_sink=sinks is not None, gate_col=gate_col)
    d = ATTN_HEADS * HEAD_DIM
    return pl.pallas_call(
        kern, grid=(b, t // Q_BLOCK), in_specs=in_specs,
        out_specs=pl.BlockSpec((1, Q_BLOCK, d), lambda i, n: (i, n, 0)),
        out_shape=jax.ShapeDtypeStruct((b, t, d), F32),
        compiler_params=_cparams("parallel", "parallel"), name="band_attn",
    )(*args)


def _swa_layer(x2d, g, w_in, sinks, w_out, t5_table, final_g, batch, seq):
    nq, nkv = ATTN_HEADS * HEAD_DIM, KV_HEADS * HEAD_DIM
    ws = [w_in[:, :nq], w_in[:, nq:nq + nkv], w_in[:, nq + nkv:nq + 2 * nkv], w_in[:, nq + 2 * nkv:]]
    q, k, vt, z = _norm_proj(x2d, g, ws, ["heads", "heads", "cols", "rows"], [BF16, BF16, BF16, F32],
                             batch, seq)
    o = _band_attn(q, k, vt, t5_table, SWA_WINDOW, sinks=sinks)
    return _out_proj(x2d, [o.reshape(x2d.shape)], z, w_out, final_g)


def _rwkv_pre_kernel(x_ref, xp_ref, g_ref, mu_ref, wr_ref, wk_ref, wv_ref, wz_ref,
                     w0_ref, w1_ref, w2_ref, a0_ref, a1_ref, a2_ref,
                     r_ref, k_ref, v_ref, z_ref, lw_ref, a_ref, *, tiles_per_seq):
    i = pl.program_id(0)
    g = g_ref[...]
    xn = _rms(x_ref[...], g)
    prev = _rms(xp_ref[...], g)[7:8]
    prev = jnp.where(i % tiles_per_seq == 0, 0.0, prev)
    row = lax.broadcasted_iota(jnp.int32, xn.shape, 0)
    xprev = jnp.where(row == 0, prev, pltpu.roll(xn, 1, 0))
    xx = xprev - xn
    lerp = lambda s: xn + xx * mu_ref[s:s + 1, :]
    r_ref[...] = _bdot(lerp(0), wr_ref[...])
    k_ref[...] = _bdot(lerp(1), wk_ref[...])
    v_ref[...] = _bdot(lerp(2), wv_ref[...])
    z_ref[...] = _bdot(lerp(3), wz_ref[...])
    wl = w0_ref[...] + _bdot(jnp.tanh(_bdot(lerp(4), w1_ref[...])), w2_ref[...])
    sp = jnp.maximum(-wl, 0.0) + jnp.log1p(jnp.exp(-jnp.abs(wl)))
    lw_ref[...] = -jnp.exp(-sp - 0.5)
    al = a0_ref[...] + _bdot(_bdot(lerp(5), a1_ref[...]), a2_ref[...])
    a_ref[...] = jax.nn.sigmoid(al)


def _rwkv_pre(x2d, g, mu, w_in, w0, w1, w2, a0, a1, a2, seq):
    n, d = x2d.shape
    c = w0.shape[0]
    tm = min(ROW_TILE, seq)
    tiles_per_seq = seq // tm
    row = lambda i: (i, 0)
    fixed = lambda i: (0, 0)
    ws = [w_in[:, s * c:(s + 1) * c].astype(BF16) for s in range(4)]
    in_specs = [
        pl.BlockSpec((tm, d), row),
        pl.BlockSpec((8, d), lambda i: (jnp.maximum(i * (tm // 8) - 1, 0), 0)),
        pl.BlockSpec((1, d), fixed), pl.BlockSpec((6, d), fixed),
    ] + [pl.BlockSpec((d, c), fixed)] * 4 + [
        pl.BlockSpec((1, c), fixed), pl.BlockSpec(w1.shape, fixed), pl.BlockSpec(w2.shape, fixed),
        pl.BlockSpec((1, c), fixed), pl.BlockSpec(a1.shape, fixed), pl.BlockSpec(a2.shape, fixed),
    ]
    out = jax.ShapeDtypeStruct((n, c), F32)
    kern = functools.partial(_rwkv_pre_kernel, tiles_per_seq=tiles_per_seq)
    return pl.pallas_call(
        kern, grid=(n // tm,), in_specs=in_specs, out_specs=[pl.BlockSpec((tm, c), row)] * 6,
        out_shape=[out] * 6, compiler_params=_cparams("parallel"), name="rwkv_pre",
    )(x2d, x2d, g.reshape(1, d), mu, *ws, w0.reshape(1, c), w1.astype(BF16), w2.astype(BF16),
      a0.reshape(1, c), a1.astype(BF16), a2.astype(BF16))


def _dot3_many(a_list, b_list, dot=_bdot, passes=3):
    if passes == 1:
        return [dot(a, b) for a, b in zip(a_list, b_list)]
    sa = [_split2(a) for a in a_list]
    sb = [_split2(b) for b in b_list]
    hh = [dot(x[0], y[0]) for x, y in zip(sa, sb)]
    hl = [dot(x[0], y[1]) for x, y in zip(sa, sb)]
    lh = [dot(x[1], y[0]) for x, y in zip(sa, sb)]
    return [p + (q + r) for p, q, r in zip(hh, hl, lh)]


def _lane_sums(x_list, ones):
    parts = [_split2(x) for x in x_list]
    hi = [jnp.dot(p[0], ones, preferred_element_type=F32) for p in parts]
    lo = [jnp.dot(p[1], ones, preferred_element_type=F32) for p in parts]
    return [a + b for a, b in zip(hi, lo)]


def _rwkv_chunk_ops(ats, rts, bts, kts, bhs, khs, vs, wls, lower_strict, lower_incl, eye, blockdiag):
    L = RWKV_CHUNK
    n = len(ats)
    n_ch = ats[0].shape[1]
    bks = [jnp.concatenate([bt, kt], axis=0) for bt, kt in zip(bts, kts)]
    ars = [jnp.concatenate([at, rt], axis=0) for at, rt in zip(ats, rts)]
    As = _dot3_many(ars, bks, dot=_bdot_nt, passes=RWKV_SCORE_PASSES)
    a_ab = [jnp.where(lower_strict, A[:L, :L], 0.0) for A in As]
    a_rb = [jnp.where(lower_incl, A[L:, :L], 0.0) for A in As]
    a_k = [jnp.concatenate([jnp.where(lower_strict, A[:L, L:], 0.0),
                            jnp.where(lower_incl, A[L:, L:], 0.0)], axis=0) for A in As]
    dblk = [jnp.where(blockdiag, x, 0.0) for x in a_ab]
    akv = _dot3_many(a_k, vs, passes=RWKV_VALUE_PASSES)
    inv = functools.partial(_dot3_many, passes=RWKV_INVERSE_PASSES)
    d2 = inv(dblk, dblk)
    res = inv([eye + d for d in dblk] + d2, [eye + d for d in d2] + d2)
    s4, d4 = res[:n], res[n:]
    res = inv(d4 + d4, s4 + d4)
    s8 = [s + x for s, x in zip(s4, res[:n])]
    d8 = res[n:]
    tdiag = [s + x for s, x in zip(s8, inv(d8, s8))]
    rhs = [jnp.concatenate([at, x[:L]], axis=1) for at, x in zip(ats, akv)]
    res = inv(tdiag + tdiag, rhs + [x - d for x, d in zip(a_ab, dblk)])
    xt, nt = res[:n], res[n:]
    res = inv(nt + nt, xt + nt)
    u = [x + y for x, y in zip(xt, res[:n])]
    nt2 = res[n:]
    pq = [x + y for x, y in zip(u, inv(nt2, u))]
    res = _dot3_many(a_rb, pq, passes=RWKV_VALUE_PASSES)
    gh = [jnp.concatenate([rt, x[L:]], axis=1) + y for rt, x, y in zip(rts, akv, res)]
    res = _dot3_many(bhs + khs, pq + vs, dot=_bdot_tn, passes=RWKV_VALUE_PASSES)
    out = []
    for i in range(n):
        mc = res[i]
        m_op = mc[:, :n_ch] + eye * wls[i]
        c_op = mc[:, n_ch:] + res[n + i]
        out.append((gh[i][:, :n_ch], gh[i][:, n_ch:], m_op, c_op))
    return out


def _rwkv_scan_kernel(r_ref, k_ref, v_ref, z_ref, lw_ref, a_ref, kk_ref, ka_ref, rk_ref,
                      lnw_ref, lnb_ref, o_ref, state_ref):
    tb = pl.program_id(2)

    @pl.when(tb == 0)
    def _():
        state_ref[...] = jnp.zeros_like(state_ref)

    L = RWKV_CHUNK
    N = RWKV_HEAD
    ri = lax.broadcasted_iota(jnp.int32, (L, L), 0)
    ci = lax.broadcasted_iota(jnp.int32, (L, L), 1)
    lower_strict = ri > ci
    lower_incl = ri >= ci
    tri_incl = jnp.where(lower_incl, 1.0, 0.0).astype(BF16)
    eye = jnp.where(ri == ci, 1.0, 0.0).astype(F32)
    blockdiag = (ri // 16) == (ci // 16)
    n_heads = r_ref.shape[2] // N
    n_chunks = r_ref.shape[1] // L
    width = n_heads * N
    hi_ = lax.broadcasted_iota(jnp.int32, (width, width), 0) // N
    hj_ = lax.broadcasted_iota(jnp.int32, (width, width), 1) // N
    head_ones = jnp.where(hi_ == hj_, 1.0, 0.0).astype(BF16)
    chunk_rows = [slice(c * L, (c + 1) * L) for c in range(n_chunks)]
    r_c = [r_ref[0, rows, :] for rows in chunk_rows]
    v_c = [v_ref[0, rows, :] for rows in chunk_rows]
    lw_c = [lw_ref[0, rows, :] for rows in chunk_rows]
    kk_c, kp_c, a_c = [], [], []
    for rows in chunk_rows:
        k = k_ref[0, rows, :]
        a = a_ref[0, rows, :]
        kk_c.append(k * kk_ref[...])
        kp_c.append(k * (1.0 + (a - 1.0) * ka_ref[...]))
        a_c.append(a)
    sums = _lane_sums([x * x for x in kk_c] + [r * kp * rk_ref[...] for r, kp in zip(r_c, kp_c)],
                      head_ones)
    bonus_c = [sums[n_chunks + c] * v_c[c] for c in range(n_chunks)]
    cum_c = []
    for c in range(n_chunks):
        hi = lw_c[c].astype(BF16)
        r1 = lw_c[c] - hi.astype(F32)
        mid = r1.astype(BF16)
        lo = (r1 - mid.astype(F32)).astype(BF16)
        cum_c.append([jnp.dot(tri_incl, p, preferred_element_type=F32) for p in (hi, mid, lo)])
    cum_c = [a + (b + c) for a, b, c in cum_c]
    slabs = []
    for c in range(n_chunks):
        kk = kk_c[c] / jnp.maximum(jnp.sqrt(sums[c]), 1e-12)
        cum = cum_c[c]
        w_inv = jnp.exp(-cum)
        cum_last = cum[L - 1:L, :]
        w_tail = jnp.exp(cum_last - cum)
        bb = kk * a_c[c]
        slabs.append((-kk * jnp.exp(cum - lw_c[c]), r_c[c] * jnp.exp(cum), bb * w_inv,
                      kp_c[c] * w_inv, bb * w_tail, kp_c[c] * w_tail, v_c[c], jnp.exp(cum_last)))
    probs = [(hh, c) for c in range(n_chunks) for hh in range(n_heads)]
    per_head = [[slabs[c][j][:, hh * N:(hh + 1) * N] for hh, c in probs] for j in range(8)]
    ops = _rwkv_chunk_ops(*per_head, lower_strict, lower_incl, eye, blockdiag)
    states = [state_ref[hh] for hh in range(n_heads)]
    ys = {}
    for c in range(n_chunks):
        idx = [c * n_heads + hh for hh in range(n_heads)]
        upd = _dot3_many([jnp.concatenate([ops[i][0], ops[i][2]], axis=0) for i in idx], states,
                         passes=RWKV_STATE_PASSES)
        for hh, i in enumerate(idx):
            ys[(hh, c)] = upd[hh][:L] + ops[i][1]
            states[hh] = upd[hh][L:] + ops[i][3]
    for hh in range(n_heads):
        state_ref[hh] = states[hh]
    y_c = [jnp.concatenate([ys[(hh, c)] for hh in range(n_heads)], axis=1) for c in range(n_chunks)]
    yc_c = [y - s * (1.0 / N) for y, s in zip(y_c, _lane_sums(y_c, head_ones))]
    var_c = [s * (1.0 / N) for s in _lane_sums([yc * yc for yc in yc_c], head_ones)]
    for c, rows in enumerate(chunk_rows):
        yn = yc_c[c] * lax.rsqrt(var_c[c] + RWKV_GN_EPS) * lnw_ref[...] + lnb_ref[...]
        o_ref[0, rows, :] = (yn + bonus_c[c]) * _silu(z_ref[0, rows, :])


def _rwkv_scan(r, k, v, z, lw, a, k_k, k_a, r_k, lnx_w, lnx_b):
    b, t, c = r.shape
    tb = min(RWKV_BLOCK, t)
    seq_spec = pl.BlockSpec((1, tb, RWKV_LANES), lambda i, p, j: (i, j, p))
    par_spec = pl.BlockSpec((1, RWKV_LANES), lambda i, p, j: (0, p))
    params = [x.reshape(1, c).astype(F32) for x in (k_k, k_a, r_k, lnx_w, lnx_b)]
    return pl.pallas_call(
        _rwkv_scan_kernel, grid=(b, c // RWKV_LANES, t // tb),
        in_specs=[seq_spec] * 6 + [par_spec] * 5, out_specs=seq_spec,
        out_shape=jax.ShapeDtypeStruct((b, t, c), F32),
        scratch_shapes=[pltpu.VMEM((RWKV_LANES // RWKV_HEAD, RWKV_HEAD, RWKV_HEAD), F32)],
        compiler_params=_cparams("parallel", "parallel", "arbitrary"), name="rwkv_scan",
    )(r, k, v, z, lw, a, *params)


def _rwkv_layer(x2d, g, mu, w_in, w0, w1, w2, a0, a1, a2, k_k, k_a, r_k, lnx_w, lnx_b, w_out,
                final_g, batch, seq):
    c = w0.shape[0]
    r, k, v, z, lw, a = _rwkv_pre(x2d, g, mu, w_in, w0, w1, w2, a0, a1, a2, seq)
    sh = (batch, seq, c)
    y = _rwkv_scan(r.reshape(sh), k.reshape(sh), v.reshape(sh), z.reshape(sh), lw.reshape(sh),
                   a.reshape(sh), k_k, k_a, r_k, lnx_w, lnx_b)
    return _out_proj(x2d, [y.reshape(x2d.shape[0], c)], None, w_out, final_g)


def _nsa_compress_kernel(u_ref, posa_ref, posb_ref, w1a_ref, w1b_ref, w2_ref, o_ref):
    u = u_ref[0, 0]
    ha = _bdot(u + posa_ref[...], w1a_ref[...])
    hb = _bdot(u + posb_ref[...], w1b_ref[...])
    h = ha + pltpu.roll(hb, hb.shape[0] - 1, 0)
    o_ref[0, 0] = _bdot(_silu(h), w2_ref[...])


def _nsa_compress(t_hm, pos, w1, w2):
    b, g, t, dh = t_hm.shape
    nch = t // NSA_CMP_STRIDE
    half = NSA_CMP_STRIDE * dh
    u = t_hm.reshape(b, g, nch, half)
    posf = pos.astype(F32).reshape(2, 1, half)
    hid = w1.shape[1]
    fixed = lambda i, j: (0, 0)
    return pl.pallas_call(
        _nsa_compress_kernel, grid=(b, g),
        in_specs=[pl.BlockSpec((1, 1, nch, half), lambda i, j: (i, j, 0, 0)),
                  pl.BlockSpec((1, half), fixed), pl.BlockSpec((1, half), fixed),
                  pl.BlockSpec((half, hid), fixed), pl.BlockSpec((half, hid), fixed),
                  pl.BlockSpec((hid, dh), fixed)],
        out_specs=pl.BlockSpec((1, 1, nch, dh), lambda i, j: (i, j, 0, 0)),
        out_shape=jax.ShapeDtypeStruct((b, g, nch, dh), F32),
        compiler_params=_cparams("parallel", "parallel"), name="nsa_compress",
    )(u, posf[0], posf[1], w1[:half].astype(BF16), w1[half:].astype(BF16), w2.astype(BF16))


def _nsa_cmp_attn_kernel(q_ref, kc_ref, vct_ref, gate_ref, o_ref, sel_ref, *, n_sel):
    g = pl.program_id(1)
    n = pl.program_id(2)
    tq = q_ref.shape[2]
    ncmp = kc_ref.shape[2]
    q = q_ref[0].reshape(GQA * tq, HEAD_DIM) * (HEAD_DIM ** -0.5)
    s = _bdot_nt(kc_ref[0, 0], q)
    cend = lax.broadcasted_iota(jnp.int32, (ncmp, tq), 0) * NSA_CMP_STRIDE + (NSA_CMP_LEN - 1)
    tpos = n * tq + lax.broadcasted_iota(jnp.int32, (ncmp, tq), 1)
    ok = jnp.concatenate([cend <= tpos] * GQA, axis=1)
    s = jnp.where(ok, s, NEG)
    m = jnp.max(s, axis=0, keepdims=True)
    e = jnp.where(ok, jnp.exp(s - m), 0.0)
    l = jnp.sum(e, axis=0, keepdims=True)
    p = e / jnp.where(l > 0.0, l, 1.0)
    o_t = jnp.dot(vct_ref[0, 0].astype(BF16), p.astype(BF16), preferred_element_type=F32)
    outs = []
    for r in range(GQA):
        outs.append((o_t[:, r * tq:(r + 1) * tq] * _gate_row(gate_ref, GQA * g + r)).T)
    o_ref[0] = jnp.concatenate(outs, axis=-1)

    psum = p[:, :tq]
    for r in range(1, GQA):
        psum = psum + p[:, r * tq:(r + 1) * tq]
    si = lax.broadcasted_iota(jnp.int32, (n_sel, ncmp), 0)
    ni = lax.broadcasted_iota(jnp.int32, (n_sel, ncmp), 1)
    ratio = NSA_SEL_LEN // NSA_CMP_STRIDE
    overlap = ((ni < ratio * (si + 1)) & (ni * NSA_CMP_STRIDE + NSA_CMP_LEN - 1 >= si * NSA_SEL_LEN))
    overlap = jnp.where(overlap, 1.0, 0.0).astype(BF16)
    imp = _dot_exact_rhs(psum, overlap, dot=lambda a, b: _bdot(b, a))
    blk = lax.broadcasted_iota(jnp.int32, (n_sel, tq), 0)
    tq_pos = n * tq + lax.broadcasted_iota(jnp.int32, (n_sel, tq), 1)
    cur = tq_pos // NSA_SEL_LEN
    forced = (blk == 0) | (blk == cur) | (blk == cur - 1)
    future = blk * NSA_SEL_LEN > tq_pos
    imp = jnp.where(forced, BIG, jnp.where(future, NEG, imp))
    n_grp = n_sel // SUBLANES
    blk_g = blk[:SUBLANES, :LANES]
    for c0 in range(0, tq, LANES):
        imp_c = imp[:, c0:c0 + LANES]
        imp_g = [imp_c[SUBLANES * b:SUBLANES * (b + 1)] for b in range(n_grp)]
        ranks = [jnp.zeros((SUBLANES, LANES), F32) for _ in range(n_grp)]
        for j in range(n_sel):
            row = imp_c[j:j + 1, :]
            for b in range(n_grp):
                if b > j // SUBLANES:
                    ahead = row >= imp_g[b]
                elif b < j // SUBLANES:
                    ahead = row > imp_g[b]
                else:
                    ahead = (row > imp_g[b]) | ((row == imp_g[b]) & (blk_g > j % SUBLANES))
                ranks[b] = ranks[b] + jnp.where(ahead, 1.0, 0.0)
        rank = jnp.concatenate(ranks, axis=0)
        sel = jnp.where(rank < float(min(NSA_TOPK, n_sel)), 1.0, 0.0)
        sel_ref[0, 0, :, c0:c0 + LANES] = sel


def _nsa_cmp_attn(q, k_cmp, v_cmp, gates_t):
    b, _, t, _ = q.shape
    ncmp = k_cmp.shape[2]
    n_sel = t // NSA_SEL_LEN
    d = ATTN_HEADS * HEAD_DIM
    kern = functools.partial(_nsa_cmp_attn_kernel, n_sel=n_sel)
    tq = min(NSA_CMP_QUERIES, t)
    cmp_spec = pl.BlockSpec((1, 1, ncmp, HEAD_DIM), lambda i, g, n: (i, g, 0, 0))
    return pl.pallas_call(
        kern, grid=(b, KV_HEADS, t // tq),
        in_specs=[pl.BlockSpec((1, GQA, tq, HEAD_DIM), lambda i, g, n: (i, g, n, 0)),
                  cmp_spec, pl.BlockSpec((1, 1, HEAD_DIM, ncmp), lambda i, g, n: (i, g, 0, 0)),
                  pl.BlockSpec((1, LANES, tq), lambda i, g, n: (i, 0, n))],
        out_specs=[pl.BlockSpec((1, tq, GQA * HEAD_DIM), lambda i, g, n: (i, n, g)),
                   pl.BlockSpec((1, 1, n_sel, tq), lambda i, g, n: (i, g, 0, n))],
        out_shape=[jax.ShapeDtypeStruct((b, t, d), F32),
                   jax.ShapeDtypeStruct((b, KV_HEADS, n_sel, t), F32)],
        compiler_params=_cparams("parallel", "parallel", "parallel"), name="nsa_cmp_attn",
    )(q, k_cmp, jnp.swapaxes(v_cmp, 2, 3), gates_t)


def _nsa_sel_attn_kernel(q_ref, k_ref, vt_ref, selt_ref, bias_ref, far_ref, gate_ref, o_ref):
    g = pl.program_id(1)
    n = pl.program_id(2)
    tq = q_ref.shape[2]
    pw = Q_BLOCK
    bl = NSA_SEL_LEN
    fk = NSA_FAR_KEYS
    q_all = q_ref[0].reshape(GQA * tq, HEAD_DIM) * (HEAD_DIM ** -0.5)
    q_t = q_all.astype(F32).T.astype(BF16)

    def block_rows(first_block, count, limit):
        rows = []
        for i in range(count):
            blk = first_block + i
            row = selt_ref[0, 0, pl.ds(jnp.clip(blk, 0, jnp.maximum(limit - 1, 0)), 1), :]
            rows.append(jnp.broadcast_to(jnp.where((blk >= 0) & (blk < limit), row, 0.0), (bl, tq)))
        return jnp.concatenate(rows, axis=0) > 0.5

    def values_t(*starts_widths):
        return jnp.concatenate([vt_ref[0, :, pl.ds(s, w)] for s, w in starts_widths], axis=1)

    prev = pl.multiple_of(jnp.maximum(n * tq - pw, 0), pw)
    diag = pl.multiple_of(n * tq, tq)
    kt = jnp.concatenate([k_ref[0, 0, pl.ds(prev, pw), :], k_ref[0, 0, pl.ds(diag, tq), :]], axis=0)
    vt = values_t((prev, pw), (diag, tq))
    kj = lax.broadcasted_iota(jnp.int32, (pw + tq, tq), 0)
    qi = lax.broadcasted_iota(jnp.int32, (pw + tq, tq), 1)
    n_blocks = (n + 1) * (tq // bl)
    first_near = n * (tq // bl) - pw // bl
    ok = block_rows(first_near, (pw + tq) // bl, n_blocks) & (pw + qi - kj >= 0)
    ok = jnp.concatenate([ok] * GQA, axis=1)
    tbl = bias_ref[0]
    far_row = far_ref[0]
    cols = []
    for r in range(GQA):
        prev_p = tbl[:pw, r * pw:(r + 1) * pw]
        diag_p = tbl[pw:, r * pw:(r + 1) * pw]
        for a in range(tq // pw):
            far_p = jnp.broadcast_to(far_row[:, r * tq + a * pw:r * tq + (a + 1) * pw], (pw, pw))
            below = [jnp.zeros((pw, pw), F32)] * (tq // pw - 1 - a)
            cols.append(jnp.concatenate([far_p] * a + [prev_p, diag_p] + below, axis=0))
    bias = jnp.concatenate(cols, axis=1)
    s = jnp.where(ok, jnp.dot(kt, q_t, preferred_element_type=F32) + bias, NEG)
    m = jnp.max(s, axis=0, keepdims=True)
    p = jnp.exp(s - m).astype(BF16)
    acc = jnp.dot(vt, p, preferred_element_type=F32)
    m = m - far_ref[0]
    far_blocks = jnp.maximum(first_near, 0)

    def body(c, carry):
        m, acc = carry
        start = pl.multiple_of(c * fk, fk)
        qk = jnp.dot(k_ref[0, 0, pl.ds(start, fk), :], q_t, preferred_element_type=F32)
        parts = []
        for i in range(fk // bl):
            blk = c * (fk // bl) + i
            row = selt_ref[0, 0, pl.ds(jnp.minimum(blk, jnp.maximum(far_blocks - 1, 0)), 1), :]
            row = jnp.where((blk < far_blocks) & (row > 0.5), 0.0, NEG)
            parts.append(qk[i * bl:(i + 1) * bl] + jnp.concatenate([row] * GQA, axis=1))
        s = jnp.concatenate(parts, axis=0)
        m_new = jnp.maximum(m, jnp.max(s, axis=0, keepdims=True))
        p = jnp.exp(s - m_new).astype(BF16)
        pv = jnp.dot(values_t((start, fk)), p, preferred_element_type=F32)
        return m_new, jnp.exp(m - m_new) * acc + pv

    n_far = (far_blocks * bl + fk - 1) // fk
    _, acc = lax.fori_loop(0, n_far, body, (m, acc))
    o_t = acc[:HEAD_DIM] / acc[HEAD_DIM:HEAD_DIM + 1]
    outs = []
    for r in range(GQA):
        o = o_t[:, r * tq:(r + 1) * tq] * _gate_row(gate_ref, ATTN_HEADS + GQA * g + r)
        outs.append(o.T)
    o_ref[0] = jnp.concatenate(outs, axis=-1)


def _nsa_sel_attn(q, ks, vst, sel_t, t5_table, gates_t):
    b, _, t, _ = q.shape
    n_sel = t // NSA_SEL_LEN
    d = ATTN_HEADS * HEAD_DIM
    tq = min(NSA_SEL_QUERIES, t)
    near = Q_BLOCK + tq
    width = GQA * tq
    assert t % NSA_FAR_KEYS == 0 or t <= near
    bias_near = _band_bias(t5_table, Q_BLOCK, 2 * Q_BLOCK, Q_BLOCK)
    bias_near = bias_near.reshape(KV_HEADS, GQA, Q_BLOCK, 2 * Q_BLOCK).transpose(0, 3, 1, 2)
    bias_near = bias_near.reshape(KV_HEADS, 2 * Q_BLOCK, GQA * Q_BLOCK)
    far = t5_table[T5_BUCKETS - 1].astype(F32).reshape(KV_HEADS, GQA, 1)
    far = jnp.broadcast_to(far, (KV_HEADS, GQA, tq)).reshape(KV_HEADS, 1, width)
    vst_ones = vst.reshape(b, KV_HEADS, HEAD_DIM, t)
    vst_ones = jnp.concatenate([vst_ones, jnp.ones_like(vst_ones)], axis=2)
    vst_ones = vst_ones.reshape(b, 2 * KV_HEADS * HEAD_DIM, t)
    return pl.pallas_call(
        _nsa_sel_attn_kernel, grid=(b, KV_HEADS, t // tq),
        in_specs=[pl.BlockSpec((1, GQA, tq, HEAD_DIM), lambda i, g, n: (i, g, n, 0)),
                  pl.BlockSpec((1, 1, t, HEAD_DIM), lambda i, g, n: (i, g, 0, 0)),
                  pl.BlockSpec((1, 2 * HEAD_DIM, t), lambda i, g, n: (i, g, 0)),
                  pl.BlockSpec((1, 1, n_sel, tq), lambda i, g, n: (i, g, 0, n)),
                  pl.BlockSpec((1, 2 * Q_BLOCK, GQA * Q_BLOCK), lambda i, g, n: (g, 0, 0)),
                  pl.BlockSpec((1, 1, width), lambda i, g, n: (g, 0, 0)),
                  pl.BlockSpec((1, LANES, tq), lambda i, g, n: (i, 0, n))],
        out_specs=pl.BlockSpec((1, tq, GQA * HEAD_DIM), lambda i, g, n: (i, n, g)),
        out_shape=jax.ShapeDtypeStruct((b, t, d), F32),
        compiler_params=_cparams("parallel", "parallel", "parallel"), name="nsa_sel_attn",
    )(q, ks, vst_ones, sel_t, bias_near, far, gates_t)


def _nsa_layer(x2d, g, w_in, pos_k, k_w1, k_w2, pos_v, v_w1, v_w2, w_out, t5_table, final_g,
               batch, seq):
    nq, nkv = ATTN_HEADS * HEAD_DIM, KV_HEADS * HEAD_DIM
    offs = [0, nq] + [nq + nkv * (i + 1) for i in range(6)]
    ws = [w_in[:, offs[i]:offs[i + 1]] for i in range(7)]
    n_gate = 3 * ATTN_HEADS
    wg = jnp.pad(w_in[:, offs[7]:offs[7] + n_gate], ((0, 0), (0, LANES - n_gate)))
    wz = w_in[:, offs[7] + n_gate:]
    q, kc, vc, ks, vst, kw, vwt, gates_t, z = _norm_proj(
        x2d, g, ws + [wg, wz],
        ["heads", "heads", "heads", "heads", "cols", "heads", "cols", "cols", "rows"],
        [BF16, F32, F32, BF16, BF16, BF16, BF16, F32, F32], batch, seq)
    k_cmp = _nsa_compress(kc, pos_k, k_w1, k_w2)
    v_cmp = _nsa_compress(vc, pos_v, v_w1, v_w2)
    o_cmp, sel_t = _nsa_cmp_attn(q, k_cmp, v_cmp, gates_t)
    o_sel = _nsa_sel_attn(q, ks, vst, sel_t, t5_table, gates_t)
    o_win = _band_attn(q, kw, vwt, t5_table, NSA_WINDOW, gates_t=gates_t, gate_col=2 * ATTN_HEADS)
    sh = x2d.shape
    return _out_proj(x2d, [o_cmp.reshape(sh), o_sel.reshape(sh), o_win.reshape(sh)], z, w_out,
                     final_g)


def _lru_kernel(x_ref, g_ref, win_u_ref, win_z_ref, cw_ref, cb_ref, wa_ref, ba_ref, wx_ref, bx_ref,
                sp_ref, wout_ref, *rest, half, has_fg):
    fg_ref = rest[0] if has_fg else None
    o_ref, tail_ref, h_ref = rest[-3:]
    j = pl.program_id(1)

    @pl.when(j == 0)
    def _():
        tail_ref[...] = jnp.zeros_like(tail_ref)
        h_ref[...] = jnp.zeros_like(h_ref)

    x = x_ref[...]
    xb = _rms(x, g_ref[...]).astype(BF16)
    u = jnp.dot(xb, win_u_ref[...], preferred_element_type=F32)
    z = jnp.dot(xb, win_z_ref[...], preferred_element_type=F32)
    tm, width = u.shape
    ext = jnp.concatenate([tail_ref[...], u], axis=0)
    tail_ref[...] = u[tm - 8:, :]
    uc = cb_ref[...] + cw_ref[CONV_WIDTH - 1:CONV_WIDTH, :] * u
    for s in range(1, CONV_WIDTH):
        uc = uc + cw_ref[CONV_WIDTH - 1 - s:CONV_WIDTH - s, :] * ext[8 - s:8 - s + tm, :]
    ucb = uc.astype(BF16)
    gr, gi = [], []
    for c in range(width // half):
        blk = ucb[:, c * half:(c + 1) * half]
        gr.append(jnp.dot(blk, wa_ref[c], preferred_element_type=F32))
        gi.append(jnp.dot(blk, wx_ref[c], preferred_element_type=F32))
    rg = jax.nn.sigmoid(jnp.concatenate(gr, axis=1) + ba_ref[...])
    ig = jax.nn.sigmoid(jnp.concatenate(gi, axis=1) + bx_ref[...])
    log_a = -LRU_C * rg * sp_ref[...]
    a = jnp.exp(log_a)
    bv = jnp.sqrt(1.0 - a * a) * (ig * uc)
    n_grp = tm // SUBLANES
    a = a.reshape(n_grp, SUBLANES, width)
    bv = bv.reshape(n_grp, SUBLANES, width)
    row = lax.broadcasted_iota(jnp.int32, (n_grp, SUBLANES, width), 1)
    sh = 1
    while sh < SUBLANES:
        a_s = jnp.where(row >= sh, pltpu.roll(a, sh, 1), 1.0)
        b_s = jnp.where(row >= sh, pltpu.roll(bv, sh, 1), 0.0)
        bv = a * b_s + bv
        a = a * a_s
        sh *= 2
    carry = h_ref[...]
    groups = []
    for i in range(n_grp):
        hg = bv[i] + a[i] * carry
        carry = hg[SUBLANES - 1:SUBLANES, :]
        groups.append(hg)
    h_ref[...] = carry
    gated = (jnp.concatenate(groups, axis=0) * _silu(z)).astype(BF16)
    y = x + jnp.dot(gated, wout_ref[...], preferred_element_type=F32)
    o_ref[...] = _rms(y, fg_ref[...]) if has_fg else y


def _block_diag(w, group):
    nb, n, _ = w.shape
    w = w.reshape(nb // group, group, n, n)
    eye = jnp.eye(group, dtype=w.dtype)
    return jnp.einsum('cgij,gh->cgihj', w, eye).reshape(nb // group, group * n, group * n)


def _lru_layer(x2d, g, w_in, conv_w, conv_b, ga_w, ga_b, gx_w, gx_b, lam, w_out, final_g, batch, seq):
    n, d = x2d.shape
    width = w_in.shape[1] // 2
    blk = ga_w.shape[1]
    group = LANES // math.gcd(blk, LANES)
    group = min(group, ga_w.shape[0])
    half = group * blk
    nsup = width // half
    tm = min(LRU_TILE, seq)
    tiles = seq // tm
    row = lambda i, j: (i * tiles + j, 0)
    fixed = lambda i, j: (0, 0)
    fixed3 = lambda i, j: (0, 0, 0)
    vec = lambda a: a.astype(F32).reshape(1, -1)
    softplus_neg_lam = jax.nn.softplus(-lam.astype(F32))
    has_fg = final_g is not None
    in_specs = [pl.BlockSpec((tm, d), row), pl.BlockSpec((1, d), fixed),
                pl.BlockSpec((d, width), fixed), pl.BlockSpec((d, width), fixed),
                pl.BlockSpec((CONV_WIDTH, width), fixed), pl.BlockSpec((1, width), fixed),
                pl.BlockSpec((nsup, half, half), fixed3), pl.BlockSpec((1, width), fixed),
                pl.BlockSpec((nsup, half, half), fixed3), pl.BlockSpec((1, width), fixed),
                pl.BlockSpec((1, width), fixed), pl.BlockSpec((width, d), fixed)]
    args = [x2d, g.reshape(1, d), w_in[:, :width].astype(BF16), w_in[:, width:].astype(BF16),
            conv_w.astype(F32), vec(conv_b), _block_diag(ga_w, group).astype(BF16), vec(ga_b),
            _block_diag(gx_w, group).astype(BF16), vec(gx_b), vec(softplus_neg_lam),
            w_out.astype(BF16)]
    if has_fg:
        in_specs.append(pl.BlockSpec((1, d), fixed))
        args.append(final_g.reshape(1, d))
    kern = functools.partial(_lru_kernel, half=half, has_fg=has_fg)
    return pl.pallas_call(
        kern, grid=(batch, tiles), in_specs=in_specs, out_specs=pl.BlockSpec((tm, d), row),
        out_shape=jax.ShapeDtypeStruct((n, d), F32),
        scratch_shapes=[pltpu.VMEM((8, width), F32), pltpu.VMEM((1, width), F32)],
        compiler_params=_cparams("parallel", "arbitrary"), name="rglru_layer",
    )(*args)


def kernel(x, t5_table, norm_g, final_g, a_w_in, a_sinks, a_w_out, b_mu, b_w_in, b_w0, b_w1, b_w2, b_a0, b_a1, b_a2, b_k_k, b_k_a, b_r_k, b_lnx_w, b_lnx_b, b_w_out, c_w_in, c_cmp_pos_k, c_cmp_k_w1, c_cmp_k_w2, c_cmp_pos_v, c_cmp_v_w1, c_cmp_v_w2, c_w_out, d_w_in, d_conv_w, d_conv_b, d_gate_a_w, d_gate_a_b, d_gate_x_w, d_gate_x_b, d_lambda, d_w_out):
    batch, seq, d = x.shape
    depth = norm_g.shape[0]
    h = x.reshape(batch * seq, d)
    for layer in range(depth):
        m, j = layer % 4, layer // 4
        g = norm_g[layer]
        fg = final_g if layer == depth - 1 else None
        if m == 0:
            h = _swa_layer(h, g, a_w_in[j], a_sinks[j], a_w_out[j], t5_table, fg, batch, seq)
        elif m == 1:
            h = _rwkv_layer(h, g, b_mu[j], b_w_in[j], b_w0[j], b_w1[j], b_w2[j], b_a0[j], b_a1[j],
                            b_a2[j], b_k_k[j], b_k_a[j], b_r_k[j], b_lnx_w[j], b_lnx_b[j],
                            b_w_out[j], fg, batch, seq)
        elif m == 2:
            h = _nsa_layer(h, g, c_w_in[j], c_cmp_pos_k[j], c_cmp_k_w1[j], c_cmp_k_w2[j],
                           c_cmp_pos_v[j], c_cmp_v_w1[j], c_cmp_v_w2[j], c_w_out[j], t5_table,
                           fg, batch, seq)
        else:
            h = _lru_layer(h, g, d_w_in[j], d_conv_w[j], d_conv_b[j], d_gate_a_w[j], d_gate_a_b[j],
                           d_gate_x_w[j], d_gate_x_b[j], d_lambda[j], d_w_out[j], fg, batch, seq)
    return h.reshape(batch, seq, d)
```

```python
import functools
import math

import jax
import jax.numpy as jnp
from jax import lax
from jax.experimental import pallas as pl
from jax.experimental.pallas import tpu as pltpu

F32 = jnp.float32
BF16 = jnp.bfloat16

EPS = 1e-6
NEG = -1e30
BIG = 1e30
T5_BUCKETS = 32
T5_MAX_DIST = 128
ATTN_HEADS = 16
HEAD_DIM = 64
KV_HEADS = 4
GQA = ATTN_HEADS // KV_HEADS
Q_BLOCK = 128
SWA_WINDOW = 128
RWKV_HEAD = 64
RWKV_GN_EPS = 64e-5
NSA_CMP_LEN = 32
NSA_CMP_STRIDE = 16
NSA_SEL_LEN = 64
NSA_TOPK = 16
NSA_WINDOW = 512
LRU_BLOCKS = 16
LRU_C = 8.0
CONV_WIDTH = 4

LANES = 128
SUBLANES = 8
VMEM_LIMIT = 56 * 1024 * 1024
ROW_TILE = 512
RWKV_CHUNK = 64
RWKV_BLOCK = 512
RWKV_LANES = 256
RWKV_SCORE_PASSES = 1
RWKV_VALUE_PASSES = 1
RWKV_INVERSE_PASSES = 1
RWKV_STATE_PASSES = 3
NSA_FAR_KEYS = 512
BAND_QUERIES = 512
NSA_CMP_QUERIES = 512
NSA_SEL_QUERIES = 512
LRU_TILE = 512


def _cparams(*sem):
    return pltpu.CompilerParams(dimension_semantics=sem, vmem_limit_bytes=VMEM_LIMIT)


def _bdot(a, b):
    return jnp.dot(a.astype(BF16), b.astype(BF16), preferred_element_type=F32)


def _bdot_nt(a, b):
    return lax.dot_general(a.astype(BF16), b.astype(BF16), (((1,), (1,)), ((), ())),
                           preferred_element_type=F32)


def _bdot_tn(a, b):
    return lax.dot_general(a.astype(BF16), b.astype(BF16), (((0,), (0,)), ((), ())),
                           preferred_element_type=F32)


def _split2(x):
    hi = x.astype(BF16)
    lo = (x - hi.astype(F32)).astype(BF16)
    return hi, lo


def _dot3(a, b, dot=_bdot):
    ah, al = _split2(a)
    bh, bl = _split2(b)
    return dot(ah, bh) + (dot(ah, bl) + dot(al, bh))


def _dot_exact_rhs(a, b01, dot=_bdot):
    ah = a.astype(BF16)
    r1 = a - ah.astype(F32)
    am = r1.astype(BF16)
    al = (r1 - am.astype(F32)).astype(BF16)
    return dot(ah, b01) + (dot(am, b01) + dot(al, b01))


def _rms(x, g):
    return x * lax.rsqrt(jnp.mean(x * x, axis=-1, keepdims=True) + EPS) * g


def _silu(z):
    return z * jax.nn.sigmoid(z)


def _gate_row(gate_ref, c):
    return jax.nn.sigmoid(gate_ref[0, pl.ds(c, 1), :])


def _norm_proj_kernel(x_ref, g_ref, *refs, n_out, layouts):
    w_refs, o_refs = refs[:n_out], refs[n_out:]
    xb = _rms(x_ref[...], g_ref[...]).astype(BF16)
    for w_ref, o_ref, layout in zip(w_refs, o_refs, layouts):
        width = w_ref.shape[1]
        for c0 in range(0, width, 512):
            cw = min(512, width - c0)
            acc = jnp.dot(xb, w_ref[:, c0:c0 + cw], preferred_element_type=F32)
            if layout == "heads":
                for j in range(cw // HEAD_DIM):
                    o_ref[0, (c0 // HEAD_DIM) + j] = acc[:, j * HEAD_DIM:(j + 1) * HEAD_DIM].astype(o_ref.dtype)
            elif layout == "cols":
                o_ref[0, c0:c0 + cw, :] = acc.T.astype(o_ref.dtype)
            else:
                o_ref[:, c0:c0 + cw] = acc.astype(o_ref.dtype)


def _norm_proj(x2d, g, weights, layouts, dtypes, batch, seq):
    n, d = x2d.shape
    tm = min(ROW_TILE, seq)
    tiles_per_seq = seq // tm
    in_specs = [pl.BlockSpec((tm, d), lambda i: (i, 0)), pl.BlockSpec((1, d), lambda i: (0, 0))]
    out_specs, out_shapes = [], []
    for w, layout, dt in zip(weights, layouts, dtypes):
        width = w.shape[1]
        in_specs.append(pl.BlockSpec((d, width), lambda i: (0, 0)))
        if layout == "heads":
            nh = width // HEAD_DIM
            out_shapes.append(jax.ShapeDtypeStruct((batch, nh, seq, HEAD_DIM), dt))
            out_specs.append(pl.BlockSpec((1, nh, tm, HEAD_DIM),
                                          lambda i: (i // tiles_per_seq, 0, i % tiles_per_seq, 0)))
        elif layout == "cols":
            out_shapes.append(jax.ShapeDtypeStruct((batch, width, seq), dt))
            out_specs.append(pl.BlockSpec((1, width, tm),
                                          lambda i: (i // tiles_per_seq, 0, i % tiles_per_seq)))
        else:
            out_shapes.append(jax.ShapeDtypeStruct((n, width), dt))
            out_specs.append(pl.BlockSpec((tm, width), lambda i: (i, 0)))
    kern = functools.partial(_norm_proj_kernel, n_out=len(weights), layouts=tuple(layouts))
    return pl.pallas_call(
        kern, grid=(n // tm,), in_specs=in_specs, out_specs=out_specs, out_shape=out_shapes,
        compiler_params=_cparams("parallel"), name="norm_proj",
    )(x2d, g.reshape(1, d), *[w.astype(BF16) for w in weights])


def _out_proj_kernel(*refs, n_a, has_z, has_g):
    x_ref = refs[0]
    a_refs = refs[1:1 + n_a]
    pos = 1 + n_a
    z_ref = refs[pos] if has_z else None
    pos += int(has_z)
    w_ref = refs[pos]
    pos += 1
    g_ref = refs[pos] if has_g else None
    o_ref = refs[-1]
    a = a_refs[0][...].astype(F32)
    for r in a_refs[1:]:
        a = a + r[...].astype(F32)
    if has_z:
        a = a * _silu(z_ref[...])
    y = x_ref[...] + jnp.dot(a.astype(BF16), w_ref[...], preferred_element_type=F32)
    if has_g:
        y = _rms(y, g_ref[...])
    o_ref[...] = y


def _out_proj(x2d, a_list, z, w, final_g=None):
    n, d = x2d.shape
    c = w.shape[0]
    tm = min(ROW_TILE, n)
    row = lambda i: (i, 0)
    fixed = lambda i: (0, 0)
    in_specs = [pl.BlockSpec((tm, d), row)] + [pl.BlockSpec((tm, c), row) for _ in a_list]
    args = [x2d] + list(a_list)
    if z is not None:
        in_specs.append(pl.BlockSpec((tm, c), row))
        args.append(z)
    in_specs.append(pl.BlockSpec((c, d), fixed))
    args.append(w.astype(BF16))
    if final_g is not None:
        in_specs.append(pl.BlockSpec((1, d), fixed))
        args.append(final_g.reshape(1, d))
    kern = functools.partial(_out_proj_kernel, n_a=len(a_list), has_z=z is not None,
                             has_g=final_g is not None)
    return pl.pallas_call(
        kern, grid=(n // tm,), in_specs=in_specs, out_specs=pl.BlockSpec((tm, d), row),
        out_shape=jax.ShapeDtypeStruct((n, d), F32), compiler_params=_cparams("parallel"),
        name="out_proj",
    )(*args)


def _t5_bucket(dist):
    max_exact = T5_BUCKETS // 2
    d = jnp.maximum(dist, 0)
    df = jnp.maximum(d, 1).astype(F32)
    large = max_exact + (jnp.log(df / max_exact) / math.log(T5_MAX_DIST / max_exact)
                         * (T5_BUCKETS - max_exact)).astype(jnp.int32)
    large = jnp.minimum(large, T5_BUCKETS - 1)
    return jnp.where(d < max_exact, d, large)


def _band_bias(t5_table, q_rows, kc, offset):
    period = kc + q_rows
    j = jnp.arange(period)
    j = jnp.where(j >= kc, j - period, j)
    vec = jnp.take(t5_table, _t5_bucket(offset - j), axis=0).astype(F32).T
    heads = vec.shape[0]
    flat = jnp.tile(vec, (1, q_rows))[:, :q_rows * (period - 1)]
    return flat.reshape(heads, q_rows, period - 1)[:, :, :kc]


def _band_attn_kernel(*refs, nprev, window, has_sink, gate_col):
    q_ref, k_ref, vt_ref, bias_ref = refs[:4]
    pos = 4
    sink_ref = refs[pos] if has_sink else None
    pos += int(has_sink)
    gate_ref = refs[pos] if gate_col is not None else None
    o_ref = refs[-1]
    tq = Q_BLOCK
    kc = (nprev + 1) * tq
    n_blocks = q_ref.shape[2] // tq
    kj = lax.broadcasted_iota(jnp.int32, (kc, tq), 0)
    qi = lax.broadcasted_iota(jnp.int32, (kc, tq), 1)
    dist = nprev * tq + qi - kj
    band = (dist >= 0) & (dist < window)
    ones = jnp.ones((HEAD_DIM, kc), BF16)
    probs_idx = [(sb, g) for sb in range(n_blocks) for g in range(KV_HEADS)]
    valids, starts = [], []
    for sb in range(n_blocks):
        n = pl.program_id(1) * n_blocks + sb
        valid = band & ((n - nprev) * tq + kj >= 0)
        valids.append(jnp.concatenate([valid] * GQA, axis=1))
        starts.append([pl.multiple_of(jnp.maximum(n - nprev + j, 0) * tq, tq)
                       for j in range(nprev + 1)])
    scores = []
    for sb, g in probs_idx:
        rows = slice(sb * tq, (sb + 1) * tq)
        kg = jnp.concatenate([k_ref[0, g, pl.ds(s, tq), :] for s in starts[sb]], axis=0)
        qg = q_ref[0, GQA * g:GQA * (g + 1), rows].reshape(GQA * tq, HEAD_DIM) * (HEAD_DIM ** -0.5)
        scores.append(_bdot_nt(kg, qg))
    probs, maxes = [], []
    for i, (sb, g) in enumerate(probs_idx):
        s = jnp.where(valids[sb], scores[i] + bias_ref[g], NEG)
        m = jnp.max(s, axis=0, keepdims=True)
        if has_sink:
            m = jnp.maximum(m, sink_ref[g])
        probs.append(jnp.exp(s - m).astype(BF16))
        maxes.append(m)
    accs = []
    for i, (sb, g) in enumerate(probs_idx):
        vtg = jnp.concatenate([vt_ref[0, g * HEAD_DIM:(g + 1) * HEAD_DIM, pl.ds(s, tq)]
                               for s in starts[sb]], axis=1)
        vtg = jnp.concatenate([vtg, ones], axis=0)
        accs.append(jnp.dot(vtg, probs[i], preferred_element_type=F32))
    for i, (sb, g) in enumerate(probs_idx):
        l = accs[i][HEAD_DIM:HEAD_DIM + 1]
        if has_sink:
            l = l + jnp.exp(sink_ref[g] - maxes[i])
        o_t = accs[i][:HEAD_DIM] / l
        outs = []
        for r in range(GQA):
            oh = o_t[:, r * tq:(r + 1) * tq]
            if gate_col is not None:
                c = gate_col + GQA * g + r
                oh = oh * jax.nn.sigmoid(gate_ref[0, c:c + 1, sb * tq:(sb + 1) * tq])
            outs.append(oh.T)
        o_ref[0, sb * tq:(sb + 1) * tq, g * GQA * HEAD_DIM:(g + 1) * GQA * HEAD_DIM] = (
            jnp.concatenate(outs, axis=-1))


def _band_attn(q, k, vt, t5_table, window, sinks=None, gates_t=None, gate_col=None):
    b, _, t, _ = q.shape
    nprev = -(-window // Q_BLOCK)
    kc = (nprev + 1) * Q_BLOCK
    width = GQA * Q_BLOCK
    bias = _band_bias(t5_table, Q_BLOCK, kc, nprev * Q_BLOCK)
    tb = min(BAND_QUERIES, t)
    in_specs = [
        pl.BlockSpec((1, ATTN_HEADS, tb, HEAD_DIM), lambda i, n: (i, 0, n, 0)),
        pl.BlockSpec((1, KV_HEADS, t, HEAD_DIM), lambda i, n: (i, 0, 0, 0)),
        pl.BlockSpec((1, KV_HEADS * HEAD_DIM, t), lambda i, n: (i, 0, 0)),
        pl.BlockSpec((KV_HEADS, kc, width), lambda i, n: (0, 0, 0)),
    ]
    bias_t = bias.reshape(KV_HEADS, GQA, Q_BLOCK, kc).transpose(0, 3, 1, 2).reshape(KV_HEADS, kc, width)
    args = [q, k, vt, bias_t]
    if sinks is not None:
        in_specs.append(pl.BlockSpec((KV_HEADS, 1, width), lambda i, n: (0, 0, 0)))
        sk = jnp.broadcast_to(sinks.astype(F32).reshape(KV_HEADS, GQA, 1), (KV_HEADS, GQA, Q_BLOCK))
        args.append(sk.reshape(KV_HEADS, 1, width))
    if gates_t is not None:
        in_specs.append(pl.BlockSpec((1, LANES, tb), lambda i, n: (i, 0, n)))
        args.append(gates_t)
    kern = functools.partial(_band_attn_kernel, nprev=nprev, window=window,
                             has_sink=sinks is not None, gate_col=gate_col)
    d = ATTN_HEADS * HEAD_DIM
    return pl.pallas_call(
        kern, grid=(b, t // tb), in_specs=in_specs,
        out_specs=pl.BlockSpec((1, tb, d), lambda i, n: (i, n, 0)),
        out_shape=jax.ShapeDtypeStruct((b, t, d), F32),
        compiler_params=_cparams("parallel", "parallel"), name="band_attn",
    )(*args)


def _swa_layer(x2d, g, w_in, sinks, w_out, t5_table, final_g, batch, seq):
    nq, nkv = ATTN_HEADS * HEAD_DIM, KV_HEADS * HEAD_DIM
    ws = [w_in[:, :nq], w_in[:, nq:nq + nkv], w_in[:, nq + nkv:nq + 2 * nkv], w_in[:, nq + 2 * nkv:]]
    q, k, vt, z = _norm_proj(x2d, g, ws, ["heads", "heads", "cols", "rows"], [BF16, BF16, BF16, F32],
                             batch, seq)
    o = _band_attn(q, k, vt, t5_table, SWA_WINDOW, sinks=sinks)
    return _out_proj(x2d, [o.reshape(x2d.shape)], z, w_out, final_g)


def _rwkv_pre_kernel(x_ref, xp_ref, g_ref, mu_ref, wr_ref, wk_ref, wv_ref, wz_ref,
                     w0_ref, w1_ref, w2_ref, a0_ref, a1_ref, a2_ref,
                     r_ref, k_ref, v_ref, z_ref, lw_ref, a_ref, *, tiles_per_seq):
    i = pl.program_id(0)
    g = g_ref[...]
    xn = _rms(x_ref[...], g)
    prev = _rms(xp_ref[...], g)[7:8]
    prev = jnp.where(i % tiles_per_seq == 0, 0.0, prev)
    row = lax.broadcasted_iota(jnp.int32, xn.shape, 0)
    xprev = jnp.where(row == 0, prev, pltpu.roll(xn, 1, 0))
    xx = xprev - xn
    lerp = lambda s: xn + xx * mu_ref[s:s + 1, :]
    r_ref[...] = _bdot(lerp(0), wr_ref[...])
    k_ref[...] = _bdot(lerp(1), wk_ref[...])
    v_ref[...] = _bdot(lerp(2), wv_ref[...])
    z_ref[...] = _bdot(lerp(3), wz_ref[...])
    wl = w0_ref[...] + _bdot(jnp.tanh(_bdot(lerp(4), w1_ref[...])), w2_ref[...])
    sp = jnp.maximum(-wl, 0.0) + jnp.log1p(jnp.exp(-jnp.abs(wl)))
    lw_ref[...] = -jnp.exp(-sp - 0.5)
    al = a0_ref[...] + _bdot(_bdot(lerp(5), a1_ref[...]), a2_ref[...])
    a_ref[...] = jax.nn.sigmoid(al)


def _rwkv_pre(x2d, g, mu, w_in, w0, w1, w2, a0, a1, a2, seq):
    n, d = x2d.shape
    c = w0.shape[0]
    tm = min(ROW_TILE, seq)
    tiles_per_seq = seq // tm
    row = lambda i: (i, 0)
    fixed = lambda i: (0, 0)
    ws = [w_in[:, s * c:(s + 1) * c].astype(BF16) for s in range(4)]
    in_specs = [
        pl.BlockSpec((tm, d), row),
        pl.BlockSpec((8, d), lambda i: (jnp.maximum(i * (tm // 8) - 1, 0), 0)),
        pl.BlockSpec((1, d), fixed), pl.BlockSpec((6, d), fixed),
    ] + [pl.BlockSpec((d, c), fixed)] * 4 + [
        pl.BlockSpec((1, c), fixed), pl.BlockSpec(w1.shape, fixed), pl.BlockSpec(w2.shape, fixed),
        pl.BlockSpec((1, c), fixed), pl.BlockSpec(a1.shape, fixed), pl.BlockSpec(a2.shape, fixed),
    ]
    out = jax.ShapeDtypeStruct((n, c), F32)
    kern = functools.partial(_rwkv_pre_kernel, tiles_per_seq=tiles_per_seq)
    return pl.pallas_call(
        kern, grid=(n // tm,), in_specs=in_specs, out_specs=[pl.BlockSpec((tm, c), row)] * 6,
        out_shape=[out] * 6, compiler_params=_cparams("parallel"), name="rwkv_pre",
    )(x2d, x2d, g.reshape(1, d), mu, *ws, w0.reshape(1, c), w1.astype(BF16), w2.astype(BF16),
      a0.reshape(1, c), a1.astype(BF16), a2.astype(BF16))


def _dot3_many(a_list, b_list, dot=_bdot, passes=3):
    if passes == 1:
        return [dot(a, b) for a, b in zip(a_list, b_list)]
    sa = [_split2(a) for a in a_list]
    sb = [_split2(b) for b in b_list]
    hh = [dot(x[0], y[0]) for x, y in zip(sa, sb)]
    hl = [dot(x[0], y[1]) for x, y in zip(sa, sb)]
    lh = [dot(x[1], y[0]) for x, y in zip(sa, sb)]
    return [p + (q + r) for p, q, r in zip(hh, hl, lh)]


def _lane_sums(x_list, ones):
    parts = [_split2(x) for x in x_list]
    hi = [jnp.dot(p[0], ones, preferred_element_type=F32) for p in parts]
    lo = [jnp.dot(p[1], ones, preferred_element_type=F32) for p in parts]
    return [a + b for a, b in zip(hi, lo)]


def _rwkv_chunk_ops(ats, rts, bts, kts, bhs, khs, vs, wls, lower_strict, lower_incl, eye, blockdiag):
    L = RWKV_CHUNK
    n = len(ats)
    n_ch = ats[0].shape[1]
    bks = [jnp.concatenate([bt, kt], axis=0) for bt, kt in zip(bts, kts)]
    ars = [jnp.concatenate([at, rt], axis=0) for at, rt in zip(ats, rts)]
    As = _dot3_many(ars, bks, dot=_bdot_nt, passes=RWKV_SCORE_PASSES)
    a_ab = [jnp.where(lower_strict, A[:L, :L], 0.0) for A in As]
    a_rb = [jnp.where(lower_incl, A[L:, :L], 0.0) for A in As]
    a_k = [jnp.concatenate([jnp.where(lower_strict, A[:L, L:], 0.0),
                            jnp.where(lower_incl, A[L:, L:], 0.0)], axis=0) for A in As]
    dblk = [jnp.where(blockdiag, x, 0.0) for x in a_ab]
    akv = _dot3_many(a_k, vs, passes=RWKV_VALUE_PASSES)
    inv = functools.partial(_dot3_many, passes=RWKV_INVERSE_PASSES)
    d2 = inv(dblk, dblk)
    res = inv([eye + d for d in dblk] + d2, [eye + d for d in d2] + d2)
    s4, d4 = res[:n], res[n:]
    res = inv(d4 + d4, s4 + d4)
    s8 = [s + x for s, x in zip(s4, res[:n])]
    d8 = res[n:]
    tdiag = [s + x for s, x in zip(s8, inv(d8, s8))]
    rhs = [jnp.concatenate([at, x[:L]], axis=1) for at, x in zip(ats, akv)]
    res = inv(tdiag + tdiag, rhs + [x - d for x, d in zip(a_ab, dblk)])
    xt, nt = res[:n], res[n:]
    res = inv(nt + nt, xt + nt)
    u = [x + y for x, y in zip(xt, res[:n])]
    nt2 = res[n:]
    pq = [x + y for x, y in zip(u, inv(nt2, u))]
    res = _dot3_many(a_rb, pq, passes=RWKV_VALUE_PASSES)
    gh = [jnp.concatenate([rt, x[L:]], axis=1) + y for rt, x, y in zip(rts, akv, res)]
    res = _dot3_many(bhs + khs, pq + vs, dot=_bdot_tn, passes=RWKV_VALUE_PASSES)
    out = []
    for i in range(n):
        mc = res[i]
        m_op = mc[:, :n_ch] + eye * wls[i]
        c_op = mc[:, n_ch:] + res[n + i]
        out.append((gh[i][:, :n_ch], gh[i][:, n_ch:], m_op, c_op))
    return out


def _rwkv_scan_kernel(r_ref, k_ref, v_ref, z_ref, lw_ref, a_ref, kk_ref, ka_ref, rk_ref,
                      lnw_ref, lnb_ref, o_ref, state_ref):
    tb = pl.program_id(2)

    @pl.when(tb == 0)
    def _():
        state_ref[...] = jnp.zeros_like(state_ref)

    L = RWKV_CHUNK
    N = RWKV_HEAD
    ri = lax.broadcasted_iota(jnp.int32, (L, L), 0)
    ci = lax.broadcasted_iota(jnp.int32, (L, L), 1)
    lower_strict = ri > ci
    lower_incl = ri >= ci
    tri_incl = jnp.where(lower_incl, 1.0, 0.0).astype(BF16)
    eye = jnp.where(ri == ci, 1.0, 0.0).astype(F32)
    blockdiag = (ri // 16) == (ci // 16)
    n_heads = r_ref.shape[2] // N
    n_chunks = r_ref.shape[1] // L
    width = n_heads * N
    hi_ = lax.broadcasted_iota(jnp.int32, (width, width), 0) // N
    hj_ = lax.broadcasted_iota(jnp.int32, (width, width), 1) // N
    head_ones = jnp.where(hi_ == hj_, 1.0, 0.0).astype(BF16)
    chunk_rows = [slice(c * L, (c + 1) * L) for c in range(n_chunks)]
    r_c = [r_ref[0, rows, :] for rows in chunk_rows]
    v_c = [v_ref[0, rows, :] for rows in chunk_rows]
    lw_c = [lw_ref[0, rows, :] for rows in chunk_rows]
    kk_c, kp_c, a_c = [], [], []
    for rows in chunk_rows:
        k = k_ref[0, rows, :]
        a = a_ref[0, rows, :]
        kk_c.append(k * kk_ref[...])
        kp_c.append(k * (1.0 + (a - 1.0) * ka_ref[...]))
        a_c.append(a)
    sums = _lane_sums([x * x for x in kk_c] + [r * kp * rk_ref[...] for r, kp in zip(r_c, kp_c)],
                      head_ones)
    bonus_c = [sums[n_chunks + c] * v_c[c] for c in range(n_chunks)]
    cum_c = []
    for c in range(n_chunks):
        hi = lw_c[c].astype(BF16)
        r1 = lw_c[c] - hi.astype(F32)
        mid = r1.astype(BF16)
        lo = (r1 - mid.astype(F32)).astype(BF16)
        cum_c.append([jnp.dot(tri_incl, p, preferred_element_type=F32) for p in (hi, mid, lo)])
    cum_c = [a + (b + c) for a, b, c in cum_c]
    slabs = []
    for c in range(n_chunks):
        kk = kk_c[c] / jnp.maximum(jnp.sqrt(sums[c]), 1e-12)
        cum = cum_c[c]
        w_inv = jnp.exp(-cum)
        cum_last = cum[L - 1:L, :]
        w_tail = jnp.exp(cum_last - cum)
        bb = kk * a_c[c]
        slabs.append((-kk * jnp.exp(cum - lw_c[c]), r_c[c] * jnp.exp(cum), bb * w_inv,
                      kp_c[c] * w_inv, bb * w_tail, kp_c[c] * w_tail, v_c[c], jnp.exp(cum_last)))
    probs = [(hh, c) for c in range(n_chunks) for hh in range(n_heads)]
    per_head = [[slabs[c][j][:, hh * N:(hh + 1) * N] for hh, c in probs] for j in range(8)]
    ops = _rwkv_chunk_ops(*per_head, lower_strict, lower_incl, eye, blockdiag)
    states = [state_ref[hh] for hh in range(n_heads)]
    ys = {}
    for c in range(n_chunks):
        idx = [c * n_heads + hh for hh in range(n_heads)]
        out = _dot3_many([ops[i][0] for i in idx], states, passes=RWKV_VALUE_PASSES)
        upd = _dot3_many([ops[i][2] for i in idx], states, passes=RWKV_STATE_PASSES)
        for hh, i in enumerate(idx):
            ys[(hh, c)] = out[hh] + ops[i][1]
            states[hh] = upd[hh] + ops[i][3]
    for hh in range(n_heads):
        state_ref[hh] = states[hh]
    y_c = [jnp.concatenate([ys[(hh, c)] for hh in range(n_heads)], axis=1) for c in range(n_chunks)]
    yc_c = [y - s * (1.0 / N) for y, s in zip(y_c, _lane_sums(y_c, head_ones))]
    var_c = [s * (1.0 / N) for s in _lane_sums([yc * yc for yc in yc_c], head_ones)]
    for c, rows in enumerate(chunk_rows):
        yn = yc_c[c] * lax.rsqrt(var_c[c] + RWKV_GN_EPS) * lnw_ref[...] + lnb_ref[...]
        o_ref[0, rows, :] = (yn + bonus_c[c]) * _silu(z_ref[0, rows, :])


def _rwkv_scan(r, k, v, z, lw, a, k_k, k_a, r_k, lnx_w, lnx_b):
    b, t, c = r.shape
    tb = min(RWKV_BLOCK, t)
    seq_spec = pl.BlockSpec((1, tb, RWKV_LANES), lambda i, p, j: (i, j, p))
    par_spec = pl.BlockSpec((1, RWKV_LANES), lambda i, p, j: (0, p))
    params = [x.reshape(1, c).astype(F32) for x in (k_k, k_a, r_k, lnx_w, lnx_b)]
    return pl.pallas_call(
        _rwkv_scan_kernel, grid=(b, c // RWKV_LANES, t // tb),
        in_specs=[seq_spec] * 6 + [par_spec] * 5, out_specs=seq_spec,
        out_shape=jax.ShapeDtypeStruct((b, t, c), F32),
        scratch_shapes=[pltpu.VMEM((RWKV_LANES // RWKV_HEAD, RWKV_HEAD, RWKV_HEAD), F32)],
        compiler_params=_cparams("parallel", "parallel", "arbitrary"), name="rwkv_scan",
    )(r, k, v, z, lw, a, *params)


def _rwkv_layer(x2d, g, mu, w_in, w0, w1, w2, a0, a1, a2, k_k, k_a, r_k, lnx_w, lnx_b, w_out,
                final_g, batch, seq):
    c = w0.shape[0]
    r, k, v, z, lw, a = _rwkv_pre(x2d, g, mu, w_in, w0, w1, w2, a0, a1, a2, seq)
    sh = (batch, seq, c)
    y = _rwkv_scan(r.reshape(sh), k.reshape(sh), v.reshape(sh), z.reshape(sh), lw.reshape(sh),
                   a.reshape(sh), k_k, k_a, r_k, lnx_w, lnx_b)
    return _out_proj(x2d, [y.reshape(x2d.shape[0], c)], None, w_out, final_g)


def _nsa_compress_kernel(u_ref, posa_ref, posb_ref, w1a_ref, w1b_ref, w2_ref, o_ref):
    u = u_ref[0, 0]
    ha = _bdot(u + posa_ref[...], w1a_ref[...])
    hb = _bdot(u + posb_ref[...], w1b_ref[...])
    h = ha + pltpu.roll(hb, hb.shape[0] - 1, 0)
    o_ref[0, 0] = _bdot(_silu(h), w2_ref[...])


def _nsa_compress(t_hm, pos, w1, w2):
    b, g, t, dh = t_hm.shape
    nch = t // NSA_CMP_STRIDE
    half = NSA_CMP_STRIDE * dh
    u = t_hm.reshape(b, g, nch, half)
    posf = pos.astype(F32).reshape(2, 1, half)
    hid = w1.shape[1]
    fixed = lambda i, j: (0, 0)
    return pl.pallas_call(
        _nsa_compress_kernel, grid=(b, g),
        in_specs=[pl.BlockSpec((1, 1, nch, half), lambda i, j: (i, j, 0, 0)),
                  pl.BlockSpec((1, half), fixed), pl.BlockSpec((1, half), fixed),
                  pl.BlockSpec((half, hid), fixed), pl.BlockSpec((half, hid), fixed),
                  pl.BlockSpec((hid, dh), fixed)],
        out_specs=pl.BlockSpec((1, 1, nch, dh), lambda i, j: (i, j, 0, 0)),
        out_shape=jax.ShapeDtypeStruct((b, g, nch, dh), F32),
        compiler_params=_cparams("parallel", "parallel"), name="nsa_compress",
    )(u, posf[0], posf[1], w1[:half].astype(BF16), w1[half:].astype(BF16), w2.astype(BF16))


def _nsa_cmp_attn_kernel(q_ref, kc_ref, vct_ref, gate_ref, o_ref, sel_ref, *, n_sel):
    g = pl.program_id(1)
    n = pl.program_id(2)
    tq = q_ref.shape[2]
    ncmp = kc_ref.shape[2]
    q = q_ref[0].reshape(GQA * tq, HEAD_DIM) * (HEAD_DIM ** -0.5)
    s = _bdot_nt(kc_ref[0, 0], q)
    cend = lax.broadcasted_iota(jnp.int32, (ncmp, tq), 0) * NSA_CMP_STRIDE + (NSA_CMP_LEN - 1)
    tpos = n * tq + lax.broadcasted_iota(jnp.int32, (ncmp, tq), 1)
    ok = jnp.concatenate([cend <= tpos] * GQA, axis=1)
    s = jnp.where(ok, s, NEG)
    m = jnp.max(s, axis=0, keepdims=True)
    e = jnp.where(ok, jnp.exp(s - m), 0.0)
    l = jnp.sum(e, axis=0, keepdims=True)
    p = e / jnp.where(l > 0.0, l, 1.0)
    o_t = jnp.dot(vct_ref[0, 0].astype(BF16), p.astype(BF16), preferred_element_type=F32)
    outs = []
    for r in range(GQA):
        outs.append((o_t[:, r * tq:(r + 1) * tq] * _gate_row(gate_ref, GQA * g + r)).T)
    o_ref[0] = jnp.concatenate(outs, axis=-1)

    psum = p[:, :tq]
    for r in range(1, GQA):
        psum = psum + p[:, r * tq:(r + 1) * tq]
    si = lax.broadcasted_iota(jnp.int32, (n_sel, ncmp), 0)
    ni = lax.broadcasted_iota(jnp.int32, (n_sel, ncmp), 1)
    ratio = NSA_SEL_LEN // NSA_CMP_STRIDE
    overlap = ((ni < ratio * (si + 1)) & (ni * NSA_CMP_STRIDE + NSA_CMP_LEN - 1 >= si * NSA_SEL_LEN))
    overlap = jnp.where(overlap, 1.0, 0.0).astype(BF16)
    imp = _dot_exact_rhs(psum, overlap, dot=lambda a, b: _bdot(b, a))
    blk = lax.broadcasted_iota(jnp.int32, (n_sel, tq), 0)
    tq_pos = n * tq + lax.broadcasted_iota(jnp.int32, (n_sel, tq), 1)
    cur = tq_pos // NSA_SEL_LEN
    forced = (blk == 0) | (blk == cur) | (blk == cur - 1)
    future = blk * NSA_SEL_LEN > tq_pos
    imp = jnp.where(forced, BIG, jnp.where(future, NEG, imp))
    n_grp = n_sel // SUBLANES
    blk_g = blk[:SUBLANES, :LANES]
    for c0 in range(0, tq, LANES):
        imp_c = imp[:, c0:c0 + LANES]
        imp_g = [imp_c[SUBLANES * b:SUBLANES * (b + 1)] for b in range(n_grp)]
        ranks = [jnp.zeros((SUBLANES, LANES), F32) for _ in range(n_grp)]
        for j in range(n_sel):
            row = imp_c[j:j + 1, :]
            for b in range(n_grp):
                if b > j // SUBLANES:
                    ahead = row >= imp_g[b]
                elif b < j // SUBLANES:
                    ahead = row > imp_g[b]
                else:
                    ahead = (row > imp_g[b]) | ((row == imp_g[b]) & (blk_g > j % SUBLANES))
                ranks[b] = ranks[b] + jnp.where(ahead, 1.0, 0.0)
        rank = jnp.concatenate(ranks, axis=0)
        sel = jnp.where(rank < float(min(NSA_TOPK, n_sel)), 1.0, 0.0)
        sel_ref[0, 0, :, c0:c0 + LANES] = sel


def _nsa_cmp_attn(q, k_cmp, v_cmp, gates_t):
    b, _, t, _ = q.shape
    ncmp = k_cmp.shape[2]
    n_sel = t // NSA_SEL_LEN
    d = ATTN_HEADS * HEAD_DIM
    kern = functools.partial(_nsa_cmp_attn_kernel, n_sel=n_sel)
    tq = min(NSA_CMP_QUERIES, t)
    cmp_spec = pl.BlockSpec((1, 1, ncmp, HEAD_DIM), lambda i, g, n: (i, g, 0, 0))
    return pl.pallas_call(
        kern, grid=(b, KV_HEADS, t // tq),
        in_specs=[pl.BlockSpec((1, GQA, tq, HEAD_DIM), lambda i, g, n: (i, g, n, 0)),
                  cmp_spec, pl.BlockSpec((1, 1, HEAD_DIM, ncmp), lambda i, g, n: (i, g, 0, 0)),
                  pl.BlockSpec((1, LANES, tq), lambda i, g, n: (i, 0, n))],
        out_specs=[pl.BlockSpec((1, tq, GQA * HEAD_DIM), lambda i, g, n: (i, n, g)),
                   pl.BlockSpec((1, 1, n_sel, tq), lambda i, g, n: (i, g, 0, n))],
        out_shape=[jax.ShapeDtypeStruct((b, t, d), F32),
                   jax.ShapeDtypeStruct((b, KV_HEADS, n_sel, t), F32)],
        compiler_params=_cparams("parallel", "parallel", "parallel"), name="nsa_cmp_attn",
    )(q, k_cmp, jnp.swapaxes(v_cmp, 2, 3), gates_t)


def _nsa_sel_attn_kernel(q_ref, k_ref, vt_ref, selt_ref, bias_ref, far_ref, gate_ref, o_ref):
    g = pl.program_id(1)
    n = pl.program_id(2)
    tq = q_ref.shape[2]
    pw = Q_BLOCK
    bl = NSA_SEL_LEN
    fk = NSA_FAR_KEYS
    q_all = q_ref[0].reshape(GQA * tq, HEAD_DIM) * (HEAD_DIM ** -0.5)
    q_t = q_all.astype(F32).T.astype(BF16)

    def block_rows(first_block, count, limit):
        rows = []
        for i in range(count):
            blk = first_block + i
            row = selt_ref[0, 0, pl.ds(jnp.clip(blk, 0, jnp.maximum(limit - 1, 0)), 1), :]
            rows.append(jnp.broadcast_to(jnp.where((blk >= 0) & (blk < limit), row, 0.0), (bl, tq)))
        return jnp.concatenate(rows, axis=0) > 0.5

    def values_t(*starts_widths):
        return jnp.concatenate([vt_ref[0, :, pl.ds(s, w)] for s, w in starts_widths], axis=1)

    prev = pl.multiple_of(jnp.maximum(n * tq - pw, 0), pw)
    diag = pl.multiple_of(n * tq, tq)
    kt = jnp.concatenate([k_ref[0, 0, pl.ds(prev, pw), :], k_ref[0, 0, pl.ds(diag, tq), :]], axis=0)
    vt = values_t((prev, pw), (diag, tq))
    kj = lax.broadcasted_iota(jnp.int32, (pw + tq, tq), 0)
    qi = lax.broadcasted_iota(jnp.int32, (pw + tq, tq), 1)
    n_blocks = (n + 1) * (tq // bl)
    first_near = n * (tq // bl) - pw // bl
    ok = block_rows(first_near, (pw + tq) // bl, n_blocks) & (pw + qi - kj >= 0)
    ok = jnp.concatenate([ok] * GQA, axis=1)
    tbl = bias_ref[0]
    far_row = far_ref[0]
    cols = []
    for r in range(GQA):
        prev_p = tbl[:pw, r * pw:(r + 1) * pw]
        diag_p = tbl[pw:, r * pw:(r + 1) * pw]
        for a in range(tq // pw):
            far_p = jnp.broadcast_to(far_row[:, r * tq + a * pw:r * tq + (a + 1) * pw], (pw, pw))
            below = [jnp.zeros((pw, pw), F32)] * (tq // pw - 1 - a)
            cols.append(jnp.concatenate([far_p] * a + [prev_p, diag_p] + below, axis=0))
    bias = jnp.concatenate(cols, axis=1)
    s = jnp.where(ok, jnp.dot(kt, q_t, preferred_element_type=F32) + bias, NEG)
    m = jnp.max(s, axis=0, keepdims=True)
    p = jnp.exp(s - m).astype(BF16)
    acc = jnp.dot(vt, p, preferred_element_type=F32)
    m = m - far_ref[0]
    far_blocks = jnp.maximum(first_near, 0)

    def body(c, carry):
        m, acc = carry
        start = pl.multiple_of(c * fk, fk)
        qk = jnp.dot(k_ref[0, 0, pl.ds(start, fk), :], q_t, preferred_element_type=F32)
        parts = []
        for i in range(fk // bl):
            blk = c * (fk // bl) + i
            row = selt_ref[0, 0, pl.ds(jnp.minimum(blk, jnp.maximum(far_blocks - 1, 0)), 1), :]
            row = jnp.where((blk < far_blocks) & (row > 0.5), 0.0, NEG)
            parts.append(qk[i * bl:(i + 1) * bl] + jnp.concatenate([row] * GQA, axis=1))
        s = jnp.concatenate(parts, axis=0)
        m_new = jnp.maximum(m, jnp.max(s, axis=0, keepdims=True))
        p = jnp.exp(s - m_new).astype(BF16)
        pv = jnp.dot(values_t((start, fk)), p, preferred_element_type=F32)
        return m_new, jnp.exp(m - m_new) * acc + pv

    n_far = (far_blocks * bl + fk - 1) // fk
    _, acc = lax.fori_loop(0, n_far, body, (m, acc))
    o_t = acc[:HEAD_DIM] / acc[HEAD_DIM:HEAD_DIM + 1]
    outs = []
    for r in range(GQA):
        o = o_t[:, r * tq:(r + 1) * tq] * _gate_row(gate_ref, ATTN_HEADS + GQA * g + r)
        outs.append(o.T)
    o_ref[0] = jnp.concatenate(outs, axis=-1)


def _nsa_sel_attn(q, ks, vst, sel_t, t5_table, gates_t):
    b, _, t, _ = q.shape
    n_sel = t // NSA_SEL_LEN
    d = ATTN_HEADS * HEAD_DIM
    tq = min(NSA_SEL_QUERIES, t)
    near = Q_BLOCK + tq
    width = GQA * tq
    assert t % NSA_FAR_KEYS == 0 or t <= near
    bias_near = _band_bias(t5_table, Q_BLOCK, 2 * Q_BLOCK, Q_BLOCK)
    bias_near = bias_near.reshape(KV_HEADS, GQA, Q_BLOCK, 2 * Q_BLOCK).transpose(0, 3, 1, 2)
    bias_near = bias_near.reshape(KV_HEADS, 2 * Q_BLOCK, GQA * Q_BLOCK)
    far = t5_table[T5_BUCKETS - 1].astype(F32).reshape(KV_HEADS, GQA, 1)
    far = jnp.broadcast_to(far, (KV_HEADS, GQA, tq)).reshape(KV_HEADS, 1, width)
    vst_ones = vst.reshape(b, KV_HEADS, HEAD_DIM, t)
    vst_ones = jnp.concatenate([vst_ones, jnp.ones_like(vst_ones)], axis=2)
    vst_ones = vst_ones.reshape(b, 2 * KV_HEADS * HEAD_DIM, t)
    return pl.pallas_call(
        _nsa_sel_attn_kernel, grid=(b, KV_HEADS, t // tq),
        in_specs=[pl.BlockSpec((1, GQA, tq, HEAD_DIM), lambda i, g, n: (i, g, n, 0)),
                  pl.BlockSpec((1, 1, t, HEAD_DIM), lambda i, g, n: (i, g, 0, 0)),
                  pl.BlockSpec((1, 2 * HEAD_DIM, t), lambda i, g, n: (i, g, 0)),
                  pl.BlockSpec((1, 1, n_sel, tq), lambda i, g, n: (i, g, 0, n)),
                  pl.BlockSpec((1, 2 * Q_BLOCK, GQA * Q_BLOCK), lambda i, g, n: (g, 0, 0)),
                  pl.BlockSpec((1, 1, width), lambda i, g, n: (g, 0, 0)),
                  pl.BlockSpec((1, LANES, tq), lambda i, g, n: (i, 0, n))],
        out_specs=pl.BlockSpec((1, tq, GQA * HEAD_DIM), lambda i, g, n: (i, n, g)),
        out_shape=jax.ShapeDtypeStruct((b, t, d), F32),
        compiler_params=_cparams("parallel", "parallel", "parallel"), name="nsa_sel_attn",
    )(q, ks, vst_ones, sel_t, bias_near, far, gates_t)


def _nsa_layer(x2d, g, w_in, pos_k, k_w1, k_w2, pos_v, v_w1, v_w2, w_out, t5_table, final_g,
               batch, seq):
    nq, nkv = ATTN_HEADS * HEAD_DIM, KV_HEADS * HEAD_DIM
    offs = [0, nq] + [nq + nkv * (i + 1) for i in range(6)]
    ws = [w_in[:, offs[i]:offs[i + 1]] for i in range(7)]
    n_gate = 3 * ATTN_HEADS
    wg = jnp.pad(w_in[:, offs[7]:offs[7] + n_gate], ((0, 0), (0, LANES - n_gate)))
    wz = w_in[:, offs[7] + n_gate:]
    q, kc, vc, ks, vst, kw, vwt, gates_t, z = _norm_proj(
        x2d, g, ws + [wg, wz],
        ["heads", "heads", "heads", "heads", "cols", "heads", "cols", "cols", "rows"],
        [BF16, F32, F32, BF16, BF16, BF16, BF16, F32, F32], batch, seq)
    k_cmp = _nsa_compress(kc, pos_k, k_w1, k_w2)
    v_cmp = _nsa_compress(vc, pos_v, v_w1, v_w2)
    o_cmp, sel_t = _nsa_cmp_attn(q, k_cmp, v_cmp, gates_t)
    o_sel = _nsa_sel_attn(q, ks, vst, sel_t, t5_table, gates_t)
    o_win = _band_attn(q, kw, vwt, t5_table, NSA_WINDOW, gates_t=gates_t, gate_col=2 * ATTN_HEADS)
    sh = x2d.shape
    return _out_proj(x2d, [o_cmp.reshape(sh), o_sel.reshape(sh), o_win.reshape(sh)], z, w_out,
                     final_g)


def _lru_kernel(x_ref, g_ref, win_u_ref, win_z_ref, cw_ref, cb_ref, wa_ref, ba_ref, wx_ref, bx_ref,
                sp_ref, wout_ref, *rest, half, has_fg):
    fg_ref = rest[0] if has_fg else None
    o_ref, tail_ref, h_ref = rest[-3:]
    j = pl.program_id(1)

    @pl.when(j == 0)
    def _():
        tail_ref[...] = jnp.zeros_like(tail_ref)
        h_ref[...] = jnp.zeros_like(h_ref)

    x = x_ref[...]
    xb = _rms(x, g_ref[...]).astype(BF16)
    u = jnp.dot(xb, win_u_ref[...], preferred_element_type=F32)
    z = jnp.dot(xb, win_z_ref[...], preferred_element_type=F32)
    tm, width = u.shape
    ext = jnp.concatenate([tail_ref[...], u], axis=0)
    tail_ref[...] = u[tm - 8:, :]
    uc = cb_ref[...] + cw_ref[CONV_WIDTH - 1:CONV_WIDTH, :] * u
    for s in range(1, CONV_WIDTH):
        uc = uc + cw_ref[CONV_WIDTH - 1 - s:CONV_WIDTH - s, :] * ext[8 - s:8 - s + tm, :]
    ucb = uc.astype(BF16)
    gr, gi = [], []
    for c in range(width // half):
        blk = ucb[:, c * half:(c + 1) * half]
        gr.append(jnp.dot(blk, wa_ref[c], preferred_element_type=F32))
        gi.append(jnp.dot(blk, wx_ref[c], preferred_element_type=F32))
    rg = jax.nn.sigmoid(jnp.concatenate(gr, axis=1) + ba_ref[...])
    ig = jax.nn.sigmoid(jnp.concatenate(gi, axis=1) + bx_ref[...])
    log_a = -LRU_C * rg * sp_ref[...]
    a = jnp.exp(log_a)
    bv = jnp.sqrt(1.0 - a * a) * (ig * uc)
    n_grp = tm // SUBLANES
    a = a.reshape(n_grp, SUBLANES, width)
    bv = bv.reshape(n_grp, SUBLANES, width)
    row = lax.broadcasted_iota(jnp.int32, (n_grp, SUBLANES, width), 1)
    sh = 1
    while sh < SUBLANES:
        a_s = jnp.where(row >= sh, pltpu.roll(a, sh, 1), 1.0)
        b_s = jnp.where(row >= sh, pltpu.roll(bv, sh, 1), 0.0)
        bv = a * b_s + bv
        a = a * a_s
        sh *= 2
    carry = h_ref[...]
    groups = []
    for i in range(n_grp):
        hg = bv[i] + a[i] * carry
        carry = hg[SUBLANES - 1:SUBLANES, :]
        groups.append(hg)
    h_ref[...] = carry
    gated = (jnp.concatenate(groups, axis=0) * _silu(z)).astype(BF16)
    y = x + jnp.dot(gated, wout_ref[...], preferred_element_type=F32)
    o_ref[...] = _rms(y, fg_ref[...]) if has_fg else y


def _block_diag(w, group):
    nb, n, _ = w.shape
    w = w.reshape(nb // group, group, n, n)
    eye = jnp.eye(group, dtype=w.dtype)
    return jnp.einsum('cgij,gh->cgihj', w, eye).reshape(nb // group, group * n, group * n)


def _lru_layer(x2d, g, w_in, conv_w, conv_b, ga_w, ga_b, gx_w, gx_b, lam, w_out, final_g, batch, seq):
    n, d = x2d.shape
    width = w_in.shape[1] // 2
    blk = ga_w.shape[1]
    group = LANES // math.gcd(blk, LANES)
    group = min(group, ga_w.shape[0])
    half = group * blk
    nsup = width // half
    tm = min(LRU_TILE, seq)
    tiles = seq // tm
    row = lambda i, j: (i * tiles + j, 0)
    fixed = lambda i, j: (0, 0)
    fixed3 = lambda i, j: (0, 0, 0)
    vec = lambda a: a.astype(F32).reshape(1, -1)
    softplus_neg_lam = jax.nn.softplus(-lam.astype(F32))
    has_fg = final_g is not None
    in_specs = [pl.BlockSpec((tm, d), row), pl.BlockSpec((1, d), fixed),
                pl.BlockSpec((d, width), fixed), pl.BlockSpec((d, width), fixed),
                pl.BlockSpec((CONV_WIDTH, width), fixed), pl.BlockSpec((1, width), fixed),
                pl.BlockSpec((nsup, half, half), fixed3), pl.BlockSpec((1, width), fixed),
                pl.BlockSpec((nsup, half, half), fixed3), pl.BlockSpec((1, width), fixed),
                pl.BlockSpec((1, width), fixed), pl.BlockSpec((width, d), fixed)]
    args = [x2d, g.reshape(1, d), w_in[:, :width].astype(BF16), w_in[:, width:].astype(BF16),
            conv_w.astype(F32), vec(conv_b), _block_diag(ga_w, group).astype(BF16), vec(ga_b),
            _block_diag(gx_w, group).astype(BF16), vec(gx_b), vec(softplus_neg_lam),
            w_out.astype(BF16)]
    if has_fg:
        in_specs.append(pl.BlockSpec((1, d), fixed))
        args.append(final_g.reshape(1, d))
    kern = functools.partial(_lru_kernel, half=half, has_fg=has_fg)
    return pl.pallas_call(
        kern, grid=(batch, tiles), in_specs=in_specs, out_specs=pl.BlockSpec((tm, d), row),
        out_shape=jax.ShapeDtypeStruct((n, d), F32),
        scratch_shapes=[pltpu.VMEM((8, width), F32), pltpu.VMEM((1, width), F32)],
        compiler_params=_cparams("parallel", "arbitrary"), name="rglru_layer",
    )(*args)


def kernel(x, t5_table, norm_g, final_g, a_w_in, a_sinks, a_w_out, b_mu, b_w_in, b_w0, b_w1, b_w2, b_a0, b_a1, b_a2, b_k_k, b_k_a, b_r_k, b_lnx_w, b_lnx_b, b_w_out, c_w_in, c_cmp_pos_k, c_cmp_k_w1, c_cmp_k_w2, c_cmp_pos_v, c_cmp_v_w1, c_cmp_v_w2, c_w_out, d_w_in, d_conv_w, d_conv_b, d_gate_a_w, d_gate_a_b, d_gate_x_w, d_gate_x_b, d_lambda, d_w_out):
    batch, seq, d = x.shape
    depth = norm_g.shape[0]
    h = x.reshape(batch * seq, d)
    for layer in range(depth):
        m, j = layer % 4, layer // 4
        g = norm_g[layer]
        fg = final_g if layer == depth - 1 else None
        if m == 0:
            h = _swa_layer(h, g, a_w_in[j], a_sinks[j], a_w_out[j], t5_table, fg, batch, seq)
        elif m == 1:
            h = _rwkv_layer(h, g, b_mu[j], b_w_in[j], b_w0[j], b_w1[j], b_w2[j], b_a0[j], b_a1[j],
                            b_a2[j], b_k_k[j], b_k_a[j], b_r_k[j], b_lnx_w[j], b_lnx_b[j],
                            b_w_out[j], fg, batch, seq)
        elif m == 2:
            h = _nsa_layer(h, g, c_w_in[j], c_cmp_pos_k[j], c_cmp_k_w1[j], c_cmp_k_w2[j],
                           c_cmp_pos_v[j], c_cmp_v_w1[j], c_cmp_v_w2[j], c_w_out[j], t5_table,
                           fg, batch, seq)
        else:
            h = _lru_layer(h, g, d_w_in[j], d_conv_w[j], d_conv_b[j], d_gate_a_w[j], d_gate_a_b[j],
                           d_gate_x_w[j], d_gate_x_b[j], d_lambda[j], d_w_out[j], fg, batch, seq)
    return h.reshape(batch, seq, d)
```

```python
import functools
import math

import jax
import jax.numpy as jnp
from jax import lax
from jax.experimental import pallas as pl
from jax.experimental.pallas import tpu as pltpu

F32 = jnp.float32
BF16 = jnp.bfloat16

EPS = 1e-6
NEG = -1e30
BIG = 1e30
T5_BUCKETS = 32
T5_MAX_DIST = 128
ATTN_HEADS = 16
HEAD_DIM = 64
KV_HEADS = 4
GQA = ATTN_HEADS // KV_HEADS
Q_BLOCK = 128
SWA_WINDOW = 128
RWKV_HEAD = 64
RWKV_GN_EPS = 64e-5
NSA_CMP_LEN = 32
NSA_CMP_STRIDE = 16
NSA_SEL_LEN = 64
NSA_TOPK = 16
NSA_WINDOW = 512
LRU_C = 8.0
CONV_WIDTH = 4

LANES = 128
SUBLANES = 8
VMEM_LIMIT = 56 * 1024 * 1024
ROW_TILE = 512
RWKV_CHUNK = 64
RWKV_BLOCK = 512
RWKV_LANES = 256
RWKV_SCORE_PASSES = 1
RWKV_VALUE_PASSES = 1
RWKV_INVERSE_PASSES = 1
RWKV_STATE_PASSES = 3
NSA_FAR_KEYS = 512
BAND_QUERIES = 512
NSA_CMP_QUERIES = 512
NSA_SEL_QUERIES = 512
LRU_TILE = 512


def _cparams(*sem):
    return pltpu.CompilerParams(dimension_semantics=sem, vmem_limit_bytes=VMEM_LIMIT)


def _bdot(a, b):
    return jnp.dot(a.astype(BF16), b.astype(BF16), preferred_element_type=F32)


def _bdot_nt(a, b):
    return lax.dot_general(a.astype(BF16), b.astype(BF16), (((1,), (1,)), ((), ())),
                           preferred_element_type=F32)


def _bdot_tn(a, b):
    return lax.dot_general(a.astype(BF16), b.astype(BF16), (((0,), (0,)), ((), ())),
                           preferred_element_type=F32)


def _split2(x):
    hi = x.astype(BF16)
    lo = (x - hi.astype(F32)).astype(BF16)
    return hi, lo


def _split3(x):
    hi = x.astype(BF16)
    r1 = x - hi.astype(F32)
    mid = r1.astype(BF16)
    return hi, mid, (r1 - mid.astype(F32)).astype(BF16)


def _dot_exact_rhs(a, b01, dot=_bdot):
    ah, am, al = _split3(a)
    return dot(ah, b01) + (dot(am, b01) + dot(al, b01))


def _rms(x, g):
    return x * lax.rsqrt(jnp.mean(x * x, axis=-1, keepdims=True) + EPS) * g


def _silu(z):
    return z * jax.nn.sigmoid(z)


def _gate_row(gate_ref, c):
    return jax.nn.sigmoid(gate_ref[0, pl.ds(c, 1), :])


def _norm_proj_kernel(x_ref, g_ref, *refs, n_out, layouts):
    w_refs, o_refs = refs[:n_out], refs[n_out:]
    xb = _rms(x_ref[...], g_ref[...]).astype(BF16)
    for w_ref, o_ref, layout in zip(w_refs, o_refs, layouts):
        width = w_ref.shape[1]
        for c0 in range(0, width, 512):
            cw = min(512, width - c0)
            acc = jnp.dot(xb, w_ref[:, c0:c0 + cw], preferred_element_type=F32)
            if layout == "heads":
                for j in range(cw // HEAD_DIM):
                    o_ref[0, (c0 // HEAD_DIM) + j] = acc[:, j * HEAD_DIM:(j + 1) * HEAD_DIM].astype(o_ref.dtype)
            elif layout == "cols":
                o_ref[0, c0:c0 + cw, :] = acc.T.astype(o_ref.dtype)
            else:
                o_ref[:, c0:c0 + cw] = acc.astype(o_ref.dtype)


def _norm_proj(x2d, g, weights, layouts, dtypes, batch, seq):
    n, d = x2d.shape
    tm = min(ROW_TILE, seq)
    tiles_per_seq = seq // tm
    in_specs = [pl.BlockSpec((tm, d), lambda i: (i, 0)), pl.BlockSpec((1, d), lambda i: (0, 0))]
    out_specs, out_shapes = [], []
    for w, layout, dt in zip(weights, layouts, dtypes):
        width = w.shape[1]
        in_specs.append(pl.BlockSpec((d, width), lambda i: (0, 0)))
        if layout == "heads":
            nh = width // HEAD_DIM
            out_shapes.append(jax.ShapeDtypeStruct((batch, nh, seq, HEAD_DIM), dt))
            out_specs.append(pl.BlockSpec((1, nh, tm, HEAD_DIM),
                                          lambda i: (i // tiles_per_seq, 0, i % tiles_per_seq, 0)))
        elif layout == "cols":
            out_shapes.append(jax.ShapeDtypeStruct((batch, width, seq), dt))
            out_specs.append(pl.BlockSpec((1, width, tm),
                                          lambda i: (i // tiles_per_seq, 0, i % tiles_per_seq)))
        else:
            out_shapes.append(jax.ShapeDtypeStruct((n, width), dt))
            out_specs.append(pl.BlockSpec((tm, width), lambda i: (i, 0)))
    kern = functools.partial(_norm_proj_kernel, n_out=len(weights), layouts=tuple(layouts))
    return pl.pallas_call(
        kern, grid=(n // tm,), in_specs=in_specs, out_specs=out_specs, out_shape=out_shapes,
        compiler_params=_cparams("parallel"), name="norm_proj",
    )(x2d, g.reshape(1, d), *[w.astype(BF16) for w in weights])


def _out_proj_kernel(*refs, n_a, has_z, has_g):
    x_ref = refs[0]
    a_refs = refs[1:1 + n_a]
    pos = 1 + n_a
    z_ref = refs[pos] if has_z else None
    pos += int(has_z)
    w_ref = refs[pos]
    pos += 1
    g_ref = refs[pos] if has_g else None
    o_ref = refs[-1]
    a = a_refs[0][...].astype(F32)
    for r in a_refs[1:]:
        a = a + r[...].astype(F32)
    if has_z:
        a = a * _silu(z_ref[...])
    y = x_ref[...] + jnp.dot(a.astype(BF16), w_ref[...], preferred_element_type=F32)
    if has_g:
        y = _rms(y, g_ref[...])
    o_ref[...] = y


def _out_proj(x2d, a_list, z, w, final_g=None):
    n, d = x2d.shape
    c = w.shape[0]
    tm = min(ROW_TILE, n)
    row = lambda i: (i, 0)
    fixed = lambda i: (0, 0)
    in_specs = [pl.BlockSpec((tm, d), row)] + [pl.BlockSpec((tm, c), row) for _ in a_list]
    args = [x2d] + list(a_list)
    if z is not None:
        in_specs.append(pl.BlockSpec((tm, c), row))
        args.append(z)
    in_specs.append(pl.BlockSpec((c, d), fixed))
    args.append(w.astype(BF16))
    if final_g is not None:
        in_specs.append(pl.BlockSpec((1, d), fixed))
        args.append(final_g.reshape(1, d))
    kern = functools.partial(_out_proj_kernel, n_a=len(a_list), has_z=z is not None,
                             has_g=final_g is not None)
    return pl.pallas_call(
        kern, grid=(n // tm,), in_specs=in_specs, out_specs=pl.BlockSpec((tm, d), row),
        out_shape=jax.ShapeDtypeStruct((n, d), F32), compiler_params=_cparams("parallel"),
        name="out_proj",
    )(*args)


def _t5_bucket(dist):
    max_exact = T5_BUCKETS // 2
    d = jnp.maximum(dist, 0)
    df = jnp.maximum(d, 1).astype(F32)
    large = max_exact + (jnp.log(df / max_exact) / math.log(T5_MAX_DIST / max_exact)
                         * (T5_BUCKETS - max_exact)).astype(jnp.int32)
    large = jnp.minimum(large, T5_BUCKETS - 1)
    return jnp.where(d < max_exact, d, large)


def _band_bias(t5_table, q_rows, kc, offset):
    period = kc + q_rows
    j = jnp.arange(period)
    j = jnp.where(j >= kc, j - period, j)
    vec = jnp.take(t5_table, _t5_bucket(offset - j), axis=0).astype(F32).T
    heads = vec.shape[0]
    flat = jnp.tile(vec, (1, q_rows))[:, :q_rows * (period - 1)]
    return flat.reshape(heads, q_rows, period - 1)[:, :, :kc]


def _band_attn_kernel(*refs, nprev, window, has_sink, gate_col):
    q_ref, k_ref, vt_ref, bias_ref = refs[:4]
    pos = 4
    sink_ref = refs[pos] if has_sink else None
    pos += int(has_sink)
    gate_ref = refs[pos] if gate_col is not None else None
    o_ref = refs[-1]
    tq = Q_BLOCK
    kc = (nprev + 1) * tq
    n_blocks = q_ref.shape[2] // tq
    kj = lax.broadcasted_iota(jnp.int32, (kc, tq), 0)
    qi = lax.broadcasted_iota(jnp.int32, (kc, tq), 1)
    dist = nprev * tq + qi - kj
    band = (dist >= 0) & (dist < window)
    ones = jnp.ones((HEAD_DIM, kc), BF16)
    probs_idx = [(sb, g) for sb in range(n_blocks) for g in range(KV_HEADS)]
    valids, starts = [], []
    for sb in range(n_blocks):
        n = pl.program_id(1) * n_blocks + sb
        valid = band & ((n - nprev) * tq + kj >= 0)
        valids.append(jnp.concatenate([valid] * GQA, axis=1))
        starts.append([pl.multiple_of(jnp.maximum(n - nprev + j, 0) * tq, tq)
                       for j in range(nprev + 1)])
    scores = []
    for sb, g in probs_idx:
        rows = slice(sb * tq, (sb + 1) * tq)
        kg = jnp.concatenate([k_ref[0, g, pl.ds(s, tq), :] for s in starts[sb]], axis=0)
        qg = q_ref[0, GQA * g:GQA * (g + 1), rows].reshape(GQA * tq, HEAD_DIM) * (HEAD_DIM ** -0.5)
        scores.append(_bdot_nt(kg, qg))
    probs, maxes = [], []
    for i, (sb, g) in enumerate(probs_idx):
        s = jnp.where(valids[sb], scores[i] + bias_ref[g], NEG)
        m = jnp.max(s, axis=0, keepdims=True)
        if has_sink:
            m = jnp.maximum(m, sink_ref[g])
        probs.append(jnp.exp(s - m).astype(BF16))
        maxes.append(m)
    accs = []
    for i, (sb, g) in enumerate(probs_idx):
        vtg = jnp.concatenate([vt_ref[0, g * HEAD_DIM:(g + 1) * HEAD_DIM, pl.ds(s, tq)]
                               for s in starts[sb]], axis=1)
        vtg = jnp.concatenate([vtg, ones], axis=0)
        accs.append(jnp.dot(vtg, probs[i], preferred_element_type=F32))
    for i, (sb, g) in enumerate(probs_idx):
        l = accs[i][HEAD_DIM:HEAD_DIM + 1]
        if has_sink:
            l = l + jnp.exp(sink_ref[g] - maxes[i])
        o_t = accs[i][:HEAD_DIM] / l
        outs = []
        for r in range(GQA):
            oh = o_t[:, r * tq:(r + 1) * tq]
            if gate_col is not None:
                c = gate_col + GQA * g + r
                oh = oh * jax.nn.sigmoid(gate_ref[0, c:c + 1, sb * tq:(sb + 1) * tq])
            outs.append(oh.T)
        o_ref[0, sb * tq:(sb + 1) * tq, g * GQA * HEAD_DIM:(g + 1) * GQA * HEAD_DIM] = (
            jnp.concatenate(outs, axis=-1))


def _band_attn(q, k, vt, t5_table, window, sinks=None, gates_t=None, gate_col=None):
    b, _, t, _ = q.shape
    nprev = -(-window // Q_BLOCK)
    kc = (nprev + 1) * Q_BLOCK
    width = GQA * Q_BLOCK
    bias = _band_bias(t5_table, Q_BLOCK, kc, nprev * Q_BLOCK)
    tb = min(BAND_QUERIES, t)
    in_specs = [
        pl.BlockSpec((1, ATTN_HEADS, tb, HEAD_DIM), lambda i, n: (i, 0, n, 0)),
        pl.BlockSpec((1, KV_HEADS, t, HEAD_DIM), lambda i, n: (i, 0, 0, 0)),
        pl.BlockSpec((1, KV_HEADS * HEAD_DIM, t), lambda i, n: (i, 0, 0)),
        pl.BlockSpec((KV_HEADS, kc, width), lambda i, n: (0, 0, 0)),
    ]
    bias_t = bias.reshape(KV_HEADS, GQA, Q_BLOCK, kc).transpose(0, 3, 1, 2).reshape(KV_HEADS, kc, width)
    args = [q, k, vt, bias_t]
    if sinks is not None:
        in_specs.append(pl.BlockSpec((KV_HEADS, 1, width), lambda i, n: (0, 0, 0)))
        sk = jnp.broadcast_to(sinks.astype(F32).reshape(KV_HEADS, GQA, 1), (KV_HEADS, GQA, Q_BLOCK))
        args.append(sk.reshape(KV_HEADS, 1, width))
    if gates_t is not None:
        in_specs.append(pl.BlockSpec((1, LANES, tb), lambda i, n: (i, 0, n)))
        args.append(gates_t)
    kern = functools.partial(_band_attn_kernel, nprev=nprev, window=window,
                             has_sink=sinks is not None, gate_col=gate_col)
    d = ATTN_HEADS * HEAD_DIM
    return pl.pallas_call(
        kern, grid=(b, t // tb), in_specs=in_specs,
        out_specs=pl.BlockSpec((1, tb, d), lambda i, n: (i, n, 0)),
        out_shape=jax.ShapeDtypeStruct((b, t, d), F32),
        compiler_params=_cparams("parallel", "parallel"), name="band_attn",
    )(*args)


def _swa_layer(x2d, g, w_in, sinks, w_out, t5_table, final_g, batch, seq):
    nq, nkv = ATTN_HEADS * HEAD_DIM, KV_HEADS * HEAD_DIM
    ws = [w_in[:, :nq], w_in[:, nq:nq + nkv], w_in[:, nq + nkv:nq + 2 * nkv], w_in[:, nq + 2 * nkv:]]
    q, k, vt, z = _norm_proj(x2d, g, ws, ["heads", "heads", "cols", "rows"], [BF16, BF16, BF16, F32],
                             batch, seq)
    o = _band_attn(q, k, vt, t5_table, SWA_WINDOW, sinks=sinks)
    return _out_proj(x2d, [o.reshape(x2d.shape)], z, w_out, final_g)


def _rwkv_pre_kernel(x_ref, xp_ref, g_ref, mu_ref, wr_ref, wk_ref, wv_ref, wz_ref,
                     w0_ref, w1_ref, w2_ref, a0_ref, a1_ref, a2_ref,
                     r_ref, k_ref, v_ref, z_ref, lw_ref, a_ref, *, tiles_per_seq):
    i = pl.program_id(0)
    g = g_ref[...]
    xn = _rms(x_ref[...], g)
    prev = _rms(xp_ref[...], g)[SUBLANES - 1:SUBLANES]
    prev = jnp.where(i % tiles_per_seq == 0, 0.0, prev)
    row = lax.broadcasted_iota(jnp.int32, xn.shape, 0)
    xprev = jnp.where(row == 0, prev, pltpu.roll(xn, 1, 0))
    xx = xprev - xn
    lerp = lambda s: xn + xx * mu_ref[s:s + 1, :]
    r_ref[...] = _bdot(lerp(0), wr_ref[...])
    k_ref[...] = _bdot(lerp(1), wk_ref[...])
    v_ref[...] = _bdot(lerp(2), wv_ref[...])
    z_ref[...] = _bdot(lerp(3), wz_ref[...])
    wl = w0_ref[...] + _bdot(jnp.tanh(_bdot(lerp(4), w1_ref[...])), w2_ref[...])
    sp = jnp.maximum(-wl, 0.0) + jnp.log1p(jnp.exp(-jnp.abs(wl)))
    lw_ref[...] = -jnp.exp(-sp - 0.5)
    al = a0_ref[...] + _bdot(_bdot(lerp(5), a1_ref[...]), a2_ref[...])
    a_ref[...] = jax.nn.sigmoid(al)


def _rwkv_pre(x2d, g, mu, w_in, w0, w1, w2, a0, a1, a2, seq):
    n, d = x2d.shape
    c = w0.shape[0]
    tm = min(ROW_TILE, seq)
    tiles_per_seq = seq // tm
    row = lambda i: (i, 0)
    fixed = lambda i: (0, 0)
    ws = [w_in[:, s * c:(s + 1) * c].astype(BF16) for s in range(4)]
    in_specs = [
        pl.BlockSpec((tm, d), row),
        pl.BlockSpec((SUBLANES, d), lambda i: (jnp.maximum(i * (tm // SUBLANES) - 1, 0), 0)),
        pl.BlockSpec((1, d), fixed), pl.BlockSpec((6, d), fixed),
    ] + [pl.BlockSpec((d, c), fixed)] * 4 + [
        pl.BlockSpec((1, c), fixed), pl.BlockSpec(w1.shape, fixed), pl.BlockSpec(w2.shape, fixed),
        pl.BlockSpec((1, c), fixed), pl.BlockSpec(a1.shape, fixed), pl.BlockSpec(a2.shape, fixed),
    ]
    out = jax.ShapeDtypeStruct((n, c), F32)
    kern = functools.partial(_rwkv_pre_kernel, tiles_per_seq=tiles_per_seq)
    return pl.pallas_call(
        kern, grid=(n // tm,), in_specs=in_specs, out_specs=[pl.BlockSpec((tm, c), row)] * 6,
        out_shape=[out] * 6, compiler_params=_cparams("parallel"), name="rwkv_pre",
    )(x2d, x2d, g.reshape(1, d), mu, *ws, w0.reshape(1, c), w1.astype(BF16), w2.astype(BF16),
      a0.reshape(1, c), a1.astype(BF16), a2.astype(BF16))


def _dot3_many(a_list, b_list, dot=_bdot, passes=3):
    if passes == 1:
        return [dot(a, b) for a, b in zip(a_list, b_list)]
    sa = [_split2(a) for a in a_list]
    sb = [_split2(b) for b in b_list]
    hh = [dot(x[0], y[0]) for x, y in zip(sa, sb)]
    hl = [dot(x[0], y[1]) for x, y in zip(sa, sb)]
    lh = [dot(x[1], y[0]) for x, y in zip(sa, sb)]
    return [p + (q + r) for p, q, r in zip(hh, hl, lh)]


def _lane_sums(x_list, ones):
    parts = [_split2(x) for x in x_list]
    hi = [jnp.dot(p[0], ones, preferred_element_type=F32) for p in parts]
    lo = [jnp.dot(p[1], ones, preferred_element_type=F32) for p in parts]
    return [a + b for a, b in zip(hi, lo)]


def _rwkv_chunk_ops(ats, rts, bts, kts, bhs, khs, vs, wls, lower_strict, lower_incl, eye, blockdiag):
    L = RWKV_CHUNK
    n = len(ats)
    n_ch = ats[0].shape[1]
    bks = [jnp.concatenate([bt, kt], axis=0) for bt, kt in zip(bts, kts)]
    ars = [jnp.concatenate([at, rt], axis=0) for at, rt in zip(ats, rts)]
    As = _dot3_many(ars, bks, dot=_bdot_nt, passes=RWKV_SCORE_PASSES)
    a_ab = [jnp.where(lower_strict, A[:L, :L], 0.0) for A in As]
    a_rb = [jnp.where(lower_incl, A[L:, :L], 0.0) for A in As]
    a_k = [jnp.concatenate([jnp.where(lower_strict, A[:L, L:], 0.0),
                            jnp.where(lower_incl, A[L:, L:], 0.0)], axis=0) for A in As]
    dblk = [jnp.where(blockdiag, x, 0.0) for x in a_ab]
    akv = _dot3_many(a_k, vs, passes=RWKV_VALUE_PASSES)
    inv = functools.partial(_dot3_many, passes=RWKV_INVERSE_PASSES)
    d2 = inv(dblk, dblk)
    res = inv([eye + d for d in dblk] + d2, [eye + d for d in d2] + d2)
    s4, d4 = res[:n], res[n:]
    res = inv(d4 + d4, s4 + d4)
    s8 = [s + x for s, x in zip(s4, res[:n])]
    d8 = res[n:]
    tdiag = [s + x for s, x in zip(s8, inv(d8, s8))]
    rhs = [jnp.concatenate([at, x[:L]], axis=1) for at, x in zip(ats, akv)]
    res = inv(tdiag + tdiag, rhs + [x - d for x, d in zip(a_ab, dblk)])
    xt, nt = res[:n], res[n:]
    res = inv(nt + nt, xt + nt)
    u = [x + y for x, y in zip(xt, res[:n])]
    nt2 = res[n:]
    pq = [x + y for x, y in zip(u, inv(nt2, u))]
    res = _dot3_many(a_rb, pq, passes=RWKV_VALUE_PASSES)
    gh = [jnp.concatenate([rt, x[L:]], axis=1) + y for rt, x, y in zip(rts, akv, res)]
    res = _dot3_many(bhs + khs, pq + vs, dot=_bdot_tn, passes=RWKV_VALUE_PASSES)
    out = []
    for i in range(n):
        mc = res[i]
        m_op = mc[:, :n_ch] + eye * wls[i]
        c_op = mc[:, n_ch:] + res[n + i]
        out.append((gh[i][:, :n_ch], gh[i][:, n_ch:], m_op, c_op))
    return out


def _rwkv_scan_kernel(r_ref, k_ref, v_ref, z_ref, lw_ref, a_ref, kk_ref, ka_ref, rk_ref,
                      lnw_ref, lnb_ref, o_ref, state_ref):
    tb = pl.program_id(2)

    @pl.when(tb == 0)
    def _():
        state_ref[...] = jnp.zeros_like(state_ref)

    L = RWKV_CHUNK
    N = RWKV_HEAD
    ri = lax.broadcasted_iota(jnp.int32, (L, L), 0)
    ci = lax.broadcasted_iota(jnp.int32, (L, L), 1)
    lower_strict = ri > ci
    lower_incl = ri >= ci
    tri_incl = jnp.where(lower_incl, 1.0, 0.0).astype(BF16)
    eye = jnp.where(ri == ci, 1.0, 0.0).astype(F32)
    blockdiag = (ri // 16) == (ci // 16)
    n_heads = r_ref.shape[2] // N
    n_chunks = r_ref.shape[1] // L
    width = n_heads * N
    hi_ = lax.broadcasted_iota(jnp.int32, (width, width), 0) // N
    hj_ = lax.broadcasted_iota(jnp.int32, (width, width), 1) // N
    head_ones = jnp.where(hi_ == hj_, 1.0, 0.0).astype(BF16)
    chunk_rows = [slice(c * L, (c + 1) * L) for c in range(n_chunks)]
    r_c = [r_ref[0, rows, :] for rows in chunk_rows]
    v_c = [v_ref[0, rows, :] for rows in chunk_rows]
    lw_c = [lw_ref[0, rows, :] for rows in chunk_rows]
    kk_c, kp_c, a_c = [], [], []
    for rows in chunk_rows:
        k = k_ref[0, rows, :]
        a = a_ref[0, rows, :]
        kk_c.append(k * kk_ref[...])
        kp_c.append(k * (1.0 + (a - 1.0) * ka_ref[...]))
        a_c.append(a)
    sums = _lane_sums([x * x for x in kk_c] + [r * kp * rk_ref[...] for r, kp in zip(r_c, kp_c)],
                      head_ones)
    bonus_c = [sums[n_chunks + c] * v_c[c] for c in range(n_chunks)]
    cum_c = [[jnp.dot(tri_incl, p, preferred_element_type=F32) for p in _split3(lw)] for lw in lw_c]
    cum_c = [a + (b + c) for a, b, c in cum_c]
    slabs = []
    for c in range(n_chunks):
        kk = kk_c[c] / jnp.maximum(jnp.sqrt(sums[c]), 1e-12)
        cum = cum_c[c]
        w_inv = jnp.exp(-cum)
        cum_last = cum[L - 1:L, :]
        w_tail = jnp.exp(cum_last - cum)
        bb = kk * a_c[c]
        slabs.append((-kk * jnp.exp(cum - lw_c[c]), r_c[c] * jnp.exp(cum), bb * w_inv,
                      kp_c[c] * w_inv, bb * w_tail, kp_c[c] * w_tail, v_c[c], jnp.exp(cum_last)))
    probs = [(hh, c) for c in range(n_chunks) for hh in range(n_heads)]
    per_head = [[slabs[c][j][:, hh * N:(hh + 1) * N] for hh, c in probs] for j in range(8)]
    ops = _rwkv_chunk_ops(*per_head, lower_strict, lower_incl, eye, blockdiag)
    states = [state_ref[hh] for hh in range(n_heads)]
    ys = {}
    for c in range(n_chunks):
        idx = [c * n_heads + hh for hh in range(n_heads)]
        out = _dot3_many([ops[i][0] for i in idx], states, passes=RWKV_VALUE_PASSES)
        upd = _dot3_many([ops[i][2] for i in idx], states, passes=RWKV_STATE_PASSES)
        for hh, i in enumerate(idx):
            ys[(hh, c)] = out[hh] + ops[i][1]
            states[hh] = upd[hh] + ops[i][3]
    for hh in range(n_heads):
        state_ref[hh] = states[hh]
    y_c = [jnp.concatenate([ys[(hh, c)] for hh in range(n_heads)], axis=1) for c in range(n_chunks)]
    yc_c = [y - s * (1.0 / N) for y, s in zip(y_c, _lane_sums(y_c, head_ones))]
    var_c = [s * (1.0 / N) for s in _lane_sums([yc * yc for yc in yc_c], head_ones)]
    for c, rows in enumerate(chunk_rows):
        yn = yc_c[c] * lax.rsqrt(var_c[c] + RWKV_GN_EPS) * lnw_ref[...] + lnb_ref[...]
        o_ref[0, rows, :] = (yn + bonus_c[c]) * _silu(z_ref[0, rows, :])


def _rwkv_scan(r, k, v, z, lw, a, k_k, k_a, r_k, lnx_w, lnx_b):
    b, t, c = r.shape
    tb = min(RWKV_BLOCK, t)
    seq_spec = pl.BlockSpec((1, tb, RWKV_LANES), lambda i, p, j: (i, j, p))
    par_spec = pl.BlockSpec((1, RWKV_LANES), lambda i, p, j: (0, p))
    params = [x.reshape(1, c).astype(F32) for x in (k_k, k_a, r_k, lnx_w, lnx_b)]
    return pl.pallas_call(
        _rwkv_scan_kernel, grid=(b, c // RWKV_LANES, t // tb),
        in_specs=[seq_spec] * 6 + [par_spec] * 5, out_specs=seq_spec,
        out_shape=jax.ShapeDtypeStruct((b, t, c), F32),
        scratch_shapes=[pltpu.VMEM((RWKV_LANES // RWKV_HEAD, RWKV_HEAD, RWKV_HEAD), F32)],
        compiler_params=_cparams("parallel", "parallel", "arbitrary"), name="rwkv_scan",
    )(r, k, v, z, lw, a, *params)


def _rwkv_layer(x2d, g, mu, w_in, w0, w1, w2, a0, a1, a2, k_k, k_a, r_k, lnx_w, lnx_b, w_out,
                final_g, batch, seq):
    c = w0.shape[0]
    r, k, v, z, lw, a = _rwkv_pre(x2d, g, mu, w_in, w0, w1, w2, a0, a1, a2, seq)
    sh = (batch, seq, c)
    y = _rwkv_scan(r.reshape(sh), k.reshape(sh), v.reshape(sh), z.reshape(sh), lw.reshape(sh),
                   a.reshape(sh), k_k, k_a, r_k, lnx_w, lnx_b)
    return _out_proj(x2d, [y.reshape(x2d.shape[0], c)], None, w_out, final_g)


def _nsa_compress_kernel(u_ref, posa_ref, posb_ref, w1a_ref, w1b_ref, w2_ref, o_ref):
    u = u_ref[0, 0]
    ha = _bdot(u + posa_ref[...], w1a_ref[...])
    hb = _bdot(u + posb_ref[...], w1b_ref[...])
    h = ha + pltpu.roll(hb, hb.shape[0] - 1, 0)
    o_ref[0, 0] = _bdot(_silu(h), w2_ref[...])


def _nsa_compress(t_hm, pos, w1, w2):
    b, g, t, dh = t_hm.shape
    nch = t // NSA_CMP_STRIDE
    half = NSA_CMP_STRIDE * dh
    u = t_hm.reshape(b, g, nch, half)
    posf = pos.astype(F32).reshape(2, 1, half)
    hid = w1.shape[1]
    fixed = lambda i, j: (0, 0)
    return pl.pallas_call(
        _nsa_compress_kernel, grid=(b, g),
        in_specs=[pl.BlockSpec((1, 1, nch, half), lambda i, j: (i, j, 0, 0)),
                  pl.BlockSpec((1, half), fixed), pl.BlockSpec((1, half), fixed),
                  pl.BlockSpec((half, hid), fixed), pl.BlockSpec((half, hid), fixed),
                  pl.BlockSpec((hid, dh), fixed)],
        out_specs=pl.BlockSpec((1, 1, nch, dh), lambda i, j: (i, j, 0, 0)),
        out_shape=jax.ShapeDtypeStruct((b, g, nch, dh), F32),
        compiler_params=_cparams("parallel", "parallel"), name="nsa_compress",
    )(u, posf[0], posf[1], w1[:half].astype(BF16), w1[half:].astype(BF16), w2.astype(BF16))


def _nsa_cmp_attn_kernel(q_ref, kc_ref, vct_ref, gate_ref, o_ref, sel_ref, *, n_sel):
    g = pl.program_id(1)
    n = pl.program_id(2)
    tq = q_ref.shape[2]
    ncmp = kc_ref.shape[2]
    q = q_ref[0].reshape(GQA * tq, HEAD_DIM) * (HEAD_DIM ** -0.5)
    s = _bdot_nt(kc_ref[0, 0], q)
    cend = lax.broadcasted_iota(jnp.int32, (ncmp, tq), 0) * NSA_CMP_STRIDE + (NSA_CMP_LEN - 1)
    tpos = n * tq + lax.broadcasted_iota(jnp.int32, (ncmp, tq), 1)
    ok = jnp.concatenate([cend <= tpos] * GQA, axis=1)
    s = jnp.where(ok, s, NEG)
    m = jnp.max(s, axis=0, keepdims=True)
    e = jnp.where(ok, jnp.exp(s - m), 0.0)
    l = jnp.sum(e, axis=0, keepdims=True)
    p = e / jnp.where(l > 0.0, l, 1.0)
    o_t = jnp.dot(vct_ref[0, 0].astype(BF16), p.astype(BF16), preferred_element_type=F32)
    outs = []
    for r in range(GQA):
        outs.append((o_t[:, r * tq:(r + 1) * tq] * _gate_row(gate_ref, GQA * g + r)).T)
    o_ref[0] = jnp.concatenate(outs, axis=-1)

    psum = p[:, :tq]
    for r in range(1, GQA):
        psum = psum + p[:, r * tq:(r + 1) * tq]
    si = lax.broadcasted_iota(jnp.int32, (n_sel, ncmp), 0)
    ni = lax.broadcasted_iota(jnp.int32, (n_sel, ncmp), 1)
    ratio = NSA_SEL_LEN // NSA_CMP_STRIDE
    overlap = ((ni < ratio * (si + 1)) & (ni * NSA_CMP_STRIDE + NSA_CMP_LEN - 1 >= si * NSA_SEL_LEN))
    overlap = jnp.where(overlap, 1.0, 0.0).astype(BF16)
    imp = _dot_exact_rhs(psum, overlap, dot=lambda a, b: _bdot(b, a))
    blk = lax.broadcasted_iota(jnp.int32, (n_sel, tq), 0)
    tq_pos = n * tq + lax.broadcasted_iota(jnp.int32, (n_sel, tq), 1)
    cur = tq_pos // NSA_SEL_LEN
    forced = (blk == 0) | (blk == cur) | (blk == cur - 1)
    future = blk * NSA_SEL_LEN > tq_pos
    imp = jnp.where(forced, BIG, jnp.where(future, NEG, imp))
    n_grp = n_sel // SUBLANES
    blk_g = blk[:SUBLANES, :LANES]
    for c0 in range(0, tq, LANES):
        imp_c = imp[:, c0:c0 + LANES]
        imp_g = [imp_c[SUBLANES * b:SUBLANES * (b + 1)] for b in range(n_grp)]
        ranks = [jnp.zeros((SUBLANES, LANES), F32) for _ in range(n_grp)]
        for j in range(n_sel):
            row = imp_c[j:j + 1, :]
            for b in range(n_grp):
                if b > j // SUBLANES:
                    ahead = row >= imp_g[b]
                elif b < j // SUBLANES:
                    ahead = row > imp_g[b]
                else:
                    ahead = (row > imp_g[b]) | ((row == imp_g[b]) & (blk_g > j % SUBLANES))
                ranks[b] = ranks[b] + jnp.where(ahead, 1.0, 0.0)
        rank = jnp.concatenate(ranks, axis=0)
        sel = jnp.where(rank < float(min(NSA_TOPK, n_sel)), 1.0, 0.0)
        sel_ref[0, 0, :, c0:c0 + LANES] = sel


def _nsa_cmp_attn(q, k_cmp, v_cmp, gates_t):
    b, _, t, _ = q.shape
    ncmp = k_cmp.shape[2]
    n_sel = t // NSA_SEL_LEN
    d = ATTN_HEADS * HEAD_DIM
    kern = functools.partial(_nsa_cmp_attn_kernel, n_sel=n_sel)
    tq = min(NSA_CMP_QUERIES, t)
    cmp_spec = pl.BlockSpec((1, 1, ncmp, HEAD_DIM), lambda i, g, n: (i, g, 0, 0))
    return pl.pallas_call(
        kern, grid=(b, KV_HEADS, t // tq),
        in_specs=[pl.BlockSpec((1, GQA, tq, HEAD_DIM), lambda i, g, n: (i, g, n, 0)),
                  cmp_spec, pl.BlockSpec((1, 1, HEAD_DIM, ncmp), lambda i, g, n: (i, g, 0, 0)),
                  pl.BlockSpec((1, LANES, tq), lambda i, g, n: (i, 0, n))],
        out_specs=[pl.BlockSpec((1, tq, GQA * HEAD_DIM), lambda i, g, n: (i, n, g)),
                   pl.BlockSpec((1, 1, n_sel, tq), lambda i, g, n: (i, g, 0, n))],
        out_shape=[jax.ShapeDtypeStruct((b, t, d), F32),
                   jax.ShapeDtypeStruct((b, KV_HEADS, n_sel, t), F32)],
        compiler_params=_cparams("parallel", "parallel", "parallel"), name="nsa_cmp_attn",
    )(q, k_cmp, jnp.swapaxes(v_cmp, 2, 3), gates_t)


def _nsa_sel_attn_kernel(q_ref, k_ref, vt_ref, selt_ref, bias_ref, far_ref, gate_ref, o_ref):
    g = pl.program_id(1)
    n = pl.program_id(2)
    tq = q_ref.shape[2]
    pw = Q_BLOCK
    bl = NSA_SEL_LEN
    fk = NSA_FAR_KEYS
    q_all = q_ref[0].reshape(GQA * tq, HEAD_DIM) * (HEAD_DIM ** -0.5)
    q_t = q_all.astype(F32).T.astype(BF16)

    def block_rows(first_block, count, limit):
        rows = []
        for i in range(count):
            blk = first_block + i
            row = selt_ref[0, 0, pl.ds(jnp.clip(blk, 0, jnp.maximum(limit - 1, 0)), 1), :]
            rows.append(jnp.broadcast_to(jnp.where((blk >= 0) & (blk < limit), row, 0.0), (bl, tq)))
        return jnp.concatenate(rows, axis=0) > 0.5

    def values_t(*starts_widths):
        return jnp.concatenate([vt_ref[0, :, pl.ds(s, w)] for s, w in starts_widths], axis=1)

    prev = pl.multiple_of(jnp.maximum(n * tq - pw, 0), pw)
    diag = pl.multiple_of(n * tq, tq)
    kt = jnp.concatenate([k_ref[0, 0, pl.ds(prev, pw), :], k_ref[0, 0, pl.ds(diag, tq), :]], axis=0)
    vt = values_t((prev, pw), (diag, tq))
    kj = lax.broadcasted_iota(jnp.int32, (pw + tq, tq), 0)
    qi = lax.broadcasted_iota(jnp.int32, (pw + tq, tq), 1)
    n_blocks = (n + 1) * (tq // bl)
    first_near = n * (tq // bl) - pw // bl
    ok = block_rows(first_near, (pw + tq) // bl, n_blocks) & (pw + qi - kj >= 0)
    ok = jnp.concatenate([ok] * GQA, axis=1)
    tbl = bias_ref[0]
    far_row = far_ref[0]
    cols = []
    for r in range(GQA):
        prev_p = tbl[:pw, r * pw:(r + 1) * pw]
        diag_p = tbl[pw:, r * pw:(r + 1) * pw]
        for a in range(tq // pw):
            far_p = jnp.broadcast_to(far_row[:, r * tq + a * pw:r * tq + (a + 1) * pw], (pw, pw))
            below = [jnp.zeros((pw, pw), F32)] * (tq // pw - 1 - a)
            cols.append(jnp.concatenate([far_p] * a + [prev_p, diag_p] + below, axis=0))
    bias = jnp.concatenate(cols, axis=1)
    s = jnp.where(ok, jnp.dot(kt, q_t, preferred_element_type=F32) + bias, NEG)
    m = jnp.max(s, axis=0, keepdims=True)
    p = jnp.exp(s - m).astype(BF16)
    acc = jnp.dot(vt, p, preferred_element_type=F32)
    m = m - far_ref[0]
    far_blocks = jnp.maximum(first_near, 0)

    def body(c, carry):
        m, acc = carry
        start = pl.multiple_of(c * fk, fk)
        qk = jnp.dot(k_ref[0, 0, pl.ds(start, fk), :], q_t, preferred_element_type=F32)
        parts = []
        for i in range(fk // bl):
            blk = c * (fk // bl) + i
            row = selt_ref[0, 0, pl.ds(jnp.minimum(blk, jnp.maximum(far_blocks - 1, 0)), 1), :]
            row = jnp.where((blk < far_blocks) & (row > 0.5), 0.0, NEG)
            parts.append(qk[i * bl:(i + 1) * bl] + jnp.concatenate([row] * GQA, axis=1))
        s = jnp.concatenate(parts, axis=0)
        m_new = jnp.maximum(m, jnp.max(s, axis=0, keepdims=True))
        p = jnp.exp(s - m_new).astype(BF16)
        pv = jnp.dot(values_t((start, fk)), p, preferred_element_type=F32)
        return m_new, jnp.exp(m - m_new) * acc + pv

    n_far = (far_blocks * bl + fk - 1) // fk
    _, acc = lax.fori_loop(0, n_far, body, (m, acc))
    o_t = acc[:HEAD_DIM] / acc[HEAD_DIM:HEAD_DIM + 1]
    outs = []
    for r in range(GQA):
        o = o_t[:, r * tq:(r + 1) * tq] * _gate_row(gate_ref, ATTN_HEADS + GQA * g + r)
        outs.append(o.T)
    o_ref[0] = jnp.concatenate(outs, axis=-1)


def _nsa_sel_attn(q, ks, vst, sel_t, t5_table, gates_t):
    b, _, t, _ = q.shape
    n_sel = t // NSA_SEL_LEN
    d = ATTN_HEADS * HEAD_DIM
    tq = min(NSA_SEL_QUERIES, t)
    near = Q_BLOCK + tq
    width = GQA * tq
    assert t % NSA_FAR_KEYS == 0 or t <= near
    bias_near = _band_bias(t5_table, Q_BLOCK, 2 * Q_BLOCK, Q_BLOCK)
    bias_near = bias_near.reshape(KV_HEADS, GQA, Q_BLOCK, 2 * Q_BLOCK).transpose(0, 3, 1, 2)
    bias_near = bias_near.reshape(KV_HEADS, 2 * Q_BLOCK, GQA * Q_BLOCK)
    far = t5_table[T5_BUCKETS - 1].astype(F32).reshape(KV_HEADS, GQA, 1)
    far = jnp.broadcast_to(far, (KV_HEADS, GQA, tq)).reshape(KV_HEADS, 1, width)
    vst_ones = vst.reshape(b, KV_HEADS, HEAD_DIM, t)
    vst_ones = jnp.concatenate([vst_ones, jnp.ones_like(vst_ones)], axis=2)
    vst_ones = vst_ones.reshape(b, 2 * KV_HEADS * HEAD_DIM, t)
    return pl.pallas_call(
        _nsa_sel_attn_kernel, grid=(b, KV_HEADS, t // tq),
        in_specs=[pl.BlockSpec((1, GQA, tq, HEAD_DIM), lambda i, g, n: (i, g, n, 0)),
                  pl.BlockSpec((1, 1, t, HEAD_DIM), lambda i, g, n: (i, g, 0, 0)),
                  pl.BlockSpec((1, 2 * HEAD_DIM, t), lambda i, g, n: (i, g, 0)),
                  pl.BlockSpec((1, 1, n_sel, tq), lambda i, g, n: (i, g, 0, n)),
                  pl.BlockSpec((1, 2 * Q_BLOCK, GQA * Q_BLOCK), lambda i, g, n: (g, 0, 0)),
                  pl.BlockSpec((1, 1, width), lambda i, g, n: (g, 0, 0)),
                  pl.BlockSpec((1, LANES, tq), lambda i, g, n: (i, 0, n))],
        out_specs=pl.BlockSpec((1, tq, GQA * HEAD_DIM), lambda i, g, n: (i, n, g)),
        out_shape=jax.ShapeDtypeStruct((b, t, d), F32),
        compiler_params=_cparams("parallel", "parallel", "parallel"), name="nsa_sel_attn",
    )(q, ks, vst_ones, sel_t, bias_near, far, gates_t)


def _nsa_layer(x2d, g, w_in, pos_k, k_w1, k_w2, pos_v, v_w1, v_w2, w_out, t5_table, final_g,
               batch, seq):
    nq, nkv = ATTN_HEADS * HEAD_DIM, KV_HEADS * HEAD_DIM
    offs = [0, nq] + [nq + nkv * (i + 1) for i in range(6)]
    ws = [w_in[:, offs[i]:offs[i + 1]] for i in range(7)]
    n_gate = 3 * ATTN_HEADS
    wg = jnp.pad(w_in[:, offs[7]:offs[7] + n_gate], ((0, 0), (0, LANES - n_gate)))
    wz = w_in[:, offs[7] + n_gate:]
    q, kc, vc, ks, vst, kw, vwt, gates_t, z = _norm_proj(
        x2d, g, ws + [wg, wz],
        ["heads", "heads", "heads", "heads", "cols", "heads", "cols", "cols", "rows"],
        [BF16, F32, F32, BF16, BF16, BF16, BF16, F32, F32], batch, seq)
    k_cmp = _nsa_compress(kc, pos_k, k_w1, k_w2)
    v_cmp = _nsa_compress(vc, pos_v, v_w1, v_w2)
    o_cmp, sel_t = _nsa_cmp_attn(q, k_cmp, v_cmp, gates_t)
    o_sel = _nsa_sel_attn(q, ks, vst, sel_t, t5_table, gates_t)
    o_win = _band_attn(q, kw, vwt, t5_table, NSA_WINDOW, gates_t=gates_t, gate_col=2 * ATTN_HEADS)
    sh = x2d.shape
    return _out_proj(x2d, [o_cmp.reshape(sh), o_sel.reshape(sh), o_win.reshape(sh)], z, w_out,
                     final_g)


def _lru_kernel(x_ref, g_ref, win_u_ref, win_z_ref, cw_ref, cb_ref, wa_ref, ba_ref, wx_ref, bx_ref,
                sp_ref, wout_ref, *rest, half, has_fg):
    fg_ref = rest[0] if has_fg else None
    o_ref, tail_ref, h_ref = rest[-3:]
    j = pl.program_id(1)

    @pl.when(j == 0)
    def _():
        tail_ref[...] = jnp.zeros_like(tail_ref)
        h_ref[...] = jnp.zeros_like(h_ref)

    x = x_ref[...]
    xb = _rms(x, g_ref[...]).astype(BF16)
    u = jnp.dot(xb, win_u_ref[...], preferred_element_type=F32)
    z = jnp.dot(xb, win_z_ref[...], preferred_element_type=F32)
    tm, width = u.shape
    ext = jnp.concatenate([tail_ref[...], u], axis=0)
    tail_ref[...] = u[tm - SUBLANES:, :]
    uc = cb_ref[...] + cw_ref[CONV_WIDTH - 1:CONV_WIDTH, :] * u
    for s in range(1, CONV_WIDTH):
        uc = uc + cw_ref[CONV_WIDTH - 1 - s:CONV_WIDTH - s, :] * ext[SUBLANES - s:SUBLANES - s + tm, :]
    ucb = uc.astype(BF16)
    gr, gi = [], []
    for c in range(width // half):
        blk = ucb[:, c * half:(c + 1) * half]
        gr.append(jnp.dot(blk, wa_ref[c], preferred_element_type=F32))
        gi.append(jnp.dot(blk, wx_ref[c], preferred_element_type=F32))
    rg = jax.nn.sigmoid(jnp.concatenate(gr, axis=1) + ba_ref[...])
    ig = jax.nn.sigmoid(jnp.concatenate(gi, axis=1) + bx_ref[...])
    log_a = -LRU_C * rg * sp_ref[...]
    a = jnp.exp(log_a)
    bv = jnp.sqrt(1.0 - a * a) * (ig * uc)
    n_grp = tm // SUBLANES
    a = a.reshape(n_grp, SUBLANES, width)
    bv = bv.reshape(n_grp, SUBLANES, width)
    row = lax.broadcasted_iota(jnp.int32, (n_grp, SUBLANES, width), 1)
    sh = 1
    while sh < SUBLANES:
        a_s = jnp.where(row >= sh, pltpu.roll(a, sh, 1), 1.0)
        b_s = jnp.where(row >= sh, pltpu.roll(bv, sh, 1), 0.0)
        bv = a * b_s + bv
        a = a * a_s
        sh *= 2
    carry = h_ref[...]
    groups = []
    for i in range(n_grp):
        hg = bv[i] + a[i] * carry
        carry = hg[SUBLANES - 1:SUBLANES, :]
        groups.append(hg)
    h_ref[...] = carry
    gated = (jnp.concatenate(groups, axis=0) * _silu(z)).astype(BF16)
    y = x + jnp.dot(gated, wout_ref[...], preferred_element_type=F32)
    o_ref[...] = _rms(y, fg_ref[...]) if has_fg else y


def _block_diag(w, group):
    nb, n, _ = w.shape
    w = w.reshape(nb // group, group, n, n)
    eye = jnp.eye(group, dtype=w.dtype)
    return jnp.einsum('cgij,gh->cgihj', w, eye).reshape(nb // group, group * n, group * n)


def _lru_layer(x2d, g, w_in, conv_w, conv_b, ga_w, ga_b, gx_w, gx_b, lam, w_out, final_g, batch, seq):
    n, d = x2d.shape
    width = w_in.shape[1] // 2
    blk = ga_w.shape[1]
    group = LANES // math.gcd(blk, LANES)
    group = min(group, ga_w.shape[0])
    half = group * blk
    nsup = width // half
    tm = min(LRU_TILE, seq)
    tiles = seq // tm
    row = lambda i, j: (i * tiles + j, 0)
    fixed = lambda i, j: (0, 0)
    fixed3 = lambda i, j: (0, 0, 0)
    vec = lambda a: a.astype(F32).reshape(1, -1)
    softplus_neg_lam = jax.nn.softplus(-lam.astype(F32))
    has_fg = final_g is not None
    in_specs = [pl.BlockSpec((tm, d), row), pl.BlockSpec((1, d), fixed),
                pl.BlockSpec((d, width), fixed), pl.BlockSpec((d, width), fixed),
                pl.BlockSpec((CONV_WIDTH, width), fixed), pl.BlockSpec((1, width), fixed),
                pl.BlockSpec((nsup, half, half), fixed3), pl.BlockSpec((1, width), fixed),
                pl.BlockSpec((nsup, half, half), fixed3), pl.BlockSpec((1, width), fixed),
                pl.BlockSpec((1, width), fixed), pl.BlockSpec((width, d), fixed)]
    args = [x2d, g.reshape(1, d), w_in[:, :width].astype(BF16), w_in[:, width:].astype(BF16),
            conv_w.astype(F32), vec(conv_b), _block_diag(ga_w, group).astype(BF16), vec(ga_b),
            _block_diag(gx_w, group).astype(BF16), vec(gx_b), vec(softplus_neg_lam),
            w_out.astype(BF16)]
    if has_fg:
        in_specs.append(pl.BlockSpec((1, d), fixed))
        args.append(final_g.reshape(1, d))
    kern = functools.partial(_lru_kernel, half=half, has_fg=has_fg)
    return pl.pallas_call(
        kern, grid=(batch, tiles), in_specs=in_specs, out_specs=pl.BlockSpec((tm, d), row),
        out_shape=jax.ShapeDtypeStruct((n, d), F32),
        scratch_shapes=[pltpu.VMEM((SUBLANES, width), F32), pltpu.VMEM((1, width), F32)],
        compiler_params=_cparams("parallel", "arbitrary"), name="rglru_layer",
    )(*args)


def kernel(x, t5_table, norm_g, final_g, a_w_in, a_sinks, a_w_out, b_mu, b_w_in, b_w0, b_w1, b_w2, b_a0, b_a1, b_a2, b_k_k, b_k_a, b_r_k, b_lnx_w, b_lnx_b, b_w_out, c_w_in, c_cmp_pos_k, c_cmp_k_w1, c_cmp_k_w2, c_cmp_pos_v, c_cmp_v_w1, c_cmp_v_w2, c_w_out, d_w_in, d_conv_w, d_conv_b, d_gate_a_w, d_gate_a_b, d_gate_x_w, d_gate_x_b, d_lambda, d_w_out):
    batch, seq, d = x.shape
    depth = norm_g.shape[0]
    h = x.reshape(batch * seq, d)
    for layer in range(depth):
        m, j = layer % 4, layer // 4
        g = norm_g[layer]
        fg = final_g if layer == depth - 1 else None
        if m == 0:
            h = _swa_layer(h, g, a_w_in[j], a_sinks[j], a_w_out[j], t5_table, fg, batch, seq)
        elif m == 1:
            h = _rwkv_layer(h, g, b_mu[j], b_w_in[j], b_w0[j], b_w1[j], b_w2[j], b_a0[j], b_a1[j],
                            b_a2[j], b_k_k[j], b_k_a[j], b_r_k[j], b_lnx_w[j], b_lnx_b[j],
                            b_w_out[j], fg, batch, seq)
        elif m == 2:
            h = _nsa_layer(h, g, c_w_in[j], c_cmp_pos_k[j], c_cmp_k_w1[j], c_cmp_k_w2[j],
                           c_cmp_pos_v[j], c_cmp_v_w1[j], c_cmp_v_w2[j], c_w_out[j], t5_table,
                           fg, batch, seq)
        else:
            h = _lru_layer(h, g, d_w_in[j], d_conv_w[j], d_conv_b[j], d_gate_a_w[j], d_gate_a_b[j],
                           d_gate_x_w[j], d_gate_x_b[j], d_lambda[j], d_w_out[j], fg, batch, seq)
    return h.reshape(batch, seq, d)
```

```python
import functools
import math

import jax
import jax.numpy as jnp
from jax import lax
from jax.experimental import pallas as pl
from jax.experimental.pallas import tpu as pltpu

F32 = jnp.float32
BF16 = jnp.bfloat16

EPS = 1e-6
NEG = -1e30
BIG = 1e30
T5_BUCKETS = 32
T5_MAX_DIST = 128
ATTN_HEADS = 16
HEAD_DIM = 64
KV_HEADS = 4
GQA = ATTN_HEADS // KV_HEADS
Q_BLOCK = 128
SWA_WINDOW = 128
RWKV_HEAD = 64
RWKV_GN_EPS = 64e-5
NSA_CMP_LEN = 32
NSA_CMP_STRIDE = 16
NSA_SEL_LEN = 64
NSA_TOPK = 16
NSA_WINDOW = 512
LRU_C = 8.0
CONV_WIDTH = 4

LANES = 128
SUBLANES = 8
VMEM_LIMIT = 56 * 1024 * 1024
ROW_TILE = 512
RWKV_CHUNK = 64
RWKV_BLOCK = 512
RWKV_LANES = 256
RWKV_SCORE_PASSES = 1
RWKV_VALUE_PASSES = 1
RWKV_INVERSE_PASSES = 1
RWKV_STATE_PASSES = 3
NSA_FAR_KEYS = 512
BAND_QUERIES = 512
NSA_CMP_QUERIES = 512
NSA_SEL_QUERIES = 512
LRU_TILE = 512


def _cparams(*sem):
    return pltpu.CompilerParams(dimension_semantics=sem, vmem_limit_bytes=VMEM_LIMIT)


def _bdot(a, b):
    return jnp.dot(a.astype(BF16), b.astype(BF16), preferred_element_type=F32)


def _bdot_nt(a, b):
    return lax.dot_general(a.astype(BF16), b.astype(BF16), (((1,), (1,)), ((), ())),
                           preferred_element_type=F32)


def _bdot_tn(a, b):
    return lax.dot_general(a.astype(BF16), b.astype(BF16), (((0,), (0,)), ((), ())),
                           preferred_element_type=F32)


def _split2(x):
    hi = x.astype(BF16)
    lo = (x - hi.astype(F32)).astype(BF16)
    return hi, lo


def _split3(x):
    hi = x.astype(BF16)
    r1 = x - hi.astype(F32)
    mid = r1.astype(BF16)
    return hi, mid, (r1 - mid.astype(F32)).astype(BF16)


def _dot_exact_rhs(a, b01, dot=_bdot):
    ah, am, al = _split3(a)
    return dot(ah, b01) + (dot(am, b01) + dot(al, b01))


def _rms(x, g):
    return x * lax.rsqrt(jnp.mean(x * x, axis=-1, keepdims=True) + EPS) * g


def _silu(z):
    return z * jax.nn.sigmoid(z)


def _gate_row(gate_ref, c):
    return jax.nn.sigmoid(gate_ref[0, pl.ds(c, 1), :])


def _norm_proj_kernel(x_ref, g_ref, *refs, n_out, layouts):
    w_refs, o_refs = refs[:n_out], refs[n_out:]
    xb = _rms(x_ref[...], g_ref[...]).astype(BF16)
    for w_ref, o_ref, layout in zip(w_refs, o_refs, layouts):
        width = w_ref.shape[1]
        for c0 in range(0, width, 512):
            cw = min(512, width - c0)
            acc = jnp.dot(xb, w_ref[:, c0:c0 + cw], preferred_element_type=F32)
            if layout == "heads":
                for j in range(cw // HEAD_DIM):
                    o_ref[0, (c0 // HEAD_DIM) + j] = acc[:, j * HEAD_DIM:(j + 1) * HEAD_DIM].astype(o_ref.dtype)
            elif layout == "cols":
                o_ref[0, c0:c0 + cw, :] = acc.T.astype(o_ref.dtype)
            else:
                o_ref[:, c0:c0 + cw] = acc.astype(o_ref.dtype)


def _norm_proj(x2d, g, weights, layouts, dtypes, batch, seq):
    n, d = x2d.shape
    tm = min(ROW_TILE, seq)
    tiles_per_seq = seq // tm
    in_specs = [pl.BlockSpec((tm, d), lambda i: (i, 0)), pl.BlockSpec((1, d), lambda i: (0, 0))]
    out_specs, out_shapes = [], []
    for w, layout, dt in zip(weights, layouts, dtypes):
        width = w.shape[1]
        in_specs.append(pl.BlockSpec((d, width), lambda i: (0, 0)))
        if layout == "heads":
            nh = width // HEAD_DIM
            out_shapes.append(jax.ShapeDtypeStruct((batch, nh, seq, HEAD_DIM), dt))
            out_specs.append(pl.BlockSpec((1, nh, tm, HEAD_DIM),
                                          lambda i: (i // tiles_per_seq, 0, i % tiles_per_seq, 0)))
        elif layout == "cols":
            out_shapes.append(jax.ShapeDtypeStruct((batch, width, seq), dt))
            out_specs.append(pl.BlockSpec((1, width, tm),
                                          lambda i: (i // tiles_per_seq, 0, i % tiles_per_seq)))
        else:
            out_shapes.append(jax.ShapeDtypeStruct((n, width), dt))
            out_specs.append(pl.BlockSpec((tm, width), lambda i: (i, 0)))
    kern = functools.partial(_norm_proj_kernel, n_out=len(weights), layouts=tuple(layouts))
    return pl.pallas_call(
        kern, grid=(n // tm,), in_specs=in_specs, out_specs=out_specs, out_shape=out_shapes,
        compiler_params=_cparams("parallel"), name="norm_proj",
    )(x2d, g.reshape(1, d), *[w.astype(BF16) for w in weights])


def _out_proj_kernel(*refs, n_a, has_z, has_g):
    x_ref = refs[0]
    a_refs = refs[1:1 + n_a]
    pos = 1 + n_a
    z_ref = refs[pos] if has_z else None
    pos += int(has_z)
    w_ref = refs[pos]
    pos += 1
    g_ref = refs[pos] if has_g else None
    o_ref = refs[-1]
    a = a_refs[0][...].astype(F32)
    for r in a_refs[1:]:
        a = a + r[...].astype(F32)
    if has_z:
        a = a * _silu(z_ref[...])
    y = x_ref[...] + jnp.dot(a.astype(BF16), w_ref[...], preferred_element_type=F32)
    if has_g:
        y = _rms(y, g_ref[...])
    o_ref[...] = y


def _out_proj(x2d, a_list, z, w, final_g=None):
    n, d = x2d.shape
    c = w.shape[0]
    tm = min(ROW_TILE, n)
    row = lambda i: (i, 0)
    fixed = lambda i: (0, 0)
    in_specs = [pl.BlockSpec((tm, d), row)] + [pl.BlockSpec((tm, c), row) for _ in a_list]
    args = [x2d] + list(a_list)
    if z is not None:
        in_specs.append(pl.BlockSpec((tm, c), row))
        args.append(z)
    in_specs.append(pl.BlockSpec((c, d), fixed))
    args.append(w.astype(BF16))
    if final_g is not None:
        in_specs.append(pl.BlockSpec((1, d), fixed))
        args.append(final_g.reshape(1, d))
    kern = functools.partial(_out_proj_kernel, n_a=len(a_list), has_z=z is not None,
                             has_g=final_g is not None)
    return pl.pallas_call(
        kern, grid=(n // tm,), in_specs=in_specs, out_specs=pl.BlockSpec((tm, d), row),
        out_shape=jax.ShapeDtypeStruct((n, d), F32), compiler_params=_cparams("parallel"),
        name="out_proj",
    )(*args)


def _t5_bucket(dist):
    max_exact = T5_BUCKETS // 2
    d = jnp.maximum(dist, 0)
    df = jnp.maximum(d, 1).astype(F32)
    large = max_exact + (jnp.log(df / max_exact) / math.log(T5_MAX_DIST / max_exact)
                         * (T5_BUCKETS - max_exact)).astype(jnp.int32)
    large = jnp.minimum(large, T5_BUCKETS - 1)
    return jnp.where(d < max_exact, d, large)


def _band_bias(t5_table, q_rows, kc, offset):
    period = kc + q_rows
    j = jnp.arange(period)
    j = jnp.where(j >= kc, j - period, j)
    vec = jnp.take(t5_table, _t5_bucket(offset - j), axis=0).astype(F32).T
    heads = vec.shape[0]
    flat = jnp.tile(vec, (1, q_rows))[:, :q_rows * (period - 1)]
    return flat.reshape(heads, q_rows, period - 1)[:, :, :kc]


def _band_attn_kernel(*refs, nprev, window, has_sink, gate_col):
    q_ref, k_ref, vt_ref, bias_ref = refs[:4]
    pos = 4
    sink_ref = refs[pos] if has_sink else None
    pos += int(has_sink)
    gate_ref = refs[pos] if gate_col is not None else None
    o_ref = refs[-1]
    tq = Q_BLOCK
    kc = (nprev + 1) * tq
    n_blocks = q_ref.shape[2] // tq
    kj = lax.broadcasted_iota(jnp.int32, (kc, tq), 0)
    qi = lax.broadcasted_iota(jnp.int32, (kc, tq), 1)
    dist = nprev * tq + qi - kj
    band = (dist >= 0) & (dist < window)
    ones = jnp.ones((HEAD_DIM, kc), BF16)
    probs_idx = [(sb, g) for sb in range(n_blocks) for g in range(KV_HEADS)]
    valids, starts = [], []
    for sb in range(n_blocks):
        n = pl.program_id(1) * n_blocks + sb
        valid = band & ((n - nprev) * tq + kj >= 0)
        valids.append(jnp.concatenate([valid] * GQA, axis=1))
        starts.append([pl.multiple_of(jnp.maximum(n - nprev + j, 0) * tq, tq)
                       for j in range(nprev + 1)])
    scores = []
    for sb, g in probs_idx:
        rows = slice(sb * tq, (sb + 1) * tq)
        kg = jnp.concatenate([k_ref[0, g, pl.ds(s, tq), :] for s in starts[sb]], axis=0)
        qg = q_ref[0, GQA * g:GQA * (g + 1), rows].reshape(GQA * tq, HEAD_DIM) * (HEAD_DIM ** -0.5)
        scores.append(_bdot_nt(kg, qg))
    probs, maxes = [], []
    for i, (sb, g) in enumerate(probs_idx):
        s = jnp.where(valids[sb], scores[i] + bias_ref[g], NEG)
        m = jnp.max(s, axis=0, keepdims=True)
        if has_sink:
            m = jnp.maximum(m, sink_ref[g])
        probs.append(jnp.exp(s - m).astype(BF16))
        maxes.append(m)
    accs = []
    for i, (sb, g) in enumerate(probs_idx):
        vtg = jnp.concatenate([vt_ref[0, g * HEAD_DIM:(g + 1) * HEAD_DIM, pl.ds(s, tq)]
                               for s in starts[sb]], axis=1)
        vtg = jnp.concatenate([vtg, ones], axis=0)
        accs.append(jnp.dot(vtg, probs[i], preferred_element_type=F32))
    for i, (sb, g) in enumerate(probs_idx):
        l = accs[i][HEAD_DIM:HEAD_DIM + 1]
        if has_sink:
            l = l + jnp.exp(sink_ref[g] - maxes[i])
        o_t = accs[i][:HEAD_DIM] / l
        outs = []
        for r in range(GQA):
            oh = o_t[:, r * tq:(r + 1) * tq]
            if gate_col is not None:
                c = gate_col + GQA * g + r
                oh = oh * jax.nn.sigmoid(gate_ref[0, c:c + 1, sb * tq:(sb + 1) * tq])
            outs.append(oh.T)
        o_ref[0, sb * tq:(sb + 1) * tq, g * GQA * HEAD_DIM:(g + 1) * GQA * HEAD_DIM] = (
            jnp.concatenate(outs, axis=-1))


def _band_attn(q, k, vt, t5_table, window, sinks=None, gates_t=None, gate_col=None):
    b, _, t, _ = q.shape
    nprev = -(-window // Q_BLOCK)
    kc = (nprev + 1) * Q_BLOCK
    width = GQA * Q_BLOCK
    bias = _band_bias(t5_table, Q_BLOCK, kc, nprev * Q_BLOCK)
    tb = min(BAND_QUERIES, t)
    in_specs = [
        pl.BlockSpec((1, ATTN_HEADS, tb, HEAD_DIM), lambda i, n: (i, 0, n, 0)),
        pl.BlockSpec((1, KV_HEADS, t, HEAD_DIM), lambda i, n: (i, 0, 0, 0)),
        pl.BlockSpec((1, KV_HEADS * HEAD_DIM, t), lambda i, n: (i, 0, 0)),
        pl.BlockSpec((KV_HEADS, kc, width), lambda i, n: (0, 0, 0)),
    ]
    bias_t = bias.reshape(KV_HEADS, GQA, Q_BLOCK, kc).transpose(0, 3, 1, 2).reshape(KV_HEADS, kc, width)
    args = [q, k, vt, bias_t]
    if sinks is not None:
        in_specs.append(pl.BlockSpec((KV_HEADS, 1, width), lambda i, n: (0, 0, 0)))
        sk = jnp.broadcast_to(sinks.astype(F32).reshape(KV_HEADS, GQA, 1), (KV_HEADS, GQA, Q_BLOCK))
        args.append(sk.reshape(KV_HEADS, 1, width))
    if gates_t is not None:
        in_specs.append(pl.BlockSpec((1, LANES, tb), lambda i, n: (i, 0, n)))
        args.append(gates_t)
    kern = functools.partial(_band_attn_kernel, nprev=nprev, window=window,
                             has_sink=sinks is not None, gate_col=gate_col)
    d = ATTN_HEADS * HEAD_DIM
    return pl.pallas_call(
        kern, grid=(b, t // tb), in_specs=in_specs,
        out_specs=pl.BlockSpec((1, tb, d), lambda i, n: (i, n, 0)),
        out_shape=jax.ShapeDtypeStruct((b, t, d), F32),
        compiler_params=_cparams("parallel", "parallel"), name="band_attn",
    )(*args)


def _swa_layer(x2d, g, w_in, sinks, w_out, t5_table, final_g, batch, seq):
    nq, nkv = ATTN_HEADS * HEAD_DIM, KV_HEADS * HEAD_DIM
    ws = [w_in[:, :nq], w_in[:, nq:nq + nkv], w_in[:, nq + nkv:nq + 2 * nkv], w_in[:, nq + 2 * nkv:]]
    q, k, vt, z = _norm_proj(x2d, g, ws, ["heads", "heads", "cols", "rows"], [BF16, BF16, BF16, F32],
                             batch, seq)
    o = _band_attn(q, k, vt, t5_table, SWA_WINDOW, sinks=sinks)
    return _out_proj(x2d, [o.reshape(x2d.shape)], z, w_out, final_g)


def _rwkv_pre_kernel(x_ref, xp_ref, g_ref, mu_ref, wr_ref, wk_ref, wv_ref, wz_ref,
                     w0_ref, w1_ref, w2_ref, a0_ref, a1_ref, a2_ref,
                     r_ref, k_ref, v_ref, z_ref, lw_ref, a_ref, *, tiles_per_seq):
    i = pl.program_id(0)
    g = g_ref[...]
    xn = _rms(x_ref[...], g)
    prev = _rms(xp_ref[...], g)[SUBLANES - 1:SUBLANES]
    prev = jnp.where(i % tiles_per_seq == 0, 0.0, prev)
    row = lax.broadcasted_iota(jnp.int32, xn.shape, 0)
    xprev = jnp.where(row == 0, prev, pltpu.roll(xn, 1, 0))
    xx = xprev - xn
    lerp = lambda s: xn + xx * mu_ref[s:s + 1, :]
    r_ref[...] = _bdot(lerp(0), wr_ref[...])
    k_ref[...] = _bdot(lerp(1), wk_ref[...])
    v_ref[...] = _bdot(lerp(2), wv_ref[...])
    z_ref[...] = _bdot(lerp(3), wz_ref[...])
    wl = w0_ref[...] + _bdot(jnp.tanh(_bdot(lerp(4), w1_ref[...])), w2_ref[...])
    sp = jnp.maximum(-wl, 0.0) + jnp.log1p(jnp.exp(-jnp.abs(wl)))
    lw_ref[...] = -jnp.exp(-sp - 0.5)
    al = a0_ref[...] + _bdot(_bdot(lerp(5), a1_ref[...]), a2_ref[...])
    a_ref[...] = jax.nn.sigmoid(al)


def _rwkv_pre(x2d, g, mu, w_in, w0, w1, w2, a0, a1, a2, seq):
    n, d = x2d.shape
    c = w0.shape[0]
    tm = min(ROW_TILE, seq)
    tiles_per_seq = seq // tm
    row = lambda i: (i, 0)
    fixed = lambda i: (0, 0)
    ws = [w_in[:, s * c:(s + 1) * c].astype(BF16) for s in range(4)]
    in_specs = [
        pl.BlockSpec((tm, d), row),
        pl.BlockSpec((SUBLANES, d), lambda i: (jnp.maximum(i * (tm // SUBLANES) - 1, 0), 0)),
        pl.BlockSpec((1, d), fixed), pl.BlockSpec((6, d), fixed),
    ] + [pl.BlockSpec((d, c), fixed)] * 4 + [
        pl.BlockSpec((1, c), fixed), pl.BlockSpec(w1.shape, fixed), pl.BlockSpec(w2.shape, fixed),
        pl.BlockSpec((1, c), fixed), pl.BlockSpec(a1.shape, fixed), pl.BlockSpec(a2.shape, fixed),
    ]
    out = jax.ShapeDtypeStruct((n, c), F32)
    kern = functools.partial(_rwkv_pre_kernel, tiles_per_seq=tiles_per_seq)
    return pl.pallas_call(
        kern, grid=(n // tm,), in_specs=in_specs, out_specs=[pl.BlockSpec((tm, c), row)] * 6,
        out_shape=[out] * 6, compiler_params=_cparams("parallel"), name="rwkv_pre",
    )(x2d, x2d, g.reshape(1, d), mu, *ws, w0.reshape(1, c), w1.astype(BF16), w2.astype(BF16),
      a0.reshape(1, c), a1.astype(BF16), a2.astype(BF16))


def _dot3_many(a_list, b_list, dot=_bdot, passes=3):
    if passes == 1:
        return [dot(a, b) for a, b in zip(a_list, b_list)]
    sa = [_split2(a) for a in a_list]
    sb = [_split2(b) for b in b_list]
    hh = [dot(x[0], y[0]) for x, y in zip(sa, sb)]
    hl = [dot(x[0], y[1]) for x, y in zip(sa, sb)]
    lh = [dot(x[1], y[0]) for x, y in zip(sa, sb)]
    return [p + (q + r) for p, q, r in zip(hh, hl, lh)]


def _lane_sums(x_list, ones):
    parts = [_split2(x) for x in x_list]
    hi = [jnp.dot(p[0], ones, preferred_element_type=F32) for p in parts]
    lo = [jnp.dot(p[1], ones, preferred_element_type=F32) for p in parts]
    return [a + b for a, b in zip(hi, lo)]


def _rwkv_chunk_ops(ats, rts, bts, kts, bhs, khs, vs, wls, lower_strict, lower_incl, eye, blockdiag):
    L = RWKV_CHUNK
    n = len(ats)
    n_ch = ats[0].shape[1]
    bks = [jnp.concatenate([bt, kt], axis=0) for bt, kt in zip(bts, kts)]
    ars = [jnp.concatenate([at, rt], axis=0) for at, rt in zip(ats, rts)]
    As = _dot3_many(ars, bks, dot=_bdot_nt, passes=RWKV_SCORE_PASSES)
    a_ab = [jnp.where(lower_strict, A[:L, :L], 0.0) for A in As]
    a_rb = [jnp.where(lower_incl, A[L:, :L], 0.0) for A in As]
    a_k = [jnp.concatenate([jnp.where(lower_strict, A[:L, L:], 0.0),
                            jnp.where(lower_incl, A[L:, L:], 0.0)], axis=0) for A in As]
    dblk = [jnp.where(blockdiag, x, 0.0) for x in a_ab]
    akv = _dot3_many(a_k, vs, passes=RWKV_VALUE_PASSES)
    inv = functools.partial(_dot3_many, passes=RWKV_INVERSE_PASSES)
    d2 = inv(dblk, dblk)
    res = inv([eye + d for d in dblk] + d2, [eye + d for d in d2] + d2)
    s4, d4 = res[:n], res[n:]
    res = inv(d4 + d4, s4 + d4)
    s8 = [s + x for s, x in zip(s4, res[:n])]
    d8 = res[n:]
    tdiag = [s + x for s, x in zip(s8, inv(d8, s8))]
    rhs = [jnp.concatenate([at, x[:L]], axis=1) for at, x in zip(ats, akv)]
    res = inv(tdiag + tdiag, rhs + [x - d for x, d in zip(a_ab, dblk)])
    xt, nt = res[:n], res[n:]
    res = inv(nt + nt, xt + nt)
    u = [x + y for x, y in zip(xt, res[:n])]
    nt2 = res[n:]
    pq = [x + y for x, y in zip(u, inv(nt2, u))]
    res = _dot3_many(a_rb, pq, passes=RWKV_VALUE_PASSES)
    gh = [jnp.concatenate([rt, x[L:]], axis=1) + y for rt, x, y in zip(rts, akv, res)]
    res = _dot3_many(bhs + khs, pq + vs, dot=_bdot_tn, passes=RWKV_VALUE_PASSES)
    out = []
    for i in range(n):
        mc = res[i]
        m_op = mc[:, :n_ch] + eye * wls[i]
        c_op = mc[:, n_ch:] + res[n + i]
        out.append((gh[i][:, :n_ch], gh[i][:, n_ch:], m_op, c_op))
    return out


def _rwkv_scan_kernel(r_ref, k_ref, v_ref, z_ref, lw_ref, a_ref, kk_ref, ka_ref, rk_ref,
                      lnw_ref, lnb_ref, o_ref, state_ref):
    tb = pl.program_id(2)

    @pl.when(tb == 0)
    def _():
        state_ref[...] = jnp.zeros_like(state_ref)

    L = RWKV_CHUNK
    N = RWKV_HEAD
    ri = lax.broadcasted_iota(jnp.int32, (L, L), 0)
    ci = lax.broadcasted_iota(jnp.int32, (L, L), 1)
    lower_strict = ri > ci
    lower_incl = ri >= ci
    tri_incl = jnp.where(lower_incl, 1.0, 0.0).astype(BF16)
    eye = jnp.where(ri == ci, 1.0, 0.0).astype(F32)
    blockdiag = (ri // 16) == (ci // 16)
    n_heads = r_ref.shape[2] // N
    n_chunks = r_ref.shape[1] // L
    width = n_heads * N
    hi_ = lax.broadcasted_iota(jnp.int32, (width, width), 0) // N
    hj_ = lax.broadcasted_iota(jnp.int32, (width, width), 1) // N
    head_ones = jnp.where(hi_ == hj_, 1.0, 0.0).astype(BF16)
    chunk_rows = [slice(c * L, (c + 1) * L) for c in range(n_chunks)]
    r_c = [r_ref[0, rows, :] for rows in chunk_rows]
    v_c = [v_ref[0, rows, :] for rows in chunk_rows]
    lw_c = [lw_ref[0, rows, :] for rows in chunk_rows]
    kk_c, kp_c, a_c = [], [], []
    for rows in chunk_rows:
        k = k_ref[0, rows, :]
        a = a_ref[0, rows, :]
        kk_c.append(k * kk_ref[...])
        kp_c.append(k * (1.0 + (a - 1.0) * ka_ref[...]))
        a_c.append(a)
    sums = _lane_sums([x * x for x in kk_c] + [r * kp * rk_ref[...] for r, kp in zip(r_c, kp_c)],
                      head_ones)
    bonus_c = [sums[n_chunks + c] * v_c[c] for c in range(n_chunks)]
    cum_c = [[jnp.dot(tri_incl, p, preferred_element_type=F32) for p in _split3(lw)] for lw in lw_c]
    cum_c = [a + (b + c) for a, b, c in cum_c]
    slabs = []
    for c in range(n_chunks):
        kk = kk_c[c] / jnp.maximum(jnp.sqrt(sums[c]), 1e-12)
        cum = cum_c[c]
        w_inv = jnp.exp(-cum)
        cum_last = cum[L - 1:L, :]
        w_tail = jnp.exp(cum_last - cum)
        bb = kk * a_c[c]
        slabs.append((-kk * jnp.exp(cum - lw_c[c]), r_c[c] * jnp.exp(cum), bb * w_inv,
                      kp_c[c] * w_inv, bb * w_tail, kp_c[c] * w_tail, v_c[c], jnp.exp(cum_last)))
    probs = [(hh, c) for c in range(n_chunks) for hh in range(n_heads)]
    per_head = [[slabs[c][j][:, hh * N:(hh + 1) * N] for hh, c in probs] for j in range(8)]
    ops = _rwkv_chunk_ops(*per_head, lower_strict, lower_incl, eye, blockdiag)
    states = [state_ref[hh] for hh in range(n_heads)]
    ys = {}
    for c in range(n_chunks):
        idx = [c * n_heads + hh for hh in range(n_heads)]
        out = _dot3_many([ops[i][0] for i in idx], states, passes=RWKV_VALUE_PASSES)
        upd = _dot3_many([ops[i][2] for i in idx], states, passes=RWKV_STATE_PASSES)
        for hh, i in enumerate(idx):
            ys[(hh, c)] = out[hh] + ops[i][1]
            states[hh] = upd[hh] + ops[i][3]
    for hh in range(n_heads):
        state_ref[hh] = states[hh]
    y_c = [jnp.concatenate([ys[(hh, c)] for hh in range(n_heads)], axis=1) for c in range(n_chunks)]
    yc_c = [y - s * (1.0 / N) for y, s in zip(y_c, _lane_sums(y_c, head_ones))]
    var_c = [s * (1.0 / N) for s in _lane_sums([yc * yc for yc in yc_c], head_ones)]
    for c, rows in enumerate(chunk_rows):
        yn = yc_c[c] * lax.rsqrt(var_c[c] + RWKV_GN_EPS) * lnw_ref[...] + lnb_ref[...]
        o_ref[0, rows, :] = (yn + bonus_c[c]) * _silu(z_ref[0, rows, :])


def _rwkv_scan(r, k, v, z, lw, a, k_k, k_a, r_k, lnx_w, lnx_b):
    b, t, c = r.shape
    tb = min(RWKV_BLOCK, t)
    seq_spec = pl.BlockSpec((1, tb, RWKV_LANES), lambda i, p, j: (i, j, p))
    par_spec = pl.BlockSpec((1, RWKV_LANES), lambda i, p, j: (0, p))
    params = [x.reshape(1, c).astype(F32) for x in (k_k, k_a, r_k, lnx_w, lnx_b)]
    return pl.pallas_call(
        _rwkv_scan_kernel, grid=(b, c // RWKV_LANES, t // tb),
        in_specs=[seq_spec] * 6 + [par_spec] * 5, out_specs=seq_spec,
        out_shape=jax.ShapeDtypeStruct((b, t, c), F32),
        scratch_shapes=[pltpu.VMEM((RWKV_LANES // RWKV_HEAD, RWKV_HEAD, RWKV_HEAD), F32)],
        compiler_params=_cparams("parallel", "parallel", "arbitrary"), name="rwkv_scan",
    )(r, k, v, z, lw, a, *params)


def _rwkv_layer(x2d, g, mu, w_in, w0, w1, w2, a0, a1, a2, k_k, k_a, r_k, lnx_w, lnx_b, w_out,
                final_g, batch, seq):
    c = w0.shape[0]
    r, k, v, z, lw, a = _rwkv_pre(x2d, g, mu, w_in, w0, w1, w2, a0, a1, a2, seq)
    sh = (batch, seq, c)
    y = _rwkv_scan(r.reshape(sh), k.reshape(sh), v.reshape(sh), z.reshape(sh), lw.reshape(sh),
                   a.reshape(sh), k_k, k_a, r_k, lnx_w, lnx_b)
    return _out_proj(x2d, [y.reshape(x2d.shape[0], c)], None, w_out, final_g)


def _nsa_compress_kernel(u_ref, posa_ref, posb_ref, w1a_ref, w1b_ref, w2_ref, o_ref):
    u = u_ref[0]
    ha = _bdot(u + posa_ref[...], w1a_ref[...])
    hb = _bdot(u + posb_ref[...], w1b_ref[...])
    h = ha + pltpu.roll(hb, hb.shape[0] - 1, 0)
    o_ref[0] = _bdot(_silu(h), w2_ref[...])


def _nsa_compress(t_rows, pos, w1, w2, batch, seq):
    width = t_rows.shape[1]
    dh = HEAD_DIM
    g = width // dh
    stride = NSA_CMP_STRIDE
    nch = seq // stride
    hid = w1.shape[1]
    u = t_rows.reshape(batch, nch, stride * width)
    eye = jnp.eye(g, dtype=F32)
    pos_x = jnp.broadcast_to(pos.astype(F32).reshape(2, stride, 1, dh), (2, stride, g, dh))
    pos_x = pos_x.reshape(2, 1, stride * width)
    w1_x = jnp.einsum('hldj,ge->hlgdej', w1.astype(F32).reshape(2, stride, dh, hid), eye)
    w1_x = w1_x.reshape(2, stride * width, g * hid).astype(BF16)
    w2_x = jnp.einsum('jd,ge->gjed', w2.astype(F32), eye).reshape(g * hid, width).astype(BF16)
    fixed = lambda i: (0, 0)
    return pl.pallas_call(
        _nsa_compress_kernel, grid=(batch,),
        in_specs=[pl.BlockSpec((1, nch, stride * width), lambda i: (i, 0, 0)),
                  pl.BlockSpec((1, stride * width), fixed), pl.BlockSpec((1, stride * width), fixed),
                  pl.BlockSpec((stride * width, g * hid), fixed),
                  pl.BlockSpec((stride * width, g * hid), fixed),
                  pl.BlockSpec((g * hid, width), fixed)],
        out_specs=pl.BlockSpec((1, nch, width), lambda i: (i, 0, 0)),
        out_shape=jax.ShapeDtypeStruct((batch, nch, width), F32),
        compiler_params=_cparams("parallel"), name="nsa_compress",
    )(u, pos_x[0], pos_x[1], w1_x[0], w1_x[1], w2_x)


def _nsa_cmp_attn_kernel(q_ref, kc_ref, vct_ref, gate_ref, o_ref, sel_ref, *, n_sel):
    g = pl.program_id(1)
    n = pl.program_id(2)
    tq = q_ref.shape[2]
    ncmp = kc_ref.shape[2]
    q = q_ref[0].reshape(GQA * tq, HEAD_DIM) * (HEAD_DIM ** -0.5)
    s = _bdot_nt(kc_ref[0, 0], q)
    cend = lax.broadcasted_iota(jnp.int32, (ncmp, tq), 0) * NSA_CMP_STRIDE + (NSA_CMP_LEN - 1)
    tpos = n * tq + lax.broadcasted_iota(jnp.int32, (ncmp, tq), 1)
    ok = jnp.concatenate([cend <= tpos] * GQA, axis=1)
    s = jnp.where(ok, s, NEG)
    m = jnp.max(s, axis=0, keepdims=True)
    e = jnp.where(ok, jnp.exp(s - m), 0.0)
    l = jnp.sum(e, axis=0, keepdims=True)
    p = e / jnp.where(l > 0.0, l, 1.0)
    o_t = jnp.dot(vct_ref[0, 0].astype(BF16), p.astype(BF16), preferred_element_type=F32)
    outs = []
    for r in range(GQA):
        outs.append((o_t[:, r * tq:(r + 1) * tq] * _gate_row(gate_ref, GQA * g + r)).T)
    o_ref[0] = jnp.concatenate(outs, axis=-1)

    psum = p[:, :tq]
    for r in range(1, GQA):
        psum = psum + p[:, r * tq:(r + 1) * tq]
    si = lax.broadcasted_iota(jnp.int32, (n_sel, ncmp), 0)
    ni = lax.broadcasted_iota(jnp.int32, (n_sel, ncmp), 1)
    ratio = NSA_SEL_LEN // NSA_CMP_STRIDE
    overlap = ((ni < ratio * (si + 1)) & (ni * NSA_CMP_STRIDE + NSA_CMP_LEN - 1 >= si * NSA_SEL_LEN))
    overlap = jnp.where(overlap, 1.0, 0.0).astype(BF16)
    imp = _dot_exact_rhs(psum, overlap, dot=lambda a, b: _bdot(b, a))
    blk = lax.broadcasted_iota(jnp.int32, (n_sel, tq), 0)
    tq_pos = n * tq + lax.broadcasted_iota(jnp.int32, (n_sel, tq), 1)
    cur = tq_pos // NSA_SEL_LEN
    forced = (blk == 0) | (blk == cur) | (blk == cur - 1)
    future = blk * NSA_SEL_LEN > tq_pos
    imp = jnp.where(forced, BIG, jnp.where(future, NEG, imp))
    n_grp = n_sel // SUBLANES
    blk_g = blk[:SUBLANES, :LANES]
    for c0 in range(0, tq, LANES):
        imp_c = imp[:, c0:c0 + LANES]
        imp_g = [imp_c[SUBLANES * b:SUBLANES * (b + 1)] for b in range(n_grp)]
        ranks = [jnp.zeros((SUBLANES, LANES), F32) for _ in range(n_grp)]
        for j in range(n_sel):
            row = imp_c[j:j + 1, :]
            for b in range(n_grp):
                if b > j // SUBLANES:
                    ahead = row >= imp_g[b]
                elif b < j // SUBLANES:
                    ahead = row > imp_g[b]
                else:
                    ahead = (row > imp_g[b]) | ((row == imp_g[b]) & (blk_g > j % SUBLANES))
                ranks[b] = ranks[b] + jnp.where(ahead, 1.0, 0.0)
        rank = jnp.concatenate(ranks, axis=0)
        sel = jnp.where(rank < float(min(NSA_TOPK, n_sel)), 1.0, 0.0)
        sel_ref[0, 0, :, c0:c0 + LANES] = sel


def _nsa_cmp_attn(q, k_cmp, v_cmp, gates_t):
    b, _, t, _ = q.shape
    ncmp = k_cmp.shape[2]
    n_sel = t // NSA_SEL_LEN
    d = ATTN_HEADS * HEAD_DIM
    kern = functools.partial(_nsa_cmp_attn_kernel, n_sel=n_sel)
    tq = min(NSA_CMP_QUERIES, t)
    cmp_spec = pl.BlockSpec((1, 1, ncmp, HEAD_DIM), lambda i, g, n: (i, g, 0, 0))
    return pl.pallas_call(
        kern, grid=(b, KV_HEADS, t // tq),
        in_specs=[pl.BlockSpec((1, GQA, tq, HEAD_DIM), lambda i, g, n: (i, g, n, 0)),
                  cmp_spec, pl.BlockSpec((1, 1, HEAD_DIM, ncmp), lambda i, g, n: (i, g, 0, 0)),
                  pl.BlockSpec((1, LANES, tq), lambda i, g, n: (i, 0, n))],
        out_specs=[pl.BlockSpec((1, tq, GQA * HEAD_DIM), lambda i, g, n: (i, n, g)),
                   pl.BlockSpec((1, 1, n_sel, tq), lambda i, g, n: (i, g, 0, n))],
        out_shape=[jax.ShapeDtypeStruct((b, t, d), F32),
                   jax.ShapeDtypeStruct((b, KV_HEADS, n_sel, t), F32)],
        compiler_params=_cparams("parallel", "parallel", "parallel"), name="nsa_cmp_attn",
    )(q, k_cmp, jnp.swapaxes(v_cmp, 2, 3), gates_t)


def _nsa_sel_attn_kernel(q_ref, k_ref, vt_ref, selt_ref, bias_ref, far_ref, gate_ref, o_ref):
    g = pl.program_id(1)
    n = pl.program_id(2)
    tq = q_ref.shape[2]
    pw = Q_BLOCK
    bl = NSA_SEL_LEN
    fk = NSA_FAR_KEYS
    q_all = q_ref[0].reshape(GQA * tq, HEAD_DIM) * (HEAD_DIM ** -0.5)
    q_t = q_all.astype(F32).T.astype(BF16)

    def block_rows(first_block, count, limit):
        rows = []
        for i in range(count):
            blk = first_block + i
            row = selt_ref[0, 0, pl.ds(jnp.clip(blk, 0, jnp.maximum(limit - 1, 0)), 1), :]
            rows.append(jnp.broadcast_to(jnp.where((blk >= 0) & (blk < limit), row, 0.0), (bl, tq)))
        return jnp.concatenate(rows, axis=0) > 0.5

    def values_t(*starts_widths):
        return jnp.concatenate([vt_ref[0, :, pl.ds(s, w)] for s, w in starts_widths], axis=1)

    prev = pl.multiple_of(jnp.maximum(n * tq - pw, 0), pw)
    diag = pl.multiple_of(n * tq, tq)
    kt = jnp.concatenate([k_ref[0, 0, pl.ds(prev, pw), :], k_ref[0, 0, pl.ds(diag, tq), :]], axis=0)
    vt = values_t((prev, pw), (diag, tq))
    kj = lax.broadcasted_iota(jnp.int32, (pw + tq, tq), 0)
    qi = lax.broadcasted_iota(jnp.int32, (pw + tq, tq), 1)
    n_blocks = (n + 1) * (tq // bl)
    first_near = n * (tq // bl) - pw // bl
    ok = block_rows(first_near, (pw + tq) // bl, n_blocks) & (pw + qi - kj >= 0)
    ok = jnp.concatenate([ok] * GQA, axis=1)
    tbl = bias_ref[0]
    far_row = far_ref[0]
    cols = []
    for r in range(GQA):
        prev_p = tbl[:pw, r * pw:(r + 1) * pw]
        diag_p = tbl[pw:, r * pw:(r + 1) * pw]
        for a in range(tq // pw):
            far_p = jnp.broadcast_to(far_row[:, r * tq + a * pw:r * tq + (a + 1) * pw], (pw, pw))
            below = [jnp.zeros((pw, pw), F32)] * (tq // pw - 1 - a)
            cols.append(jnp.concatenate([far_p] * a + [prev_p, diag_p] + below, axis=0))
    bias = jnp.concatenate(cols, axis=1)
    s = jnp.where(ok, jnp.dot(kt, q_t, preferred_element_type=F32) + bias, NEG)
    m = jnp.max(s, axis=0, keepdims=True)
    p = jnp.exp(s - m).astype(BF16)
    acc = jnp.dot(vt, p, preferred_element_type=F32)
    m = m - far_ref[0]
    far_blocks = jnp.maximum(first_near, 0)

    def body(c, carry):
        m, acc = carry
        start = pl.multiple_of(c * fk, fk)
        qk = jnp.dot(k_ref[0, 0, pl.ds(start, fk), :], q_t, preferred_element_type=F32)
        parts = []
        for i in range(fk // bl):
            blk = c * (fk // bl) + i
            row = selt_ref[0, 0, pl.ds(jnp.minimum(blk, jnp.maximum(far_blocks - 1, 0)), 1), :]
            row = jnp.where((blk < far_blocks) & (row > 0.5), 0.0, NEG)
            parts.append(qk[i * bl:(i + 1) * bl] + jnp.concatenate([row] * GQA, axis=1))
        s = jnp.concatenate(parts, axis=0)
        m_new = jnp.maximum(m, jnp.max(s, axis=0, keepdims=True))
        p = jnp.exp(s - m_new).astype(BF16)
        pv = jnp.dot(values_t((start, fk)), p, preferred_element_type=F32)
        return m_new, jnp.exp(m - m_new) * acc + pv

    n_far = (far_blocks * bl + fk - 1) // fk
    _, acc = lax.fori_loop(0, n_far, body, (m, acc))
    o_t = acc[:HEAD_DIM] / acc[HEAD_DIM:HEAD_DIM + 1]
    outs = []
    for r in range(GQA):
        o = o_t[:, r * tq:(r + 1) * tq] * _gate_row(gate_ref, ATTN_HEADS + GQA * g + r)
        outs.append(o.T)
    o_ref[0] = jnp.concatenate(outs, axis=-1)


def _nsa_sel_attn(q, ks, vst, sel_t, t5_table, gates_t):
    b, _, t, _ = q.shape
    n_sel = t // NSA_SEL_LEN
    d = ATTN_HEADS * HEAD_DIM
    tq = min(NSA_SEL_QUERIES, t)
    near = Q_BLOCK + tq
    width = GQA * tq
    assert t % NSA_FAR_KEYS == 0 or t <= near
    bias_near = _band_bias(t5_table, Q_BLOCK, 2 * Q_BLOCK, Q_BLOCK)
    bias_near = bias_near.reshape(KV_HEADS, GQA, Q_BLOCK, 2 * Q_BLOCK).transpose(0, 3, 1, 2)
    bias_near = bias_near.reshape(KV_HEADS, 2 * Q_BLOCK, GQA * Q_BLOCK)
    far = t5_table[T5_BUCKETS - 1].astype(F32).reshape(KV_HEADS, GQA, 1)
    far = jnp.broadcast_to(far, (KV_HEADS, GQA, tq)).reshape(KV_HEADS, 1, width)
    vst_ones = vst.reshape(b, KV_HEADS, HEAD_DIM, t)
    vst_ones = jnp.concatenate([vst_ones, jnp.ones_like(vst_ones)], axis=2)
    vst_ones = vst_ones.reshape(b, 2 * KV_HEADS * HEAD_DIM, t)
    return pl.pallas_call(
        _nsa_sel_attn_kernel, grid=(b, KV_HEADS, t // tq),
        in_specs=[pl.BlockSpec((1, GQA, tq, HEAD_DIM), lambda i, g, n: (i, g, n, 0)),
                  pl.BlockSpec((1, 1, t, HEAD_DIM), lambda i, g, n: (i, g, 0, 0)),
                  pl.BlockSpec((1, 2 * HEAD_DIM, t), lambda i, g, n: (i, g, 0)),
                  pl.BlockSpec((1, 1, n_sel, tq), lambda i, g, n: (i, g, 0, n)),
                  pl.BlockSpec((1, 2 * Q_BLOCK, GQA * Q_BLOCK), lambda i, g, n: (g, 0, 0)),
                  pl.BlockSpec((1, 1, width), lambda i, g, n: (g, 0, 0)),
                  pl.BlockSpec((1, LANES, tq), lambda i, g, n: (i, 0, n))],
        out_specs=pl.BlockSpec((1, tq, GQA * HEAD_DIM), lambda i, g, n: (i, n, g)),
        out_shape=jax.ShapeDtypeStruct((b, t, d), F32),
        compiler_params=_cparams("parallel", "parallel", "parallel"), name="nsa_sel_attn",
    )(q, ks, vst_ones, sel_t, bias_near, far, gates_t)


def _nsa_layer(x2d, g, w_in, pos_k, k_w1, k_w2, pos_v, v_w1, v_w2, w_out, t5_table, final_g,
               batch, seq):
    nq, nkv = ATTN_HEADS * HEAD_DIM, KV_HEADS * HEAD_DIM
    offs = [0, nq] + [nq + nkv * (i + 1) for i in range(6)]
    ws = [w_in[:, offs[i]:offs[i + 1]] for i in range(7)]
    n_gate = 3 * ATTN_HEADS
    wg = jnp.pad(w_in[:, offs[7]:offs[7] + n_gate], ((0, 0), (0, LANES - n_gate)))
    wz = w_in[:, offs[7] + n_gate:]
    q, kc, vc, ks, vst, kw, vwt, gates_t, z = _norm_proj(
        x2d, g, ws + [wg, wz],
        ["heads", "rows", "rows", "heads", "cols", "heads", "cols", "cols", "rows"],
        [BF16, F32, F32, BF16, BF16, BF16, BF16, F32, F32], batch, seq)
    per_group = lambda c: c.reshape(batch, -1, KV_HEADS, HEAD_DIM).transpose(0, 2, 1, 3)
    k_cmp = per_group(_nsa_compress(kc, pos_k, k_w1, k_w2, batch, seq))
    v_cmp = per_group(_nsa_compress(vc, pos_v, v_w1, v_w2, batch, seq))
    o_cmp, sel_t = _nsa_cmp_attn(q, k_cmp, v_cmp, gates_t)
    o_sel = _nsa_sel_attn(q, ks, vst, sel_t, t5_table, gates_t)
    o_win = _band_attn(q, kw, vwt, t5_table, NSA_WINDOW, gates_t=gates_t, gate_col=2 * ATTN_HEADS)
    sh = x2d.shape
    return _out_proj(x2d, [o_cmp.reshape(sh), o_sel.reshape(sh), o_win.reshape(sh)], z, w_out,
                     final_g)


def _lru_kernel(x_ref, g_ref, win_u_ref, win_z_ref, cw_ref, cb_ref, wa_ref, ba_ref, wx_ref, bx_ref,
                sp_ref, wout_ref, *rest, half, has_fg):
    fg_ref = rest[0] if has_fg else None
    o_ref, tail_ref, h_ref = rest[-3:]
    j = pl.program_id(1)

    @pl.when(j == 0)
    def _():
        tail_ref[...] = jnp.zeros_like(tail_ref)
        h_ref[...] = jnp.zeros_like(h_ref)

    x = x_ref[...]
    xb = _rms(x, g_ref[...]).astype(BF16)
    u = jnp.dot(xb, win_u_ref[...], preferred_element_type=F32)
    z = jnp.dot(xb, win_z_ref[...], preferred_element_type=F32)
    tm, width = u.shape
    ext = jnp.concatenate([tail_ref[...], u], axis=0)
    tail_ref[...] = u[tm - SUBLANES:, :]
    uc = cb_ref[...] + cw_ref[CONV_WIDTH - 1:CONV_WIDTH, :] * u
    for s in range(1, CONV_WIDTH):
        uc = uc + cw_ref[CONV_WIDTH - 1 - s:CONV_WIDTH - s, :] * ext[SUBLANES - s:SUBLANES - s + tm, :]
    ucb = uc.astype(BF16)
    gr, gi = [], []
    for c in range(width // half):
        blk = ucb[:, c * half:(c + 1) * half]
        gr.append(jnp.dot(blk, wa_ref[c], preferred_element_type=F32))
        gi.append(jnp.dot(blk, wx_ref[c], preferred_element_type=F32))
    rg = jax.nn.sigmoid(jnp.concatenate(gr, axis=1) + ba_ref[...])
    ig = jax.nn.sigmoid(jnp.concatenate(gi, axis=1) + bx_ref[...])
    log_a = -LRU_C * rg * sp_ref[...]
    a = jnp.exp(log_a)
    bv = jnp.sqrt(1.0 - a * a) * (ig * uc)
    n_grp = tm // SUBLANES
    a = a.reshape(n_grp, SUBLANES, width)
    bv = bv.reshape(n_grp, SUBLANES, width)
    row = lax.broadcasted_iota(jnp.int32, (n_grp, SUBLANES, width), 1)
    sh = 1
    while sh < SUBLANES:
        a_s = jnp.where(row >= sh, pltpu.roll(a, sh, 1), 1.0)
        b_s = jnp.where(row >= sh, pltpu.roll(bv, sh, 1), 0.0)
        bv = a * b_s + bv
        a = a * a_s
        sh *= 2
    carry = h_ref[...]
    groups = []
    for i in range(n_grp):
        hg = bv[i] + a[i] * carry
        carry = hg[SUBLANES - 1:SUBLANES, :]
        groups.append(hg)
    h_ref[...] = carry
    gated = (jnp.concatenate(groups, axis=0) * _silu(z)).astype(BF16)
    y = x + jnp.dot(gated, wout_ref[...], preferred_element_type=F32)
    o_ref[...] = _rms(y, fg_ref[...]) if has_fg else y


def _block_diag(w, group):
    nb, n, _ = w.shape
    w = w.reshape(nb // group, group, n, n)
    eye = jnp.eye(group, dtype=w.dtype)
    return jnp.einsum('cgij,gh->cgihj', w, eye).reshape(nb // group, group * n, group * n)


def _lru_layer(x2d, g, w_in, conv_w, conv_b, ga_w, ga_b, gx_w, gx_b, lam, w_out, final_g, batch, seq):
    n, d = x2d.shape
    width = w_in.shape[1] // 2
    blk = ga_w.shape[1]
    group = LANES // math.gcd(blk, LANES)
    group = min(group, ga_w.shape[0])
    half = group * blk
    nsup = width // half
    tm = min(LRU_TILE, seq)
    tiles = seq // tm
    row = lambda i, j: (i * tiles + j, 0)
    fixed = lambda i, j: (0, 0)
    fixed3 = lambda i, j: (0, 0, 0)
    vec = lambda a: a.astype(F32).reshape(1, -1)
    softplus_neg_lam = jax.nn.softplus(-lam.astype(F32))
    has_fg = final_g is not None
    in_specs = [pl.BlockSpec((tm, d), row), pl.BlockSpec((1, d), fixed),
                pl.BlockSpec((d, width), fixed), pl.BlockSpec((d, width), fixed),
                pl.BlockSpec((CONV_WIDTH, width), fixed), pl.BlockSpec((1, width), fixed),
                pl.BlockSpec((nsup, half, half), fixed3), pl.BlockSpec((1, width), fixed),
                pl.BlockSpec((nsup, half, half), fixed3), pl.BlockSpec((1, width), fixed),
                pl.BlockSpec((1, width), fixed), pl.BlockSpec((width, d), fixed)]
    args = [x2d, g.reshape(1, d), w_in[:, :width].astype(BF16), w_in[:, width:].astype(BF16),
            conv_w.astype(F32), vec(conv_b), _block_diag(ga_w, group).astype(BF16), vec(ga_b),
            _block_diag(gx_w, group).astype(BF16), vec(gx_b), vec(softplus_neg_lam),
            w_out.astype(BF16)]
    if has_fg:
        in_specs.append(pl.BlockSpec((1, d), fixed))
        args.append(final_g.reshape(1, d))
    kern = functools.partial(_lru_kernel, half=half, has_fg=has_fg)
    return pl.pallas_call(
        kern, grid=(batch, tiles), in_specs=in_specs, out_specs=pl.BlockSpec((tm, d), row),
        out_shape=jax.ShapeDtypeStruct((n, d), F32),
        scratch_shapes=[pltpu.VMEM((SUBLANES, width), F32), pltpu.VMEM((1, width), F32)],
        compiler_params=_cparams("parallel", "arbitrary"), name="rglru_layer",
    )(*args)


def kernel(x, t5_table, norm_g, final_g, a_w_in, a_sinks, a_w_out, b_mu, b_w_in, b_w0, b_w1, b_w2, b_a0, b_a1, b_a2, b_k_k, b_k_a, b_r_k, b_lnx_w, b_lnx_b, b_w_out, c_w_in, c_cmp_pos_k, c_cmp_k_w1, c_cmp_k_w2, c_cmp_pos_v, c_cmp_v_w1, c_cmp_v_w2, c_w_out, d_w_in, d_conv_w, d_conv_b, d_gate_a_w, d_gate_a_b, d_gate_x_w, d_gate_x_b, d_lambda, d_w_out):
    batch, seq, d = x.shape
    depth = norm_g.shape[0]
    h = x.reshape(batch * seq, d)
    for layer in range(depth):
        m, j = layer % 4, layer // 4
        g = norm_g[layer]
        fg = final_g if layer == depth - 1 else None
        if m == 0:
            h = _swa_layer(h, g, a_w_in[j], a_sinks[j], a_w_out[j], t5_table, fg, batch, seq)
        elif m == 1:
            h = _rwkv_layer(h, g, b_mu[j], b_w_in[j], b_w0[j], b_w1[j], b_w2[j], b_a0[j], b_a1[j],
                            b_a2[j], b_k_k[j], b_k_a[j], b_r_k[j], b_lnx_w[j], b_lnx_b[j],
                            b_w_out[j], fg, batch, seq)
        elif m == 2:
            h = _nsa_layer(h, g, c_w_in[j], c_cmp_pos_k[j], c_cmp_k_w1[j], c_cmp_k_w2[j],
                           c_cmp_pos_v[j], c_cmp_v_w1[j], c_cmp_v_w2[j], c_w_out[j], t5_table,
                           fg, batch, seq)
        else:
            h = _lru_layer(h, g, d_w_in[j], d_conv_w[j], d_conv_b[j], d_gate_a_w[j], d_gate_a_b[j],
                           d_gate_x_w[j], d_gate_x_b[j], d_lambda[j], d_w_out[j], fg, batch, seq)
    return h.reshape(batch, seq, d)
```

```python
import functools
import math

import jax
import jax.numpy as jnp
from jax import lax
from jax.experimental import pallas as pl
from jax.experimental.pallas import tpu as pltpu

F32 = jnp.float32
BF16 = jnp.bfloat16

EPS = 1e-6
NEG = -1e30
BIG = 1e30
T5_BUCKETS = 32
T5_MAX_DIST = 128
ATTN_HEADS = 16
HEAD_DIM = 64
KV_HEADS = 4
GQA = ATTN_HEADS // KV_HEADS
Q_BLOCK = 128
SWA_WINDOW = 128
RWKV_HEAD = 64
RWKV_GN_EPS = 64e-5
NSA_CMP_LEN = 32
NSA_CMP_STRIDE = 16
NSA_SEL_LEN = 64
NSA_TOPK = 16
NSA_WINDOW = 512
LRU_C = 8.0
CONV_WIDTH = 4

LANES = 128
SUBLANES = 8
VMEM_LIMIT = 56 * 1024 * 1024
ROW_TILE = 512
RWKV_CHUNK = 64
RWKV_BLOCK = 512
RWKV_LANES = 256
RWKV_SCORE_PASSES = 1
RWKV_VALUE_PASSES = 1
RWKV_INVERSE_PASSES = 1
RWKV_STATE_PASSES = 3
NSA_FAR_KEYS = 512
BAND_QUERIES = 512
NSA_CMP_QUERIES = 512
NSA_SEL_QUERIES = 512
LRU_TILE = 512


def _cparams(*sem):
    return pltpu.CompilerParams(dimension_semantics=sem, vmem_limit_bytes=VMEM_LIMIT)


def _bdot(a, b):
    return jnp.dot(a.astype(BF16), b.astype(BF16), preferred_element_type=F32)


def _bdot_nt(a, b):
    return lax.dot_general(a.astype(BF16), b.astype(BF16), (((1,), (1,)), ((), ())),
                           preferred_element_type=F32)


def _bdot_tn(a, b):
    return lax.dot_general(a.astype(BF16), b.astype(BF16), (((0,), (0,)), ((), ())),
                           preferred_element_type=F32)


def _split2(x):
    hi = x.astype(BF16)
    lo = (x - hi.astype(F32)).astype(BF16)
    return hi, lo


def _split3(x):
    hi = x.astype(BF16)
    r1 = x - hi.astype(F32)
    mid = r1.astype(BF16)
    return hi, mid, (r1 - mid.astype(F32)).astype(BF16)


def _dot_exact_rhs(a, b01, dot=_bdot):
    ah, am, al = _split3(a)
    return dot(ah, b01) + (dot(am, b01) + dot(al, b01))


def _rms(x, g):
    return x * lax.rsqrt(jnp.mean(x * x, axis=-1, keepdims=True) + EPS) * g


def _silu(z):
    return z * jax.nn.sigmoid(z)


def _gate_row(gate_ref, c):
    return jax.nn.sigmoid(gate_ref[0, pl.ds(c, 1), :])


def _norm_proj_kernel(x_ref, g_ref, *refs, n_out, layouts):
    w_refs, o_refs = refs[:n_out], refs[n_out:]
    xb = _rms(x_ref[...], g_ref[...]).astype(BF16)
    for w_ref, o_ref, layout in zip(w_refs, o_refs, layouts):
        width = w_ref.shape[1]
        for c0 in range(0, width, 512):
            cw = min(512, width - c0)
            acc = jnp.dot(xb, w_ref[:, c0:c0 + cw], preferred_element_type=F32)
            if layout == "heads":
                for j in range(cw // HEAD_DIM):
                    o_ref[0, (c0 // HEAD_DIM) + j] = acc[:, j * HEAD_DIM:(j + 1) * HEAD_DIM].astype(o_ref.dtype)
            elif layout == "cols":
                o_ref[0, c0:c0 + cw, :] = acc.T.astype(o_ref.dtype)
            else:
                o_ref[:, c0:c0 + cw] = acc.astype(o_ref.dtype)


def _norm_proj(x2d, g, weights, layouts, dtypes, batch, seq):
    n, d = x2d.shape
    tm = min(ROW_TILE, seq)
    tiles_per_seq = seq // tm
    in_specs = [pl.BlockSpec((tm, d), lambda i: (i, 0)), pl.BlockSpec((1, d), lambda i: (0, 0))]
    out_specs, out_shapes = [], []
    for w, layout, dt in zip(weights, layouts, dtypes):
        width = w.shape[1]
        in_specs.append(pl.BlockSpec((d, width), lambda i: (0, 0)))
        if layout == "heads":
            nh = width // HEAD_DIM
            out_shapes.append(jax.ShapeDtypeStruct((batch, nh, seq, HEAD_DIM), dt))
            out_specs.append(pl.BlockSpec((1, nh, tm, HEAD_DIM),
                                          lambda i: (i // tiles_per_seq, 0, i % tiles_per_seq, 0)))
        elif layout == "cols":
            out_shapes.append(jax.ShapeDtypeStruct((batch, width, seq), dt))
            out_specs.append(pl.BlockSpec((1, width, tm),
                                          lambda i: (i // tiles_per_seq, 0, i % tiles_per_seq)))
        else:
            out_shapes.append(jax.ShapeDtypeStruct((n, width), dt))
            out_specs.append(pl.BlockSpec((tm, width), lambda i: (i, 0)))
    kern = functools.partial(_norm_proj_kernel, n_out=len(weights), layouts=tuple(layouts))
    return pl.pallas_call(
        kern, grid=(n // tm,), in_specs=in_specs, out_specs=out_specs, out_shape=out_shapes,
        compiler_params=_cparams("parallel"), name="norm_proj",
    )(x2d, g.reshape(1, d), *[w.astype(BF16) for w in weights])


def _out_proj_kernel(*refs, n_a, has_z, has_g):
    x_ref = refs[0]
    a_refs = refs[1:1 + n_a]
    pos = 1 + n_a
    z_ref = refs[pos] if has_z else None
    pos += int(has_z)
    w_ref = refs[pos]
    pos += 1
    g_ref = refs[pos] if has_g else None
    o_ref = refs[-1]
    a = a_refs[0][...].astype(F32)
    for r in a_refs[1:]:
        a = a + r[...].astype(F32)
    if has_z:
        a = a * _silu(z_ref[...])
    y = x_ref[...] + jnp.dot(a.astype(BF16), w_ref[...], preferred_element_type=F32)
    if has_g:
        y = _rms(y, g_ref[...])
    o_ref[...] = y


def _out_proj(x2d, a_list, z, w, final_g=None):
    n, d = x2d.shape
    c = w.shape[0]
    tm = min(ROW_TILE, n)
    row = lambda i: (i, 0)
    fixed = lambda i: (0, 0)
    in_specs = [pl.BlockSpec((tm, d), row)] + [pl.BlockSpec((tm, c), row) for _ in a_list]
    args = [x2d] + list(a_list)
    if z is not None:
        in_specs.append(pl.BlockSpec((tm, c), row))
        args.append(z)
    in_specs.append(pl.BlockSpec((c, d), fixed))
    args.append(w.astype(BF16))
    if final_g is not None:
        in_specs.append(pl.BlockSpec((1, d), fixed))
        args.append(final_g.reshape(1, d))
    kern = functools.partial(_out_proj_kernel, n_a=len(a_list), has_z=z is not None,
                             has_g=final_g is not None)
    return pl.pallas_call(
        kern, grid=(n // tm,), in_specs=in_specs, out_specs=pl.BlockSpec((tm, d), row),
        out_shape=jax.ShapeDtypeStruct((n, d), F32), compiler_params=_cparams("parallel"),
        name="out_proj",
    )(*args)


def _t5_bucket(dist):
    max_exact = T5_BUCKETS // 2
    d = jnp.maximum(dist, 0)
    df = jnp.maximum(d, 1).astype(F32)
    large = max_exact + (jnp.log(df / max_exact) / math.log(T5_MAX_DIST / max_exact)
                         * (T5_BUCKETS - max_exact)).astype(jnp.int32)
    large = jnp.minimum(large, T5_BUCKETS - 1)
    return jnp.where(d < max_exact, d, large)


def _band_bias(t5_table, q_rows, kc, offset):
    period = kc + q_rows
    j = jnp.arange(period)
    j = jnp.where(j >= kc, j - period, j)
    vec = jnp.take(t5_table, _t5_bucket(offset - j), axis=0).astype(F32).T
    heads = vec.shape[0]
    flat = jnp.tile(vec, (1, q_rows))[:, :q_rows * (period - 1)]
    return flat.reshape(heads, q_rows, period - 1)[:, :, :kc]


def _band_attn_kernel(*refs, nprev, window, has_sink, gate_col):
    q_ref, k_ref, vt_ref, bias_ref = refs[:4]
    pos = 4
    sink_ref = refs[pos] if has_sink else None
    pos += int(has_sink)
    gate_ref = refs[pos] if gate_col is not None else None
    o_ref = refs[-1]
    tq = Q_BLOCK
    kc = (nprev + 1) * tq
    n_blocks = q_ref.shape[2] // tq
    kj = lax.broadcasted_iota(jnp.int32, (kc, tq), 0)
    qi = lax.broadcasted_iota(jnp.int32, (kc, tq), 1)
    dist = nprev * tq + qi - kj
    band = (dist >= 0) & (dist < window)
    ones = jnp.ones((HEAD_DIM, kc), BF16)
    probs_idx = [(sb, g) for sb in range(n_blocks) for g in range(KV_HEADS)]
    valids, starts = [], []
    for sb in range(n_blocks):
        n = pl.program_id(1) * n_blocks + sb
        valid = band & ((n - nprev) * tq + kj >= 0)
        valids.append(jnp.concatenate([valid] * GQA, axis=1))
        starts.append([pl.multiple_of(jnp.maximum(n - nprev + j, 0) * tq, tq)
                       for j in range(nprev + 1)])
    scores = []
    for sb, g in probs_idx:
        rows = slice(sb * tq, (sb + 1) * tq)
        kg = jnp.concatenate([k_ref[0, g, pl.ds(s, tq), :] for s in starts[sb]], axis=0)
        qg = q_ref[0, GQA * g:GQA * (g + 1), rows].reshape(GQA * tq, HEAD_DIM) * (HEAD_DIM ** -0.5)
        scores.append(_bdot_nt(kg, qg))
    probs, maxes = [], []
    for i, (sb, g) in enumerate(probs_idx):
        s = jnp.where(valids[sb], scores[i] + bias_ref[g], NEG)
        m = jnp.max(s, axis=0, keepdims=True)
        if has_sink:
            m = jnp.maximum(m, sink_ref[g])
        probs.append(jnp.exp(s - m).astype(BF16))
        maxes.append(m)
    accs = []
    for i, (sb, g) in enumerate(probs_idx):
        vtg = jnp.concatenate([vt_ref[0, g * HEAD_DIM:(g + 1) * HEAD_DIM, pl.ds(s, tq)]
                               for s in starts[sb]], axis=1)
        vtg = jnp.concatenate([vtg, ones], axis=0)
        accs.append(jnp.dot(vtg, probs[i], preferred_element_type=F32))
    for i, (sb, g) in enumerate(probs_idx):
        l = accs[i][HEAD_DIM:HEAD_DIM + 1]
        if has_sink:
            l = l + jnp.exp(sink_ref[g] - maxes[i])
        o_t = accs[i][:HEAD_DIM] / l
        outs = []
        for r in range(GQA):
            oh = o_t[:, r * tq:(r + 1) * tq]
            if gate_col is not None:
                c = gate_col + GQA * g + r
                oh = oh * jax.nn.sigmoid(gate_ref[0, c:c + 1, sb * tq:(sb + 1) * tq])
            outs.append(oh.T)
        o_ref[0, sb * tq:(sb + 1) * tq, g * GQA * HEAD_DIM:(g + 1) * GQA * HEAD_DIM] = (
            jnp.concatenate(outs, axis=-1).astype(o_ref.dtype))


def _band_attn(q, k, vt, t5_table, window, sinks=None, gates_t=None, gate_col=None):
    b, _, t, _ = q.shape
    nprev = -(-window // Q_BLOCK)
    kc = (nprev + 1) * Q_BLOCK
    width = GQA * Q_BLOCK
    bias = _band_bias(t5_table, Q_BLOCK, kc, nprev * Q_BLOCK)
    tb = min(BAND_QUERIES, t)
    in_specs = [
        pl.BlockSpec((1, ATTN_HEADS, tb, HEAD_DIM), lambda i, n: (i, 0, n, 0)),
        pl.BlockSpec((1, KV_HEADS, t, HEAD_DIM), lambda i, n: (i, 0, 0, 0)),
        pl.BlockSpec((1, KV_HEADS * HEAD_DIM, t), lambda i, n: (i, 0, 0)),
        pl.BlockSpec((KV_HEADS, kc, width), lambda i, n: (0, 0, 0)),
    ]
    bias_t = bias.reshape(KV_HEADS, GQA, Q_BLOCK, kc).transpose(0, 3, 1, 2).reshape(KV_HEADS, kc, width)
    args = [q, k, vt, bias_t]
    if sinks is not None:
        in_specs.append(pl.BlockSpec((KV_HEADS, 1, width), lambda i, n: (0, 0, 0)))
        sk = jnp.broadcast_to(sinks.astype(F32).reshape(KV_HEADS, GQA, 1), (KV_HEADS, GQA, Q_BLOCK))
        args.append(sk.reshape(KV_HEADS, 1, width))
    if gates_t is not None:
        in_specs.append(pl.BlockSpec((1, LANES, tb), lambda i, n: (i, 0, n)))
        args.append(gates_t)
    kern = functools.partial(_band_attn_kernel, nprev=nprev, window=window,
                             has_sink=sinks is not None, gate_col=gate_col)
    d = ATTN_HEADS * HEAD_DIM
    return pl.pallas_call(
        kern, grid=(b, t // tb), in_specs=in_specs,
        out_specs=pl.BlockSpec((1, tb, d), lambda i, n: (i, n, 0)),
        out_shape=jax.ShapeDtypeStruct((b, t, d), BF16),
        compiler_params=_cparams("parallel", "parallel"), name="band_attn",
    )(*args)


def _swa_layer(x2d, g, w_in, sinks, w_out, t5_table, final_g, batch, seq):
    nq, nkv = ATTN_HEADS * HEAD_DIM, KV_HEADS * HEAD_DIM
    ws = [w_in[:, :nq], w_in[:, nq:nq + nkv], w_in[:, nq + nkv:nq + 2 * nkv], w_in[:, nq + 2 * nkv:]]
    q, k, vt, z = _norm_proj(x2d, g, ws, ["heads", "heads", "cols", "rows"], [BF16, BF16, BF16, F32],
                             batch, seq)
    o = _band_attn(q, k, vt, t5_table, SWA_WINDOW, sinks=sinks)
    return _out_proj(x2d, [o.reshape(x2d.shape)], z, w_out, final_g)


def _rwkv_pre_kernel(x_ref, xp_ref, g_ref, mu_ref, wr_ref, wk_ref, wv_ref, wz_ref,
                     w0_ref, w1_ref, w2_ref, a0_ref, a1_ref, a2_ref,
                     r_ref, k_ref, v_ref, z_ref, lw_ref, a_ref, *, tiles_per_seq):
    i = pl.program_id(0)
    g = g_ref[...]
    xn = _rms(x_ref[...], g)
    prev = _rms(xp_ref[...], g)[SUBLANES - 1:SUBLANES]
    prev = jnp.where(i % tiles_per_seq == 0, 0.0, prev)
    row = lax.broadcasted_iota(jnp.int32, xn.shape, 0)
    xprev = jnp.where(row == 0, prev, pltpu.roll(xn, 1, 0))
    xx = xprev - xn
    lerp = lambda s: xn + xx * mu_ref[s:s + 1, :]
    r_ref[...] = _bdot(lerp(0), wr_ref[...])
    k_ref[...] = _bdot(lerp(1), wk_ref[...])
    v_ref[...] = _bdot(lerp(2), wv_ref[...])
    z_ref[...] = _bdot(lerp(3), wz_ref[...])
    wl = w0_ref[...] + _bdot(jnp.tanh(_bdot(lerp(4), w1_ref[...])), w2_ref[...])
    sp = jnp.maximum(-wl, 0.0) + jnp.log1p(jnp.exp(-jnp.abs(wl)))
    lw_ref[...] = -jnp.exp(-sp - 0.5)
    al = a0_ref[...] + _bdot(_bdot(lerp(5), a1_ref[...]), a2_ref[...])
    a_ref[...] = jax.nn.sigmoid(al)


def _rwkv_pre(x2d, g, mu, w_in, w0, w1, w2, a0, a1, a2, seq):
    n, d = x2d.shape
    c = w0.shape[0]
    tm = min(ROW_TILE, seq)
    tiles_per_seq = seq // tm
    row = lambda i: (i, 0)
    fixed = lambda i: (0, 0)
    ws = [w_in[:, s * c:(s + 1) * c].astype(BF16) for s in range(4)]
    in_specs = [
        pl.BlockSpec((tm, d), row),
        pl.BlockSpec((SUBLANES, d), lambda i: (jnp.maximum(i * (tm // SUBLANES) - 1, 0), 0)),
        pl.BlockSpec((1, d), fixed), pl.BlockSpec((6, d), fixed),
    ] + [pl.BlockSpec((d, c), fixed)] * 4 + [
        pl.BlockSpec((1, c), fixed), pl.BlockSpec(w1.shape, fixed), pl.BlockSpec(w2.shape, fixed),
        pl.BlockSpec((1, c), fixed), pl.BlockSpec(a1.shape, fixed), pl.BlockSpec(a2.shape, fixed),
    ]
    out = jax.ShapeDtypeStruct((n, c), F32)
    kern = functools.partial(_rwkv_pre_kernel, tiles_per_seq=tiles_per_seq)
    return pl.pallas_call(
        kern, grid=(n // tm,), in_specs=in_specs, out_specs=[pl.BlockSpec((tm, c), row)] * 6,
        out_shape=[out] * 6, compiler_params=_cparams("parallel"), name="rwkv_pre",
    )(x2d, x2d, g.reshape(1, d), mu, *ws, w0.reshape(1, c), w1.astype(BF16), w2.astype(BF16),
      a0.reshape(1, c), a1.astype(BF16), a2.astype(BF16))


def _dot3_many(a_list, b_list, dot=_bdot, passes=3):
    if passes == 1:
        return [dot(a, b) for a, b in zip(a_list, b_list)]
    sa = [_split2(a) for a in a_list]
    sb = [_split2(b) for b in b_list]
    hh = [dot(x[0], y[0]) for x, y in zip(sa, sb)]
    hl = [dot(x[0], y[1]) for x, y in zip(sa, sb)]
    lh = [dot(x[1], y[0]) for x, y in zip(sa, sb)]
    return [p + (q + r) for p, q, r in zip(hh, hl, lh)]


def _lane_sums(x_list, ones):
    parts = [_split2(x) for x in x_list]
    hi = [jnp.dot(p[0], ones, preferred_element_type=F32) for p in parts]
    lo = [jnp.dot(p[1], ones, preferred_element_type=F32) for p in parts]
    return [a + b for a, b in zip(hi, lo)]


def _rwkv_chunk_ops(ats, rts, bts, kts, bhs, khs, vs, wls, lower_strict, lower_incl, eye, blockdiag):
    L = RWKV_CHUNK
    n = len(ats)
    n_ch = ats[0].shape[1]
    bks = [jnp.concatenate([bt, kt], axis=0) for bt, kt in zip(bts, kts)]
    ars = [jnp.concatenate([at, rt], axis=0) for at, rt in zip(ats, rts)]
    As = _dot3_many(ars, bks, dot=_bdot_nt, passes=RWKV_SCORE_PASSES)
    a_ab = [jnp.where(lower_strict, A[:L, :L], 0.0) for A in As]
    a_rb = [jnp.where(lower_incl, A[L:, :L], 0.0) for A in As]
    a_k = [jnp.concatenate([jnp.where(lower_strict, A[:L, L:], 0.0),
                            jnp.where(lower_incl, A[L:, L:], 0.0)], axis=0) for A in As]
    dblk = [jnp.where(blockdiag, x, 0.0) for x in a_ab]
    akv = _dot3_many(a_k, vs, passes=RWKV_VALUE_PASSES)
    inv = functools.partial(_dot3_many, passes=RWKV_INVERSE_PASSES)
    d2 = inv(dblk, dblk)
    res = inv([eye + d for d in dblk] + d2, [eye + d for d in d2] + d2)
    s4, d4 = res[:n], res[n:]
    res = inv(d4 + d4, s4 + d4)
    s8 = [s + x for s, x in zip(s4, res[:n])]
    d8 = res[n:]
    tdiag = [s + x for s, x in zip(s8, inv(d8, s8))]
    rhs = [jnp.concatenate([at, x[:L]], axis=1) for at, x in zip(ats, akv)]
    res = inv(tdiag + tdiag, rhs + [x - d for x, d in zip(a_ab, dblk)])
    xt, nt = res[:n], res[n:]
    res = inv(nt + nt, xt + nt)
    u = [x + y for x, y in zip(xt, res[:n])]
    nt2 = res[n:]
    pq = [x + y for x, y in zip(u, inv(nt2, u))]
    res = _dot3_many(a_rb, pq, passes=RWKV_VALUE_PASSES)
    gh = [jnp.concatenate([rt, x[L:]], axis=1) + y for rt, x, y in zip(rts, akv, res)]
    res = _dot3_many(bhs + khs, pq + vs, dot=_bdot_tn, passes=RWKV_VALUE_PASSES)
    out = []
    for i in range(n):
        mc = res[i]
        m_op = mc[:, :n_ch] + eye * wls[i]
        c_op = mc[:, n_ch:] + res[n + i]
        out.append((gh[i][:, :n_ch], gh[i][:, n_ch:], m_op, c_op))
    return out


def _rwkv_scan_kernel(r_ref, k_ref, v_ref, z_ref, lw_ref, a_ref, kk_ref, ka_ref, rk_ref,
                      lnw_ref, lnb_ref, o_ref, state_ref):
    tb = pl.program_id(2)

    @pl.when(tb == 0)
    def _():
        state_ref[...] = jnp.zeros_like(state_ref)

    L = RWKV_CHUNK
    N = RWKV_HEAD
    ri = lax.broadcasted_iota(jnp.int32, (L, L), 0)
    ci = lax.broadcasted_iota(jnp.int32, (L, L), 1)
    lower_strict = ri > ci
    lower_incl = ri >= ci
    tri_incl = jnp.where(lower_incl, 1.0, 0.0).astype(BF16)
    eye = jnp.where(ri == ci, 1.0, 0.0).astype(F32)
    blockdiag = (ri // 16) == (ci // 16)
    n_heads = r_ref.shape[2] // N
    n_chunks = r_ref.shape[1] // L
    width = n_heads * N
    hi_ = lax.broadcasted_iota(jnp.int32, (width, width), 0) // N
    hj_ = lax.broadcasted_iota(jnp.int32, (width, width), 1) // N
    head_ones = jnp.where(hi_ == hj_, 1.0, 0.0).astype(BF16)
    chunk_rows = [slice(c * L, (c + 1) * L) for c in range(n_chunks)]
    r_c = [r_ref[0, rows, :] for rows in chunk_rows]
    v_c = [v_ref[0, rows, :] for rows in chunk_rows]
    lw_c = [lw_ref[0, rows, :] for rows in chunk_rows]
    kk_c, kp_c, a_c = [], [], []
    for rows in chunk_rows:
        k = k_ref[0, rows, :]
        a = a_ref[0, rows, :]
        kk_c.append(k * kk_ref[...])
        kp_c.append(k * (1.0 + (a - 1.0) * ka_ref[...]))
        a_c.append(a)
    sums = _lane_sums([x * x for x in kk_c] + [r * kp * rk_ref[...] for r, kp in zip(r_c, kp_c)],
                      head_ones)
    bonus_c = [sums[n_chunks + c] * v_c[c] for c in range(n_chunks)]
    cum_c = [[jnp.dot(tri_incl, p, preferred_element_type=F32) for p in _split3(lw)] for lw in lw_c]
    cum_c = [a + (b + c) for a, b, c in cum_c]
    slabs = []
    for c in range(n_chunks):
        kk = kk_c[c] / jnp.maximum(jnp.sqrt(sums[c]), 1e-12)
        cum = cum_c[c]
        w_inv = jnp.exp(-cum)
        cum_last = cum[L - 1:L, :]
        w_tail = jnp.exp(cum_last - cum)
        bb = kk * a_c[c]
        slabs.append((-kk * jnp.exp(cum - lw_c[c]), r_c[c] * jnp.exp(cum), bb * w_inv,
                      kp_c[c] * w_inv, bb * w_tail, kp_c[c] * w_tail, v_c[c], jnp.exp(cum_last)))
    probs = [(hh, c) for c in range(n_chunks) for hh in range(n_heads)]
    per_head = [[slabs[c][j][:, hh * N:(hh + 1) * N] for hh, c in probs] for j in range(8)]
    ops = _rwkv_chunk_ops(*per_head, lower_strict, lower_incl, eye, blockdiag)
    states = [state_ref[hh] for hh in range(n_heads)]
    ys = {}
    for c in range(n_chunks):
        idx = [c * n_heads + hh for hh in range(n_heads)]
        out = _dot3_many([ops[i][0] for i in idx], states, passes=RWKV_VALUE_PASSES)
        upd = _dot3_many([ops[i][2] for i in idx], states, passes=RWKV_STATE_PASSES)
        for hh, i in enumerate(idx):
            ys[(hh, c)] = out[hh] + ops[i][1]
            states[hh] = upd[hh] + ops[i][3]
    for hh in range(n_heads):
        state_ref[hh] = states[hh]
    y_c = [jnp.concatenate([ys[(hh, c)] for hh in range(n_heads)], axis=1) for c in range(n_chunks)]
    yc_c = [y - s * (1.0 / N) for y, s in zip(y_c, _lane_sums(y_c, head_ones))]
    var_c = [s * (1.0 / N) for s in _lane_sums([yc * yc for yc in yc_c], head_ones)]
    for c, rows in enumerate(chunk_rows):
        yn = yc_c[c] * lax.rsqrt(var_c[c] + RWKV_GN_EPS) * lnw_ref[...] + lnb_ref[...]
        o_ref[0, rows, :] = ((yn + bonus_c[c]) * _silu(z_ref[0, rows, :])).astype(o_ref.dtype)


def _rwkv_scan(r, k, v, z, lw, a, k_k, k_a, r_k, lnx_w, lnx_b):
    b, t, c = r.shape
    tb = min(RWKV_BLOCK, t)
    seq_spec = pl.BlockSpec((1, tb, RWKV_LANES), lambda i, p, j: (i, j, p))
    par_spec = pl.BlockSpec((1, RWKV_LANES), lambda i, p, j: (0, p))
    params = [x.reshape(1, c).astype(F32) for x in (k_k, k_a, r_k, lnx_w, lnx_b)]
    return pl.pallas_call(
        _rwkv_scan_kernel, grid=(b, c // RWKV_LANES, t // tb),
        in_specs=[seq_spec] * 6 + [par_spec] * 5, out_specs=seq_spec,
        out_shape=jax.ShapeDtypeStruct((b, t, c), BF16),
        scratch_shapes=[pltpu.VMEM((RWKV_LANES // RWKV_HEAD, RWKV_HEAD, RWKV_HEAD), F32)],
        compiler_params=_cparams("parallel", "parallel", "arbitrary"), name="rwkv_scan",
    )(r, k, v, z, lw, a, *params)


def _rwkv_layer(x2d, g, mu, w_in, w0, w1, w2, a0, a1, a2, k_k, k_a, r_k, lnx_w, lnx_b, w_out,
                final_g, batch, seq):
    c = w0.shape[0]
    r, k, v, z, lw, a = _rwkv_pre(x2d, g, mu, w_in, w0, w1, w2, a0, a1, a2, seq)
    sh = (batch, seq, c)
    y = _rwkv_scan(r.reshape(sh), k.reshape(sh), v.reshape(sh), z.reshape(sh), lw.reshape(sh),
                   a.reshape(sh), k_k, k_a, r_k, lnx_w, lnx_b)
    return _out_proj(x2d, [y.reshape(x2d.shape[0], c)], None, w_out, final_g)


def _nsa_compress_kernel(u_ref, posa_ref, posb_ref, w1a_ref, w1b_ref, w2_ref, o_ref):
    u = u_ref[0, 0]
    ha = _bdot(u + posa_ref[...], w1a_ref[...])
    hb = _bdot(u + posb_ref[...], w1b_ref[...])
    h = ha + pltpu.roll(hb, hb.shape[0] - 1, 0)
    o_ref[0, 0] = _bdot(_silu(h), w2_ref[...])


def _nsa_compress(t_hm, pos, w1, w2):
    b, g, t, dh = t_hm.shape
    nch = t // NSA_CMP_STRIDE
    half = NSA_CMP_STRIDE * dh
    u = t_hm.reshape(b, g, nch, half)
    posf = pos.astype(F32).reshape(2, 1, half)
    hid = w1.shape[1]
    fixed = lambda i, j: (0, 0)
    return pl.pallas_call(
        _nsa_compress_kernel, grid=(b, g),
        in_specs=[pl.BlockSpec((1, 1, nch, half), lambda i, j: (i, j, 0, 0)),
                  pl.BlockSpec((1, half), fixed), pl.BlockSpec((1, half), fixed),
                  pl.BlockSpec((half, hid), fixed), pl.BlockSpec((half, hid), fixed),
                  pl.BlockSpec((hid, dh), fixed)],
        out_specs=pl.BlockSpec((1, 1, nch, dh), lambda i, j: (i, j, 0, 0)),
        out_shape=jax.ShapeDtypeStruct((b, g, nch, dh), F32),
        compiler_params=_cparams("parallel", "parallel"), name="nsa_compress",
    )(u, posf[0], posf[1], w1[:half].astype(BF16), w1[half:].astype(BF16), w2.astype(BF16))


def _nsa_cmp_attn_kernel(q_ref, kc_ref, vct_ref, gate_ref, o_ref, sel_ref, *, n_sel):
    g = pl.program_id(1)
    n = pl.program_id(2)
    tq = q_ref.shape[2]
    ncmp = kc_ref.shape[2]
    q = q_ref[0].reshape(GQA * tq, HEAD_DIM) * (HEAD_DIM ** -0.5)
    s = _bdot_nt(kc_ref[0, 0], q)
    cend = lax.broadcasted_iota(jnp.int32, (ncmp, tq), 0) * NSA_CMP_STRIDE + (NSA_CMP_LEN - 1)
    tpos = n * tq + lax.broadcasted_iota(jnp.int32, (ncmp, tq), 1)
    ok = jnp.concatenate([cend <= tpos] * GQA, axis=1)
    s = jnp.where(ok, s, NEG)
    m = jnp.max(s, axis=0, keepdims=True)
    e = jnp.where(ok, jnp.exp(s - m), 0.0)
    l = jnp.sum(e, axis=0, keepdims=True)
    p = e / jnp.where(l > 0.0, l, 1.0)
    o_t = jnp.dot(vct_ref[0, 0].astype(BF16), p.astype(BF16), preferred_element_type=F32)
    outs = []
    for r in range(GQA):
        outs.append((o_t[:, r * tq:(r + 1) * tq] * _gate_row(gate_ref, GQA * g + r)).T)
    o_ref[0] = jnp.concatenate(outs, axis=-1).astype(o_ref.dtype)

    psum = p[:, :tq]
    for r in range(1, GQA):
        psum = psum + p[:, r * tq:(r + 1) * tq]
    si = lax.broadcasted_iota(jnp.int32, (n_sel, ncmp), 0)
    ni = lax.broadcasted_iota(jnp.int32, (n_sel, ncmp), 1)
    ratio = NSA_SEL_LEN // NSA_CMP_STRIDE
    overlap = ((ni < ratio * (si + 1)) & (ni * NSA_CMP_STRIDE + NSA_CMP_LEN - 1 >= si * NSA_SEL_LEN))
    overlap = jnp.where(overlap, 1.0, 0.0).astype(BF16)
    imp = _dot_exact_rhs(psum, overlap, dot=lambda a, b: _bdot(b, a))
    blk = lax.broadcasted_iota(jnp.int32, (n_sel, tq), 0)
    tq_pos = n * tq + lax.broadcasted_iota(jnp.int32, (n_sel, tq), 1)
    cur = tq_pos // NSA_SEL_LEN
    forced = (blk == 0) | (blk == cur) | (blk == cur - 1)
    future = blk * NSA_SEL_LEN > tq_pos
    imp = jnp.where(forced, BIG, jnp.where(future, NEG, imp))
    n_grp = n_sel // SUBLANES
    blk_g = blk[:SUBLANES, :LANES]
    for c0 in range(0, tq, LANES):
        imp_c = imp[:, c0:c0 + LANES]
        imp_g = [imp_c[SUBLANES * b:SUBLANES * (b + 1)] for b in range(n_grp)]
        ranks = [jnp.zeros((SUBLANES, LANES), F32) for _ in range(n_grp)]
        for j in range(n_sel):
            row = imp_c[j:j + 1, :]
            for b in range(n_grp):
                if b > j // SUBLANES:
                    ahead = row >= imp_g[b]
                elif b < j // SUBLANES:
                    ahead = row > imp_g[b]
                else:
                    ahead = (row > imp_g[b]) | ((row == imp_g[b]) & (blk_g > j % SUBLANES))
                ranks[b] = ranks[b] + jnp.where(ahead, 1.0, 0.0)
        rank = jnp.concatenate(ranks, axis=0)
        sel = jnp.where(rank < float(min(NSA_TOPK, n_sel)), 1.0, 0.0)
        sel_ref[0, 0, :, c0:c0 + LANES] = sel


def _nsa_cmp_attn(q, k_cmp, v_cmp, gates_t):
    b, _, t, _ = q.shape
    ncmp = k_cmp.shape[2]
    n_sel = t // NSA_SEL_LEN
    d = ATTN_HEADS * HEAD_DIM
    kern = functools.partial(_nsa_cmp_attn_kernel, n_sel=n_sel)
    tq = min(NSA_CMP_QUERIES, t)
    cmp_spec = pl.BlockSpec((1, 1, ncmp, HEAD_DIM), lambda i, g, n: (i, g, 0, 0))
    return pl.pallas_call(
        kern, grid=(b, KV_HEADS, t // tq),
        in_specs=[pl.BlockSpec((1, GQA, tq, HEAD_DIM), lambda i, g, n: (i, g, n, 0)),
                  cmp_spec, pl.BlockSpec((1, 1, HEAD_DIM, ncmp), lambda i, g, n: (i, g, 0, 0)),
                  pl.BlockSpec((1, LANES, tq), lambda i, g, n: (i, 0, n))],
        out_specs=[pl.BlockSpec((1, tq, GQA * HEAD_DIM), lambda i, g, n: (i, n, g)),
                   pl.BlockSpec((1, 1, n_sel, tq), lambda i, g, n: (i, g, 0, n))],
        out_shape=[jax.ShapeDtypeStruct((b, t, d), BF16),
                   jax.ShapeDtypeStruct((b, KV_HEADS, n_sel, t), F32)],
        compiler_params=_cparams("parallel", "parallel", "parallel"), name="nsa_cmp_attn",
    )(q, k_cmp, jnp.swapaxes(v_cmp, 2, 3), gates_t)


def _nsa_sel_attn_kernel(q_ref, k_ref, vt_ref, selt_ref, bias_ref, far_ref, gate_ref, o_ref):
    g = pl.program_id(1)
    n = pl.program_id(2)
    tq = q_ref.shape[2]
    pw = Q_BLOCK
    bl = NSA_SEL_LEN
    fk = NSA_FAR_KEYS
    q_all = q_ref[0].reshape(GQA * tq, HEAD_DIM) * (HEAD_DIM ** -0.5)
    q_t = q_all.astype(F32).T.astype(BF16)

    def block_rows(first_block, count, limit):
        rows = []
        for i in range(count):
            blk = first_block + i
            row = selt_ref[0, 0, pl.ds(jnp.clip(blk, 0, jnp.maximum(limit - 1, 0)), 1), :]
            rows.append(jnp.broadcast_to(jnp.where((blk >= 0) & (blk < limit), row, 0.0), (bl, tq)))
        return jnp.concatenate(rows, axis=0) > 0.5

    def values_t(*starts_widths):
        return jnp.concatenate([vt_ref[0, :, pl.ds(s, w)] for s, w in starts_widths], axis=1)

    prev = pl.multiple_of(jnp.maximum(n * tq - pw, 0), pw)
    diag = pl.multiple_of(n * tq, tq)
    kt = jnp.concatenate([k_ref[0, 0, pl.ds(prev, pw), :], k_ref[0, 0, pl.ds(diag, tq), :]], axis=0)
    vt = values_t((prev, pw), (diag, tq))
    kj = lax.broadcasted_iota(jnp.int32, (pw + tq, tq), 0)
    qi = lax.broadcasted_iota(jnp.int32, (pw + tq, tq), 1)
    n_blocks = (n + 1) * (tq // bl)
    first_near = n * (tq // bl) - pw // bl
    ok = block_rows(first_near, (pw + tq) // bl, n_blocks) & (pw + qi - kj >= 0)
    ok = jnp.concatenate([ok] * GQA, axis=1)
    tbl = bias_ref[0]
    far_row = far_ref[0]
    cols = []
    for r in range(GQA):
        prev_p = tbl[:pw, r * pw:(r + 1) * pw]
        diag_p = tbl[pw:, r * pw:(r + 1) * pw]
        for a in range(tq // pw):
            far_p = jnp.broadcast_to(far_row[:, r * tq + a * pw:r * tq + (a + 1) * pw], (pw, pw))
            below = [jnp.zeros((pw, pw), F32)] * (tq // pw - 1 - a)
            cols.append(jnp.concatenate([far_p] * a + [prev_p, diag_p] + below, axis=0))
    bias = jnp.concatenate(cols, axis=1)
    s = jnp.where(ok, jnp.dot(kt, q_t, preferred_element_type=F32) + bias, NEG)
    m = jnp.max(s, axis=0, keepdims=True)
    p = jnp.exp(s - m).astype(BF16)
    acc = jnp.dot(vt, p, preferred_element_type=F32)
    m = m - far_ref[0]
    far_blocks = jnp.maximum(first_near, 0)

    def body(c, carry):
        m, acc = carry
        start = pl.multiple_of(c * fk, fk)
        qk = jnp.dot(k_ref[0, 0, pl.ds(start, fk), :], q_t, preferred_element_type=F32)
        parts = []
        for i in range(fk // bl):
            blk = c * (fk // bl) + i
            row = selt_ref[0, 0, pl.ds(jnp.minimum(blk, jnp.maximum(far_blocks - 1, 0)), 1), :]
            row = jnp.where((blk < far_blocks) & (row > 0.5), 0.0, NEG)
            parts.append(qk[i * bl:(i + 1) * bl] + jnp.concatenate([row] * GQA, axis=1))
        s = jnp.concatenate(parts, axis=0)
        m_new = jnp.maximum(m, jnp.max(s, axis=0, keepdims=True))
        p = jnp.exp(s - m_new).astype(BF16)
        pv = jnp.dot(values_t((start, fk)), p, preferred_element_type=F32)
        return m_new, jnp.exp(m - m_new) * acc + pv

    n_far = (far_blocks * bl + fk - 1) // fk
    _, acc = lax.fori_loop(0, n_far, body, (m, acc))
    o_t = acc[:HEAD_DIM] / acc[HEAD_DIM:HEAD_DIM + 1]
    outs = []
    for r in range(GQA):
        o = o_t[:, r * tq:(r + 1) * tq] * _gate_row(gate_ref, ATTN_HEADS + GQA * g + r)
        outs.append(o.T)
    o_ref[0] = jnp.concatenate(outs, axis=-1).astype(o_ref.dtype)


def _nsa_sel_attn(q, ks, vst, sel_t, t5_table, gates_t):
    b, _, t, _ = q.shape
    n_sel = t // NSA_SEL_LEN
    d = ATTN_HEADS * HEAD_DIM
    tq = min(NSA_SEL_QUERIES, t)
    near = Q_BLOCK + tq
    width = GQA * tq
    assert t % NSA_FAR_KEYS == 0 or t <= near
    bias_near = _band_bias(t5_table, Q_BLOCK, 2 * Q_BLOCK, Q_BLOCK)
    bias_near = bias_near.reshape(KV_HEADS, GQA, Q_BLOCK, 2 * Q_BLOCK).transpose(0, 3, 1, 2)
    bias_near = bias_near.reshape(KV_HEADS, 2 * Q_BLOCK, GQA * Q_BLOCK)
    far = t5_table[T5_BUCKETS - 1].astype(F32).reshape(KV_HEADS, GQA, 1)
    far = jnp.broadcast_to(far, (KV_HEADS, GQA, tq)).reshape(KV_HEADS, 1, width)
    vst_ones = vst.reshape(b, KV_HEADS, HEAD_DIM, t)
    vst_ones = jnp.concatenate([vst_ones, jnp.ones_like(vst_ones)], axis=2)
    vst_ones = vst_ones.reshape(b, 2 * KV_HEADS * HEAD_DIM, t)
    return pl.pallas_call(
        _nsa_sel_attn_kernel, grid=(b, KV_HEADS, t // tq),
        in_specs=[pl.BlockSpec((1, GQA, tq, HEAD_DIM), lambda i, g, n: (i, g, n, 0)),
                  pl.BlockSpec((1, 1, t, HEAD_DIM), lambda i, g, n: (i, g, 0, 0)),
                  pl.BlockSpec((1, 2 * HEAD_DIM, t), lambda i, g, n: (i, g, 0)),
                  pl.BlockSpec((1, 1, n_sel, tq), lambda i, g, n: (i, g, 0, n)),
                  pl.BlockSpec((1, 2 * Q_BLOCK, GQA * Q_BLOCK), lambda i, g, n: (g, 0, 0)),
                  pl.BlockSpec((1, 1, width), lambda i, g, n: (g, 0, 0)),
                  pl.BlockSpec((1, LANES, tq), lambda i, g, n: (i, 0, n))],
        out_specs=pl.BlockSpec((1, tq, GQA * HEAD_DIM), lambda i, g, n: (i, n, g)),
        out_shape=jax.ShapeDtypeStruct((b, t, d), BF16),
        compiler_params=_cparams("parallel", "parallel", "parallel"), name="nsa_sel_attn",
    )(q, ks, vst_ones, sel_t, bias_near, far, gates_t)


def _nsa_layer(x2d, g, w_in, pos_k, k_w1, k_w2, pos_v, v_w1, v_w2, w_out, t5_table, final_g,
               batch, seq):
    nq, nkv = ATTN_HEADS * HEAD_DIM, KV_HEADS * HEAD_DIM
    offs = [0, nq] + [nq + nkv * (i + 1) for i in range(6)]
    ws = [w_in[:, offs[i]:offs[i + 1]] for i in range(7)]
    n_gate = 3 * ATTN_HEADS
    wg = jnp.pad(w_in[:, offs[7]:offs[7] + n_gate], ((0, 0), (0, LANES - n_gate)))
    wz = w_in[:, offs[7] + n_gate:]
    q, kc, vc, ks, vst, kw, vwt, gates_t, z = _norm_proj(
        x2d, g, ws + [wg, wz],
        ["heads", "heads", "heads", "heads", "cols", "heads", "cols", "cols", "rows"],
        [BF16, F32, F32, BF16, BF16, BF16, BF16, F32, F32], batch, seq)
    k_cmp = _nsa_compress(kc, pos_k, k_w1, k_w2)
    v_cmp = _nsa_compress(vc, pos_v, v_w1, v_w2)
    o_cmp, sel_t = _nsa_cmp_attn(q, k_cmp, v_cmp, gates_t)
    o_sel = _nsa_sel_attn(q, ks, vst, sel_t, t5_table, gates_t)
    o_win = _band_attn(q, kw, vwt, t5_table, NSA_WINDOW, gates_t=gates_t, gate_col=2 * ATTN_HEADS)
    sh = x2d.shape
    return _out_proj(x2d, [o_cmp.reshape(sh), o_sel.reshape(sh), o_win.reshape(sh)], z, w_out,
                     final_g)


def _lru_kernel(x_ref, g_ref, win_u_ref, win_z_ref, cw_ref, cb_ref, wa_ref, ba_ref, wx_ref, bx_ref,
                sp_ref, wout_ref, *rest, half, has_fg):
    fg_ref = rest[0] if has_fg else None
    o_ref, tail_ref, h_ref = rest[-3:]
    j = pl.program_id(1)

    @pl.when(j == 0)
    def _():
        tail_ref[...] = jnp.zeros_like(tail_ref)
        h_ref[...] = jnp.zeros_like(h_ref)

    x = x_ref[...]
    xb = _rms(x, g_ref[...]).astype(BF16)
    u = jnp.dot(xb, win_u_ref[...], preferred_element_type=F32)
    z = jnp.dot(xb, win_z_ref[...], preferred_element_type=F32)
    tm, width = u.shape
    ext = jnp.concatenate([tail_ref[...], u], axis=0)
    tail_ref[...] = u[tm - SUBLANES:, :]
    uc = cb_ref[...] + cw_ref[CONV_WIDTH - 1:CONV_WIDTH, :] * u
    for s in range(1, CONV_WIDTH):
        uc = uc + cw_ref[CONV_WIDTH - 1 - s:CONV_WIDTH - s, :] * ext[SUBLANES - s:SUBLANES - s + tm, :]
    ucb = uc.astype(BF16)
    gr, gi = [], []
    for c in range(width // half):
        blk = ucb[:, c * half:(c + 1) * half]
        gr.append(jnp.dot(blk, wa_ref[c], preferred_element_type=F32))
        gi.append(jnp.dot(blk, wx_ref[c], preferred_element_type=F32))
    rg = jax.nn.sigmoid(jnp.concatenate(gr, axis=1) + ba_ref[...])
    ig = jax.nn.sigmoid(jnp.concatenate(gi, axis=1) + bx_ref[...])
    log_a = -LRU_C * rg * sp_ref[...]
    a = jnp.exp(log_a)
    bv = jnp.sqrt(1.0 - a * a) * (ig * uc)
    n_grp = tm // SUBLANES
    a = a.reshape(n_grp, SUBLANES, width)
    bv = bv.reshape(n_grp, SUBLANES, width)
    row = lax.broadcasted_iota(jnp.int32, (n_grp, SUBLANES, width), 1)
    sh = 1
    while sh < SUBLANES:
        a_s = jnp.where(row >= sh, pltpu.roll(a, sh, 1), 1.0)
        b_s = jnp.where(row >= sh, pltpu.roll(bv, sh, 1), 0.0)
        bv = a * b_s + bv
        a = a * a_s
        sh *= 2
    carry = h_ref[...]
    groups = []
    for i in range(n_grp):
        hg = bv[i] + a[i] * carry
        carry = hg[SUBLANES - 1:SUBLANES, :]
        groups.append(hg)
    h_ref[...] = carry
    gated = (jnp.concatenate(groups, axis=0) * _silu(z)).astype(BF16)
    y = x + jnp.dot(gated, wout_ref[...], preferred_element_type=F32)
    o_ref[...] = _rms(y, fg_ref[...]) if has_fg else y


def _block_diag(w, group):
    nb, n, _ = w.shape
    w = w.reshape(nb // group, group, n, n)
    eye = jnp.eye(group, dtype=w.dtype)
    return jnp.einsum('cgij,gh->cgihj', w, eye).reshape(nb // group, group * n, group * n)


def _lru_layer(x2d, g, w_in, conv_w, conv_b, ga_w, ga_b, gx_w, gx_b, lam, w_out, final_g, batch, seq):
    n, d = x2d.shape
    width = w_in.shape[1] // 2
    blk = ga_w.shape[1]
    group = LANES // math.gcd(blk, LANES)
    group = min(group, ga_w.shape[0])
    half = group * blk
    nsup = width // half
    tm = min(LRU_TILE, seq)
    tiles = seq // tm
    row = lambda i, j: (i * tiles + j, 0)
    fixed = lambda i, j: (0, 0)
    fixed3 = lambda i, j: (0, 0, 0)
    vec = lambda a: a.astype(F32).reshape(1, -1)
    softplus_neg_lam = jax.nn.softplus(-lam.astype(F32))
    has_fg = final_g is not None
    in_specs = [pl.BlockSpec((tm, d), row), pl.BlockSpec((1, d), fixed),
                pl.BlockSpec((d, width), fixed), pl.BlockSpec((d, width), fixed),
                pl.BlockSpec((CONV_WIDTH, width), fixed), pl.BlockSpec((1, width), fixed),
                pl.BlockSpec((nsup, half, half), fixed3), pl.BlockSpec((1, width), fixed),
                pl.BlockSpec((nsup, half, half), fixed3), pl.BlockSpec((1, width), fixed),
                pl.BlockSpec((1, width), fixed), pl.BlockSpec((width, d), fixed)]
    args = [x2d, g.reshape(1, d), w_in[:, :width].astype(BF16), w_in[:, width:].astype(BF16),
            conv_w.astype(F32), vec(conv_b), _block_diag(ga_w, group).astype(BF16), vec(ga_b),
            _block_diag(gx_w, group).astype(BF16), vec(gx_b), vec(softplus_neg_lam),
            w_out.astype(BF16)]
    if has_fg:
        in_specs.append(pl.BlockSpec((1, d), fixed))
        args.append(final_g.reshape(1, d))
    kern = functools.partial(_lru_kernel, half=half, has_fg=has_fg)
    return pl.pallas_call(
        kern, grid=(batch, tiles), in_specs=in_specs, out_specs=pl.BlockSpec((tm, d), row),
        out_shape=jax.ShapeDtypeStruct((n, d), F32),
        scratch_shapes=[pltpu.VMEM((SUBLANES, width), F32), pltpu.VMEM((1, width), F32)],
        compiler_params=_cparams("parallel", "arbitrary"), name="rglru_layer",
    )(*args)


def kernel(x, t5_table, norm_g, final_g, a_w_in, a_sinks, a_w_out, b_mu, b_w_in, b_w0, b_w1, b_w2, b_a0, b_a1, b_a2, b_k_k, b_k_a, b_r_k, b_lnx_w, b_lnx_b, b_w_out, c_w_in, c_cmp_pos_k, c_cmp_k_w1, c_cmp_k_w2, c_cmp_pos_v, c_cmp_v_w1, c_cmp_v_w2, c_w_out, d_w_in, d_conv_w, d_conv_b, d_gate_a_w, d_gate_a_b, d_gate_x_w, d_gate_x_b, d_lambda, d_w_out):
    batch, seq, d = x.shape
    depth = norm_g.shape[0]
    h = x.reshape(batch * seq, d)
    for layer in range(depth):
        m, j = layer % 4, layer // 4
        g = norm_g[layer]
        fg = final_g if layer == depth - 1 else None
        if m == 0:
            h = _swa_layer(h, g, a_w_in[j], a_sinks[j], a_w_out[j], t5_table, fg, batch, seq)
        elif m == 1:
            h = _rwkv_layer(h, g, b_mu[j], b_w_in[j], b_w0[j], b_w1[j], b_w2[j], b_a0[j], b_a1[j],
                            b_a2[j], b_k_k[j], b_k_a[j], b_r_k[j], b_lnx_w[j], b_lnx_b[j],
                            b_w_out[j], fg, batch, seq)
        elif m == 2:
            h = _nsa_layer(h, g, c_w_in[j], c_cmp_pos_k[j], c_cmp_k_w1[j], c_cmp_k_w2[j],
                           c_cmp_pos_v[j], c_cmp_v_w1[j], c_cmp_v_w2[j], c_w_out[j], t5_table,
                           fg, batch, seq)
        else:
            h = _lru_layer(h, g, d_w_in[j], d_conv_w[j], d_conv_b[j], d_gate_a_w[j], d_gate_a_b[j],
                           d_gate_x_w[j], d_gate_x_b[j], d_lambda[j], d_w_out[j], fg, batch, seq)
    return h.reshape(batch, seq, d)
```
